```python
import jax, jax.numpy as jnp
from jax import lax
import numpy as np

D_MODEL = 1024
BATCH = 2
SEQ = 16384
DEPTH = 2
DEC_BATCH = 16
DEC_SEQ = 64
PAST_LEN = 1024

CHUNK = 64
D_CONV = 512
CONV_W = 3
N_HEADS = 8
N_KV_HEADS = 2
GROUP = N_HEADS // N_KV_HEADS
HEAD_DIM = 64
D_ATTN = N_HEADS * HEAD_DIM
D_KV = N_KV_HEADS * HEAD_DIM
WINDOW = 128
WIN_CHUNKS = WINDOW // CHUNK
D_MIX = D_CONV + D_ATTN
D_IN = 3 * D_CONV + D_ATTN + 2 * D_KV
N_GROUPS = 4
EXP_PER_GROUP = 8
N_EXPERTS = N_GROUPS * EXP_PER_GROUP
TOP_K = 2
D_EXP = 512
MOE_BLOCK = 128
ALPHA = (2 * DEPTH) ** 0.25
BETA = (8 * DEPTH) ** -0.25
LN_EPS = 1e-5
NEG_INF = -1e30

kernel_name = 'hymba_conv_swa_hmoe_stream_step'


def _alibi_slopes():
    return jnp.asarray([2.0 ** (-8.0 * (h + 1) / N_HEADS) for h in range(N_HEADS)], jnp.float32)


def _layer_norm(x, g, b):
    xf = x.astype(jnp.float32)
    mu = jnp.mean(xf, -1, keepdims=True)
    var = jnp.mean(jnp.square(xf - mu), -1, keepdims=True)
    y = (xf - mu) * lax.rsqrt(var + LN_EPS)
    return (y * g.astype(jnp.float32) + b.astype(jnp.float32)).astype(x.dtype)


def _rms_norm(x, g):
    xf = x.astype(jnp.float32)
    y = xf * lax.rsqrt(jnp.mean(jnp.square(xf), -1, keepdims=True) + LN_EPS)
    return (y * g.astype(jnp.float32)).astype(x.dtype)


def _short_conv(u, buf, w, b):
    T = u.shape[1]
    ext = jnp.concatenate([buf, u], axis=1)
    y = b + ext[:, 0:T] * w[0]
    for j in range(1, CONV_W):
        y = y + ext[:, j:j + T] * w[j]
    return y, ext[:, -(CONV_W - 1):]


def _sink_softmax(logits, sink):
    m = jnp.maximum(jnp.max(logits, -1, keepdims=True), sink)
    e = jnp.exp(logits - m)
    return e / (jnp.sum(e, -1, keepdims=True) + jnp.exp(sink - m))


def _swa_prompt(q, k, v, sinks):
    B, T = q.shape[0], q.shape[1]
    NC = T // CHUNK
    pad = WIN_CHUNKS * CHUNK
    KB = (WIN_CHUNKS + 1) * CHUNK
    qb = q.reshape(B, NC, CHUNK, N_KV_HEADS, GROUP, HEAD_DIM)
    kp = jnp.pad(k, ((0, 0), (pad, 0), (0, 0), (0, 0))).reshape(B, NC + WIN_CHUNKS, CHUNK, N_KV_HEADS, HEAD_DIM)
    vp = jnp.pad(v, ((0, 0), (pad, 0), (0, 0), (0, 0))).reshape(B, NC + WIN_CHUNKS, CHUNK, N_KV_HEADS, HEAD_DIM)
    kb = jnp.concatenate([kp[:, j:j + NC] for j in range(WIN_CHUNKS + 1)], axis=2)
    vb = jnp.concatenate([vp[:, j:j + NC] for j in range(WIN_CHUNKS + 1)], axis=2)
    scores = jnp.einsum('bcqkgd,bcskd->bckgqs', qb, kb, preferred_element_type=jnp.float32) * (HEAD_DIM ** -0.5)
    qi = jnp.arange(CHUNK, dtype=jnp.int32)
    sj = jnp.arange(KB, dtype=jnp.int32)
    dist = jnp.abs(qi[:, None] + pad - sj[None, :]).astype(jnp.float32)
    bias = -_alibi_slopes().reshape(N_KV_HEADS, GROUP, 1, 1) * dist
    valid = (jnp.arange(NC, dtype=jnp.int32)[:, None] * CHUNK - pad + sj[None, :]) >= 0
    logits = jnp.where(valid[None, :, None, None, None, :], scores + bias, NEG_INF)
    p = _sink_softmax(logits, sinks.astype(jnp.float32).reshape(N_KV_HEADS, GROUP, 1, 1))
    out = jnp.einsum('bckgqs,bcskd->bcqkgd', p.astype(v.dtype), vb)
    return out.reshape(B, T, D_ATTN)


def _swa_sample(q, k, v, k_buf, v_buf, sinks):
    B, T = q.shape[0], q.shape[1]
    Wc = k_buf.shape[1]
    kk = jnp.concatenate([k_buf, k], axis=1)
    vv = jnp.concatenate([v_buf, v], axis=1)
    scores = jnp.einsum('btkgd,bskd->bkgts', q, kk, preferred_element_type=jnp.float32) * (HEAD_DIM ** -0.5)
    dist = jnp.abs(jnp.arange(T, dtype=jnp.int32)[:, None] + Wc - jnp.arange(Wc + T, dtype=jnp.int32)[None, :]).astype(jnp.float32)
    bias = -_alibi_slopes().reshape(N_KV_HEADS, GROUP, 1, 1) * dist
    p = _sink_softmax(scores + bias, sinks.astype(jnp.float32).reshape(N_KV_HEADS, GROUP, 1, 1))
    out = jnp.einsum('bkgts,bskd->btkgd', p.astype(v.dtype), vv).reshape(B, T, D_ATTN)
    return out, kk[:, -Wc:], vv[:, -Wc:]


def _moe_dispatch(xf, eidx, gates, w_gate, w_up, w_down):
    N, D = xf.shape
    NK = N * TOP_K
    NB = -(-NK // MOE_BLOCK) + N_EXPERTS
    flat_e = eidx.reshape(NK).astype(jnp.int32)
    flat_tok = jnp.repeat(jnp.arange(N, dtype=jnp.int32), TOP_K)
    flat_w = gates.reshape(NK)
    counts = jnp.zeros((N_EXPERTS,), jnp.int32).at[flat_e].add(1)
    padded = (counts + MOE_BLOCK - 1) // MOE_BLOCK * MOE_BLOCK
    pad_end = jnp.cumsum(padded)
    pad_start = pad_end - padded
    start = jnp.cumsum(counts) - counts
    order = jnp.argsort(flat_e)
    se = flat_e[order]
    dest = pad_start[se] + jnp.arange(NK, dtype=jnp.int32) - start[se]
    buf_tok = jnp.full((NB * MOE_BLOCK,), N, jnp.int32).at[dest].set(flat_tok[order])
    buf_w = jnp.zeros((NB * MOE_BLOCK,), xf.dtype).at[dest].set(flat_w[order].astype(xf.dtype))
    block_e = jnp.minimum(jnp.searchsorted(pad_end, jnp.arange(NB, dtype=jnp.int32) * MOE_BLOCK, side='right'), N_EXPERTS - 1).astype(jnp.int32)
    x_pad = jnp.concatenate([xf, jnp.zeros((1, D), xf.dtype)], axis=0)
    xb = x_pad[buf_tok].reshape(NB, MOE_BLOCK, D)

    def expert_block(args):
        xblk, e = args
        hid = jax.nn.silu(xblk @ w_gate[e]) * (xblk @ w_up[e])
        return hid @ w_down[e]

    yb = lax.map(expert_block, (xb, block_e)).reshape(NB * MOE_BLOCK, D)
    y = jnp.zeros((N + 1, D), xf.dtype).at[buf_tok].add(yb * buf_w[:, None])
    return y[:N]


def _hier_moe(x, rg_w, rg_b, re_w, re_b, w_gate, w_up, w_down):
    B, T, D = x.shape
    xf = x.reshape(B * T, D)
    lg = (xf @ rg_w).astype(jnp.float32) + rg_b.astype(jnp.float32)
    pg = jax.nn.softmax(lg, axis=-1)
    grp = jnp.argmax(lg, axis=-1).astype(jnp.int32)
    pg_sel = jnp.take_along_axis(pg, grp[:, None], axis=-1)
    le = ((xf @ re_w).astype(jnp.float32) + re_b.astype(jnp.float32)).reshape(B * T, N_GROUPS, EXP_PER_GROUP)
    le_sel = jnp.take_along_axis(le, grp[:, None, None], axis=1)[:, 0]
    top_v, top_i = lax.top_k(le_sel, TOP_K)
    gates = pg_sel * jax.nn.softmax(top_v, axis=-1)
    eidx = grp[:, None] * EXP_PER_GROUP + top_i
    return _moe_dispatch(xf, eidx, gates, w_gate, w_up, w_down).reshape(B, T, D)


def _layer(x, conv_buf, k_buf, v_buf, prm):
    (w_in, conv_w, conv_b, sinks, g_conv, g_attn, w_out, ln1_g, ln1_b,
     rg_w, rg_b, re_w, re_b, we_g, we_u, we_d, ln2_g, ln2_b) = prm
    B, T, _ = x.shape
    z = jnp.einsum('btd,de->bte', x, w_in)
    b_gate = z[..., 0:D_CONV]
    c_gate = z[..., D_CONV:2 * D_CONV]
    h = z[..., 2 * D_CONV:3 * D_CONV]
    o = 3 * D_CONV
    q = z[..., o:o + D_ATTN].reshape(B, T, N_KV_HEADS, GROUP, HEAD_DIM)
    o = o + D_ATTN
    k = z[..., o:o + D_KV].reshape(B, T, N_KV_HEADS, HEAD_DIM)
    o = o + D_KV
    v = z[..., o:o + D_KV].reshape(B, T, N_KV_HEADS, HEAD_DIM)
    yc, new_conv = _short_conv(c_gate * h, conv_buf, conv_w, conv_b)
    y_conv = b_gate * yc
    if k_buf is None:
        y_attn = _swa_prompt(q, k, v, sinks)
        new_k = k[:, -WINDOW:]
        new_v = v[:, -WINDOW:]
    else:
        y_attn, new_k, new_v = _swa_sample(q, k, v, k_buf, v_buf, sinks)
    merged = jnp.concatenate([_rms_norm(y_conv, g_conv), _rms_norm(y_attn, g_attn)], axis=-1)
    mixed = jnp.einsum('btm,md->btd', merged, w_out)
    x = _layer_norm(ALPHA * x + mixed, ln1_g, ln1_b)
    x = _layer_norm(ALPHA * x + _hier_moe(x, rg_w, rg_b, re_w, re_b, we_g, we_u, we_d), ln2_g, ln2_b)
    return x, new_conv, new_k, new_v


def setup_inputs(seed: int = 0) -> dict:
    key = jax.random.key(seed)
    ks = jax.random.split(key, 24)

    def nrm(k, shape, scale):
        return jax.random.normal(k, shape, jnp.float32) * scale

    win_rows = min(WINDOW, PAST_LEN)
    w_in = nrm(ks[5], (DEPTH, D_MODEL, D_IN), D_MODEL ** -0.5)
    w_in = w_in.at[..., D_IN - D_KV:].multiply(BETA)
    return {
        'x_prompt': nrm(ks[0], (BATCH, SEQ, D_MODEL), 1.0),
        'x_sample': nrm(ks[1], (DEC_BATCH, DEC_SEQ, D_MODEL), 1.0),
        'cache_conv': nrm(ks[2], (DEPTH, DEC_BATCH, CONV_W - 1, D_CONV), 1.0),
        'cache_k': nrm(ks[3], (DEPTH, DEC_BATCH, win_rows, N_KV_HEADS, HEAD_DIM), 1.0),
        'cache_v': nrm(ks[4], (DEPTH, DEC_BATCH, win_rows, N_KV_HEADS, HEAD_DIM), BETA),
        'w_in': w_in,
        'conv_w': nrm(ks[6], (DEPTH, CONV_W, D_CONV), CONV_W ** -0.5),
        'conv_b': nrm(ks[7], (DEPTH, D_CONV), 0.01),
        'attn_sinks': nrm(ks[8], (DEPTH, N_HEADS), 0.5),
        'g_conv': 1.0 + nrm(ks[9], (DEPTH, D_CONV), 0.01),
        'g_attn': 1.0 + nrm(ks[10], (DEPTH, D_ATTN), 0.01),
        'w_out': nrm(ks[11], (DEPTH, D_MIX, D_MODEL), BETA * D_MIX ** -0.5),
        'ln1_g': 1.0 + nrm(ks[12], (DEPTH, D_MODEL), 0.01),
        'ln1_b': nrm(ks[13], (DEPTH, D_MODEL), 0.01),
        'router_group_w': nrm(ks[14], (DEPTH, D_MODEL, N_GROUPS), D_MODEL ** -0.5),
        'router_group_b': nrm(ks[15], (DEPTH, N_GROUPS), 0.01),
        'router_expert_w': nrm(ks[16], (DEPTH, D_MODEL, N_EXPERTS), D_MODEL ** -0.5),
        'router_expert_b': nrm(ks[17], (DEPTH, N_EXPERTS), 0.01),
        'expert_w_gate': nrm(ks[18], (DEPTH, N_EXPERTS, D_MODEL, D_EXP), D_MODEL ** -0.5),
        'expert_w_up': nrm(ks[19], (DEPTH, N_EXPERTS, D_MODEL, D_EXP), D_MODEL ** -0.5),
        'expert_w_down': nrm(ks[20], (DEPTH, N_EXPERTS, D_EXP, D_MODEL), BETA * D_EXP ** -0.5),
        'ln2_g': 1.0 + nrm(ks[21], (DEPTH, D_MODEL), 0.01),
        'ln2_b': nrm(ks[22], (DEPTH, D_MODEL), 0.01),
    }


def reference(x_prompt, x_sample, cache_conv, cache_k, cache_v, w_in, conv_w, conv_b, attn_sinks,
              g_conv, g_attn, w_out, ln1_g, ln1_b, router_group_w, router_group_b, router_expert_w,
              router_expert_b, expert_w_gate, expert_w_up, expert_w_down, ln2_g, ln2_b):
    xp = x_prompt
    xs = x_sample
    p_conv, p_k, p_v, s_conv, s_k, s_v = [], [], [], [], [], []
    for l in range(DEPTH):
        prm = (w_in[l], conv_w[l], conv_b[l], attn_sinks[l], g_conv[l], g_attn[l], w_out[l],
               ln1_g[l], ln1_b[l], router_group_w[l], router_group_b[l], router_expert_w[l],
               router_expert_b[l], expert_w_gate[l], expert_w_up[l], expert_w_down[l], ln2_g[l], ln2_b[l])
        zero_buf = jnp.zeros((xp.shape[0], CONV_W - 1, D_CONV), xp.dtype)
        xp, c_new, k_new, v_new = _layer(xp, zero_buf, None, None, prm)
        p_conv.append(c_new)
        p_k.append(k_new)
        p_v.append(v_new)
        xs, c_new, k_new, v_new = _layer(xs, cache_conv[l], cache_k[l], cache_v[l], prm)
        s_conv.append(c_new)
        s_k.append(k_new)
        s_v.append(v_new)
    return (xp, xs, jnp.stack(p_conv), jnp.stack(p_k), jnp.stack(p_v), jnp.stack(s_conv), jnp.stack(s_k), jnp.stack(s_v))
```

```python
import functools

import numpy as np
import jax
import jax.numpy as jnp
from jax import lax
from jax.experimental import pallas as pl
from jax.experimental.pallas import tpu as pltpu

D_MODEL = 1024
D_CONV = 512
CONV_W = 3
N_HEADS = 8
N_KV_HEADS = 2
GROUP = N_HEADS // N_KV_HEADS
HEAD_DIM = 64
D_ATTN = N_HEADS * HEAD_DIM
D_KV = N_KV_HEADS * HEAD_DIM
WINDOW = 128
CHUNK = 64
KEYS = WINDOW + CHUNK
N_GROUPS = 4
EXP_PER_GROUP = 8
N_EXPERTS = N_GROUPS * EXP_PER_GROUP
TOP_K = 2
D_EXP = 512
LN_EPS = 1e-5
NEG_INF = -1e30

SUBLANES = 8
LANES = 128
ROW_TILES = D_MODEL // LANES
assert ROW_TILES == SUBLANES
ROWS = 512
EXPERT_BLOCK = 512
RANK_CHUNK = 256
VMEM_LIMIT = 56 * 1024 * 1024

F32 = jnp.float32
BF16 = jnp.bfloat16

_OFF_B, _OFF_C, _OFF_H = 0, D_CONV, 2 * D_CONV
_OFF_Q = 3 * D_CONV
_OFF_K = _OFF_Q + D_ATTN
_OFF_V = _OFF_K + D_KV
D_IN = _OFF_V + D_KV

_R_E1, _R_E2, _R_RANK1, _R_RANK2, _R_G1, _R_G2 = 0, 1, 2, 3, 4, 5


def _load_rows(ref, n):
    return jnp.concatenate([ref[pl.ds(j, n, stride=ROW_TILES), :] for j in range(ROW_TILES)], axis=1)


def _store_rows(ref, val, n):
    for j in range(ROW_TILES):
        ref[pl.ds(j, n, stride=ROW_TILES), :] = val[:, j * LANES:(j + 1) * LANES]


def _dot(a, b):
    return jnp.dot(a, b, preferred_element_type=F32)


def _rms_norm(x, g):
    return x * lax.rsqrt(jnp.mean(jnp.square(x), -1, keepdims=True) + LN_EPS) * g


def _layer_norm(x, g, b):
    mu = jnp.mean(x, -1, keepdims=True)
    xc = x - mu
    var = jnp.mean(jnp.square(xc), -1, keepdims=True)
    return xc * lax.rsqrt(var + LN_EPS) * g + b


def _mixer_kernel(x_ref, cconv_ref, ck_ref, cv_ref, w_in_ref, convw_ref, convb_ref, abias_ref,
                  sink_ref, gconv_ref, gattn_ref, w_out_ref, ln1g_ref, ln1b_ref, wr_ref, br_ref,
                  tri_ref,
                  x1_ref, route_ref, gcol_ref, counts_ref, sconv_ref, sk_ref, sv_ref,
                  uext, kext, vext, q_s, yattn_s, cnt_s, *, nseq, tl, alpha, mask_history):
    b = pl.program_id(0)
    s = pl.program_id(1)
    rows = nseq * tl
    upitch = tl + SUBLANES
    kpitch = WINDOW + tl

    @pl.when(jnp.logical_and(b == 0, s == 0))
    def _():
        cnt_s[...] = jnp.zeros_like(cnt_s)

    @pl.when(s == 0)
    def _():
        for j in range(nseq):
            uext[j * upitch + SUBLANES - 2:j * upitch + SUBLANES, :] = cconv_ref[j]
            kext[j * kpitch:j * kpitch + WINDOW, :] = ck_ref[j]
            vext[j * kpitch:j * kpitch + WINDOW, :] = cv_ref[j]

    x = x_ref[...].reshape(rows, D_MODEL)
    xb = x.astype(BF16)

    u = _dot(xb, w_in_ref[:, _OFF_C:_OFF_C + D_CONV]) * _dot(xb, w_in_ref[:, _OFF_H:_OFF_H + D_CONV])
    w0, w1, w2 = convw_ref[0:1, :], convw_ref[1:2, :], convw_ref[2:3, :]
    ycs = []
    for j in range(nseq):
        base = j * upitch + SUBLANES
        uext[base:base + tl, :] = u[j * tl:(j + 1) * tl]
        yc = convb_ref[...] + uext[base - 2:base - 2 + tl, :] * w0
        yc = yc + uext[base - 1:base - 1 + tl, :] * w1
        yc = yc + uext[base:base + tl, :] * w2
        ycs.append(yc)
        tail = uext[base + tl - 2:base + tl, :]
        sconv_ref[j] = tail
        uext[base - 2:base, :] = tail
    yc = ycs[0] if nseq == 1 else jnp.concatenate(ycs, axis=0)
    y_conv = _dot(xb, w_in_ref[:, _OFF_B:_OFF_B + D_CONV]) * yc
    n_conv = _rms_norm(y_conv, gconv_ref[...]).astype(BF16)

    q_s[...] = (_dot(xb, w_in_ref[:, _OFF_Q:_OFF_Q + D_ATTN]) * (HEAD_DIM ** -0.5)).astype(BF16)
    k = _dot(xb, w_in_ref[:, _OFF_K:_OFF_K + D_KV])
    v = _dot(xb, w_in_ref[:, _OFF_V:_OFF_V + D_KV])
    for j in range(nseq):
        kext[j * kpitch + WINDOW:(j + 1) * kpitch, :] = k[j * tl:(j + 1) * tl]
        vext[j * kpitch + WINDOW:(j + 1) * kpitch, :] = v[j * tl:(j + 1) * tl]

    chunks_per_seq = tl // CHUNK
    assert nseq == 1 or chunks_per_seq == 1
    key_stride = CHUNK if nseq == 1 else kpitch
    col = lax.broadcasted_iota(jnp.int32, (1, KEYS), 1)

    def attend(i):
        r0 = pl.multiple_of(i * CHUNK, CHUNK)
        off = pl.multiple_of(i * key_stride, CHUNK)
        qc = q_s[pl.ds(r0, CHUNK), :]
        kc = kext[pl.ds(off, KEYS), :].astype(BF16)
        vc = vext[pl.ds(off, KEYS), :].astype(BF16)
        if mask_history:
            first_pos = s * tl + i * CHUNK - WINDOW
            maskrow = jnp.where(first_pos + col < 0, NEG_INF, 0.0).astype(F32)
        for kvh in range(N_KV_HEADS):
            heads = [qc[:, (kvh * GROUP + g) * HEAD_DIM:(kvh * GROUP + g + 1) * HEAD_DIM] for g in range(GROUP)]
            qst = jnp.concatenate(heads, axis=0)
            kk = kc[:, kvh * HEAD_DIM:(kvh + 1) * HEAD_DIM]
            logits = lax.dot_general(qst, kk, (((1,), (1,)), ((), ())), preferred_element_type=F32)
            logits = logits + abias_ref[kvh]
            if mask_history:
                logits = logits + maskrow
            sink = sink_ref[kvh]
            m = jnp.maximum(jnp.max(logits, -1, keepdims=True), sink)
            e = jnp.exp(logits - m)
            den = jnp.sum(e, -1, keepdims=True) + jnp.exp(sink - m)
            p = (e * (1.0 / den)).astype(BF16)
            o = _dot(p, vc[:, kvh * HEAD_DIM:(kvh + 1) * HEAD_DIM])
            for g in range(GROUP):
                h = kvh * GROUP + g
                yattn_s[pl.ds(r0, CHUNK), h * HEAD_DIM:(h + 1) * HEAD_DIM] = o[g * CHUNK:(g + 1) * CHUNK]

    def attend_body(i, carry):
        attend(i)
        return carry

    lax.fori_loop(0, rows // CHUNK, attend_body, 0)

    for j in range(nseq):
        sk_ref[j] = kext[j * kpitch + tl:(j + 1) * kpitch, :]
        sv_ref[j] = vext[j * kpitch + tl:(j + 1) * kpitch, :]
    if nseq == 1:
        kext[0:WINDOW, :] = kext[tl:tl + WINDOW, :]
        vext[0:WINDOW, :] = vext[tl:tl + WINDOW, :]

    n_attn = _rms_norm(yattn_s[...], gattn_ref[...]).astype(BF16)
    mixed = _dot(n_conv, w_out_ref[0:D_CONV, :]) + _dot(n_attn, w_out_ref[D_CONV:D_CONV + D_ATTN, :])
    x1 = _layer_norm(alpha * x + mixed, ln1g_ref[...], ln1b_ref[...])
    _store_rows(x1_ref, x1, rows)

    logits_t = (_dot(x1.astype(BF16), wr_ref[...]) + br_ref[...]).T
    sub = lax.broadcasted_iota(jnp.int32, (SUBLANES, rows), 0)
    gl = jnp.where(sub < N_GROUPS, logits_t[N_EXPERTS:N_EXPERTS + SUBLANES, :], -jnp.inf)
    gmax = jnp.max(gl, axis=0, keepdims=True)
    grp = jnp.min(jnp.where(gl == gmax, sub, SUBLANES), axis=0, keepdims=True)
    p_grp = 1.0 / jnp.sum(jnp.exp(gl - gmax), axis=0, keepdims=True)
    el = logits_t[(N_GROUPS - 1) * EXP_PER_GROUP:N_GROUPS * EXP_PER_GROUP, :]
    for g in range(N_GROUPS - 2, -1, -1):
        el = jnp.where(grp == g, logits_t[g * EXP_PER_GROUP:(g + 1) * EXP_PER_GROUP, :], el)
    v1 = jnp.max(el, axis=0, keepdims=True)
    i1 = jnp.min(jnp.where(el == v1, sub, SUBLANES), axis=0, keepdims=True)
    el2 = jnp.where(sub == i1, -jnp.inf, el)
    v2 = jnp.max(el2, axis=0, keepdims=True)
    i2 = jnp.min(jnp.where(el2 == v2, sub, SUBLANES), axis=0, keepdims=True)
    e2 = jnp.exp(v2 - v1)
    gate1 = p_grp * (1.0 / (1.0 + e2))
    gate2 = p_grp * (e2 / (1.0 + e2))

    chosen = jnp.logical_or(sub == i1, sub == i2)
    onehot = jnp.concatenate(
        [jnp.where(jnp.logical_and(grp == g, chosen), 1.0, 0.0) for g in range(N_GROUPS)], axis=0)
    running = cnt_s[...]
    ranks = []
    for c in range(rows // RANK_CHUNK):
        oh = onehot[:, c * RANK_CHUNK:(c + 1) * RANK_CHUNK]
        ranks.append(_dot(oh.astype(BF16), tri_ref[...]) + running)
        running = running + jnp.sum(oh, axis=1, keepdims=True)
    rank = jnp.concatenate(ranks, axis=1)
    cnt_s[...] = running
    counts_ref[...] = jnp.broadcast_to(running, (N_EXPERTS, LANES))

    ex1 = grp * EXP_PER_GROUP + i1
    ex2 = grp * EXP_PER_GROUP + i2
    erow = lax.broadcasted_iota(jnp.int32, (N_EXPERTS, rows), 0)
    rank1 = jnp.sum(jnp.where(erow == ex1, rank, 0.0), axis=0, keepdims=True)
    rank2 = jnp.sum(jnp.where(erow == ex2, rank, 0.0), axis=0, keepdims=True)
    fields = {_R_E1: ex1.astype(F32), _R_E2: ex2.astype(F32), _R_RANK1: rank1, _R_RANK2: rank2,
              _R_G1: gate1, _R_G2: gate2}
    record = jnp.zeros((SUBLANES, rows), F32)
    for r, val in fields.items():
        record = jnp.where(sub == r, val, record)
    route_ref[...] = record
    padded = jnp.concatenate([record, jnp.zeros((LANES - SUBLANES, rows), F32)], axis=0)
    gcol_ref[...] = padded.T


def _mixer(x, cconv, ck, cv, prm, consts, *, nseq, tl, mask_history, alpha):
    nb_total, t_total = x.shape[0], x.shape[1]
    nb, ns = nb_total // nseq, t_total // tl
    rows = nseq * tl
    n_tok = nb_total * t_total
    w_in, convw, convb, sinks, gconv, gattn, w_out, ln1g, ln1b, wr, br = prm
    abias, tri = consts

    def full(a):
        return pl.BlockSpec(a.shape, lambda b, s, _n=a.ndim: (0,) * _n)

    def seq_state(width, nrows):
        return pl.BlockSpec((nseq, nrows, width), lambda b, s: (b, 0, 0))

    in_specs = [
        pl.BlockSpec((nseq, tl, D_MODEL), lambda b, s: (b, s, 0)),
        seq_state(D_CONV, CONV_W - 1), seq_state(D_KV, WINDOW), seq_state(D_KV, WINDOW),
        full(w_in), full(convw), full(convb), full(abias), full(sinks), full(gconv), full(gattn),
        full(w_out), full(ln1g), full(ln1b), full(wr), full(br), full(tri),
    ]
    out_shape = [
        jax.ShapeDtypeStruct((n_tok * ROW_TILES, LANES), F32),
        jax.ShapeDtypeStruct((SUBLANES, n_tok), F32),
        jax.ShapeDtypeStruct((n_tok, LANES), F32),
        jax.ShapeDtypeStruct((N_EXPERTS, LANES), F32),
        jax.ShapeDtypeStruct((nb_total, CONV_W - 1, D_CONV), F32),
        jax.ShapeDtypeStruct((nb_total, WINDOW, D_KV), F32),
        jax.ShapeDtypeStruct((nb_total, WINDOW, D_KV), F32),
    ]
    out_specs = [
        pl.BlockSpec((rows * ROW_TILES, LANES), lambda b, s: (b * ns + s, 0)),
        pl.BlockSpec((SUBLANES, rows), lambda b, s: (0, b * ns + s)),
        pl.BlockSpec((rows, LANES), lambda b, s: (b * ns + s, 0)),
        pl.BlockSpec((N_EXPERTS, LANES), lambda b, s: (0, 0)),
        seq_state(D_CONV, CONV_W - 1), seq_state(D_KV, WINDOW), seq_state(D_KV, WINDOW),
    ]
    scratch = [
        pltpu.VMEM((nseq * (tl + SUBLANES), D_CONV), F32),
        pltpu.VMEM((nseq * (WINDOW + tl), D_KV), F32),
        pltpu.VMEM((nseq * (WINDOW + tl), D_KV), F32),
        pltpu.VMEM((rows, D_ATTN), BF16),
        pltpu.VMEM((rows, D_ATTN), F32),
        pltpu.VMEM((N_EXPERTS, 1), F32),
    ]
    kern = functools.partial(_mixer_kernel, nseq=nseq, tl=tl, alpha=alpha, mask_history=mask_history)
    return pl.pallas_call(
        kern, grid=(nb, ns), in_specs=in_specs, out_specs=out_specs, out_shape=out_shape,
        scratch_shapes=scratch, name="mixer",
        compiler_params=pltpu.CompilerParams(dimension_semantics=("arbitrary", "arbitrary"),
                                             vmem_limit_bytes=VMEM_LIMIT),
    )(x, cconv, ck, cv, w_in, convw, convb, abias, sinks, gconv, gattn, w_out, ln1g, ln1b, wr, br, tri)


_PAD_PIECES = tuple(2 ** k for k in range(EXPERT_BLOCK.bit_length() - 2, -1, -1))


def _tile_rows(row, n=1):
    return pl.ds(pl.multiple_of(row * ROW_TILES, ROW_TILES), n * ROW_TILES)


def _row_copy(src, src_row, dst, dst_row, sem):
    return pltpu.make_async_copy(src.at[_tile_rows(src_row), :], dst.at[_tile_rows(dst_row), :], sem)


def _dispatch_kernel(zstart_ref, zcount_ref, tail_ref, pos1_ref, pos2_ref, x_hbm, xs_hbm, zeros_v, sem, zsem, *, tb):
    step = pl.program_id(0)
    t0 = step * tb

    def zero_copy(start, piece):
        return pltpu.make_async_copy(zeros_v.at[_tile_rows(0, piece), :], xs_hbm.at[_tile_rows(start, piece), :], zsem)

    def pad_copy(e, piece, taken):
        return zero_copy(zstart_ref[e] + taken, piece)

    def for_each_pad_piece(fn):
        def per_tail_block(j, carry):
            for part in range(EXPERT_BLOCK // _PAD_PIECES[0]):
                fn(zero_copy(tail_ref[0] + j * EXPERT_BLOCK + part * _PAD_PIECES[0], _PAD_PIECES[0]))
            return carry

        lax.fori_loop(0, tail_ref[1], per_tail_block, 0)

        def per_expert(e, carry):
            count = zcount_ref[e]
            taken = 0
            for piece in _PAD_PIECES:
                present = (count & piece) != 0

                @pl.when(present)
                def _(piece=piece, taken=taken):
                    fn(pad_copy(e, piece, taken))

                taken = taken + jnp.where(present, piece, 0)
            return carry

        lax.fori_loop(0, N_EXPERTS, per_expert, 0)

    @pl.when(step == 0)
    def _():
        zeros_v[...] = jnp.zeros_like(zeros_v)
        for_each_pad_piece(lambda c: c.start())

    def issue(t, carry):
        _row_copy(x_hbm, t0 + t, xs_hbm, pos1_ref[t], sem).start()
        _row_copy(x_hbm, t0 + t, xs_hbm, pos2_ref[t], sem).start()
        return carry

    lax.fori_loop(0, tb, issue, 0, unroll=8)
    pltpu.make_async_copy(xs_hbm.at[_tile_rows(0, 2 * tb), :], xs_hbm.at[_tile_rows(0, 2 * tb), :], sem).wait()

    @pl.when(step == 0)
    def _():
        for_each_pad_piece(lambda c: c.wait())


def _dispatch(x_rows, pos1, pos2, zstart, zcount, tail, n_slots, *, tb):
    n_tok = x_rows.shape[0] // ROW_TILES
    kern = functools.partial(_dispatch_kernel, tb=tb)
    grid_spec = pltpu.PrefetchScalarGridSpec(
        num_scalar_prefetch=3, grid=(n_tok // tb,),
        in_specs=[pl.BlockSpec((tb,), lambda i, *_: (i,), memory_space=pltpu.SMEM),
                  pl.BlockSpec((tb,), lambda i, *_: (i,), memory_space=pltpu.SMEM),
                  pl.BlockSpec(memory_space=pl.ANY)],
        out_specs=pl.BlockSpec(memory_space=pl.ANY),
        scratch_shapes=[pltpu.VMEM((_PAD_PIECES[0] * ROW_TILES, LANES), F32),
                        pltpu.SemaphoreType.DMA(()), pltpu.SemaphoreType.DMA(())],
    )
    return pl.pallas_call(
        kern, grid_spec=grid_spec, out_shape=jax.ShapeDtypeStruct((n_slots * ROW_TILES, LANES), F32),
        name="dispatch",
        compiler_params=pltpu.CompilerParams(dimension_semantics=("arbitrary",)),
    )(zstart, zcount, tail, pos1, pos2, x_rows)


def _expert_kernel(be_ref, src_ref, used_ref, xs_ref, wg_ref, wu_ref, wd_ref, yb_ref, wg_b, wu_b, wd_b):
    i = pl.program_id(0)
    e = be_ref[i]
    prev = be_ref[jnp.maximum(i - 1, 0)]

    @pl.when(jnp.logical_or(i == 0, e != prev))
    def _():
        wg_b[...] = wg_ref[...].astype(BF16)
        wu_b[...] = wu_ref[...].astype(BF16)
        wd_b[...] = wd_ref[...].astype(BF16)

    @pl.when(used_ref[i] != 0)
    def _():
        x = _load_rows(xs_ref, EXPERT_BLOCK).astype(BF16)
        g = _dot(x, wg_b[...])
        u = _dot(x, wu_b[...])
        hid = (g / (1.0 + jnp.exp(-g))) * u
        _store_rows(yb_ref, _dot(hid.astype(BF16), wd_b[...]), EXPERT_BLOCK)

    @pl.when(used_ref[i] == 0)
    def _():
        yb_ref[...] = jnp.zeros_like(yb_ref)


def _experts(xs, blk_e, blk_src, blk_used, w_gate, w_up, w_down, layer):
    n_slots = xs.shape[0] // ROW_TILES
    nblk = n_slots // EXPERT_BLOCK

    def wspec(shape):
        return pl.BlockSpec((None, None) + shape, lambda i, be, src, used: (layer, be[i], 0, 0))

    grid_spec = pltpu.PrefetchScalarGridSpec(
        num_scalar_prefetch=3, grid=(nblk,),
        in_specs=[pl.BlockSpec((EXPERT_BLOCK * ROW_TILES, LANES), lambda i, be, src, used: (src[i], 0)),
                  wspec((D_MODEL, D_EXP)), wspec((D_MODEL, D_EXP)), wspec((D_EXP, D_MODEL))],
        out_specs=pl.BlockSpec((EXPERT_BLOCK * ROW_TILES, LANES), lambda i, be, src, used: (i, 0)),
        scratch_shapes=[pltpu.VMEM((D_MODEL, D_EXP), BF16), pltpu.VMEM((D_MODEL, D_EXP), BF16),
                        pltpu.VMEM((D_EXP, D_MODEL), BF16)],
    )
    return pl.pallas_call(
        _expert_kernel, grid_spec=grid_spec, out_shape=jax.ShapeDtypeStruct((n_slots * ROW_TILES, LANES), F32),
        name="experts",
        compiler_params=pltpu.CompilerParams(dimension_semantics=("arbitrary",), vmem_limit_bytes=VMEM_LIMIT),
    )(blk_e, blk_src, blk_used, xs, w_gate, w_up, w_down)


def _combine_kernel(pos1_ref, pos2_ref, x1_ref, gcol_ref, yb_hbm, g_ref, b_ref, out_ref, ybuf, sem, *, tb, alpha):
    def issue(t, carry):
        _row_copy(yb_hbm, pos1_ref[t], ybuf.at[0], t, sem).start()
        _row_copy(yb_hbm, pos2_ref[t], ybuf.at[1], t, sem).start()
        return carry

    lax.fori_loop(0, tb, issue, 0, unroll=8)
    for k in range(TOP_K):
        pltpu.make_async_copy(yb_hbm.at[_tile_rows(0, tb), :], ybuf.at[k], sem).wait()
    gates = gcol_ref[...]
    y = gates[:, _R_G1:_R_G1 + 1] * _load_rows(ybuf.at[0], tb) + gates[:, _R_G2:_R_G2 + 1] * _load_rows(ybuf.at[1], tb)
    out_ref[...] = _layer_norm(alpha * _load_rows(x1_ref, tb) + y, g_ref[...], b_ref[...])


def _combine(x1_rows, gcol, yb, pos1, pos2, ln2g, ln2b, *, tb, alpha):
    n_tok = x1_rows.shape[0] // ROW_TILES
    kern = functools.partial(_combine_kernel, tb=tb, alpha=alpha)
    return pl.pallas_call(
        kern, grid=(n_tok // tb,),
        in_specs=[pl.BlockSpec((tb,), lambda i: (i,), memory_space=pltpu.SMEM),
                  pl.BlockSpec((tb,), lambda i: (i,), memory_space=pltpu.SMEM),
                  pl.BlockSpec((tb * ROW_TILES, LANES), lambda i: (i, 0)),
                  pl.BlockSpec((tb, LANES), lambda i: (i, 0)),
                  pl.BlockSpec(memory_space=pl.ANY),
                  pl.BlockSpec((1, D_MODEL), lambda i: (0, 0)),
                  pl.BlockSpec((1, D_MODEL), lambda i: (0, 0))],
        out_specs=pl.BlockSpec((tb, D_MODEL), lambda i: (i, 0)),
        out_shape=jax.ShapeDtypeStruct((n_tok, D_MODEL), F32),
        scratch_shapes=[pltpu.VMEM((TOP_K, tb * ROW_TILES, LANES), F32), pltpu.SemaphoreType.DMA(())],
        name="combine",
        compiler_params=pltpu.CompilerParams(dimension_semantics=("arbitrary",), vmem_limit_bytes=VMEM_LIMIT),
    )(pos1, pos2, x1_rows, gcol, yb, ln2g, ln2b)


def _alibi_bias():
    slopes = np.asarray([2.0 ** (-8.0 * (h + 1) / N_HEADS) for h in range(N_HEADS)], np.float32)
    qi = np.arange(CHUNK, dtype=np.int32)[:, None]
    sj = np.arange(KEYS, dtype=np.int32)[None, :]
    dist = np.abs(qi + WINDOW - sj).astype(np.float32)
    bias = -slopes.reshape(N_KV_HEADS, GROUP, 1, 1) * dist
    return jnp.asarray(bias.reshape(N_KV_HEADS, GROUP * CHUNK, KEYS), F32)


def _strict_upper():
    r = np.arange(RANK_CHUNK)
    return jnp.asarray((r[:, None] < r[None, :]).astype(np.float32), BF16)


def _slot_plan(route, counts, n_tok):
    nblk = (n_tok * TOP_K) // EXPERT_BLOCK + N_EXPERTS
    experts = route[_R_E1:_R_E2 + 1].astype(jnp.int32)
    ranks = route[_R_RANK1:_R_RANK2 + 1].astype(jnp.int32)
    cnt = counts[:, 0].astype(jnp.int32)
    padded = (cnt + EXPERT_BLOCK - 1) // EXPERT_BLOCK * EXPERT_BLOCK
    pad_end = jnp.cumsum(padded)
    pad_start = pad_end - padded
    pos = pad_start[experts] + ranks
    blk_first = jnp.arange(nblk, dtype=jnp.int32) * EXPERT_BLOCK
    blk_e = jnp.minimum(jnp.searchsorted(pad_end, blk_first, side='right'), N_EXPERTS - 1).astype(jnp.int32)
    blk_used = (blk_first < pad_end[-1]).astype(jnp.int32)
    blk_src = jnp.minimum(jnp.arange(nblk, dtype=jnp.int32), pad_end[-1] // EXPERT_BLOCK - 1)
    tail = jnp.stack([pad_end[-1], nblk - pad_end[-1] // EXPERT_BLOCK]).astype(jnp.int32)
    return pos[0], pos[1], pad_start + cnt, padded - cnt, tail, blk_e, blk_src, blk_used, nblk * EXPERT_BLOCK


def _layer(x, cconv, ck, cv, mix_prm, consts, moe_prm, layer, *, nseq, tl, mask_history, alpha):
    nb_total, t_total = x.shape[0], x.shape[1]
    n_tok = nb_total * t_total
    tb = min(ROWS, n_tok)
    x1_rows, route, gcol, counts, sconv, sk, sv = _mixer(
        x, cconv, ck, cv, mix_prm, consts, nseq=nseq, tl=tl, mask_history=mask_history, alpha=alpha)
    pos1, pos2, zstart, zcount, tail, blk_e, blk_src, blk_used, n_slots = _slot_plan(route, counts, n_tok)
    xs = _dispatch(x1_rows, pos1, pos2, zstart, zcount, tail, n_slots, tb=tb)
    w_gate, w_up, w_down, ln2g, ln2b = moe_prm
    yb = _experts(xs, blk_e, blk_src, blk_used, w_gate, w_up, w_down, layer)
    out = _combine(x1_rows, gcol, yb, pos1, pos2, ln2g, ln2b, tb=tb, alpha=alpha)
    return out.reshape(nb_total, t_total, D_MODEL), sconv, sk, sv


def kernel(x_prompt, x_sample, cache_conv, cache_k, cache_v, w_in, conv_w, conv_b, attn_sinks, g_conv, g_attn,
           w_out, ln1_g, ln1_b, router_group_w, router_group_b, router_expert_w, router_expert_b,
           expert_w_gate, expert_w_up, expert_w_down, ln2_g, ln2_b):
    depth = w_in.shape[0]
    alpha = (2 * depth) ** 0.25
    batch, seq = x_prompt.shape[0], x_prompt.shape[1]
    dec_batch, dec_seq = x_sample.shape[0], x_sample.shape[1]
    assert dec_seq == CHUNK and seq % CHUNK == 0
    consts = (_alibi_bias(), _strict_upper())
    tl_prompt = min(ROWS, seq)
    nseq_sample = min(ROWS // dec_seq, dec_batch)
    assert seq % tl_prompt == 0 and dec_batch % nseq_sample == 0

    xp, xs = x_prompt, x_sample
    zeros_conv = jnp.zeros((batch, CONV_W - 1, D_CONV), F32)
    zeros_kv = jnp.zeros((batch, WINDOW, D_KV), F32)
    states = [[] for _ in range(6)]
    for l in range(depth):
        sink_col = jnp.repeat(attn_sinks[l].astype(F32), CHUNK).reshape(N_KV_HEADS, GROUP * CHUNK, 1)
        wr = jnp.zeros((D_MODEL, LANES), F32)
        wr = wr.at[:, 0:N_EXPERTS].set(router_expert_w[l]).at[:, N_EXPERTS:N_EXPERTS + N_GROUPS].set(router_group_w[l])
        br = jnp.zeros((1, LANES), F32)
        br = br.at[0, 0:N_EXPERTS].set(router_expert_b[l]).at[0, N_EXPERTS:N_EXPERTS + N_GROUPS].set(router_group_b[l])
        mix_prm = (w_in[l].astype(BF16), conv_w[l], conv_b[l].reshape(1, D_CONV), sink_col,
                   g_conv[l].reshape(1, D_CONV), g_attn[l].reshape(1, D_ATTN), w_out[l].astype(BF16),
                   ln1_g[l].reshape(1, D_MODEL), ln1_b[l].reshape(1, D_MODEL), wr.astype(BF16), br)
        moe_prm = (expert_w_gate, expert_w_up, expert_w_down,
                   ln2_g[l].reshape(1, D_MODEL), ln2_b[l].reshape(1, D_MODEL))
        xp, c_new, k_new, v_new = _layer(xp, zeros_conv, zeros_kv, zeros_kv, mix_prm, consts, moe_prm, l,
                                         nseq=1, tl=tl_prompt, mask_history=True, alpha=alpha)
        for lst, val in zip(states[0:3], (c_new, k_new, v_new)):
            lst.append(val)
        xs, c_new, k_new, v_new = _layer(xs, cache_conv[l], cache_k[l].reshape(dec_batch, WINDOW, D_KV),
                                         cache_v[l].reshape(dec_batch, WINDOW, D_KV), mix_prm, consts, moe_prm, l,
                                         nseq=nseq_sample, tl=dec_seq, mask_history=False, alpha=alpha)
        for lst, val in zip(states[3:6], (c_new, k_new, v_new)):
            lst.append(val)

    def kv(lst, nb):
        return jnp.stack(lst).reshape(depth, nb, WINDOW, N_KV_HEADS, HEAD_DIM)

    return (xp, xs, jnp.stack(states[0]), kv(states[1], batch), kv(states[2], batch),
            jnp.stack(states[3]), kv(states[4], dec_batch), kv(states[5], dec_batch))
```

```python
import functools

import numpy as np
import jax
import jax.numpy as jnp
from jax import lax
from jax.experimental import pallas as pl
from jax.experimental.pallas import tpu as pltpu

D_MODEL = 1024
D_CONV = 512
CONV_W = 3
N_HEADS = 8
N_KV_HEADS = 2
GROUP = N_HEADS // N_KV_HEADS
HEAD_DIM = 64
D_ATTN = N_HEADS * HEAD_DIM
D_KV = N_KV_HEADS * HEAD_DIM
WINDOW = 128
CHUNK = 64
KEYS = WINDOW + CHUNK
N_GROUPS = 4
EXP_PER_GROUP = 8
N_EXPERTS = N_GROUPS * EXP_PER_GROUP
TOP_K = 2
D_EXP = 512
LN_EPS = 1e-5
NEG_INF = -1e30

SUBLANES = 8
LANES = 128
ROW_TILES = D_MODEL // LANES
assert ROW_TILES == SUBLANES
ROWS = 512
EXPERT_BLOCK = 512
RANK_CHUNK = 256
VMEM_LIMIT = 56 * 1024 * 1024

F32 = jnp.float32
BF16 = jnp.bfloat16

_OFF_B, _OFF_C, _OFF_H = 0, D_CONV, 2 * D_CONV
_OFF_Q = 3 * D_CONV
_OFF_K = _OFF_Q + D_ATTN
_OFF_V = _OFF_K + D_KV
D_IN = _OFF_V + D_KV

_R_E1, _R_E2, _R_RANK1, _R_RANK2, _R_G1, _R_G2 = 0, 1, 2, 3, 4, 5


def _load_rows(ref, n):
    return jnp.concatenate([ref[pl.ds(j, n, stride=ROW_TILES), :] for j in range(ROW_TILES)], axis=1)


def _store_rows(ref, val, n):
    for j in range(ROW_TILES):
        ref[pl.ds(j, n, stride=ROW_TILES), :] = val[:, j * LANES:(j + 1) * LANES]


def _dot(a, b):
    return jnp.dot(a, b, preferred_element_type=F32)


def _rms_norm(x, g):
    return x * lax.rsqrt(jnp.mean(jnp.square(x), -1, keepdims=True) + LN_EPS) * g


def _layer_norm(x, g, b):
    mu = jnp.mean(x, -1, keepdims=True)
    xc = x - mu
    var = jnp.mean(jnp.square(xc), -1, keepdims=True)
    return xc * lax.rsqrt(var + LN_EPS) * g + b


def _mixer_kernel(x_ref, cconv_ref, ck_ref, cv_ref, w_in_ref, convw_ref, convb_ref, abias_ref,
                  sink_ref, gconv_ref, gattn_ref, w_out_ref, ln1g_ref, ln1b_ref, wr_ref, br_ref,
                  tri_ref,
                  x1_ref, route_ref, gcol_ref, counts_ref, sconv_ref, sk_ref, sv_ref,
                  uext, kext, vext, q_s, yattn_s, cnt_s, *, nseq, tl, alpha, mask_history):
    b = pl.program_id(0)
    s = pl.program_id(1)
    rows = nseq * tl
    upitch = tl + SUBLANES
    kpitch = WINDOW + tl

    @pl.when(jnp.logical_and(b == 0, s == 0))
    def _():
        cnt_s[...] = jnp.zeros_like(cnt_s)

    @pl.when(s == 0)
    def _():
        for j in range(nseq):
            uext[j * upitch + SUBLANES - 2:j * upitch + SUBLANES, :] = cconv_ref[j]
            kext[j * kpitch:j * kpitch + WINDOW, :] = ck_ref[j]
            vext[j * kpitch:j * kpitch + WINDOW, :] = cv_ref[j]

    x = x_ref[...].reshape(rows, D_MODEL)
    xb = x.astype(BF16)

    u = _dot(xb, w_in_ref[:, _OFF_C:_OFF_C + D_CONV]) * _dot(xb, w_in_ref[:, _OFF_H:_OFF_H + D_CONV])
    w0, w1, w2 = convw_ref[0:1, :], convw_ref[1:2, :], convw_ref[2:3, :]
    ycs = []
    for j in range(nseq):
        base = j * upitch + SUBLANES
        uext[base:base + tl, :] = u[j * tl:(j + 1) * tl]
        yc = convb_ref[...] + uext[base - 2:base - 2 + tl, :] * w0
        yc = yc + uext[base - 1:base - 1 + tl, :] * w1
        yc = yc + uext[base:base + tl, :] * w2
        ycs.append(yc)
        tail = uext[base + tl - 2:base + tl, :]
        sconv_ref[j] = tail
        uext[base - 2:base, :] = tail
    yc = ycs[0] if nseq == 1 else jnp.concatenate(ycs, axis=0)
    y_conv = _dot(xb, w_in_ref[:, _OFF_B:_OFF_B + D_CONV]) * yc
    n_conv = _rms_norm(y_conv, gconv_ref[...]).astype(BF16)

    q_s[...] = (_dot(xb, w_in_ref[:, _OFF_Q:_OFF_Q + D_ATTN]) * (HEAD_DIM ** -0.5)).astype(BF16)
    k = _dot(xb, w_in_ref[:, _OFF_K:_OFF_K + D_KV])
    v = _dot(xb, w_in_ref[:, _OFF_V:_OFF_V + D_KV])
    for j in range(nseq):
        kext[j * kpitch + WINDOW:(j + 1) * kpitch, :] = k[j * tl:(j + 1) * tl]
        vext[j * kpitch + WINDOW:(j + 1) * kpitch, :] = v[j * tl:(j + 1) * tl]

    chunks_per_seq = tl // CHUNK
    assert nseq == 1 or chunks_per_seq == 1
    key_stride = CHUNK if nseq == 1 else kpitch
    col = lax.broadcasted_iota(jnp.int32, (1, KEYS), 1)

    def attend(i):
        r0 = pl.multiple_of(i * CHUNK, CHUNK)
        off = pl.multiple_of(i * key_stride, CHUNK)
        qc = q_s[pl.ds(r0, CHUNK), :]
        kc = kext[pl.ds(off, KEYS), :].astype(BF16)
        vc = vext[pl.ds(off, KEYS), :].astype(BF16)
        if mask_history:
            first_pos = s * tl + i * CHUNK - WINDOW
            maskrow = jnp.where(first_pos + col < 0, NEG_INF, 0.0).astype(F32)
        for kvh in range(N_KV_HEADS):
            heads = [qc[:, (kvh * GROUP + g) * HEAD_DIM:(kvh * GROUP + g + 1) * HEAD_DIM] for g in range(GROUP)]
            qst = jnp.concatenate(heads, axis=0)
            kk = kc[:, kvh * HEAD_DIM:(kvh + 1) * HEAD_DIM]
            logits = lax.dot_general(qst, kk, (((1,), (1,)), ((), ())), preferred_element_type=F32)
            logits = logits + abias_ref[kvh]
            if mask_history:
                logits = logits + maskrow
            sink = sink_ref[kvh]
            m = jnp.maximum(jnp.max(logits, -1, keepdims=True), sink)
            e = jnp.exp(logits - m)
            den = jnp.sum(e, -1, keepdims=True) + jnp.exp(sink - m)
            p = (e * (1.0 / den)).astype(BF16)
            o = _dot(p, vc[:, kvh * HEAD_DIM:(kvh + 1) * HEAD_DIM])
            for g in range(GROUP):
                h = kvh * GROUP + g
                yattn_s[pl.ds(r0, CHUNK), h * HEAD_DIM:(h + 1) * HEAD_DIM] = o[g * CHUNK:(g + 1) * CHUNK]

    def attend_body(i, carry):
        attend(i)
        return carry

    lax.fori_loop(0, rows // CHUNK, attend_body, 0)

    for j in range(nseq):
        sk_ref[j] = kext[j * kpitch + tl:(j + 1) * kpitch, :]
        sv_ref[j] = vext[j * kpitch + tl:(j + 1) * kpitch, :]
    if nseq == 1:
        kext[0:WINDOW, :] = kext[tl:tl + WINDOW, :]
        vext[0:WINDOW, :] = vext[tl:tl + WINDOW, :]

    n_attn = _rms_norm(yattn_s[...], gattn_ref[...]).astype(BF16)
    mixed = _dot(n_conv, w_out_ref[0:D_CONV, :]) + _dot(n_attn, w_out_ref[D_CONV:D_CONV + D_ATTN, :])
    x1 = _layer_norm(alpha * x + mixed, ln1g_ref[...], ln1b_ref[...])
    _store_rows(x1_ref, x1, rows)

    logits_t = (_dot(x1.astype(BF16), wr_ref[...]) + br_ref[...]).T
    sub = lax.broadcasted_iota(jnp.int32, (SUBLANES, rows), 0)
    gl = jnp.where(sub < N_GROUPS, logits_t[N_EXPERTS:N_EXPERTS + SUBLANES, :], -jnp.inf)
    gmax = jnp.max(gl, axis=0, keepdims=True)
    grp = jnp.min(jnp.where(gl == gmax, sub, SUBLANES), axis=0, keepdims=True)
    p_grp = 1.0 / jnp.sum(jnp.exp(gl - gmax), axis=0, keepdims=True)
    el = logits_t[(N_GROUPS - 1) * EXP_PER_GROUP:N_GROUPS * EXP_PER_GROUP, :]
    for g in range(N_GROUPS - 2, -1, -1):
        el = jnp.where(grp == g, logits_t[g * EXP_PER_GROUP:(g + 1) * EXP_PER_GROUP, :], el)
    v1 = jnp.max(el, axis=0, keepdims=True)
    i1 = jnp.min(jnp.where(el == v1, sub, SUBLANES), axis=0, keepdims=True)
    el2 = jnp.where(sub == i1, -jnp.inf, el)
    v2 = jnp.max(el2, axis=0, keepdims=True)
    i2 = jnp.min(jnp.where(el2 == v2, sub, SUBLANES), axis=0, keepdims=True)
    e2 = jnp.exp(v2 - v1)
    gate1 = p_grp * (1.0 / (1.0 + e2))
    gate2 = p_grp * (e2 / (1.0 + e2))

    chosen = jnp.logical_or(sub == i1, sub == i2)
    onehot = jnp.concatenate(
        [jnp.where(jnp.logical_and(grp == g, chosen), 1.0, 0.0) for g in range(N_GROUPS)], axis=0)
    running = cnt_s[...]
    ranks = []
    for c in range(rows // RANK_CHUNK):
        oh = onehot[:, c * RANK_CHUNK:(c + 1) * RANK_CHUNK]
        ranks.append(_dot(oh.astype(BF16), tri_ref[...]) + running)
        running = running + jnp.sum(oh, axis=1, keepdims=True)
    rank = jnp.concatenate(ranks, axis=1)
    cnt_s[...] = running
    counts_ref[...] = jnp.broadcast_to(running, (N_EXPERTS, LANES))

    ex1 = grp * EXP_PER_GROUP + i1
    ex2 = grp * EXP_PER_GROUP + i2
    erow = lax.broadcasted_iota(jnp.int32, (N_EXPERTS, rows), 0)
    rank1 = jnp.sum(jnp.where(erow == ex1, rank, 0.0), axis=0, keepdims=True)
    rank2 = jnp.sum(jnp.where(erow == ex2, rank, 0.0), axis=0, keepdims=True)
    fields = {_R_E1: ex1.astype(F32), _R_E2: ex2.astype(F32), _R_RANK1: rank1, _R_RANK2: rank2,
              _R_G1: gate1, _R_G2: gate2}
    record = jnp.zeros((SUBLANES, rows), F32)
    for r, val in fields.items():
        record = jnp.where(sub == r, val, record)
    route_ref[...] = record
    padded = jnp.concatenate([record, jnp.zeros((LANES - SUBLANES, rows), F32)], axis=0)
    gcol_ref[...] = padded.T


def _mixer(x, cconv, ck, cv, prm, consts, *, nseq, tl, mask_history, alpha):
    nb_total, t_total = x.shape[0], x.shape[1]
    nb, ns = nb_total // nseq, t_total // tl
    rows = nseq * tl
    n_tok = nb_total * t_total
    w_in, convw, convb, sinks, gconv, gattn, w_out, ln1g, ln1b, wr, br = prm
    abias, tri = consts

    def full(a):
        return pl.BlockSpec(a.shape, lambda b, s, _n=a.ndim: (0,) * _n)

    def seq_state(width, nrows):
        return pl.BlockSpec((nseq, nrows, width), lambda b, s: (b, 0, 0))

    in_specs = [
        pl.BlockSpec((nseq, tl, D_MODEL), lambda b, s: (b, s, 0)),
        seq_state(D_CONV, CONV_W - 1), seq_state(D_KV, WINDOW), seq_state(D_KV, WINDOW),
        full(w_in), full(convw), full(convb), full(abias), full(sinks), full(gconv), full(gattn),
        full(w_out), full(ln1g), full(ln1b), full(wr), full(br), full(tri),
    ]
    out_shape = [
        jax.ShapeDtypeStruct((n_tok * ROW_TILES, LANES), F32),
        jax.ShapeDtypeStruct((SUBLANES, n_tok), F32),
        jax.ShapeDtypeStruct((n_tok, LANES), F32),
        jax.ShapeDtypeStruct((N_EXPERTS, LANES), F32),
        jax.ShapeDtypeStruct((nb_total, CONV_W - 1, D_CONV), F32),
        jax.ShapeDtypeStruct((nb_total, WINDOW, D_KV), F32),
        jax.ShapeDtypeStruct((nb_total, WINDOW, D_KV), F32),
    ]
    out_specs = [
        pl.BlockSpec((rows * ROW_TILES, LANES), lambda b, s: (b * ns + s, 0)),
        pl.BlockSpec((SUBLANES, rows), lambda b, s: (0, b * ns + s)),
        pl.BlockSpec((rows, LANES), lambda b, s: (b * ns + s, 0)),
        pl.BlockSpec((N_EXPERTS, LANES), lambda b, s: (0, 0)),
        seq_state(D_CONV, CONV_W - 1), seq_state(D_KV, WINDOW), seq_state(D_KV, WINDOW),
    ]
    scratch = [
        pltpu.VMEM((nseq * (tl + SUBLANES), D_CONV), F32),
        pltpu.VMEM((nseq * (WINDOW + tl), D_KV), F32),
        pltpu.VMEM((nseq * (WINDOW + tl), D_KV), F32),
        pltpu.VMEM((rows, D_ATTN), BF16),
        pltpu.VMEM((rows, D_ATTN), F32),
        pltpu.VMEM((N_EXPERTS, 1), F32),
    ]
    kern = functools.partial(_mixer_kernel, nseq=nseq, tl=tl, alpha=alpha, mask_history=mask_history)
    return pl.pallas_call(
        kern, grid=(nb, ns), in_specs=in_specs, out_specs=out_specs, out_shape=out_shape,
        scratch_shapes=scratch, name="mixer",
        compiler_params=pltpu.CompilerParams(dimension_semantics=("arbitrary", "arbitrary"),
                                             vmem_limit_bytes=VMEM_LIMIT),
    )(x, cconv, ck, cv, w_in, convw, convb, abias, sinks, gconv, gattn, w_out, ln1g, ln1b, wr, br, tri)


_PAD_PIECES = tuple(2 ** k for k in range(EXPERT_BLOCK.bit_length() - 2, -1, -1))


def _tile_rows(row, n=1):
    return pl.ds(pl.multiple_of(row * ROW_TILES, ROW_TILES), n * ROW_TILES)


def _row_copy(src, src_row, dst, dst_row, sem):
    return pltpu.make_async_copy(src.at[_tile_rows(src_row), :], dst.at[_tile_rows(dst_row), :], sem)


def _dispatch_kernel(zstart_ref, zcount_ref, tail_ref, pos1_ref, pos2_ref, x_ref, xs_hbm, zeros_v, sem, zsem, *, tb):
    step = pl.program_id(0)

    def zero_copy(start, piece):
        return pltpu.make_async_copy(zeros_v.at[_tile_rows(0, piece), :], xs_hbm.at[_tile_rows(start, piece), :], zsem)

    def pad_copy(e, piece, taken):
        return zero_copy(zstart_ref[e] + taken, piece)

    def for_each_pad_piece(fn):
        def per_tail_block(j, carry):
            for part in range(EXPERT_BLOCK // _PAD_PIECES[0]):
                fn(zero_copy(tail_ref[0] + j * EXPERT_BLOCK + part * _PAD_PIECES[0], _PAD_PIECES[0]))
            return carry

        lax.fori_loop(0, tail_ref[1], per_tail_block, 0)

        def per_expert(e, carry):
            count = zcount_ref[e]
            taken = 0
            for piece in _PAD_PIECES:
                present = (count & piece) != 0

                @pl.when(present)
                def _(piece=piece, taken=taken):
                    fn(pad_copy(e, piece, taken))

                taken = taken + jnp.where(present, piece, 0)
            return carry

        lax.fori_loop(0, N_EXPERTS, per_expert, 0)

    @pl.when(step == 0)
    def _():
        zeros_v[...] = jnp.zeros_like(zeros_v)
        for_each_pad_piece(lambda c: c.start())

    def issue(t, carry):
        _row_copy(x_ref, t, xs_hbm, pos1_ref[t], sem).start()
        _row_copy(x_ref, t, xs_hbm, pos2_ref[t], sem).start()
        return carry

    lax.fori_loop(0, tb, issue, 0, unroll=8)
    pltpu.make_async_copy(xs_hbm.at[_tile_rows(0, 2 * tb), :], xs_hbm.at[_tile_rows(0, 2 * tb), :], sem).wait()

    @pl.when(step == 0)
    def _():
        for_each_pad_piece(lambda c: c.wait())


def _dispatch(x_rows, pos1, pos2, zstart, zcount, tail, n_slots, *, tb):
    n_tok = x_rows.shape[0] // ROW_TILES
    kern = functools.partial(_dispatch_kernel, tb=tb)
    grid_spec = pltpu.PrefetchScalarGridSpec(
        num_scalar_prefetch=3, grid=(n_tok // tb,),
        in_specs=[pl.BlockSpec((tb,), lambda i, *_: (i,), memory_space=pltpu.SMEM),
                  pl.BlockSpec((tb,), lambda i, *_: (i,), memory_space=pltpu.SMEM),
                  pl.BlockSpec((tb * ROW_TILES, LANES), lambda i, *_: (i, 0))],
        out_specs=pl.BlockSpec(memory_space=pl.ANY),
        scratch_shapes=[pltpu.VMEM((_PAD_PIECES[0] * ROW_TILES, LANES), F32),
                        pltpu.SemaphoreType.DMA(()), pltpu.SemaphoreType.DMA(())],
    )
    return pl.pallas_call(
        kern, grid_spec=grid_spec, out_shape=jax.ShapeDtypeStruct((n_slots * ROW_TILES, LANES), F32),
        name="dispatch",
        compiler_params=pltpu.CompilerParams(dimension_semantics=("arbitrary",)),
    )(zstart, zcount, tail, pos1, pos2, x_rows)


def _expert_kernel(be_ref, src_ref, used_ref, xs_ref, wg_ref, wu_ref, wd_ref, yb_ref, wg_b, wu_b, wd_b):
    i = pl.program_id(0)
    e = be_ref[i]
    prev = be_ref[jnp.maximum(i - 1, 0)]

    @pl.when(jnp.logical_or(i == 0, e != prev))
    def _():
        wg_b[...] = wg_ref[...].astype(BF16)
        wu_b[...] = wu_ref[...].astype(BF16)
        wd_b[...] = wd_ref[...].astype(BF16)

    @pl.when(used_ref[i] != 0)
    def _():
        x = _load_rows(xs_ref, EXPERT_BLOCK).astype(BF16)
        g = _dot(x, wg_b[...])
        u = _dot(x, wu_b[...])
        hid = (g / (1.0 + jnp.exp(-g))) * u
        _store_rows(yb_ref, _dot(hid.astype(BF16), wd_b[...]), EXPERT_BLOCK)

    @pl.when(used_ref[i] == 0)
    def _():
        yb_ref[...] = jnp.zeros_like(yb_ref)


def _experts(xs, blk_e, blk_src, blk_used, w_gate, w_up, w_down, layer):
    n_slots = xs.shape[0] // ROW_TILES
    nblk = n_slots // EXPERT_BLOCK

    def wspec(shape):
        return pl.BlockSpec((None, None) + shape, lambda i, be, src, used: (layer, be[i], 0, 0))

    grid_spec = pltpu.PrefetchScalarGridSpec(
        num_scalar_prefetch=3, grid=(nblk,),
        in_specs=[pl.BlockSpec((EXPERT_BLOCK * ROW_TILES, LANES), lambda i, be, src, used: (src[i], 0)),
                  wspec((D_MODEL, D_EXP)), wspec((D_MODEL, D_EXP)), wspec((D_EXP, D_MODEL))],
        out_specs=pl.BlockSpec((EXPERT_BLOCK * ROW_TILES, LANES), lambda i, be, src, used: (i, 0)),
        scratch_shapes=[pltpu.VMEM((D_MODEL, D_EXP), BF16), pltpu.VMEM((D_MODEL, D_EXP), BF16),
                        pltpu.VMEM((D_EXP, D_MODEL), BF16)],
    )
    return pl.pallas_call(
        _expert_kernel, grid_spec=grid_spec, out_shape=jax.ShapeDtypeStruct((n_slots * ROW_TILES, LANES), F32),
        name="experts",
        compiler_params=pltpu.CompilerParams(dimension_semantics=("arbitrary",), vmem_limit_bytes=VMEM_LIMIT),
    )(blk_e, blk_src, blk_used, xs, w_gate, w_up, w_down)


def _combine_kernel(pos1_ref, pos2_ref, x1_ref, gcol_ref, yb_hbm, g_ref, b_ref, out_ref, ybuf, sem, *, tb, alpha):
    def issue(t, carry):
        _row_copy(yb_hbm, pos1_ref[t], ybuf.at[0], t, sem).start()
        _row_copy(yb_hbm, pos2_ref[t], ybuf.at[1], t, sem).start()
        return carry

    lax.fori_loop(0, tb, issue, 0, unroll=8)
    for k in range(TOP_K):
        pltpu.make_async_copy(yb_hbm.at[_tile_rows(0, tb), :], ybuf.at[k], sem).wait()
    gates = gcol_ref[...]
    y = gates[:, _R_G1:_R_G1 + 1] * _load_rows(ybuf.at[0], tb) + gates[:, _R_G2:_R_G2 + 1] * _load_rows(ybuf.at[1], tb)
    out_ref[...] = _layer_norm(alpha * _load_rows(x1_ref, tb) + y, g_ref[...], b_ref[...])


def _combine(x1_rows, gcol, yb, pos1, pos2, ln2g, ln2b, *, tb, alpha):
    n_tok = x1_rows.shape[0] // ROW_TILES
    kern = functools.partial(_combine_kernel, tb=tb, alpha=alpha)
    return pl.pallas_call(
        kern, grid=(n_tok // tb,),
        in_specs=[pl.BlockSpec((tb,), lambda i: (i,), memory_space=pltpu.SMEM),
                  pl.BlockSpec((tb,), lambda i: (i,), memory_space=pltpu.SMEM),
                  pl.BlockSpec((tb * ROW_TILES, LANES), lambda i: (i, 0)),
                  pl.BlockSpec((tb, LANES), lambda i: (i, 0)),
                  pl.BlockSpec(memory_space=pl.ANY),
                  pl.BlockSpec((1, D_MODEL), lambda i: (0, 0)),
                  pl.BlockSpec((1, D_MODEL), lambda i: (0, 0))],
        out_specs=pl.BlockSpec((tb, D_MODEL), lambda i: (i, 0)),
        out_shape=jax.ShapeDtypeStruct((n_tok, D_MODEL), F32),
        scratch_shapes=[pltpu.VMEM((TOP_K, tb * ROW_TILES, LANES), F32), pltpu.SemaphoreType.DMA(())],
        name="combine",
        compiler_params=pltpu.CompilerParams(dimension_semantics=("arbitrary",), vmem_limit_bytes=VMEM_LIMIT),
    )(pos1, pos2, x1_rows, gcol, yb, ln2g, ln2b)


def _alibi_bias():
    slopes = np.asarray([2.0 ** (-8.0 * (h + 1) / N_HEADS) for h in range(N_HEADS)], np.float32)
    qi = np.arange(CHUNK, dtype=np.int32)[:, None]
    sj = np.arange(KEYS, dtype=np.int32)[None, :]
    dist = np.abs(qi + WINDOW - sj).astype(np.float32)
    bias = -slopes.reshape(N_KV_HEADS, GROUP, 1, 1) * dist
    return jnp.asarray(bias.reshape(N_KV_HEADS, GROUP * CHUNK, KEYS), F32)


def _strict_upper():
    r = np.arange(RANK_CHUNK)
    return jnp.asarray((r[:, None] < r[None, :]).astype(np.float32), BF16)


def _slot_plan(route, counts, n_tok):
    nblk = (n_tok * TOP_K) // EXPERT_BLOCK + N_EXPERTS
    experts = route[_R_E1:_R_E2 + 1].astype(jnp.int32)
    ranks = route[_R_RANK1:_R_RANK2 + 1].astype(jnp.int32)
    cnt = counts[:, 0].astype(jnp.int32)
    padded = (cnt + EXPERT_BLOCK - 1) // EXPERT_BLOCK * EXPERT_BLOCK
    pad_end = jnp.cumsum(padded)
    pad_start = pad_end - padded
    ids = jnp.arange(N_EXPERTS, dtype=jnp.int32)
    pos = jnp.sum(jnp.where(experts[..., None] == ids, pad_start, 0), axis=-1) + ranks
    blk_first = jnp.arange(nblk, dtype=jnp.int32) * EXPERT_BLOCK
    blk_e = jnp.minimum(jnp.sum((pad_end[None, :] <= blk_first[:, None]).astype(jnp.int32), axis=-1), N_EXPERTS - 1)
    blk_used = (blk_first < pad_end[-1]).astype(jnp.int32)
    blk_src = jnp.minimum(jnp.arange(nblk, dtype=jnp.int32), pad_end[-1] // EXPERT_BLOCK - 1)
    tail = jnp.stack([pad_end[-1], nblk - pad_end[-1] // EXPERT_BLOCK]).astype(jnp.int32)
    return pos[0], pos[1], pad_start + cnt, padded - cnt, tail, blk_e, blk_src, blk_used, nblk * EXPERT_BLOCK


def _layer(x, cconv, ck, cv, mix_prm, consts, moe_prm, layer, *, nseq, tl, mask_history, alpha):
    nb_total, t_total = x.shape[0], x.shape[1]
    n_tok = nb_total * t_total
    tb = min(ROWS, n_tok)
    x1_rows, route, gcol, counts, sconv, sk, sv = _mixer(
        x, cconv, ck, cv, mix_prm, consts, nseq=nseq, tl=tl, mask_history=mask_history, alpha=alpha)
    pos1, pos2, zstart, zcount, tail, blk_e, blk_src, blk_used, n_slots = _slot_plan(route, counts, n_tok)
    xs = _dispatch(x1_rows, pos1, pos2, zstart, zcount, tail, n_slots, tb=tb)
    w_gate, w_up, w_down, ln2g, ln2b = moe_prm
    yb = _experts(xs, blk_e, blk_src, blk_used, w_gate, w_up, w_down, layer)
    out = _combine(x1_rows, gcol, yb, pos1, pos2, ln2g, ln2b, tb=tb, alpha=alpha)
    return out.reshape(nb_total, t_total, D_MODEL), sconv, sk, sv


def kernel(x_prompt, x_sample, cache_conv, cache_k, cache_v, w_in, conv_w, conv_b, attn_sinks, g_conv, g_attn,
           w_out, ln1_g, ln1_b, router_group_w, router_group_b, router_expert_w, router_expert_b,
           expert_w_gate, expert_w_up, expert_w_down, ln2_g, ln2_b):
    depth = w_in.shape[0]
    alpha = (2 * depth) ** 0.25
    batch, seq = x_prompt.shape[0], x_prompt.shape[1]
    dec_batch, dec_seq = x_sample.shape[0], x_sample.shape[1]
    assert dec_seq == CHUNK and seq % CHUNK == 0
    consts = (_alibi_bias(), _strict_upper())
    tl_prompt = min(ROWS, seq)
    nseq_sample = min(ROWS // dec_seq, dec_batch)
    assert seq % tl_prompt == 0 and dec_batch % nseq_sample == 0

    xp, xs = x_prompt, x_sample
    zeros_conv = jnp.zeros((batch, CONV_W - 1, D_CONV), F32)
    zeros_kv = jnp.zeros((batch, WINDOW, D_KV), F32)
    states = [[] for _ in range(6)]
    for l in range(depth):
        sink_col = jnp.repeat(attn_sinks[l].astype(F32), CHUNK).reshape(N_KV_HEADS, GROUP * CHUNK, 1)
        wr = jnp.zeros((D_MODEL, LANES), F32)
        wr = wr.at[:, 0:N_EXPERTS].set(router_expert_w[l]).at[:, N_EXPERTS:N_EXPERTS + N_GROUPS].set(router_group_w[l])
        br = jnp.zeros((1, LANES), F32)
        br = br.at[0, 0:N_EXPERTS].set(router_expert_b[l]).at[0, N_EXPERTS:N_EXPERTS + N_GROUPS].set(router_group_b[l])
        mix_prm = (w_in[l].astype(BF16), conv_w[l], conv_b[l].reshape(1, D_CONV), sink_col,
                   g_conv[l].reshape(1, D_CONV), g_attn[l].reshape(1, D_ATTN), w_out[l].astype(BF16),
                   ln1_g[l].reshape(1, D_MODEL), ln1_b[l].reshape(1, D_MODEL), wr.astype(BF16), br)
        moe_prm = (expert_w_gate, expert_w_up, expert_w_down,
                   ln2_g[l].reshape(1, D_MODEL), ln2_b[l].reshape(1, D_MODEL))
        xp, c_new, k_new, v_new = _layer(xp, zeros_conv, zeros_kv, zeros_kv, mix_prm, consts, moe_prm, l,
                                         nseq=1, tl=tl_prompt, mask_history=True, alpha=alpha)
        for lst, val in zip(states[0:3], (c_new, k_new, v_new)):
            lst.append(val)
        xs, c_new, k_new, v_new = _layer(xs, cache_conv[l], cache_k[l].reshape(dec_batch, WINDOW, D_KV),
                                         cache_v[l].reshape(dec_batch, WINDOW, D_KV), mix_prm, consts, moe_prm, l,
                                         nseq=nseq_sample, tl=dec_seq, mask_history=False, alpha=alpha)
        for lst, val in zip(states[3:6], (c_new, k_new, v_new)):
            lst.append(val)

    def kv(lst, nb):
        return jnp.stack(lst).reshape(depth, nb, WINDOW, N_KV_HEADS, HEAD_DIM)

    return (xp, xs, jnp.stack(states[0]), kv(states[1], batch), kv(states[2], batch),
            jnp.stack(states[3]), kv(states[4], dec_batch), kv(states[5], dec_batch))
```

```python
import functools

import numpy as np
import jax
import jax.numpy as jnp
from jax import lax
from jax.experimental import pallas as pl
from jax.experimental.pallas import tpu as pltpu

D_MODEL = 1024
D_CONV = 512
CONV_W = 3
N_HEADS = 8
N_KV_HEADS = 2
GROUP = N_HEADS // N_KV_HEADS
HEAD_DIM = 64
D_ATTN = N_HEADS * HEAD_DIM
D_KV = N_KV_HEADS * HEAD_DIM
WINDOW = 128
CHUNK = 64
KEYS = WINDOW + CHUNK
QROWS = GROUP * CHUNK
N_GROUPS = 4
EXP_PER_GROUP = 8
N_EXPERTS = N_GROUPS * EXP_PER_GROUP
TOP_K = 2
D_EXP = 512
LN_EPS = 1e-5
NEG_INF = -1e30

SUBLANES = 8
LANES = 128
ROW_TILES = D_MODEL // LANES
assert ROW_TILES == SUBLANES
ROWS = 512
EXPERT_BLOCK = 512
RANK_CHUNK = 256
SOFTMAX_ROWS = 64
VMEM_LIMIT = 56 * 1024 * 1024

F32 = jnp.float32
BF16 = jnp.bfloat16

_OFF_B, _OFF_C, _OFF_H = 0, D_CONV, 2 * D_CONV
_OFF_Q = 3 * D_CONV
_OFF_K = _OFF_Q + D_ATTN
_OFF_V = _OFF_K + D_KV
D_IN = _OFF_V + D_KV

_R_E1, _R_E2, _R_RANK1, _R_RANK2, _R_G1, _R_G2 = 0, 1, 2, 3, 4, 5


def _load_rows(ref, n):
    return jnp.concatenate([ref[pl.ds(j, n, stride=ROW_TILES), :] for j in range(ROW_TILES)], axis=1)


def _store_rows(ref, val, n):
    for j in range(ROW_TILES):
        ref[pl.ds(j, n, stride=ROW_TILES), :] = val[:, j * LANES:(j + 1) * LANES]


def _dot(a, b):
    return jnp.dot(a, b, preferred_element_type=F32)


def _rms_norm(x, g):
    return x * lax.rsqrt(jnp.mean(jnp.square(x), -1, keepdims=True) + LN_EPS) * g


def _layer_norm(x, g, b):
    mu = jnp.mean(x, -1, keepdims=True)
    xc = x - mu
    var = jnp.mean(jnp.square(xc), -1, keepdims=True)
    return xc * lax.rsqrt(var + LN_EPS) * g + b


def _mixer_kernel(x_ref, cconv_ref, ck_ref, cv_ref, w_in_ref, convw_ref, convb_ref, abias_ref,
                  sink_ref, gconv_ref, gattn_ref, w_out_ref, ln1g_ref, ln1b_ref, wr_ref, br_ref,
                  tri_ref, keysel_ref,
                  x1_ref, route_ref, gcol_ref, counts_ref, sconv_ref, sk_ref, sv_ref,
                  uext, kext, vext, cnt_s, s_ref, e_ref, sinkden_ref, *, nseq, tl, alpha, mask_history):
    b = pl.program_id(0)
    s = pl.program_id(1)
    rows = nseq * tl
    upitch = tl + SUBLANES
    kpitch = WINDOW + tl

    @pl.when(jnp.logical_and(b == 0, s == 0))
    def _():
        cnt_s[...] = jnp.zeros_like(cnt_s)

    @pl.when(s == 0)
    def _():
        for j in range(nseq):
            uext[j * upitch + SUBLANES - 2:j * upitch + SUBLANES, :] = cconv_ref[j]
            kext[j * kpitch:j * kpitch + WINDOW, :] = ck_ref[j]
            vext[j * kpitch:j * kpitch + WINDOW, :] = cv_ref[j]

    x = x_ref[...].reshape(rows, D_MODEL)
    xb = x.astype(BF16)

    u = _dot(xb, w_in_ref[:, _OFF_C:_OFF_C + D_CONV]) * _dot(xb, w_in_ref[:, _OFF_H:_OFF_H + D_CONV])
    w0, w1, w2 = convw_ref[0:1, :], convw_ref[1:2, :], convw_ref[2:3, :]
    ycs = []
    for j in range(nseq):
        base = j * upitch + SUBLANES
        uext[base:base + tl, :] = u[j * tl:(j + 1) * tl]
        yc = convb_ref[...] + uext[base - 2:base - 2 + tl, :] * w0
        yc = yc + uext[base - 1:base - 1 + tl, :] * w1
        yc = yc + uext[base:base + tl, :] * w2
        ycs.append(yc)
        tail = uext[base + tl - 2:base + tl, :]
        sconv_ref[j] = tail
        uext[base - 2:base, :] = tail
    yc = ycs[0] if nseq == 1 else jnp.concatenate(ycs, axis=0)
    y_conv = _dot(xb, w_in_ref[:, _OFF_B:_OFF_B + D_CONV]) * yc
    n_conv = _rms_norm(y_conv, gconv_ref[...]).astype(BF16)

    q = (_dot(xb, w_in_ref[:, _OFF_Q:_OFF_Q + D_ATTN]) * (HEAD_DIM ** -0.5)).astype(BF16)
    k = _dot(xb, w_in_ref[:, _OFF_K:_OFF_K + D_KV])
    v = _dot(xb, w_in_ref[:, _OFF_V:_OFF_V + D_KV])
    for j in range(nseq):
        kext[j * kpitch + WINDOW:(j + 1) * kpitch, :] = k[j * tl:(j + 1) * tl]
        vext[j * kpitch + WINDOW:(j + 1) * kpitch, :] = v[j * tl:(j + 1) * tl]

    nchunk = rows // CHUNK
    assert nseq == 1 or tl == CHUNK
    key_stride = CHUNK if nseq == 1 else kpitch
    head0 = lax.broadcasted_iota(jnp.int32, (1, D_KV), 1) < HEAD_DIM

    def windows(ext_ref):
        ext = ext_ref[...].astype(BF16)
        heads = (jnp.where(head0, ext, jnp.zeros_like(ext)), jnp.where(head0, jnp.zeros_like(ext), ext))
        return jnp.stack([jnp.concatenate([h[c * key_stride:c * key_stride + KEYS] for h in heads], axis=0)
                          for c in range(nchunk)])

    q3 = jnp.stack([jnp.concatenate([q[c * CHUNK:(c + 1) * CHUNK, g * D_KV:(g + 1) * D_KV] for g in range(GROUP)], axis=0)
                    for c in range(nchunk)])
    logits = lax.dot_general(q3, windows(kext), (((2,), (2,)), ((0,), (0,))), preferred_element_type=F32)
    s_ref[...] = logits.reshape(nchunk * QROWS, 2 * KEYS)

    assert 2 * KEYS == 3 * LANES
    mid0 = lax.broadcasted_iota(jnp.int32, (1, LANES), 1) < KEYS - LANES
    colk = lax.broadcasted_iota(jnp.int32, (1, 2 * KEYS), 1)
    key = jnp.where(colk < KEYS, colk, colk - KEYS)

    def exp_rows(i, masked):
        r = pl.multiple_of(i * SOFTMAX_ROWS, SOFTMAX_ROWS)
        rq = pl.multiple_of(r % QROWS, SOFTMAX_ROWS)
        lg = s_ref[pl.ds(r, SOFTMAX_ROWS), :] + abias_ref[pl.ds(rq, SOFTMAX_ROWS), :]
        if masked:
            first_pos = s * tl + (r // QROWS) * CHUNK - WINDOW
            lg = lg + jnp.where(first_pos + key < 0, NEG_INF, 0.0).astype(F32)
        t0, t1, t2 = lg[:, 0:LANES], lg[:, LANES:2 * LANES], lg[:, 2 * LANES:3 * LANES]
        sink0 = sink_ref[0, pl.ds(rq, SOFTMAX_ROWS), :]
        sink1 = sink_ref[1, pl.ds(rq, SOFTMAX_ROWS), :]
        m0 = jnp.maximum(jnp.max(jnp.maximum(t0, jnp.where(mid0, t1, NEG_INF)), -1, keepdims=True), sink0)
        m1 = jnp.maximum(jnp.max(jnp.maximum(t2, jnp.where(mid0, NEG_INF, t1)), -1, keepdims=True), sink1)
        e = jnp.concatenate([jnp.exp(t0 - m0), jnp.exp(t1 - jnp.where(mid0, m0, m1)), jnp.exp(t2 - m1)], axis=-1)
        e_ref[pl.ds(r, SOFTMAX_ROWS), :] = e.astype(BF16)
        sinkden_ref[pl.ds(r, SOFTMAX_ROWS), :] = jnp.where(head0, jnp.exp(sink0 - m0), jnp.exp(sink1 - m1))

    def loop_exp(lo_it, hi_it, masked):
        def body(i, carry):
            exp_rows(i, masked)
            return carry
        lax.fori_loop(lo_it, hi_it, body, 0, unroll=2)

    n_it = nchunk * QROWS // SOFTMAX_ROWS
    n_masked = min(WINDOW // CHUNK, nchunk) * QROWS // SOFTMAX_ROWS if mask_history else 0
    if n_masked:
        loop_exp(0, n_masked, True)
    loop_exp(n_masked, n_it, False)

    vsel = jnp.concatenate([windows(vext), jnp.broadcast_to(keysel_ref[...], (nchunk, 2 * KEYS, D_KV))], axis=-1)
    o2 = lax.dot_general(e_ref[...].reshape(nchunk, QROWS, 2 * KEYS), vsel, (((2,), (1,)), ((0,), (0,))),
                         preferred_element_type=F32)
    o = o2[..., 0:D_KV] * (1.0 / (o2[..., D_KV:2 * D_KV] + sinkden_ref[...].reshape(nchunk, QROWS, D_KV)))
    y_attn = jnp.concatenate(
        [jnp.concatenate([o[c, g * CHUNK:(g + 1) * CHUNK, :] for g in range(GROUP)], axis=1) for c in range(nchunk)],
        axis=0)

    for j in range(nseq):
        sk_ref[j] = kext[j * kpitch + tl:(j + 1) * kpitch, :]
        sv_ref[j] = vext[j * kpitch + tl:(j + 1) * kpitch, :]
    if nseq == 1:
        kext[0:WINDOW, :] = kext[tl:tl + WINDOW, :]
        vext[0:WINDOW, :] = vext[tl:tl + WINDOW, :]

    n_attn = _rms_norm(y_attn, gattn_ref[...]).astype(BF16)
    mixed = _dot(n_conv, w_out_ref[0:D_CONV, :]) + _dot(n_attn, w_out_ref[D_CONV:D_CONV + D_ATTN, :])
    x1 = _layer_norm(alpha * x + mixed, ln1g_ref[...], ln1b_ref[...])
    _store_rows(x1_ref, x1, rows)

    logits_t = (_dot(x1.astype(BF16), wr_ref[...]) + br_ref[...]).T
    sub = lax.broadcasted_iota(jnp.int32, (SUBLANES, rows), 0)
    gl = jnp.where(sub < N_GROUPS, logits_t[N_EXPERTS:N_EXPERTS + SUBLANES, :], -jnp.inf)
    gmax = jnp.max(gl, axis=0, keepdims=True)
    grp = jnp.min(jnp.where(gl == gmax, sub, SUBLANES), axis=0, keepdims=True)
    p_grp = 1.0 / jnp.sum(jnp.exp(gl - gmax), axis=0, keepdims=True)
    el = logits_t[(N_GROUPS - 1) * EXP_PER_GROUP:N_GROUPS * EXP_PER_GROUP, :]
    for g in range(N_GROUPS - 2, -1, -1):
        el = jnp.where(grp == g, logits_t[g * EXP_PER_GROUP:(g + 1) * EXP_PER_GROUP, :], el)
    v1 = jnp.max(el, axis=0, keepdims=True)
    i1 = jnp.min(jnp.where(el == v1, sub, SUBLANES), axis=0, keepdims=True)
    el2 = jnp.where(sub == i1, -jnp.inf, el)
    v2 = jnp.max(el2, axis=0, keepdims=True)
    i2 = jnp.min(jnp.where(el2 == v2, sub, SUBLANES), axis=0, keepdims=True)
    e2 = jnp.exp(v2 - v1)
    gate1 = p_grp * (1.0 / (1.0 + e2))
    gate2 = p_grp * (e2 / (1.0 + e2))

    chosen = jnp.logical_or(sub == i1, sub == i2)
    onehot = jnp.concatenate(
        [jnp.where(jnp.logical_and(grp == g, chosen), 1.0, 0.0) for g in range(N_GROUPS)], axis=0)
    running = cnt_s[...]
    ranks = []
    for c in range(rows // RANK_CHUNK):
        oh = onehot[:, c * RANK_CHUNK:(c + 1) * RANK_CHUNK]
        ranks.append(_dot(oh.astype(BF16), tri_ref[...]) + running)
        running = running + jnp.sum(oh, axis=1, keepdims=True)
    rank = jnp.concatenate(ranks, axis=1)
    cnt_s[...] = running
    counts_ref[...] = jnp.broadcast_to(running, (N_EXPERTS, LANES))

    ex1 = grp * EXP_PER_GROUP + i1
    ex2 = grp * EXP_PER_GROUP + i2
    erow = lax.broadcasted_iota(jnp.int32, (N_EXPERTS, rows), 0)
    rank1 = jnp.sum(jnp.where(erow == ex1, rank, 0.0), axis=0, keepdims=True)
    rank2 = jnp.sum(jnp.where(erow == ex2, rank, 0.0), axis=0, keepdims=True)
    fields = {_R_E1: ex1.astype(F32), _R_E2: ex2.astype(F32), _R_RANK1: rank1, _R_RANK2: rank2,
              _R_G1: gate1, _R_G2: gate2}
    record = jnp.zeros((SUBLANES, rows), F32)
    for r, val in fields.items():
        record = jnp.where(sub == r, val, record)
    route_ref[...] = record
    padded = jnp.concatenate([record, jnp.zeros((LANES - SUBLANES, rows), F32)], axis=0)
    gcol_ref[...] = padded.T


def _mixer(x, cconv, ck, cv, prm, consts, *, nseq, tl, mask_history, alpha):
    nb_total, t_total = x.shape[0], x.shape[1]
    nb, ns = nb_total // nseq, t_total // tl
    rows = nseq * tl
    n_tok = nb_total * t_total
    w_in, convw, convb, sinks, gconv, gattn, w_out, ln1g, ln1b, wr, br = prm
    abias, tri, keysel = consts

    def full(a):
        return pl.BlockSpec(a.shape, lambda b, s, _n=a.ndim: (0,) * _n)

    def seq_state(width, nrows):
        return pl.BlockSpec((nseq, nrows, width), lambda b, s: (b, 0, 0))

    in_specs = [
        pl.BlockSpec((nseq, tl, D_MODEL), lambda b, s: (b, s, 0)),
        seq_state(D_CONV, CONV_W - 1), seq_state(D_KV, WINDOW), seq_state(D_KV, WINDOW),
        full(w_in), full(convw), full(convb), full(abias), full(sinks), full(gconv), full(gattn),
        full(w_out), full(ln1g), full(ln1b), full(wr), full(br), full(tri), full(keysel),
    ]
    out_shape = [
        jax.ShapeDtypeStruct((n_tok * ROW_TILES, LANES), F32),
        jax.ShapeDtypeStruct((SUBLANES, n_tok), F32),
        jax.ShapeDtypeStruct((n_tok, LANES), F32),
        jax.ShapeDtypeStruct((N_EXPERTS, LANES), F32),
        jax.ShapeDtypeStruct((nb_total, CONV_W - 1, D_CONV), F32),
        jax.ShapeDtypeStruct((nb_total, WINDOW, D_KV), F32),
        jax.ShapeDtypeStruct((nb_total, WINDOW, D_KV), F32),
    ]
    out_specs = [
        pl.BlockSpec((rows * ROW_TILES, LANES), lambda b, s: (b * ns + s, 0)),
        pl.BlockSpec((SUBLANES, rows), lambda b, s: (0, b * ns + s)),
        pl.BlockSpec((rows, LANES), lambda b, s: (b * ns + s, 0)),
        pl.BlockSpec((N_EXPERTS, LANES), lambda b, s: (0, 0)),
        seq_state(D_CONV, CONV_W - 1), seq_state(D_KV, WINDOW), seq_state(D_KV, WINDOW),
    ]
    scratch = [
        pltpu.VMEM((nseq * (tl + SUBLANES), D_CONV), F32),
        pltpu.VMEM((nseq * (WINDOW + tl), D_KV), F32),
        pltpu.VMEM((nseq * (WINDOW + tl), D_KV), F32),
        pltpu.VMEM((N_EXPERTS, 1), F32),
        pltpu.VMEM((rows // CHUNK * QROWS, 2 * KEYS), F32),
        pltpu.VMEM((rows // CHUNK * QROWS, 2 * KEYS), BF16),
        pltpu.VMEM((rows // CHUNK * QROWS, D_KV), F32),
    ]
    kern = functools.partial(_mixer_kernel, nseq=nseq, tl=tl, alpha=alpha, mask_history=mask_history)
    return pl.pallas_call(
        kern, grid=(nb, ns), in_specs=in_specs, out_specs=out_specs, out_shape=out_shape,
        scratch_shapes=scratch, name="mixer",
        compiler_params=pltpu.CompilerParams(dimension_semantics=("arbitrary", "arbitrary"),
                                             vmem_limit_bytes=VMEM_LIMIT),
    )(x, cconv, ck, cv, w_in, convw, convb, abias, sinks, gconv, gattn, w_out, ln1g, ln1b, wr, br, tri, keysel)


_PAD_PIECES = tuple(2 ** k for k in range(EXPERT_BLOCK.bit_length() - 2, -1, -1))


def _tile_rows(row, n=1):
    return pl.ds(pl.multiple_of(row * ROW_TILES, ROW_TILES), n * ROW_TILES)


def _row_copy(src, src_row, dst, dst_row, sem):
    return pltpu.make_async_copy(src.at[_tile_rows(src_row), :], dst.at[_tile_rows(dst_row), :], sem)


def _dispatch_kernel(zstart_ref, zcount_ref, tail_ref, pos1_ref, pos2_ref, x_ref, xs_hbm, zeros_v, sem, zsem, *, tb):
    step = pl.program_id(0)

    def zero_copy(start, piece):
        return pltpu.make_async_copy(zeros_v.at[_tile_rows(0, piece), :], xs_hbm.at[_tile_rows(start, piece), :], zsem)

    def pad_copy(e, piece, taken):
        return zero_copy(zstart_ref[e] + taken, piece)

    def for_each_pad_piece(fn):
        def per_tail_block(j, carry):
            for part in range(EXPERT_BLOCK // _PAD_PIECES[0]):
                fn(zero_copy(tail_ref[0] + j * EXPERT_BLOCK + part * _PAD_PIECES[0], _PAD_PIECES[0]))
            return carry

        lax.fori_loop(0, tail_ref[1], per_tail_block, 0)

        def per_expert(e, carry):
            count = zcount_ref[e]
            taken = 0
            for piece in _PAD_PIECES:
                present = (count & piece) != 0

                @pl.when(present)
                def _(piece=piece, taken=taken):
                    fn(pad_copy(e, piece, taken))

                taken = taken + jnp.where(present, piece, 0)
            return carry

        lax.fori_loop(0, N_EXPERTS, per_expert, 0)

    @pl.when(step == 0)
    def _():
        zeros_v[...] = jnp.zeros_like(zeros_v)
        for_each_pad_piece(lambda c: c.start())

    def issue(t, carry):
        _row_copy(x_ref, t, xs_hbm, pos1_ref[t], sem).start()
        _row_copy(x_ref, t, xs_hbm, pos2_ref[t], sem).start()
        return carry

    lax.fori_loop(0, tb, issue, 0, unroll=8)
    pltpu.make_async_copy(xs_hbm.at[_tile_rows(0, 2 * tb), :], xs_hbm.at[_tile_rows(0, 2 * tb), :], sem).wait()

    @pl.when(step == 0)
    def _():
        for_each_pad_piece(lambda c: c.wait())


def _dispatch(x_rows, pos1, pos2, zstart, zcount, tail, n_slots, *, tb):
    n_tok = x_rows.shape[0] // ROW_TILES
    kern = functools.partial(_dispatch_kernel, tb=tb)
    grid_spec = pltpu.PrefetchScalarGridSpec(
        num_scalar_prefetch=3, grid=(n_tok // tb,),
        in_specs=[pl.BlockSpec((tb,), lambda i, *_: (i,), memory_space=pltpu.SMEM),
                  pl.BlockSpec((tb,), lambda i, *_: (i,), memory_space=pltpu.SMEM),
                  pl.BlockSpec((tb * ROW_TILES, LANES), lambda i, *_: (i, 0))],
        out_specs=pl.BlockSpec(memory_space=pl.ANY),
        scratch_shapes=[pltpu.VMEM((_PAD_PIECES[0] * ROW_TILES, LANES), F32),
                        pltpu.SemaphoreType.DMA(()), pltpu.SemaphoreType.DMA(())],
    )
    return pl.pallas_call(
        kern, grid_spec=grid_spec, out_shape=jax.ShapeDtypeStruct((n_slots * ROW_TILES, LANES), F32),
        name="dispatch",
        compiler_params=pltpu.CompilerParams(dimension_semantics=("arbitrary",)),
    )(zstart, zcount, tail, pos1, pos2, x_rows)


def _expert_kernel(be_ref, src_ref, used_ref, xs_ref, wg_ref, wu_ref, wd_ref, yb_ref, wg_b, wu_b, wd_b):
    i = pl.program_id(0)
    e = be_ref[i]
    prev = be_ref[jnp.maximum(i - 1, 0)]

    @pl.when(jnp.logical_or(i == 0, e != prev))
    def _():
        wg_b[...] = wg_ref[...].astype(BF16)
        wu_b[...] = wu_ref[...].astype(BF16)
        wd_b[...] = wd_ref[...].astype(BF16)

    @pl.when(used_ref[i] != 0)
    def _():
        x = _load_rows(xs_ref, EXPERT_BLOCK).astype(BF16)
        g = _dot(x, wg_b[...])
        u = _dot(x, wu_b[...])
        hid = (g / (1.0 + jnp.exp(-g))) * u
        _store_rows(yb_ref, _dot(hid.astype(BF16), wd_b[...]), EXPERT_BLOCK)

    @pl.when(used_ref[i] == 0)
    def _():
        yb_ref[...] = jnp.zeros_like(yb_ref)


def _experts(xs, blk_e, blk_src, blk_used, w_gate, w_up, w_down, layer):
    n_slots = xs.shape[0] // ROW_TILES
    nblk = n_slots // EXPERT_BLOCK

    def wspec(shape):
        return pl.BlockSpec((None, None) + shape, lambda i, be, src, used: (layer, be[i], 0, 0))

    grid_spec = pltpu.PrefetchScalarGridSpec(
        num_scalar_prefetch=3, grid=(nblk,),
        in_specs=[pl.BlockSpec((EXPERT_BLOCK * ROW_TILES, LANES), lambda i, be, src, used: (src[i], 0)),
                  wspec((D_MODEL, D_EXP)), wspec((D_MODEL, D_EXP)), wspec((D_EXP, D_MODEL))],
        out_specs=pl.BlockSpec((EXPERT_BLOCK * ROW_TILES, LANES), lambda i, be, src, used: (i, 0)),
        scratch_shapes=[pltpu.VMEM((D_MODEL, D_EXP), BF16), pltpu.VMEM((D_MODEL, D_EXP), BF16),
                        pltpu.VMEM((D_EXP, D_MODEL), BF16)],
    )
    return pl.pallas_call(
        _expert_kernel, grid_spec=grid_spec, out_shape=jax.ShapeDtypeStruct((n_slots * ROW_TILES, LANES), F32),
        name="experts",
        compiler_params=pltpu.CompilerParams(dimension_semantics=("arbitrary",), vmem_limit_bytes=VMEM_LIMIT),
    )(blk_e, blk_src, blk_used, xs, w_gate, w_up, w_down)


def _combine_kernel(pos1_ref, pos2_ref, x1_ref, gcol_ref, yb_hbm, g_ref, b_ref, out_ref, ybuf, sem, *, tb, alpha):
    def issue(t, carry):
        _row_copy(yb_hbm, pos1_ref[t], ybuf.at[0], t, sem).start()
        _row_copy(yb_hbm, pos2_ref[t], ybuf.at[1], t, sem).start()
        return carry

    lax.fori_loop(0, tb, issue, 0, unroll=8)
    for k in range(TOP_K):
        pltpu.make_async_copy(yb_hbm.at[_tile_rows(0, tb), :], ybuf.at[k], sem).wait()
    gates = gcol_ref[...]
    y = gates[:, _R_G1:_R_G1 + 1] * _load_rows(ybuf.at[0], tb) + gates[:, _R_G2:_R_G2 + 1] * _load_rows(ybuf.at[1], tb)
    out_ref[...] = _layer_norm(alpha * _load_rows(x1_ref, tb) + y, g_ref[...], b_ref[...])


def _combine(x1_rows, gcol, yb, pos1, pos2, ln2g, ln2b, *, tb, alpha):
    n_tok = x1_rows.shape[0] // ROW_TILES
    kern = functools.partial(_combine_kernel, tb=tb, alpha=alpha)
    return pl.pallas_call(
        kern, grid=(n_tok // tb,),
        in_specs=[pl.BlockSpec((tb,), lambda i: (i,), memory_space=pltpu.SMEM),
                  pl.BlockSpec((tb,), lambda i: (i,), memory_space=pltpu.SMEM),
                  pl.BlockSpec((tb * ROW_TILES, LANES), lambda i: (i, 0)),
                  pl.BlockSpec((tb, LANES), lambda i: (i, 0)),
                  pl.BlockSpec(memory_space=pl.ANY),
                  pl.BlockSpec((1, D_MODEL), lambda i: (0, 0)),
                  pl.BlockSpec((1, D_MODEL), lambda i: (0, 0))],
        out_specs=pl.BlockSpec((tb, D_MODEL), lambda i: (i, 0)),
        out_shape=jax.ShapeDtypeStruct((n_tok, D_MODEL), F32),
        scratch_shapes=[pltpu.VMEM((TOP_K, tb * ROW_TILES, LANES), F32), pltpu.SemaphoreType.DMA(())],
        name="combine",
        compiler_params=pltpu.CompilerParams(dimension_semantics=("arbitrary",), vmem_limit_bytes=VMEM_LIMIT),
    )(pos1, pos2, x1_rows, gcol, yb, ln2g, ln2b)


def _attn_column_order():
    g, kvh, d = np.meshgrid(np.arange(GROUP), np.arange(N_KV_HEADS), np.arange(HEAD_DIM), indexing='ij')
    return ((kvh * GROUP + g) * HEAD_DIM + d).reshape(-1)


def _alibi_bias():
    slopes = np.asarray([2.0 ** (-8.0 * (h + 1) / N_HEADS) for h in range(N_HEADS)], np.float32)
    qi = np.arange(CHUNK, dtype=np.int32)[:, None]
    sj = np.arange(KEYS, dtype=np.int32)[None, :]
    dist = np.abs(qi + WINDOW - sj).astype(np.float32)
    bias = -slopes.reshape(N_KV_HEADS, GROUP, 1, 1) * dist
    bias = np.transpose(bias, (1, 2, 0, 3)).reshape(GROUP * CHUNK, N_KV_HEADS * KEYS)
    return jnp.asarray(bias, F32)


def _key_selector():
    row_head = np.arange(N_KV_HEADS * KEYS)[:, None] // KEYS
    lane_head = np.arange(D_KV)[None, :] // HEAD_DIM
    return jnp.asarray((row_head == lane_head).astype(np.float32), BF16)


def _strict_upper():
    r = np.arange(RANK_CHUNK)
    return jnp.asarray((r[:, None] < r[None, :]).astype(np.float32), BF16)


def _slot_plan(route, counts, n_tok):
    nblk = (n_tok * TOP_K) // EXPERT_BLOCK + N_EXPERTS
    experts = route[_R_E1:_R_E2 + 1].astype(jnp.int32)
    ranks = route[_R_RANK1:_R_RANK2 + 1].astype(jnp.int32)
    cnt = counts[:, 0].astype(jnp.int32)
    padded = (cnt + EXPERT_BLOCK - 1) // EXPERT_BLOCK * EXPERT_BLOCK
    pad_end = jnp.cumsum(padded)
    pad_start = pad_end - padded
    ids = jnp.arange(N_EXPERTS, dtype=jnp.int32)
    pos = jnp.sum(jnp.where(experts[..., None] == ids, pad_start, 0), axis=-1) + ranks
    blk_first = jnp.arange(nblk, dtype=jnp.int32) * EXPERT_BLOCK
    blk_e = jnp.minimum(jnp.sum((pad_end[None, :] <= blk_first[:, None]).astype(jnp.int32), axis=-1), N_EXPERTS - 1)
    blk_used = (blk_first < pad_end[-1]).astype(jnp.int32)
    blk_src = jnp.minimum(jnp.arange(nblk, dtype=jnp.int32), pad_end[-1] // EXPERT_BLOCK - 1)
    tail = jnp.stack([pad_end[-1], nblk - pad_end[-1] // EXPERT_BLOCK]).astype(jnp.int32)
    return pos[0], pos[1], pad_start + cnt, padded - cnt, tail, blk_e, blk_src, blk_used, nblk * EXPERT_BLOCK


def _layer(x, cconv, ck, cv, mix_prm, consts, moe_prm, layer, *, nseq, tl, mask_history, alpha):
    nb_total, t_total = x.shape[0], x.shape[1]
    n_tok = nb_total * t_total
    tb = min(ROWS, n_tok)
    x1_rows, route, gcol, counts, sconv, sk, sv = _mixer(
        x, cconv, ck, cv, mix_prm, consts, nseq=nseq, tl=tl, mask_history=mask_history, alpha=alpha)
    pos1, pos2, zstart, zcount, tail, blk_e, blk_src, blk_used, n_slots = _slot_plan(route, counts, n_tok)
    xs = _dispatch(x1_rows, pos1, pos2, zstart, zcount, tail, n_slots, tb=tb)
    w_gate, w_up, w_down, ln2g, ln2b = moe_prm
    yb = _experts(xs, blk_e, blk_src, blk_used, w_gate, w_up, w_down, layer)
    out = _combine(x1_rows, gcol, yb, pos1, pos2, ln2g, ln2b, tb=tb, alpha=alpha)
    return out.reshape(nb_total, t_total, D_MODEL), sconv, sk, sv


def kernel(x_prompt, x_sample, cache_conv, cache_k, cache_v, w_in, conv_w, conv_b, attn_sinks, g_conv, g_attn,
           w_out, ln1_g, ln1_b, router_group_w, router_group_b, router_expert_w, router_expert_b,
           expert_w_gate, expert_w_up, expert_w_down, ln2_g, ln2_b):
    depth = w_in.shape[0]
    alpha = (2 * depth) ** 0.25
    batch, seq = x_prompt.shape[0], x_prompt.shape[1]
    dec_batch, dec_seq = x_sample.shape[0], x_sample.shape[1]
    assert dec_seq == CHUNK and seq % CHUNK == 0
    consts = (_alibi_bias(), _strict_upper(), _key_selector())
    attn_order = _attn_column_order()
    tl_prompt = min(ROWS, seq)
    nseq_sample = min(ROWS // dec_seq, dec_batch)
    assert seq % tl_prompt == 0 and dec_batch % nseq_sample == 0

    xp, xs = x_prompt, x_sample
    zeros_conv = jnp.zeros((batch, CONV_W - 1, D_CONV), F32)
    zeros_kv = jnp.zeros((batch, WINDOW, D_KV), F32)
    states = [[] for _ in range(6)]
    for l in range(depth):
        sink_col = jnp.broadcast_to(jnp.repeat(attn_sinks[l].astype(F32), CHUNK).reshape(N_KV_HEADS, QROWS, 1),
                                    (N_KV_HEADS, QROWS, LANES))
        wr = jnp.zeros((D_MODEL, LANES), F32)
        wr = wr.at[:, 0:N_EXPERTS].set(router_expert_w[l]).at[:, N_EXPERTS:N_EXPERTS + N_GROUPS].set(router_group_w[l])
        br = jnp.zeros((1, LANES), F32)
        br = br.at[0, 0:N_EXPERTS].set(router_expert_b[l]).at[0, N_EXPERTS:N_EXPERTS + N_GROUPS].set(router_group_b[l])
        w_in_l = w_in[l].at[:, _OFF_Q:_OFF_Q + D_ATTN].set(w_in[l][:, _OFF_Q + attn_order])
        w_out_l = w_out[l].at[D_CONV:D_CONV + D_ATTN, :].set(w_out[l][D_CONV + attn_order, :])
        mix_prm = (w_in_l.astype(BF16), conv_w[l], conv_b[l].reshape(1, D_CONV), sink_col,
                   g_conv[l].reshape(1, D_CONV), g_attn[l][attn_order].reshape(1, D_ATTN), w_out_l.astype(BF16),
                   ln1_g[l].reshape(1, D_MODEL), ln1_b[l].reshape(1, D_MODEL), wr.astype(BF16), br)
        moe_prm = (expert_w_gate, expert_w_up, expert_w_down,
                   ln2_g[l].reshape(1, D_MODEL), ln2_b[l].reshape(1, D_MODEL))
        xp, c_new, k_new, v_new = _layer(xp, zeros_conv, zeros_kv, zeros_kv, mix_prm, consts, moe_prm, l,
                                         nseq=1, tl=tl_prompt, mask_history=True, alpha=alpha)
        for lst, val in zip(states[0:3], (c_new, k_new, v_new)):
            lst.append(val)
        xs, c_new, k_new, v_new = _layer(xs, cache_conv[l], cache_k[l].reshape(dec_batch, WINDOW, D_KV),
                                         cache_v[l].reshape(dec_batch, WINDOW, D_KV), mix_prm, consts, moe_prm, l,
                                         nseq=nseq_sample, tl=dec_seq, mask_history=False, alpha=alpha)
        for lst, val in zip(states[3:6], (c_new, k_new, v_new)):
            lst.append(val)

    def kv(lst, nb):
        return jnp.stack(lst).reshape(depth, nb, WINDOW, N_KV_HEADS, HEAD_DIM)

    return (xp, xs, jnp.stack(states[0]), kv(states[1], batch), kv(states[2], batch),
            jnp.stack(states[3]), kv(states[4], dec_batch), kv(states[5], dec_batch))
```

```python
import functools

import numpy as np
import jax
import jax.numpy as jnp
from jax import lax
from jax.experimental import pallas as pl
from jax.experimental.pallas import tpu as pltpu

D_MODEL = 1024
D_CONV = 512
CONV_W = 3
N_HEADS = 8
N_KV_HEADS = 2
GROUP = N_HEADS // N_KV_HEADS
HEAD_DIM = 64
D_ATTN = N_HEADS * HEAD_DIM
D_KV = N_KV_HEADS * HEAD_DIM
WINDOW = 128
CHUNK = 64
KEYS = WINDOW + CHUNK
QROWS = GROUP * CHUNK
N_GROUPS = 4
EXP_PER_GROUP = 8
N_EXPERTS = N_GROUPS * EXP_PER_GROUP
TOP_K = 2
D_EXP = 512
LN_EPS = 1e-5
NEG_INF = -1e30

SUBLANES = 8
LANES = 128
ROW_TILES = D_MODEL // LANES
assert ROW_TILES == SUBLANES
ROWS = 512
EXPERT_BLOCK = 512
RANK_CHUNK = 256
SOFTMAX_ROWS = 64
VMEM_LIMIT = 56 * 1024 * 1024

F32 = jnp.float32
BF16 = jnp.bfloat16

_OFF_B, _OFF_C, _OFF_H = 0, D_CONV, 2 * D_CONV
_OFF_Q = 3 * D_CONV
_OFF_K = _OFF_Q + D_ATTN
_OFF_V = _OFF_K + D_KV
D_IN = _OFF_V + D_KV

_R_E1, _R_E2, _R_RANK1, _R_RANK2, _R_G1, _R_G2 = 0, 1, 2, 3, 4, 5


def _load_rows(ref, n):
    return jnp.concatenate([ref[pl.ds(j, n, stride=ROW_TILES), :] for j in range(ROW_TILES)], axis=1)


def _store_rows(ref, val, n):
    for j in range(ROW_TILES):
        ref[pl.ds(j, n, stride=ROW_TILES), :] = val[:, j * LANES:(j + 1) * LANES]


def _dot(a, b):
    return jnp.dot(a, b, preferred_element_type=F32)


def _rms_norm(x, g):
    return x * lax.rsqrt(jnp.mean(jnp.square(x), -1, keepdims=True) + LN_EPS) * g


def _layer_norm(x, g, b):
    mu = jnp.mean(x, -1, keepdims=True)
    xc = x - mu
    var = jnp.mean(jnp.square(xc), -1, keepdims=True)
    return xc * lax.rsqrt(var + LN_EPS) * g + b


def _mixer_kernel(x_ref, cconv_ref, ck_ref, cv_ref, w_in_ref, convw_ref, convb_ref, abias_ref,
                  sink_ref, gconv_ref, gattn_ref, w_out_ref, ln1g_ref, ln1b_ref, wr_ref, br_ref,
                  tri_ref, keysel_ref,
                  x1_ref, route_ref, gcol_ref, counts_ref, sconv_ref, sk_ref, sv_ref,
                  uext, kext, vext, cnt_s, s_ref, e_ref, sinkden_ref, m0_ref, m1_ref,
                  *, nseq, tl, alpha, mask_history):
    b = pl.program_id(0)
    s = pl.program_id(1)
    rows = nseq * tl
    upitch = tl + SUBLANES
    kpitch = WINDOW + tl

    @pl.when(jnp.logical_and(b == 0, s == 0))
    def _():
        cnt_s[...] = jnp.zeros_like(cnt_s)

    @pl.when(s == 0)
    def _():
        for j in range(nseq):
            uext[j * upitch + SUBLANES - 2:j * upitch + SUBLANES, :] = cconv_ref[j]
            kext[j * kpitch:j * kpitch + WINDOW, :] = ck_ref[j]
            vext[j * kpitch:j * kpitch + WINDOW, :] = cv_ref[j]

    x = x_ref[...].reshape(rows, D_MODEL)
    xb = x.astype(BF16)

    u = _dot(xb, w_in_ref[:, _OFF_C:_OFF_C + D_CONV]) * _dot(xb, w_in_ref[:, _OFF_H:_OFF_H + D_CONV])
    w0, w1, w2 = convw_ref[0:1, :], convw_ref[1:2, :], convw_ref[2:3, :]
    ycs = []
    for j in range(nseq):
        base = j * upitch + SUBLANES
        uext[base:base + tl, :] = u[j * tl:(j + 1) * tl]
        yc = convb_ref[...] + uext[base - 2:base - 2 + tl, :] * w0
        yc = yc + uext[base - 1:base - 1 + tl, :] * w1
        yc = yc + uext[base:base + tl, :] * w2
        ycs.append(yc)
        tail = uext[base + tl - 2:base + tl, :]
        sconv_ref[j] = tail
        uext[base - 2:base, :] = tail
    yc = ycs[0] if nseq == 1 else jnp.concatenate(ycs, axis=0)
    y_conv = _dot(xb, w_in_ref[:, _OFF_B:_OFF_B + D_CONV]) * yc
    n_conv = _rms_norm(y_conv, gconv_ref[...]).astype(BF16)

    q = (_dot(xb, w_in_ref[:, _OFF_Q:_OFF_Q + D_ATTN]) * (HEAD_DIM ** -0.5)).astype(BF16)
    k = _dot(xb, w_in_ref[:, _OFF_K:_OFF_K + D_KV])
    v = _dot(xb, w_in_ref[:, _OFF_V:_OFF_V + D_KV])
    for j in range(nseq):
        kext[j * kpitch + WINDOW:(j + 1) * kpitch, :] = k[j * tl:(j + 1) * tl]
        vext[j * kpitch + WINDOW:(j + 1) * kpitch, :] = v[j * tl:(j + 1) * tl]

    nchunk = rows // CHUNK
    assert nseq == 1 or tl == CHUNK
    key_stride = CHUNK if nseq == 1 else kpitch
    head0 = lax.broadcasted_iota(jnp.int32, (1, D_KV), 1) < HEAD_DIM

    def windows(ext_ref):
        ext = ext_ref[...].astype(BF16)
        heads = (jnp.where(head0, ext, jnp.zeros_like(ext)), jnp.where(head0, jnp.zeros_like(ext), ext))
        return jnp.stack([jnp.concatenate([h[c * key_stride:c * key_stride + KEYS] for h in heads], axis=0)
                          for c in range(nchunk)])

    q3 = jnp.stack([jnp.concatenate([q[c * CHUNK:(c + 1) * CHUNK, g * D_KV:(g + 1) * D_KV] for g in range(GROUP)], axis=0)
                    for c in range(nchunk)])
    logits = lax.dot_general(q3, windows(kext), (((2,), (2,)), ((0,), (0,))), preferred_element_type=F32)
    s_ref[...] = logits.reshape(nchunk * QROWS, 2 * KEYS)

    assert 2 * KEYS == 3 * LANES
    mid0 = lax.broadcasted_iota(jnp.int32, (1, LANES), 1) < KEYS - LANES
    colk = lax.broadcasted_iota(jnp.int32, (1, 2 * KEYS), 1)
    key = jnp.where(colk < KEYS, colk, colk - KEYS)

    def row_block(i):
        r = pl.multiple_of(i * SOFTMAX_ROWS, SOFTMAX_ROWS)
        return pl.ds(r, SOFTMAX_ROWS), pl.ds(pl.multiple_of(r % QROWS, SOFTMAX_ROWS), SOFTMAX_ROWS), r // QROWS

    def tiles(lg):
        return lg[:, 0:LANES], lg[:, LANES:2 * LANES], lg[:, 2 * LANES:3 * LANES]

    def max_rows(i, masked):
        rows_i, qrows_i, chunk_i = row_block(i)
        lg = s_ref[rows_i, :] + abias_ref[qrows_i, :]
        if masked:
            lg = lg + jnp.where(s * tl + chunk_i * CHUNK - WINDOW + key < 0, NEG_INF, 0.0).astype(F32)
        s_ref[rows_i, :] = lg
        t0, t1, t2 = tiles(lg)
        m0_ref[rows_i, :] = jnp.maximum(jnp.max(jnp.maximum(t0, jnp.where(mid0, t1, NEG_INF)), -1, keepdims=True),
                                        sink_ref[0, qrows_i, :])
        m1_ref[rows_i, :] = jnp.maximum(jnp.max(jnp.maximum(t2, jnp.where(mid0, NEG_INF, t1)), -1, keepdims=True),
                                        sink_ref[1, qrows_i, :])

    def exp_rows(i, masked):
        rows_i, qrows_i, _ = row_block(i)
        t0, t1, t2 = tiles(s_ref[rows_i, :])
        m0, m1 = m0_ref[rows_i, :], m1_ref[rows_i, :]
        e = jnp.concatenate([jnp.exp(t0 - m0), jnp.exp(t1 - jnp.where(mid0, m0, m1)), jnp.exp(t2 - m1)], axis=-1)
        e_ref[rows_i, :] = e.astype(BF16)
        sinkden_ref[rows_i, :] = jnp.where(head0, jnp.exp(sink_ref[0, qrows_i, :] - m0),
                                           jnp.exp(sink_ref[1, qrows_i, :] - m1))

    def loop_rows(fn, lo_it, hi_it, masked, unroll):
        def body(i, carry):
            fn(i, masked)
            return carry
        lax.fori_loop(lo_it, hi_it, body, 0, unroll=unroll)

    n_it = nchunk * QROWS // SOFTMAX_ROWS
    n_masked = min(WINDOW // CHUNK, nchunk) * QROWS // SOFTMAX_ROWS if mask_history else 0
    if n_masked:
        loop_rows(max_rows, 0, n_masked, True, 8)
    loop_rows(max_rows, n_masked, n_it, False, 8)
    loop_rows(exp_rows, 0, n_it, False, 2)

    vsel = jnp.concatenate([windows(vext), jnp.broadcast_to(keysel_ref[...], (nchunk, 2 * KEYS, D_KV))], axis=-1)
    o2 = lax.dot_general(e_ref[...].reshape(nchunk, QROWS, 2 * KEYS), vsel, (((2,), (1,)), ((0,), (0,))),
                         preferred_element_type=F32)
    o = o2[..., 0:D_KV] * (1.0 / (o2[..., D_KV:2 * D_KV] + sinkden_ref[...].reshape(nchunk, QROWS, D_KV)))
    y_attn = jnp.concatenate(
        [jnp.concatenate([o[c, g * CHUNK:(g + 1) * CHUNK, :] for g in range(GROUP)], axis=1) for c in range(nchunk)],
        axis=0)

    for j in range(nseq):
        sk_ref[j] = kext[j * kpitch + tl:(j + 1) * kpitch, :]
        sv_ref[j] = vext[j * kpitch + tl:(j + 1) * kpitch, :]
    if nseq == 1:
        kext[0:WINDOW, :] = kext[tl:tl + WINDOW, :]
        vext[0:WINDOW, :] = vext[tl:tl + WINDOW, :]

    n_attn = _rms_norm(y_attn, gattn_ref[...]).astype(BF16)
    mixed = _dot(n_conv, w_out_ref[0:D_CONV, :]) + _dot(n_attn, w_out_ref[D_CONV:D_CONV + D_ATTN, :])
    x1 = _layer_norm(alpha * x + mixed, ln1g_ref[...], ln1b_ref[...])
    _store_rows(x1_ref, x1, rows)

    logits_t = (_dot(x1.astype(BF16), wr_ref[...]) + br_ref[...]).T
    sub = lax.broadcasted_iota(jnp.int32, (SUBLANES, rows), 0)
    gl = jnp.where(sub < N_GROUPS, logits_t[N_EXPERTS:N_EXPERTS + SUBLANES, :], -jnp.inf)
    gmax = jnp.max(gl, axis=0, keepdims=True)
    grp = jnp.min(jnp.where(gl == gmax, sub, SUBLANES), axis=0, keepdims=True)
    p_grp = 1.0 / jnp.sum(jnp.exp(gl - gmax), axis=0, keepdims=True)
    el = logits_t[(N_GROUPS - 1) * EXP_PER_GROUP:N_GROUPS * EXP_PER_GROUP, :]
    for g in range(N_GROUPS - 2, -1, -1):
        el = jnp.where(grp == g, logits_t[g * EXP_PER_GROUP:(g + 1) * EXP_PER_GROUP, :], el)
    v1 = jnp.max(el, axis=0, keepdims=True)
    i1 = jnp.min(jnp.where(el == v1, sub, SUBLANES), axis=0, keepdims=True)
    el2 = jnp.where(sub == i1, -jnp.inf, el)
    v2 = jnp.max(el2, axis=0, keepdims=True)
    i2 = jnp.min(jnp.where(el2 == v2, sub, SUBLANES), axis=0, keepdims=True)
    e2 = jnp.exp(v2 - v1)
    gate1 = p_grp * (1.0 / (1.0 + e2))
    gate2 = p_grp * (e2 / (1.0 + e2))

    chosen = jnp.logical_or(sub == i1, sub == i2)
    onehot = jnp.concatenate(
        [jnp.where(jnp.logical_and(grp == g, chosen), 1.0, 0.0) for g in range(N_GROUPS)], axis=0)
    running = cnt_s[...]
    ranks = []
    for c in range(rows // RANK_CHUNK):
        oh = onehot[:, c * RANK_CHUNK:(c + 1) * RANK_CHUNK]
        ranks.append(_dot(oh.astype(BF16), tri_ref[...]) + running)
        running = running + jnp.sum(oh, axis=1, keepdims=True)
    rank = jnp.concatenate(ranks, axis=1)
    cnt_s[...] = running
    counts_ref[...] = jnp.broadcast_to(running, (N_EXPERTS, LANES))

    ex1 = grp * EXP_PER_GROUP + i1
    ex2 = grp * EXP_PER_GROUP + i2
    erow = lax.broadcasted_iota(jnp.int32, (N_EXPERTS, rows), 0)
    rank1 = jnp.sum(jnp.where(erow == ex1, rank, 0.0), axis=0, keepdims=True)
    rank2 = jnp.sum(jnp.where(erow == ex2, rank, 0.0), axis=0, keepdims=True)
    fields = {_R_E1: ex1.astype(F32), _R_E2: ex2.astype(F32), _R_RANK1: rank1, _R_RANK2: rank2,
              _R_G1: gate1, _R_G2: gate2}
    record = jnp.zeros((SUBLANES, rows), F32)
    for r, val in fields.items():
        record = jnp.where(sub == r, val, record)
    route_ref[...] = record
    padded = jnp.concatenate([record, jnp.zeros((LANES - SUBLANES, rows), F32)], axis=0)
    gcol_ref[...] = padded.T


def _mixer(x, cconv, ck, cv, prm, consts, *, nseq, tl, mask_history, alpha):
    nb_total, t_total = x.shape[0], x.shape[1]
    nb, ns = nb_total // nseq, t_total // tl
    rows = nseq * tl
    n_tok = nb_total * t_total
    w_in, convw, convb, sinks, gconv, gattn, w_out, ln1g, ln1b, wr, br = prm
    abias, tri, keysel = consts

    def full(a):
        return pl.BlockSpec(a.shape, lambda b, s, _n=a.ndim: (0,) * _n)

    def seq_state(width, nrows):
        return pl.BlockSpec((nseq, nrows, width), lambda b, s: (b, 0, 0))

    in_specs = [
        pl.BlockSpec((nseq, tl, D_MODEL), lambda b, s: (b, s, 0)),
        seq_state(D_CONV, CONV_W - 1), seq_state(D_KV, WINDOW), seq_state(D_KV, WINDOW),
        full(w_in), full(convw), full(convb), full(abias), full(sinks), full(gconv), full(gattn),
        full(w_out), full(ln1g), full(ln1b), full(wr), full(br), full(tri), full(keysel),
    ]
    out_shape = [
        jax.ShapeDtypeStruct((n_tok * ROW_TILES, LANES), F32),
        jax.ShapeDtypeStruct((SUBLANES, n_tok), F32),
        jax.ShapeDtypeStruct((n_tok, LANES), F32),
        jax.ShapeDtypeStruct((N_EXPERTS, LANES), F32),
        jax.ShapeDtypeStruct((nb_total, CONV_W - 1, D_CONV), F32),
        jax.ShapeDtypeStruct((nb_total, WINDOW, D_KV), F32),
        jax.ShapeDtypeStruct((nb_total, WINDOW, D_KV), F32),
    ]
    out_specs = [
        pl.BlockSpec((rows * ROW_TILES, LANES), lambda b, s: (b * ns + s, 0)),
        pl.BlockSpec((SUBLANES, rows), lambda b, s: (0, b * ns + s)),
        pl.BlockSpec((rows, LANES), lambda b, s: (b * ns + s, 0)),
        pl.BlockSpec((N_EXPERTS, LANES), lambda b, s: (0, 0)),
        seq_state(D_CONV, CONV_W - 1), seq_state(D_KV, WINDOW), seq_state(D_KV, WINDOW),
    ]
    scratch = [
        pltpu.VMEM((nseq * (tl + SUBLANES), D_CONV), F32),
        pltpu.VMEM((nseq * (WINDOW + tl), D_KV), F32),
        pltpu.VMEM((nseq * (WINDOW + tl), D_KV), F32),
        pltpu.VMEM((N_EXPERTS, 1), F32),
        pltpu.VMEM((rows // CHUNK * QROWS, 2 * KEYS), F32),
        pltpu.VMEM((rows // CHUNK * QROWS, 2 * KEYS), BF16),
        pltpu.VMEM((rows // CHUNK * QROWS, D_KV), F32),
        pltpu.VMEM((rows // CHUNK * QROWS, LANES), F32),
        pltpu.VMEM((rows // CHUNK * QROWS, LANES), F32),
    ]
    kern = functools.partial(_mixer_kernel, nseq=nseq, tl=tl, alpha=alpha, mask_history=mask_history)
    return pl.pallas_call(
        kern, grid=(nb, ns), in_specs=in_specs, out_specs=out_specs, out_shape=out_shape,
        scratch_shapes=scratch, name="mixer",
        compiler_params=pltpu.CompilerParams(dimension_semantics=("arbitrary", "arbitrary"),
                                             vmem_limit_bytes=VMEM_LIMIT),
    )(x, cconv, ck, cv, w_in, convw, convb, abias, sinks, gconv, gattn, w_out, ln1g, ln1b, wr, br, tri, keysel)


_PAD_PIECES = tuple(2 ** k for k in range(EXPERT_BLOCK.bit_length() - 2, -1, -1))


def _tile_rows(row, n=1):
    return pl.ds(pl.multiple_of(row * ROW_TILES, ROW_TILES), n * ROW_TILES)


def _row_copy(src, src_row, dst, dst_row, sem):
    return pltpu.make_async_copy(src.at[_tile_rows(src_row), :], dst.at[_tile_rows(dst_row), :], sem)


def _dispatch_kernel(zstart_ref, zcount_ref, tail_ref, pos1_ref, pos2_ref, x_ref, xs_hbm, zeros_v, sem, zsem, *, tb):
    step = pl.program_id(0)

    def zero_copy(start, piece):
        return pltpu.make_async_copy(zeros_v.at[_tile_rows(0, piece), :], xs_hbm.at[_tile_rows(start, piece), :], zsem)

    def pad_copy(e, piece, taken):
        return zero_copy(zstart_ref[e] + taken, piece)

    def for_each_pad_piece(fn):
        def per_tail_block(j, carry):
            for part in range(EXPERT_BLOCK // _PAD_PIECES[0]):
                fn(zero_copy(tail_ref[0] + j * EXPERT_BLOCK + part * _PAD_PIECES[0], _PAD_PIECES[0]))
            return carry

        lax.fori_loop(0, tail_ref[1], per_tail_block, 0)

        def per_expert(e, carry):
            count = zcount_ref[e]
            taken = 0
            for piece in _PAD_PIECES:
                present = (count & piece) != 0

                @pl.when(present)
                def _(piece=piece, taken=taken):
                    fn(pad_copy(e, piece, taken))

                taken = taken + jnp.where(present, piece, 0)
            return carry

        lax.fori_loop(0, N_EXPERTS, per_expert, 0)

    @pl.when(step == 0)
    def _():
        zeros_v[...] = jnp.zeros_like(zeros_v)
        for_each_pad_piece(lambda c: c.start())

    def issue(t, carry):
        _row_copy(x_ref, t, xs_hbm, pos1_ref[t], sem).start(priority=0)
        _row_copy(x_ref, t, xs_hbm, pos2_ref[t], sem).start(priority=1)
        return carry

    lax.fori_loop(0, tb, issue, 0, unroll=8)
    pltpu.make_async_copy(xs_hbm.at[_tile_rows(0, 2 * tb), :], xs_hbm.at[_tile_rows(0, 2 * tb), :], sem).wait()

    @pl.when(step == 0)
    def _():
        for_each_pad_piece(lambda c: c.wait())


def _dispatch(x_rows, pos1, pos2, zstart, zcount, tail, n_slots, *, tb):
    n_tok = x_rows.shape[0] // ROW_TILES
    kern = functools.partial(_dispatch_kernel, tb=tb)
    grid_spec = pltpu.PrefetchScalarGridSpec(
        num_scalar_prefetch=3, grid=(n_tok // tb,),
        in_specs=[pl.BlockSpec((tb,), lambda i, *_: (i,), memory_space=pltpu.SMEM),
                  pl.BlockSpec((tb,), lambda i, *_: (i,), memory_space=pltpu.SMEM),
                  pl.BlockSpec((tb * ROW_TILES, LANES), lambda i, *_: (i, 0))],
        out_specs=pl.BlockSpec(memory_space=pl.ANY),
        scratch_shapes=[pltpu.VMEM((_PAD_PIECES[0] * ROW_TILES, LANES), F32),
                        pltpu.SemaphoreType.DMA(()), pltpu.SemaphoreType.DMA(())],
    )
    return pl.pallas_call(
        kern, grid_spec=grid_spec, out_shape=jax.ShapeDtypeStruct((n_slots * ROW_TILES, LANES), F32),
        name="dispatch",
        compiler_params=pltpu.CompilerParams(dimension_semantics=("arbitrary",)),
    )(zstart, zcount, tail, pos1, pos2, x_rows)


def _expert_kernel(be_ref, src_ref, used_ref, xs_ref, wg_ref, wu_ref, wd_ref, yb_ref, wg_b, wu_b, wd_b):
    i = pl.program_id(0)
    e = be_ref[i]
    prev = be_ref[jnp.maximum(i - 1, 0)]

    @pl.when(jnp.logical_or(i == 0, e != prev))
    def _():
        wg_b[...] = wg_ref[...].astype(BF16)
        wu_b[...] = wu_ref[...].astype(BF16)
        wd_b[...] = wd_ref[...].astype(BF16)

    @pl.when(used_ref[i] != 0)
    def _():
        x = _load_rows(xs_ref, EXPERT_BLOCK).astype(BF16)
        g = _dot(x, wg_b[...])
        u = _dot(x, wu_b[...])
        hid = (g / (1.0 + jnp.exp(-g))) * u
        _store_rows(yb_ref, _dot(hid.astype(BF16), wd_b[...]), EXPERT_BLOCK)

    @pl.when(used_ref[i] == 0)
    def _():
        yb_ref[...] = jnp.zeros_like(yb_ref)


def _experts(xs, blk_e, blk_src, blk_used, w_gate, w_up, w_down, layer):
    n_slots = xs.shape[0] // ROW_TILES
    nblk = n_slots // EXPERT_BLOCK

    def wspec(shape):
        return pl.BlockSpec((None, None) + shape, lambda i, be, src, used: (layer, be[i], 0, 0))

    grid_spec = pltpu.PrefetchScalarGridSpec(
        num_scalar_prefetch=3, grid=(nblk,),
        in_specs=[pl.BlockSpec((EXPERT_BLOCK * ROW_TILES, LANES), lambda i, be, src, used: (src[i], 0)),
                  wspec((D_MODEL, D_EXP)), wspec((D_MODEL, D_EXP)), wspec((D_EXP, D_MODEL))],
        out_specs=pl.BlockSpec((EXPERT_BLOCK * ROW_TILES, LANES), lambda i, be, src, used: (i, 0)),
        scratch_shapes=[pltpu.VMEM((D_MODEL, D_EXP), BF16), pltpu.VMEM((D_MODEL, D_EXP), BF16),
                        pltpu.VMEM((D_EXP, D_MODEL), BF16)],
    )
    return pl.pallas_call(
        _expert_kernel, grid_spec=grid_spec, out_shape=jax.ShapeDtypeStruct((n_slots * ROW_TILES, LANES), F32),
        name="experts",
        compiler_params=pltpu.CompilerParams(dimension_semantics=("arbitrary",), vmem_limit_bytes=VMEM_LIMIT),
    )(blk_e, blk_src, blk_used, xs, w_gate, w_up, w_down)


def _combine_kernel(pos1_ref, pos2_ref, x1_ref, gcol_ref, yb_hbm, g_ref, b_ref, out_ref, ybuf, sem, *, tb, alpha):
    def issue(t, carry):
        _row_copy(yb_hbm, pos1_ref[t], ybuf.at[0], t, sem).start(priority=0)
        _row_copy(yb_hbm, pos2_ref[t], ybuf.at[1], t, sem).start(priority=1)
        return carry

    lax.fori_loop(0, tb, issue, 0, unroll=8)
    for k in range(TOP_K):
        pltpu.make_async_copy(yb_hbm.at[_tile_rows(0, tb), :], ybuf.at[k], sem).wait()
    gates = gcol_ref[...]
    y = gates[:, _R_G1:_R_G1 + 1] * _load_rows(ybuf.at[0], tb) + gates[:, _R_G2:_R_G2 + 1] * _load_rows(ybuf.at[1], tb)
    out_ref[...] = _layer_norm(alpha * _load_rows(x1_ref, tb) + y, g_ref[...], b_ref[...])


def _combine(x1_rows, gcol, yb, pos1, pos2, ln2g, ln2b, *, tb, alpha):
    n_tok = x1_rows.shape[0] // ROW_TILES
    kern = functools.partial(_combine_kernel, tb=tb, alpha=alpha)
    return pl.pallas_call(
        kern, grid=(n_tok // tb,),
        in_specs=[pl.BlockSpec((tb,), lambda i: (i,), memory_space=pltpu.SMEM),
                  pl.BlockSpec((tb,), lambda i: (i,), memory_space=pltpu.SMEM),
                  pl.BlockSpec((tb * ROW_TILES, LANES), lambda i: (i, 0)),
                  pl.BlockSpec((tb, LANES), lambda i: (i, 0)),
                  pl.BlockSpec(memory_space=pl.ANY),
                  pl.BlockSpec((1, D_MODEL), lambda i: (0, 0)),
                  pl.BlockSpec((1, D_MODEL), lambda i: (0, 0))],
        out_specs=pl.BlockSpec((tb, D_MODEL), lambda i: (i, 0)),
        out_shape=jax.ShapeDtypeStruct((n_tok, D_MODEL), F32),
        scratch_shapes=[pltpu.VMEM((TOP_K, tb * ROW_TILES, LANES), F32), pltpu.SemaphoreType.DMA(())],
        name="combine",
        compiler_params=pltpu.CompilerParams(dimension_semantics=("arbitrary",), vmem_limit_bytes=VMEM_LIMIT),
    )(pos1, pos2, x1_rows, gcol, yb, ln2g, ln2b)


def _attn_column_order():
    g, kvh, d = np.meshgrid(np.arange(GROUP), np.arange(N_KV_HEADS), np.arange(HEAD_DIM), indexing='ij')
    return ((kvh * GROUP + g) * HEAD_DIM + d).reshape(-1)


def _alibi_bias():
    slopes = np.asarray([2.0 ** (-8.0 * (h + 1) / N_HEADS) for h in range(N_HEADS)], np.float32)
    qi = np.arange(CHUNK, dtype=np.int32)[:, None]
    sj = np.arange(KEYS, dtype=np.int32)[None, :]
    dist = np.abs(qi + WINDOW - sj).astype(np.float32)
    bias = -slopes.reshape(N_KV_HEADS, GROUP, 1, 1) * dist
    bias = np.transpose(bias, (1, 2, 0, 3)).reshape(GROUP * CHUNK, N_KV_HEADS * KEYS)
    return jnp.asarray(bias, F32)


def _key_selector():
    row_head = np.arange(N_KV_HEADS * KEYS)[:, None] // KEYS
    lane_head = np.arange(D_KV)[None, :] // HEAD_DIM
    return jnp.asarray((row_head == lane_head).astype(np.float32), BF16)


def _strict_upper():
    r = np.arange(RANK_CHUNK)
    return jnp.asarray((r[:, None] < r[None, :]).astype(np.float32), BF16)


def _slot_plan(route, counts, n_tok):
    nblk = (n_tok * TOP_K) // EXPERT_BLOCK + N_EXPERTS
    experts = route[_R_E1:_R_E2 + 1].astype(jnp.int32)
    ranks = route[_R_RANK1:_R_RANK2 + 1].astype(jnp.int32)
    cnt = counts[:, 0].astype(jnp.int32)
    padded = (cnt + EXPERT_BLOCK - 1) // EXPERT_BLOCK * EXPERT_BLOCK
    pad_end = jnp.cumsum(padded)
    pad_start = pad_end - padded
    ids = jnp.arange(N_EXPERTS, dtype=jnp.int32)
    pos = jnp.sum(jnp.where(experts[..., None] == ids, pad_start, 0), axis=-1) + ranks
    blk_first = jnp.arange(nblk, dtype=jnp.int32) * EXPERT_BLOCK
    blk_e = jnp.minimum(jnp.sum((pad_end[None, :] <= blk_first[:, None]).astype(jnp.int32), axis=-1), N_EXPERTS - 1)
    blk_used = (blk_first < pad_end[-1]).astype(jnp.int32)
    blk_src = jnp.minimum(jnp.arange(nblk, dtype=jnp.int32), pad_end[-1] // EXPERT_BLOCK - 1)
    tail = jnp.stack([pad_end[-1], nblk - pad_end[-1] // EXPERT_BLOCK]).astype(jnp.int32)
    return pos[0], pos[1], pad_start + cnt, padded - cnt, tail, blk_e, blk_src, blk_used, nblk * EXPERT_BLOCK


def _layer(x, cconv, ck, cv, mix_prm, consts, moe_prm, layer, *, nseq, tl, mask_history, alpha):
    nb_total, t_total = x.shape[0], x.shape[1]
    n_tok = nb_total * t_total
    tb = min(ROWS, n_tok)
    x1_rows, route, gcol, counts, sconv, sk, sv = _mixer(
        x, cconv, ck, cv, mix_prm, consts, nseq=nseq, tl=tl, mask_history=mask_history, alpha=alpha)
    pos1, pos2, zstart, zcount, tail, blk_e, blk_src, blk_used, n_slots = _slot_plan(route, counts, n_tok)
    xs = _dispatch(x1_rows, pos1, pos2, zstart, zcount, tail, n_slots, tb=tb)
    w_gate, w_up, w_down, ln2g, ln2b = moe_prm
    yb = _experts(xs, blk_e, blk_src, blk_used, w_gate, w_up, w_down, layer)
    out = _combine(x1_rows, gcol, yb, pos1, pos2, ln2g, ln2b, tb=tb, alpha=alpha)
    return out.reshape(nb_total, t_total, D_MODEL), sconv, sk, sv


def kernel(x_prompt, x_sample, cache_conv, cache_k, cache_v, w_in, conv_w, conv_b, attn_sinks, g_conv, g_attn,
           w_out, ln1_g, ln1_b, router_group_w, router_group_b, router_expert_w, router_expert_b,
           expert_w_gate, expert_w_up, expert_w_down, ln2_g, ln2_b):
    depth = w_in.shape[0]
    alpha = (2 * depth) ** 0.25
    batch, seq = x_prompt.shape[0], x_prompt.shape[1]
    dec_batch, dec_seq = x_sample.shape[0], x_sample.shape[1]
    assert dec_seq == CHUNK and seq % CHUNK == 0
    consts = (_alibi_bias(), _strict_upper(), _key_selector())
    attn_order = _attn_column_order()
    tl_prompt = min(ROWS, seq)
    nseq_sample = min(ROWS // dec_seq, dec_batch)
    assert seq % tl_prompt == 0 and dec_batch % nseq_sample == 0

    xp, xs = x_prompt, x_sample
    zeros_conv = jnp.zeros((batch, CONV_W - 1, D_CONV), F32)
    zeros_kv = jnp.zeros((batch, WINDOW, D_KV), F32)
    states = [[] for _ in range(6)]
    for l in range(depth):
        sink_col = jnp.broadcast_to(jnp.repeat(attn_sinks[l].astype(F32), CHUNK).reshape(N_KV_HEADS, QROWS, 1),
                                    (N_KV_HEADS, QROWS, LANES))
        wr = jnp.zeros((D_MODEL, LANES), F32)
        wr = wr.at[:, 0:N_EXPERTS].set(router_expert_w[l]).at[:, N_EXPERTS:N_EXPERTS + N_GROUPS].set(router_group_w[l])
        br = jnp.zeros((1, LANES), F32)
        br = br.at[0, 0:N_EXPERTS].set(router_expert_b[l]).at[0, N_EXPERTS:N_EXPERTS + N_GROUPS].set(router_group_b[l])
        w_in_l = w_in[l].at[:, _OFF_Q:_OFF_Q + D_ATTN].set(w_in[l][:, _OFF_Q + attn_order])
        w_out_l = w_out[l].at[D_CONV:D_CONV + D_ATTN, :].set(w_out[l][D_CONV + attn_order, :])
        mix_prm = (w_in_l.astype(BF16), conv_w[l], conv_b[l].reshape(1, D_CONV), sink_col,
                   g_conv[l].reshape(1, D_CONV), g_attn[l][attn_order].reshape(1, D_ATTN), w_out_l.astype(BF16),
                   ln1_g[l].reshape(1, D_MODEL), ln1_b[l].reshape(1, D_MODEL), wr.astype(BF16), br)
        moe_prm = (expert_w_gate, expert_w_up, expert_w_down,
                   ln2_g[l].reshape(1, D_MODEL), ln2_b[l].reshape(1, D_MODEL))
        xp, c_new, k_new, v_new = _layer(xp, zeros_conv, zeros_kv, zeros_kv, mix_prm, consts, moe_prm, l,
                                         nseq=1, tl=tl_prompt, mask_history=True, alpha=alpha)
        for lst, val in zip(states[0:3], (c_new, k_new, v_new)):
            lst.append(val)
        xs, c_new, k_new, v_new = _layer(xs, cache_conv[l], cache_k[l].reshape(dec_batch, WINDOW, D_KV),
                                         cache_v[l].reshape(dec_batch, WINDOW, D_KV), mix_prm, consts, moe_prm, l,
                                         nseq=nseq_sample, tl=dec_seq, mask_history=False, alpha=alpha)
        for lst, val in zip(states[3:6], (c_new, k_new, v_new)):
            lst.append(val)

    def kv(lst, nb):
        return jnp.stack(lst).reshape(depth, nb, WINDOW, N_KV_HEADS, HEAD_DIM)

    return (xp, xs, jnp.stack(states[0]), kv(states[1], batch), kv(states[2], batch),
            jnp.stack(states[3]), kv(states[4], dec_batch), kv(states[5], dec_batch))
```

```python
import functools

import numpy as np
import jax
import jax.numpy as jnp
from jax import lax
from jax.experimental import pallas as pl
from jax.experimental.pallas import tpu as pltpu

D_MODEL = 1024
D_CONV = 512
CONV_W = 3
N_HEADS = 8
N_KV_HEADS = 2
GROUP = N_HEADS // N_KV_HEADS
HEAD_DIM = 64
D_ATTN = N_HEADS * HEAD_DIM
D_KV = N_KV_HEADS * HEAD_DIM
WINDOW = 128
CHUNK = 64
KEYS = WINDOW + CHUNK
QROWS = GROUP * CHUNK
N_GROUPS = 4
EXP_PER_GROUP = 8
N_EXPERTS = N_GROUPS * EXP_PER_GROUP
TOP_K = 2
D_EXP = 512
LN_EPS = 1e-5
NEG_INF = -1e30

SUBLANES = 8
LANES = 128
ROW_TILES = D_MODEL // LANES
assert ROW_TILES == SUBLANES
ROWS = 512
EXPERT_BLOCK = 512
RANK_CHUNK = 256
SOFTMAX_ROWS = 64
VMEM_LIMIT = 56 * 1024 * 1024

F32 = jnp.float32
BF16 = jnp.bfloat16

_OFF_B, _OFF_C, _OFF_H = 0, D_CONV, 2 * D_CONV
_OFF_Q = 3 * D_CONV
_OFF_K = _OFF_Q + D_ATTN
_OFF_V = _OFF_K + D_KV
D_IN = _OFF_V + D_KV

_R_E1, _R_E2, _R_RANK1, _R_RANK2, _R_G1, _R_G2 = 0, 1, 2, 3, 4, 5


def _load_rows(ref, n):
    return jnp.concatenate([ref[pl.ds(j, n, stride=ROW_TILES), :] for j in range(ROW_TILES)], axis=1)


def _store_rows(ref, val, n):
    for j in range(ROW_TILES):
        ref[pl.ds(j, n, stride=ROW_TILES), :] = val[:, j * LANES:(j + 1) * LANES]


def _dot(a, b):
    return jnp.dot(a, b, preferred_element_type=F32)


def _rms_norm(x, g):
    return x * lax.rsqrt(jnp.mean(jnp.square(x), -1, keepdims=True) + LN_EPS) * g


def _layer_norm(x, g, b):
    mu = jnp.mean(x, -1, keepdims=True)
    xc = x - mu
    var = jnp.mean(jnp.square(xc), -1, keepdims=True)
    return xc * lax.rsqrt(var + LN_EPS) * g + b


def _mixer_kernel(x_ref, cconv_ref, ck_ref, cv_ref, w_in_ref, convw_ref, convb_ref, abias_ref,
                  sink_ref, gconv_ref, gattn_ref, w_out_ref, ln1g_ref, ln1b_ref, wr_ref, br_ref,
                  tri_ref, keysel_ref, counts_in_ref,
                  x1_ref, route_ref, gcol_ref, counts_ref, sconv_ref, sk_ref, sv_ref,
                  uext, kext, vext, cnt_s, s_ref, e_ref, sinkden_ref, m0_ref, m1_ref,
                  *, nseq, tl, alpha, mask_history):
    b = pl.program_id(0)
    s = pl.program_id(1)
    rows = nseq * tl
    upitch = tl + SUBLANES
    kpitch = WINDOW + tl

    @pl.when(jnp.logical_and(b == 0, s == 0))
    def _():
        cnt_s[...] = counts_in_ref[:, 0:1]

    @pl.when(s == 0)
    def _():
        for j in range(nseq):
            uext[j * upitch + SUBLANES - 2:j * upitch + SUBLANES, :] = cconv_ref[j]
            kext[j * kpitch:j * kpitch + WINDOW, :] = ck_ref[j]
            vext[j * kpitch:j * kpitch + WINDOW, :] = cv_ref[j]

    x = x_ref[...].reshape(rows, D_MODEL)
    xb = x.astype(BF16)

    u = _dot(xb, w_in_ref[:, _OFF_C:_OFF_C + D_CONV]) * _dot(xb, w_in_ref[:, _OFF_H:_OFF_H + D_CONV])
    w0, w1, w2 = convw_ref[0:1, :], convw_ref[1:2, :], convw_ref[2:3, :]
    ycs = []
    for j in range(nseq):
        base = j * upitch + SUBLANES
        uext[base:base + tl, :] = u[j * tl:(j + 1) * tl]
        yc = convb_ref[...] + uext[base - 2:base - 2 + tl, :] * w0
        yc = yc + uext[base - 1:base - 1 + tl, :] * w1
        yc = yc + uext[base:base + tl, :] * w2
        ycs.append(yc)
        tail = uext[base + tl - 2:base + tl, :]
        sconv_ref[j] = tail
        uext[base - 2:base, :] = tail
    yc = ycs[0] if nseq == 1 else jnp.concatenate(ycs, axis=0)
    y_conv = _dot(xb, w_in_ref[:, _OFF_B:_OFF_B + D_CONV]) * yc
    n_conv = _rms_norm(y_conv, gconv_ref[...]).astype(BF16)

    q = (_dot(xb, w_in_ref[:, _OFF_Q:_OFF_Q + D_ATTN]) * (HEAD_DIM ** -0.5)).astype(BF16)
    k = _dot(xb, w_in_ref[:, _OFF_K:_OFF_K + D_KV])
    v = _dot(xb, w_in_ref[:, _OFF_V:_OFF_V + D_KV])
    for j in range(nseq):
        kext[j * kpitch + WINDOW:(j + 1) * kpitch, :] = k[j * tl:(j + 1) * tl]
        vext[j * kpitch + WINDOW:(j + 1) * kpitch, :] = v[j * tl:(j + 1) * tl]

    nchunk = rows // CHUNK
    assert nseq == 1 or tl == CHUNK
    key_stride = CHUNK if nseq == 1 else kpitch
    head0 = lax.broadcasted_iota(jnp.int32, (1, D_KV), 1) < HEAD_DIM

    def windows(ext_ref):
        ext = ext_ref[...].astype(BF16)
        heads = (jnp.where(head0, ext, jnp.zeros_like(ext)), jnp.where(head0, jnp.zeros_like(ext), ext))
        return jnp.stack([jnp.concatenate([h[c * key_stride:c * key_stride + KEYS] for h in heads], axis=0)
                          for c in range(nchunk)])

    q3 = jnp.stack([jnp.concatenate([q[c * CHUNK:(c + 1) * CHUNK, g * D_KV:(g + 1) * D_KV] for g in range(GROUP)], axis=0)
                    for c in range(nchunk)])
    logits = lax.dot_general(q3, windows(kext), (((2,), (2,)), ((0,), (0,))), preferred_element_type=F32)
    s_ref[...] = logits.reshape(nchunk * QROWS, 2 * KEYS)

    assert 2 * KEYS == 3 * LANES
    mid0 = lax.broadcasted_iota(jnp.int32, (1, LANES), 1) < KEYS - LANES
    colk = lax.broadcasted_iota(jnp.int32, (1, 2 * KEYS), 1)
    key = jnp.where(colk < KEYS, colk, colk - KEYS)

    def row_block(i):
        r = pl.multiple_of(i * SOFTMAX_ROWS, SOFTMAX_ROWS)
        return pl.ds(r, SOFTMAX_ROWS), pl.ds(pl.multiple_of(r % QROWS, SOFTMAX_ROWS), SOFTMAX_ROWS), r // QROWS

    def tiles(lg):
        return lg[:, 0:LANES], lg[:, LANES:2 * LANES], lg[:, 2 * LANES:3 * LANES]

    def max_rows(i, masked):
        rows_i, qrows_i, chunk_i = row_block(i)
        lg = s_ref[rows_i, :] + abias_ref[qrows_i, :]
        if masked:
            lg = lg + jnp.where(s * tl + chunk_i * CHUNK - WINDOW + key < 0, NEG_INF, 0.0).astype(F32)
        s_ref[rows_i, :] = lg
        t0, t1, t2 = tiles(lg)
        m0_ref[rows_i, :] = jnp.maximum(jnp.max(jnp.maximum(t0, jnp.where(mid0, t1, NEG_INF)), -1, keepdims=True),
                                        sink_ref[0, qrows_i, :])
        m1_ref[rows_i, :] = jnp.maximum(jnp.max(jnp.maximum(t2, jnp.where(mid0, NEG_INF, t1)), -1, keepdims=True),
                                        sink_ref[1, qrows_i, :])

    def exp_rows(i, masked):
        rows_i, qrows_i, _ = row_block(i)
        t0, t1, t2 = tiles(s_ref[rows_i, :])
        m0, m1 = m0_ref[rows_i, :], m1_ref[rows_i, :]
        e = jnp.concatenate([jnp.exp(t0 - m0), jnp.exp(t1 - jnp.where(mid0, m0, m1)), jnp.exp(t2 - m1)], axis=-1)
        e_ref[rows_i, :] = e.astype(BF16)
        sinkden_ref[rows_i, :] = jnp.where(head0, jnp.exp(sink_ref[0, qrows_i, :] - m0),
                                           jnp.exp(sink_ref[1, qrows_i, :] - m1))

    def loop_rows(fn, lo_it, hi_it, masked, unroll):
        def body(i, carry):
            fn(i, masked)
            return carry
        lax.fori_loop(lo_it, hi_it, body, 0, unroll=unroll)

    n_it = nchunk * QROWS // SOFTMAX_ROWS
    n_masked = min(WINDOW // CHUNK, nchunk) * QROWS // SOFTMAX_ROWS if mask_history else 0
    if n_masked:
        loop_rows(max_rows, 0, n_masked, True, 8)
    loop_rows(max_rows, n_masked, n_it, False, 8)
    loop_rows(exp_rows, 0, n_it, False, 2)

    vsel = jnp.concatenate([windows(vext), jnp.broadcast_to(keysel_ref[...], (nchunk, 2 * KEYS, D_KV))], axis=-1)
    o2 = lax.dot_general(e_ref[...].reshape(nchunk, QROWS, 2 * KEYS), vsel, (((2,), (1,)), ((0,), (0,))),
                         preferred_element_type=F32)
    o = o2[..., 0:D_KV] * (1.0 / (o2[..., D_KV:2 * D_KV] + sinkden_ref[...].reshape(nchunk, QROWS, D_KV)))
    y_attn = jnp.concatenate(
        [jnp.concatenate([o[c, g * CHUNK:(g + 1) * CHUNK, :] for g in range(GROUP)], axis=1) for c in range(nchunk)],
        axis=0)

    for j in range(nseq):
        sk_ref[j] = kext[j * kpitch + tl:(j + 1) * kpitch, :]
        sv_ref[j] = vext[j * kpitch + tl:(j + 1) * kpitch, :]
    if nseq == 1:
        kext[0:WINDOW, :] = kext[tl:tl + WINDOW, :]
        vext[0:WINDOW, :] = vext[tl:tl + WINDOW, :]

    n_attn = _rms_norm(y_attn, gattn_ref[...]).astype(BF16)
    mixed = _dot(n_conv, w_out_ref[0:D_CONV, :]) + _dot(n_attn, w_out_ref[D_CONV:D_CONV + D_ATTN, :])
    x1 = _layer_norm(alpha * x + mixed, ln1g_ref[...], ln1b_ref[...])
    _store_rows(x1_ref, x1, rows)

    logits_t = (_dot(x1.astype(BF16), wr_ref[...]) + br_ref[...]).T
    sub = lax.broadcasted_iota(jnp.int32, (SUBLANES, rows), 0)
    gl = jnp.where(sub < N_GROUPS, logits_t[N_EXPERTS:N_EXPERTS + SUBLANES, :], -jnp.inf)
    gmax = jnp.max(gl, axis=0, keepdims=True)
    grp = jnp.min(jnp.where(gl == gmax, sub, SUBLANES), axis=0, keepdims=True)
    p_grp = 1.0 / jnp.sum(jnp.exp(gl - gmax), axis=0, keepdims=True)
    el = logits_t[(N_GROUPS - 1) * EXP_PER_GROUP:N_GROUPS * EXP_PER_GROUP, :]
    for g in range(N_GROUPS - 2, -1, -1):
        el = jnp.where(grp == g, logits_t[g * EXP_PER_GROUP:(g + 1) * EXP_PER_GROUP, :], el)
    v1 = jnp.max(el, axis=0, keepdims=True)
    i1 = jnp.min(jnp.where(el == v1, sub, SUBLANES), axis=0, keepdims=True)
    el2 = jnp.where(sub == i1, -jnp.inf, el)
    v2 = jnp.max(el2, axis=0, keepdims=True)
    i2 = jnp.min(jnp.where(el2 == v2, sub, SUBLANES), axis=0, keepdims=True)
    e2 = jnp.exp(v2 - v1)
    gate1 = p_grp * (1.0 / (1.0 + e2))
    gate2 = p_grp * (e2 / (1.0 + e2))

    chosen = jnp.logical_or(sub == i1, sub == i2)
    onehot = jnp.concatenate(
        [jnp.where(jnp.logical_and(grp == g, chosen), 1.0, 0.0) for g in range(N_GROUPS)], axis=0)
    running = cnt_s[...]
    ranks = []
    for c in range(rows // RANK_CHUNK):
        oh = onehot[:, c * RANK_CHUNK:(c + 1) * RANK_CHUNK]
        ranks.append(_dot(oh.astype(BF16), tri_ref[...]) + running)
        running = running + jnp.sum(oh, axis=1, keepdims=True)
    rank = jnp.concatenate(ranks, axis=1)
    cnt_s[...] = running
    counts_ref[...] = jnp.broadcast_to(running, (N_EXPERTS, LANES))

    ex1 = grp * EXP_PER_GROUP + i1
    ex2 = grp * EXP_PER_GROUP + i2
    erow = lax.broadcasted_iota(jnp.int32, (N_EXPERTS, rows), 0)
    rank1 = jnp.sum(jnp.where(erow == ex1, rank, 0.0), axis=0, keepdims=True)
    rank2 = jnp.sum(jnp.where(erow == ex2, rank, 0.0), axis=0, keepdims=True)
    fields = {_R_E1: ex1.astype(F32), _R_E2: ex2.astype(F32), _R_RANK1: rank1, _R_RANK2: rank2,
              _R_G1: gate1, _R_G2: gate2}
    record = jnp.zeros((SUBLANES, rows), F32)
    for r, val in fields.items():
        record = jnp.where(sub == r, val, record)
    route_ref[...] = record
    padded = jnp.concatenate([record, jnp.zeros((LANES - SUBLANES, rows), F32)], axis=0)
    gcol_ref[...] = padded.T


def _mixer(x, cconv, ck, cv, counts_in, prm, consts, *, nseq, tl, mask_history, alpha):
    nb_total, t_total = x.shape[0], x.shape[1]
    nb, ns = nb_total // nseq, t_total // tl
    rows = nseq * tl
    n_tok = nb_total * t_total
    w_in, convw, convb, sinks, gconv, gattn, w_out, ln1g, ln1b, wr, br = prm
    abias, tri, keysel = consts

    def full(a):
        return pl.BlockSpec(a.shape, lambda b, s, _n=a.ndim: (0,) * _n)

    def seq_state(width, nrows):
        return pl.BlockSpec((nseq, nrows, width), lambda b, s: (b, 0, 0))

    in_specs = [
        pl.BlockSpec((nseq, tl, D_MODEL), lambda b, s: (b, s, 0)),
        seq_state(D_CONV, CONV_W - 1), seq_state(D_KV, WINDOW), seq_state(D_KV, WINDOW),
        full(w_in), full(convw), full(convb), full(abias), full(sinks), full(gconv), full(gattn),
        full(w_out), full(ln1g), full(ln1b), full(wr), full(br), full(tri), full(keysel), full(counts_in),
    ]
    out_shape = [
        jax.ShapeDtypeStruct((n_tok * ROW_TILES, LANES), F32),
        jax.ShapeDtypeStruct((SUBLANES, n_tok), F32),
        jax.ShapeDtypeStruct((n_tok, LANES), F32),
        jax.ShapeDtypeStruct((N_EXPERTS, LANES), F32),
        jax.ShapeDtypeStruct((nb_total, CONV_W - 1, D_CONV), F32),
        jax.ShapeDtypeStruct((nb_total, WINDOW, D_KV), F32),
        jax.ShapeDtypeStruct((nb_total, WINDOW, D_KV), F32),
    ]
    out_specs = [
        pl.BlockSpec((rows * ROW_TILES, LANES), lambda b, s: (b * ns + s, 0)),
        pl.BlockSpec((SUBLANES, rows), lambda b, s: (0, b * ns + s)),
        pl.BlockSpec((rows, LANES), lambda b, s: (b * ns + s, 0)),
        pl.BlockSpec((N_EXPERTS, LANES), lambda b, s: (0, 0)),
        seq_state(D_CONV, CONV_W - 1), seq_state(D_KV, WINDOW), seq_state(D_KV, WINDOW),
    ]
    scratch = [
        pltpu.VMEM((nseq * (tl + SUBLANES), D_CONV), F32),
        pltpu.VMEM((nseq * (WINDOW + tl), D_KV), F32),
        pltpu.VMEM((nseq * (WINDOW + tl), D_KV), F32),
        pltpu.VMEM((N_EXPERTS, 1), F32),
        pltpu.VMEM((rows // CHUNK * QROWS, 2 * KEYS), F32),
        pltpu.VMEM((rows // CHUNK * QROWS, 2 * KEYS), BF16),
        pltpu.VMEM((rows // CHUNK * QROWS, D_KV), F32),
        pltpu.VMEM((rows // CHUNK * QROWS, LANES), F32),
        pltpu.VMEM((rows // CHUNK * QROWS, LANES), F32),
    ]
    kern = functools.partial(_mixer_kernel, nseq=nseq, tl=tl, alpha=alpha, mask_history=mask_history)
    return pl.pallas_call(
        kern, grid=(nb, ns), in_specs=in_specs, out_specs=out_specs, out_shape=out_shape,
        scratch_shapes=scratch, name="mixer",
        compiler_params=pltpu.CompilerParams(dimension_semantics=("arbitrary", "arbitrary"),
                                             vmem_limit_bytes=VMEM_LIMIT),
    )(x, cconv, ck, cv, w_in, convw, convb, abias, sinks, gconv, gattn, w_out, ln1g, ln1b, wr, br, tri, keysel, counts_in)


_PAD_PIECES = tuple(2 ** k for k in range(EXPERT_BLOCK.bit_length() - 2, -1, -1))


def _tile_rows(row, n=1):
    return pl.ds(pl.multiple_of(row * ROW_TILES, ROW_TILES), n * ROW_TILES)


def _row_copy(src, src_row, dst, dst_row, sem):
    return pltpu.make_async_copy(src.at[_tile_rows(src_row), :], dst.at[_tile_rows(dst_row), :], sem)


def _dispatch_kernel(zstart_ref, zcount_ref, tail_ref, pos1_ref, pos2_ref, xa_ref, xb_ref, xs_hbm, zeros_v, sem, zsem,
                     *, tb, nba):
    step = pl.program_id(0)

    def zero_copy(start, piece):
        return pltpu.make_async_copy(zeros_v.at[_tile_rows(0, piece), :], xs_hbm.at[_tile_rows(start, piece), :], zsem)

    def pad_copy(e, piece, taken):
        return zero_copy(zstart_ref[e] + taken, piece)

    def for_each_pad_piece(fn):
        def per_tail_block(j, carry):
            for part in range(EXPERT_BLOCK // _PAD_PIECES[0]):
                fn(zero_copy(tail_ref[0] + j * EXPERT_BLOCK + part * _PAD_PIECES[0], _PAD_PIECES[0]))
            return carry

        lax.fori_loop(0, tail_ref[1], per_tail_block, 0)

        def per_expert(e, carry):
            count = zcount_ref[e]
            taken = 0
            for piece in _PAD_PIECES:
                present = (count & piece) != 0

                @pl.when(present)
                def _(piece=piece, taken=taken):
                    fn(pad_copy(e, piece, taken))

                taken = taken + jnp.where(present, piece, 0)
            return carry

        lax.fori_loop(0, N_EXPERTS, per_expert, 0)

    @pl.when(step == 0)
    def _():
        zeros_v[...] = jnp.zeros_like(zeros_v)
        for_each_pad_piece(lambda c: c.start())

    def issue_rows(x_ref):
        def issue(t, carry):
            _row_copy(x_ref, t, xs_hbm, pos1_ref[t], sem).start(priority=0)
            _row_copy(x_ref, t, xs_hbm, pos2_ref[t], sem).start(priority=1)
            return carry

        lax.fori_loop(0, tb, issue, 0, unroll=8)

    @pl.when(step < nba)
    def _():
        issue_rows(xa_ref)

    @pl.when(step >= nba)
    def _():
        issue_rows(xb_ref)

    pltpu.make_async_copy(xs_hbm.at[_tile_rows(0, 2 * tb), :], xs_hbm.at[_tile_rows(0, 2 * tb), :], sem).wait()

    @pl.when(step == 0)
    def _():
        for_each_pad_piece(lambda c: c.wait())


def _dispatch(xa_rows, xb_rows, pos1, pos2, zstart, zcount, tail, n_slots, *, tb):
    nba, nbb = xa_rows.shape[0] // ROW_TILES // tb, xb_rows.shape[0] // ROW_TILES // tb
    kern = functools.partial(_dispatch_kernel, tb=tb, nba=nba)
    grid_spec = pltpu.PrefetchScalarGridSpec(
        num_scalar_prefetch=3, grid=(nba + nbb,),
        in_specs=[pl.BlockSpec((tb,), lambda i, *_: (i,), memory_space=pltpu.SMEM),
                  pl.BlockSpec((tb,), lambda i, *_: (i,), memory_space=pltpu.SMEM),
                  pl.BlockSpec((tb * ROW_TILES, LANES), lambda i, *_: (jnp.minimum(i, nba - 1), 0)),
                  pl.BlockSpec((tb * ROW_TILES, LANES), lambda i, *_: (jnp.maximum(i - nba, 0), 0))],
        out_specs=pl.BlockSpec(memory_space=pl.ANY),
        scratch_shapes=[pltpu.VMEM((_PAD_PIECES[0] * ROW_TILES, LANES), F32),
                        pltpu.SemaphoreType.DMA(()), pltpu.SemaphoreType.DMA(())],
    )
    return pl.pallas_call(
        kern, grid_spec=grid_spec, out_shape=jax.ShapeDtypeStruct((n_slots * ROW_TILES, LANES), F32),
        name="dispatch",
        compiler_params=pltpu.CompilerParams(dimension_semantics=("arbitrary",)),
    )(zstart, zcount, tail, pos1, pos2, xa_rows, xb_rows)


def _expert_kernel(be_ref, src_ref, used_ref, xs_ref, wg_ref, wu_ref, wd_ref, yb_ref, wg_b, wu_b, wd_b):
    i = pl.program_id(0)
    e = be_ref[i]
    prev = be_ref[jnp.maximum(i - 1, 0)]

    @pl.when(jnp.logical_or(i == 0, e != prev))
    def _():
        wg_b[...] = wg_ref[...].astype(BF16)
        wu_b[...] = wu_ref[...].astype(BF16)
        wd_b[...] = wd_ref[...].astype(BF16)

    @pl.when(used_ref[i] != 0)
    def _():
        x = _load_rows(xs_ref, EXPERT_BLOCK).astype(BF16)
        g = _dot(x, wg_b[...])
        u = _dot(x, wu_b[...])
        hid = (g / (1.0 + jnp.exp(-g))) * u
        _store_rows(yb_ref, _dot(hid.astype(BF16), wd_b[...]), EXPERT_BLOCK)

    @pl.when(used_ref[i] == 0)
    def _():
        yb_ref[...] = jnp.zeros_like(yb_ref)


def _experts(xs, blk_e, blk_src, blk_used, w_gate, w_up, w_down, layer):
    n_slots = xs.shape[0] // ROW_TILES
    nblk = n_slots // EXPERT_BLOCK

    def wspec(shape):
        return pl.BlockSpec((None, None) + shape, lambda i, be, src, used: (layer, be[i], 0, 0))

    grid_spec = pltpu.PrefetchScalarGridSpec(
        num_scalar_prefetch=3, grid=(nblk,),
        in_specs=[pl.BlockSpec((EXPERT_BLOCK * ROW_TILES, LANES), lambda i, be, src, used: (src[i], 0)),
                  wspec((D_MODEL, D_EXP)), wspec((D_MODEL, D_EXP)), wspec((D_EXP, D_MODEL))],
        out_specs=pl.BlockSpec((EXPERT_BLOCK * ROW_TILES, LANES), lambda i, be, src, used: (i, 0)),
        scratch_shapes=[pltpu.VMEM((D_MODEL, D_EXP), BF16), pltpu.VMEM((D_MODEL, D_EXP), BF16),
                        pltpu.VMEM((D_EXP, D_MODEL), BF16)],
    )
    return pl.pallas_call(
        _expert_kernel, grid_spec=grid_spec, out_shape=jax.ShapeDtypeStruct((n_slots * ROW_TILES, LANES), F32),
        name="experts",
        compiler_params=pltpu.CompilerParams(dimension_semantics=("arbitrary",), vmem_limit_bytes=VMEM_LIMIT),
    )(blk_e, blk_src, blk_used, xs, w_gate, w_up, w_down)


def _combine_kernel(pos1_ref, pos2_ref, next1_ref, next2_ref, xa_ref, xb_ref, ga_ref, gb_ref, yb_hbm, g_ref, b_ref,
                    outa_ref, outb_ref, ybuf, sems, *, tb, alpha, nba, nsteps):
    step = pl.program_id(0)
    slot = step % 2

    def gather(p1_ref, p2_ref, to_slot):
        def issue(t, carry):
            _row_copy(yb_hbm, p1_ref[t], ybuf.at[to_slot].at[0], t, sems.at[to_slot]).start(priority=0)
            _row_copy(yb_hbm, p2_ref[t], ybuf.at[to_slot].at[1], t, sems.at[to_slot]).start(priority=1)
            return carry

        lax.fori_loop(0, tb, issue, 0, unroll=8)

    @pl.when(step == 0)
    def _():
        gather(pos1_ref, pos2_ref, 0)

    @pl.when(step + 1 < nsteps)
    def _():
        gather(next1_ref, next2_ref, 1 - slot)

    for k in range(TOP_K):
        pltpu.make_async_copy(yb_hbm.at[_tile_rows(0, tb), :], ybuf.at[slot].at[k], sems.at[slot]).wait()

    def finish(x1_ref, gcol_ref, out_ref):
        gates = gcol_ref[...]
        y = (gates[:, _R_G1:_R_G1 + 1] * _load_rows(ybuf.at[slot].at[0], tb)
             + gates[:, _R_G2:_R_G2 + 1] * _load_rows(ybuf.at[slot].at[1], tb))
        out_ref[...] = _layer_norm(alpha * _load_rows(x1_ref, tb) + y, g_ref[...], b_ref[...])

    @pl.when(step < nba)
    def _():
        finish(xa_ref, ga_ref, outa_ref)

    @pl.when(step >= nba)
    def _():
        finish(xb_ref, gb_ref, outb_ref)


def _combine(xa_rows, xb_rows, gcol_a, gcol_b, yb, pos1, pos2, ln2g, ln2b, *, tb, alpha):
    nba, nbb = xa_rows.shape[0] // ROW_TILES // tb, xb_rows.shape[0] // ROW_TILES // tb
    nsteps = nba + nbb
    kern = functools.partial(_combine_kernel, tb=tb, alpha=alpha, nba=nba, nsteps=nsteps)

    def first(i):
        return jnp.minimum(i, nba - 1)

    def second(i):
        return jnp.maximum(i - nba, 0)

    def nxt(i):
        return jnp.minimum(i + 1, nsteps - 1)

    smem = functools.partial(pl.BlockSpec, (tb,), memory_space=pltpu.SMEM)
    return pl.pallas_call(
        kern, grid=(nsteps,),
        in_specs=[smem(lambda i: (i,)), smem(lambda i: (i,)), smem(lambda i: (nxt(i),)), smem(lambda i: (nxt(i),)),
                  pl.BlockSpec((tb * ROW_TILES, LANES), lambda i: (first(i), 0)),
                  pl.BlockSpec((tb * ROW_TILES, LANES), lambda i: (second(i), 0)),
                  pl.BlockSpec((tb, LANES), lambda i: (first(i), 0)),
                  pl.BlockSpec((tb, LANES), lambda i: (second(i), 0)),
                  pl.BlockSpec(memory_space=pl.ANY),
                  pl.BlockSpec((1, D_MODEL), lambda i: (0, 0)),
                  pl.BlockSpec((1, D_MODEL), lambda i: (0, 0))],
        out_specs=[pl.BlockSpec((tb, D_MODEL), lambda i: (first(i), 0)),
                   pl.BlockSpec((tb, D_MODEL), lambda i: (second(i), 0))],
        out_shape=[jax.ShapeDtypeStruct((nba * tb, D_MODEL), F32), jax.ShapeDtypeStruct((nbb * tb, D_MODEL), F32)],
        scratch_shapes=[pltpu.VMEM((2, TOP_K, tb * ROW_TILES, LANES), F32), pltpu.SemaphoreType.DMA((2,))],
        name="combine",
        compiler_params=pltpu.CompilerParams(dimension_semantics=("arbitrary",), vmem_limit_bytes=VMEM_LIMIT),
    )(pos1, pos2, pos1, pos2, xa_rows, xb_rows, gcol_a, gcol_b, yb, ln2g, ln2b)


def _attn_column_order():
    g, kvh, d = np.meshgrid(np.arange(GROUP), np.arange(N_KV_HEADS), np.arange(HEAD_DIM), indexing='ij')
    return ((kvh * GROUP + g) * HEAD_DIM + d).reshape(-1)


def _alibi_bias():
    slopes = np.asarray([2.0 ** (-8.0 * (h + 1) / N_HEADS) for h in range(N_HEADS)], np.float32)
    qi = np.arange(CHUNK, dtype=np.int32)[:, None]
    sj = np.arange(KEYS, dtype=np.int32)[None, :]
    dist = np.abs(qi + WINDOW - sj).astype(np.float32)
    bias = -slopes.reshape(N_KV_HEADS, GROUP, 1, 1) * dist
    bias = np.transpose(bias, (1, 2, 0, 3)).reshape(GROUP * CHUNK, N_KV_HEADS * KEYS)
    return jnp.asarray(bias, F32)


def _key_selector():
    row_head = np.arange(N_KV_HEADS * KEYS)[:, None] // KEYS
    lane_head = np.arange(D_KV)[None, :] // HEAD_DIM
    return jnp.asarray((row_head == lane_head).astype(np.float32), BF16)


def _strict_upper():
    r = np.arange(RANK_CHUNK)
    return jnp.asarray((r[:, None] < r[None, :]).astype(np.float32), BF16)


def _slot_plan(route, counts, n_tok):
    nblk = (n_tok * TOP_K) // EXPERT_BLOCK + N_EXPERTS
    experts = route[_R_E1:_R_E2 + 1].astype(jnp.int32)
    ranks = route[_R_RANK1:_R_RANK2 + 1].astype(jnp.int32)
    cnt = counts[:, 0].astype(jnp.int32)
    padded = (cnt + EXPERT_BLOCK - 1) // EXPERT_BLOCK * EXPERT_BLOCK
    pad_end = jnp.cumsum(padded)
    pad_start = pad_end - padded
    ids = jnp.arange(N_EXPERTS, dtype=jnp.int32)
    pos = jnp.sum(jnp.where(experts[..., None] == ids, pad_start, 0), axis=-1) + ranks
    blk_first = jnp.arange(nblk, dtype=jnp.int32) * EXPERT_BLOCK
    blk_e = jnp.minimum(jnp.sum((pad_end[None, :] <= blk_first[:, None]).astype(jnp.int32), axis=-1), N_EXPERTS - 1)
    blk_used = (blk_first < pad_end[-1]).astype(jnp.int32)
    blk_src = jnp.minimum(jnp.arange(nblk, dtype=jnp.int32), pad_end[-1] // EXPERT_BLOCK - 1)
    tail = jnp.stack([pad_end[-1], nblk - pad_end[-1] // EXPERT_BLOCK]).astype(jnp.int32)
    return pos[0], pos[1], pad_start + cnt, padded - cnt, tail, blk_e, blk_src, blk_used, nblk * EXPERT_BLOCK


def _layer(xp, xs, cache, mix_prm, consts, moe_prm, layer, *, tl_prompt, nseq_sample, alpha):
    (bp, tp, _), (bs, ts, _) = xp.shape, xs.shape
    np_tok, ns_tok = bp * tp, bs * ts
    tb = ROWS
    assert np_tok % tb == 0 and ns_tok % tb == 0
    zeros_conv = jnp.zeros((bp, CONV_W - 1, D_CONV), F32)
    zeros_kv = jnp.zeros((bp, WINDOW, D_KV), F32)
    zero_counts = jnp.zeros((N_EXPERTS, LANES), F32)
    x1p, route_p, gcol_p, counts_p, *state_p = _mixer(
        xp, zeros_conv, zeros_kv, zeros_kv, zero_counts, mix_prm, consts,
        nseq=1, tl=tl_prompt, mask_history=True, alpha=alpha)
    x1s, route_s, gcol_s, counts, *state_s = _mixer(
        xs, *cache, counts_p, mix_prm, consts, nseq=nseq_sample, tl=ts, mask_history=False, alpha=alpha)
    route = jnp.concatenate([route_p, route_s], axis=1)
    pos1, pos2, zstart, zcount, tail, blk_e, blk_src, blk_used, n_slots = _slot_plan(route, counts, np_tok + ns_tok)
    sorted_rows = _dispatch(x1p, x1s, pos1, pos2, zstart, zcount, tail, n_slots, tb=tb)
    w_gate, w_up, w_down, ln2g, ln2b = moe_prm
    yb = _experts(sorted_rows, blk_e, blk_src, blk_used, w_gate, w_up, w_down, layer)
    out_p, out_s = _combine(x1p, x1s, gcol_p, gcol_s, yb, pos1, pos2, ln2g, ln2b, tb=tb, alpha=alpha)
    return out_p.reshape(bp, tp, D_MODEL), out_s.reshape(bs, ts, D_MODEL), state_p, state_s


def kernel(x_prompt, x_sample, cache_conv, cache_k, cache_v, w_in, conv_w, conv_b, attn_sinks, g_conv, g_attn,
           w_out, ln1_g, ln1_b, router_group_w, router_group_b, router_expert_w, router_expert_b,
           expert_w_gate, expert_w_up, expert_w_down, ln2_g, ln2_b):
    depth = w_in.shape[0]
    alpha = (2 * depth) ** 0.25
    batch, seq = x_prompt.shape[0], x_prompt.shape[1]
    dec_batch, dec_seq = x_sample.shape[0], x_sample.shape[1]
    assert dec_seq == CHUNK and seq % CHUNK == 0
    consts = (_alibi_bias(), _strict_upper(), _key_selector())
    attn_order = _attn_column_order()
    tl_prompt = min(ROWS, seq)
    nseq_sample = min(ROWS // dec_seq, dec_batch)
    assert seq % tl_prompt == 0 and dec_batch % nseq_sample == 0

    xp, xs = x_prompt, x_sample
    states = [[] for _ in range(6)]
    for l in range(depth):
        sink_col = jnp.broadcast_to(jnp.repeat(attn_sinks[l].astype(F32), CHUNK).reshape(N_KV_HEADS, QROWS, 1),
                                    (N_KV_HEADS, QROWS, LANES))
        wr = jnp.zeros((D_MODEL, LANES), F32)
        wr = wr.at[:, 0:N_EXPERTS].set(router_expert_w[l]).at[:, N_EXPERTS:N_EXPERTS + N_GROUPS].set(router_group_w[l])
        br = jnp.zeros((1, LANES), F32)
        br = br.at[0, 0:N_EXPERTS].set(router_expert_b[l]).at[0, N_EXPERTS:N_EXPERTS + N_GROUPS].set(router_group_b[l])
        w_in_l = w_in[l].at[:, _OFF_Q:_OFF_Q + D_ATTN].set(w_in[l][:, _OFF_Q + attn_order])
        w_out_l = w_out[l].at[D_CONV:D_CONV + D_ATTN, :].set(w_out[l][D_CONV + attn_order, :])
        mix_prm = (w_in_l.astype(BF16), conv_w[l], conv_b[l].reshape(1, D_CONV), sink_col,
                   g_conv[l].reshape(1, D_CONV), g_attn[l][attn_order].reshape(1, D_ATTN), w_out_l.astype(BF16),
                   ln1_g[l].reshape(1, D_MODEL), ln1_b[l].reshape(1, D_MODEL), wr.astype(BF16), br)
        moe_prm = (expert_w_gate, expert_w_up, expert_w_down,
                   ln2_g[l].reshape(1, D_MODEL), ln2_b[l].reshape(1, D_MODEL))
        cache = (cache_conv[l], cache_k[l].reshape(dec_batch, WINDOW, D_KV), cache_v[l].reshape(dec_batch, WINDOW, D_KV))
        xp, xs, state_p, state_s = _layer(xp, xs, cache, mix_prm, consts, moe_prm, l,
                                          tl_prompt=tl_prompt, nseq_sample=nseq_sample, alpha=alpha)
        for lst, val in zip(states, state_p + state_s):
            lst.append(val)

    def kv(lst, nb):
        return jnp.stack(lst).reshape(depth, nb, WINDOW, N_KV_HEADS, HEAD_DIM)

    return (xp, xs, jnp.stack(states[0]), kv(states[1], batch), kv(states[2], batch),
            jnp.stack(states[3]), kv(states[4], dec_batch), kv(states[5], dec_batch))
```

```python
import functools

import numpy as np
import jax
import jax.numpy as jnp
from jax import lax
from jax.experimental import pallas as pl
from jax.experimental.pallas import tpu as pltpu

D_MODEL = 1024
D_CONV = 512
CONV_W = 3
N_HEADS = 8
N_KV_HEADS = 2
GROUP = N_HEADS // N_KV_HEADS
HEAD_DIM = 64
D_ATTN = N_HEADS * HEAD_DIM
D_KV = N_KV_HEADS * HEAD_DIM
WINDOW = 128
CHUNK = 64
KEYS = WINDOW + CHUNK
QROWS = GROUP * CHUNK
N_GROUPS = 4
EXP_PER_GROUP = 8
N_EXPERTS = N_GROUPS * EXP_PER_GROUP
TOP_K = 2
D_EXP = 512
LN_EPS = 1e-5
NEG_INF = -1e30

SUBLANES = 8
LANES = 128
ROW_TILES = D_MODEL // LANES
assert ROW_TILES == SUBLANES
ROWS = 512
MIXER_ROWS = 512
EXPERT_BLOCK = 512
RANK_CHUNK = 256
SOFTMAX_ROWS = 64
VMEM_LIMIT = 56 * 1024 * 1024

F32 = jnp.float32
BF16 = jnp.bfloat16

_OFF_B, _OFF_C, _OFF_H = 0, D_CONV, 2 * D_CONV
_OFF_Q = 3 * D_CONV
_OFF_K = _OFF_Q + D_ATTN
_OFF_V = _OFF_K + D_KV
D_IN = _OFF_V + D_KV

_R_E1, _R_E2, _R_RANK1, _R_RANK2, _R_G1, _R_G2 = 0, 1, 2, 3, 4, 5


def _load_rows(ref, n):
    return jnp.concatenate([ref[pl.ds(j, n, stride=ROW_TILES), :] for j in range(ROW_TILES)], axis=1)


def _store_rows(ref, val, n):
    for j in range(ROW_TILES):
        ref[pl.ds(j, n, stride=ROW_TILES), :] = val[:, j * LANES:(j + 1) * LANES]


def _dot(a, b):
    return jnp.dot(a, b, preferred_element_type=F32)


def _rms_norm(x, g):
    return x * lax.rsqrt(jnp.mean(jnp.square(x), -1, keepdims=True) + LN_EPS) * g


def _layer_norm(x, g, b):
    mu = jnp.mean(x, -1, keepdims=True)
    xc = x - mu
    var = jnp.mean(jnp.square(xc), -1, keepdims=True)
    return xc * lax.rsqrt(var + LN_EPS) * g + b


def _mixer_kernel(x_ref, cconv_ref, ck_ref, cv_ref, w_in_ref, convw_ref, convb_ref, abias_ref,
                  sink_ref, gconv_ref, gattn_ref, w_out_ref, ln1g_ref, ln1b_ref, wr_ref, br_ref,
                  tri_ref, keysel_ref, counts_in_ref,
                  x1_ref, route_ref, gcol_ref, counts_ref, sconv_ref, sk_ref, sv_ref,
                  uext, kext, vext, cnt_s, s_ref, e_ref, sinkden_ref, m0_ref, m1_ref,
                  *, nseq, tl, alpha, mask_history):
    b = pl.program_id(0)
    s = pl.program_id(1)
    rows = nseq * tl
    upitch = tl + SUBLANES
    kpitch = WINDOW + tl

    @pl.when(jnp.logical_and(b == 0, s == 0))
    def _():
        cnt_s[...] = counts_in_ref[:, 0:1]

    @pl.when(s == 0)
    def _():
        for j in range(nseq):
            uext[j * upitch + SUBLANES - 2:j * upitch + SUBLANES, :] = cconv_ref[j]
            kext[j * kpitch:j * kpitch + WINDOW, :] = ck_ref[j]
            vext[j * kpitch:j * kpitch + WINDOW, :] = cv_ref[j]

    x = x_ref[...].reshape(rows, D_MODEL)
    xb = x.astype(BF16)

    u = _dot(xb, w_in_ref[:, _OFF_C:_OFF_C + D_CONV]) * _dot(xb, w_in_ref[:, _OFF_H:_OFF_H + D_CONV])
    w0, w1, w2 = convw_ref[0:1, :], convw_ref[1:2, :], convw_ref[2:3, :]
    ycs = []
    for j in range(nseq):
        base = j * upitch + SUBLANES
        uext[base:base + tl, :] = u[j * tl:(j + 1) * tl]
        yc = convb_ref[...] + uext[base - 2:base - 2 + tl, :] * w0
        yc = yc + uext[base - 1:base - 1 + tl, :] * w1
        yc = yc + uext[base:base + tl, :] * w2
        ycs.append(yc)
        tail = uext[base + tl - 2:base + tl, :]
        sconv_ref[j] = tail
        uext[base - 2:base, :] = tail
    yc = ycs[0] if nseq == 1 else jnp.concatenate(ycs, axis=0)
    y_conv = _dot(xb, w_in_ref[:, _OFF_B:_OFF_B + D_CONV]) * yc
    n_conv = _rms_norm(y_conv, gconv_ref[...]).astype(BF16)

    q = (_dot(xb, w_in_ref[:, _OFF_Q:_OFF_Q + D_ATTN]) * (HEAD_DIM ** -0.5)).astype(BF16)
    k = _dot(xb, w_in_ref[:, _OFF_K:_OFF_K + D_KV])
    v = _dot(xb, w_in_ref[:, _OFF_V:_OFF_V + D_KV])
    for j in range(nseq):
        kext[j * kpitch + WINDOW:(j + 1) * kpitch, :] = k[j * tl:(j + 1) * tl]
        vext[j * kpitch + WINDOW:(j + 1) * kpitch, :] = v[j * tl:(j + 1) * tl]

    nchunk = rows // CHUNK
    assert nseq == 1 or tl == CHUNK
    key_stride = CHUNK if nseq == 1 else kpitch
    head0 = lax.broadcasted_iota(jnp.int32, (1, D_KV), 1) < HEAD_DIM

    def windows(ext_ref):
        ext = ext_ref[...].astype(BF16)
        heads = (jnp.where(head0, ext, jnp.zeros_like(ext)), jnp.where(head0, jnp.zeros_like(ext), ext))
        return jnp.stack([jnp.concatenate([h[c * key_stride:c * key_stride + KEYS] for h in heads], axis=0)
                          for c in range(nchunk)])

    q3 = jnp.stack([jnp.concatenate([q[c * CHUNK:(c + 1) * CHUNK, g * D_KV:(g + 1) * D_KV] for g in range(GROUP)], axis=0)
                    for c in range(nchunk)])
    logits = lax.dot_general(q3, windows(kext), (((2,), (2,)), ((0,), (0,))), preferred_element_type=F32)
    s_ref[...] = logits.reshape(nchunk * QROWS, 2 * KEYS)

    assert 2 * KEYS == 3 * LANES
    mid0 = lax.broadcasted_iota(jnp.int32, (1, LANES), 1) < KEYS - LANES
    colk = lax.broadcasted_iota(jnp.int32, (1, 2 * KEYS), 1)
    key = jnp.where(colk < KEYS, colk, colk - KEYS)

    def row_block(i):
        r = pl.multiple_of(i * SOFTMAX_ROWS, SOFTMAX_ROWS)
        return pl.ds(r, SOFTMAX_ROWS), pl.ds(pl.multiple_of(r % QROWS, SOFTMAX_ROWS), SOFTMAX_ROWS), r // QROWS

    def tiles(lg):
        return lg[:, 0:LANES], lg[:, LANES:2 * LANES], lg[:, 2 * LANES:3 * LANES]

    def max_rows(i, masked):
        rows_i, qrows_i, chunk_i = row_block(i)
        lg = s_ref[rows_i, :] + abias_ref[qrows_i, :]
        if masked:
            lg = lg + jnp.where(s * tl + chunk_i * CHUNK - WINDOW + key < 0, NEG_INF, 0.0).astype(F32)
        s_ref[rows_i, :] = lg
        t0, t1, t2 = tiles(lg)
        m0_ref[rows_i, :] = jnp.maximum(jnp.max(jnp.maximum(t0, jnp.where(mid0, t1, NEG_INF)), -1, keepdims=True),
                                        sink_ref[0, qrows_i, :])
        m1_ref[rows_i, :] = jnp.maximum(jnp.max(jnp.maximum(t2, jnp.where(mid0, NEG_INF, t1)), -1, keepdims=True),
                                        sink_ref[1, qrows_i, :])

    def exp_rows(i, masked):
        rows_i, qrows_i, _ = row_block(i)
        t0, t1, t2 = tiles(s_ref[rows_i, :])
        m0, m1 = m0_ref[rows_i, :], m1_ref[rows_i, :]
        e = jnp.concatenate([jnp.exp(t0 - m0), jnp.exp(t1 - jnp.where(mid0, m0, m1)), jnp.exp(t2 - m1)], axis=-1)
        e_ref[rows_i, :] = e.astype(BF16)
        sinkden_ref[rows_i, :] = jnp.where(head0, jnp.exp(sink_ref[0, qrows_i, :] - m0),
                                           jnp.exp(sink_ref[1, qrows_i, :] - m1))

    def loop_rows(fn, lo_it, hi_it, masked, unroll):
        def body(i, carry):
            fn(i, masked)
            return carry
        lax.fori_loop(lo_it, hi_it, body, 0, unroll=unroll)

    n_it = nchunk * QROWS // SOFTMAX_ROWS
    n_masked = min(WINDOW // CHUNK, nchunk) * QROWS // SOFTMAX_ROWS if mask_history else 0
    if n_masked:
        loop_rows(max_rows, 0, n_masked, True, 8)
    loop_rows(max_rows, n_masked, n_it, False, 8)
    loop_rows(exp_rows, 0, n_it, False, 2)

    vsel = jnp.concatenate([windows(vext), jnp.broadcast_to(keysel_ref[...], (nchunk, 2 * KEYS, D_KV))], axis=-1)
    o2 = lax.dot_general(e_ref[...].reshape(nchunk, QROWS, 2 * KEYS), vsel, (((2,), (1,)), ((0,), (0,))),
                         preferred_element_type=F32)
    o = o2[..., 0:D_KV] * (1.0 / (o2[..., D_KV:2 * D_KV] + sinkden_ref[...].reshape(nchunk, QROWS, D_KV)))
    y_attn = jnp.concatenate(
        [jnp.concatenate([o[c, g * CHUNK:(g + 1) * CHUNK, :] for g in range(GROUP)], axis=1) for c in range(nchunk)],
        axis=0)

    for j in range(nseq):
        sk_ref[j] = kext[j * kpitch + tl:(j + 1) * kpitch, :]
        sv_ref[j] = vext[j * kpitch + tl:(j + 1) * kpitch, :]
    if nseq == 1:
        kext[0:WINDOW, :] = kext[tl:tl + WINDOW, :]
        vext[0:WINDOW, :] = vext[tl:tl + WINDOW, :]

    n_attn = _rms_norm(y_attn, gattn_ref[...]).astype(BF16)
    mixed = _dot(n_conv, w_out_ref[0:D_CONV, :]) + _dot(n_attn, w_out_ref[D_CONV:D_CONV + D_ATTN, :])
    x1 = _layer_norm(alpha * x + mixed, ln1g_ref[...], ln1b_ref[...])
    _store_rows(x1_ref, x1, rows)

    logits_t = (_dot(x1.astype(BF16), wr_ref[...]) + br_ref[...]).T
    sub = lax.broadcasted_iota(jnp.int32, (SUBLANES, rows), 0)
    gl = jnp.where(sub < N_GROUPS, logits_t[N_EXPERTS:N_EXPERTS + SUBLANES, :], -jnp.inf)
    gmax = jnp.max(gl, axis=0, keepdims=True)
    grp = jnp.min(jnp.where(gl == gmax, sub, SUBLANES), axis=0, keepdims=True)
    p_grp = 1.0 / jnp.sum(jnp.exp(gl - gmax), axis=0, keepdims=True)
    el = logits_t[(N_GROUPS - 1) * EXP_PER_GROUP:N_GROUPS * EXP_PER_GROUP, :]
    for g in range(N_GROUPS - 2, -1, -1):
        el = jnp.where(grp == g, logits_t[g * EXP_PER_GROUP:(g + 1) * EXP_PER_GROUP, :], el)
    v1 = jnp.max(el, axis=0, keepdims=True)
    i1 = jnp.min(jnp.where(el == v1, sub, SUBLANES), axis=0, keepdims=True)
    el2 = jnp.where(sub == i1, -jnp.inf, el)
    v2 = jnp.max(el2, axis=0, keepdims=True)
    i2 = jnp.min(jnp.where(el2 == v2, sub, SUBLANES), axis=0, keepdims=True)
    e2 = jnp.exp(v2 - v1)
    gate1 = p_grp * (1.0 / (1.0 + e2))
    gate2 = p_grp * (e2 / (1.0 + e2))

    chosen = jnp.logical_or(sub == i1, sub == i2)
    onehot = jnp.concatenate(
        [jnp.where(jnp.logical_and(grp == g, chosen), 1.0, 0.0) for g in range(N_GROUPS)], axis=0)
    running = cnt_s[...]
    ranks = []
    for c in range(rows // RANK_CHUNK):
        oh = onehot[:, c * RANK_CHUNK:(c + 1) * RANK_CHUNK]
        ranks.append(_dot(oh.astype(BF16), tri_ref[...]) + running)
        running = running + jnp.sum(oh, axis=1, keepdims=True)
    rank = jnp.concatenate(ranks, axis=1)
    cnt_s[...] = running
    counts_ref[...] = jnp.broadcast_to(running, (N_EXPERTS, LANES))

    ex1 = grp * EXP_PER_GROUP + i1
    ex2 = grp * EXP_PER_GROUP + i2
    erow = lax.broadcasted_iota(jnp.int32, (N_EXPERTS, rows), 0)
    rank1 = jnp.sum(jnp.where(erow == ex1, rank, 0.0), axis=0, keepdims=True)
    rank2 = jnp.sum(jnp.where(erow == ex2, rank, 0.0), axis=0, keepdims=True)
    fields = {_R_E1: ex1.astype(F32), _R_E2: ex2.astype(F32), _R_RANK1: rank1, _R_RANK2: rank2,
              _R_G1: gate1, _R_G2: gate2}
    record = jnp.zeros((SUBLANES, rows), F32)
    for r, val in fields.items():
        record = jnp.where(sub == r, val, record)
    route_ref[...] = record
    padded = jnp.concatenate([record, jnp.zeros((LANES - SUBLANES, rows), F32)], axis=0)
    gcol_ref[...] = padded.T


def _mixer(x, cconv, ck, cv, counts_in, prm, consts, *, nseq, tl, mask_history, alpha):
    nb_total, t_total = x.shape[0], x.shape[1]
    nb, ns = nb_total // nseq, t_total // tl
    rows = nseq * tl
    n_tok = nb_total * t_total
    w_in, convw, convb, sinks, gconv, gattn, w_out, ln1g, ln1b, wr, br = prm
    abias, tri, keysel = consts

    def full(a):
        return pl.BlockSpec(a.shape, lambda b, s, _n=a.ndim: (0,) * _n)

    def seq_state(width, nrows):
        return pl.BlockSpec((nseq, nrows, width), lambda b, s: (b, 0, 0))

    in_specs = [
        pl.BlockSpec((nseq, tl, D_MODEL), lambda b, s: (b, s, 0)),
        seq_state(D_CONV, CONV_W - 1), seq_state(D_KV, WINDOW), seq_state(D_KV, WINDOW),
        full(w_in), full(convw), full(convb), full(abias), full(sinks), full(gconv), full(gattn),
        full(w_out), full(ln1g), full(ln1b), full(wr), full(br), full(tri), full(keysel), full(counts_in),
    ]
    out_shape = [
        jax.ShapeDtypeStruct((n_tok * ROW_TILES, LANES), F32),
        jax.ShapeDtypeStruct((SUBLANES, n_tok), F32),
        jax.ShapeDtypeStruct((n_tok, LANES), F32),
        jax.ShapeDtypeStruct((N_EXPERTS, LANES), F32),
        jax.ShapeDtypeStruct((nb_total, CONV_W - 1, D_CONV), F32),
        jax.ShapeDtypeStruct((nb_total, WINDOW, D_KV), F32),
        jax.ShapeDtypeStruct((nb_total, WINDOW, D_KV), F32),
    ]
    out_specs = [
        pl.BlockSpec((rows * ROW_TILES, LANES), lambda b, s: (b * ns + s, 0)),
        pl.BlockSpec((SUBLANES, rows), lambda b, s: (0, b * ns + s)),
        pl.BlockSpec((rows, LANES), lambda b, s: (b * ns + s, 0)),
        pl.BlockSpec((N_EXPERTS, LANES), lambda b, s: (0, 0)),
        seq_state(D_CONV, CONV_W - 1), seq_state(D_KV, WINDOW), seq_state(D_KV, WINDOW),
    ]
    scratch = [
        pltpu.VMEM((nseq * (tl + SUBLANES), D_CONV), F32),
        pltpu.VMEM((nseq * (WINDOW + tl), D_KV), F32),
        pltpu.VMEM((nseq * (WINDOW + tl), D_KV), F32),
        pltpu.VMEM((N_EXPERTS, 1), F32),
        pltpu.VMEM((rows // CHUNK * QROWS, 2 * KEYS), F32),
        pltpu.VMEM((rows // CHUNK * QROWS, 2 * KEYS), BF16),
        pltpu.VMEM((rows // CHUNK * QROWS, D_KV), F32),
        pltpu.VMEM((rows // CHUNK * QROWS, LANES), F32),
        pltpu.VMEM((rows // CHUNK * QROWS, LANES), F32),
    ]
    kern = functools.partial(_mixer_kernel, nseq=nseq, tl=tl, alpha=alpha, mask_history=mask_history)
    return pl.pallas_call(
        kern, grid=(nb, ns), in_specs=in_specs, out_specs=out_specs, out_shape=out_shape,
        scratch_shapes=scratch, name="mixer",
        compiler_params=pltpu.CompilerParams(dimension_semantics=("arbitrary", "arbitrary"),
                                             vmem_limit_bytes=VMEM_LIMIT),
    )(x, cconv, ck, cv, w_in, convw, convb, abias, sinks, gconv, gattn, w_out, ln1g, ln1b, wr, br, tri, keysel, counts_in)


_PAD_PIECES = tuple(2 ** k for k in range(EXPERT_BLOCK.bit_length() - 2, -1, -1))


def _tile_rows(row, n=1):
    return pl.ds(pl.multiple_of(row * ROW_TILES, ROW_TILES), n * ROW_TILES)


def _row_copy(src, src_row, dst, dst_row, sem):
    return pltpu.make_async_copy(src.at[_tile_rows(src_row), :], dst.at[_tile_rows(dst_row), :], sem)


def _dispatch_kernel(zstart_ref, zcount_ref, tail_ref, pos1_ref, pos2_ref, xa_ref, xb_ref, xs_hbm, zeros_v, sem, zsem,
                     *, tb, nba):
    step = pl.program_id(0)

    def zero_copy(start, piece):
        return pltpu.make_async_copy(zeros_v.at[_tile_rows(0, piece), :], xs_hbm.at[_tile_rows(start, piece), :], zsem)

    def pad_copy(e, piece, taken):
        return zero_copy(zstart_ref[e] + taken, piece)

    def for_each_pad_piece(fn):
        def per_tail_block(j, carry):
            for part in range(EXPERT_BLOCK // _PAD_PIECES[0]):
                fn(zero_copy(tail_ref[0] + j * EXPERT_BLOCK + part * _PAD_PIECES[0], _PAD_PIECES[0]))
            return carry

        lax.fori_loop(0, tail_ref[1], per_tail_block, 0)

        def per_expert(e, carry):
            count = zcount_ref[e]
            taken = 0
            for piece in _PAD_PIECES:
                present = (count & piece) != 0

                @pl.when(present)
                def _(piece=piece, taken=taken):
                    fn(pad_copy(e, piece, taken))

                taken = taken + jnp.where(present, piece, 0)
            return carry

        lax.fori_loop(0, N_EXPERTS, per_expert, 0)

    @pl.when(step == 0)
    def _():
        zeros_v[...] = jnp.zeros_like(zeros_v)
        for_each_pad_piece(lambda c: c.start())

    def issue_rows(x_ref):
        def issue(t, carry):
            _row_copy(x_ref, t, xs_hbm, pos1_ref[t], sem).start(priority=0)
            _row_copy(x_ref, t, xs_hbm, pos2_ref[t], sem).start(priority=1)
            return carry

        lax.fori_loop(0, tb, issue, 0, unroll=8)

    @pl.when(step < nba)
    def _():
        issue_rows(xa_ref)

    @pl.when(step >= nba)
    def _():
        issue_rows(xb_ref)

    pltpu.make_async_copy(xs_hbm.at[_tile_rows(0, 2 * tb), :], xs_hbm.at[_tile_rows(0, 2 * tb), :], sem).wait()

    @pl.when(step == 0)
    def _():
        for_each_pad_piece(lambda c: c.wait())


def _dispatch(xa_rows, xb_rows, pos1, pos2, zstart, zcount, tail, n_slots, *, tb):
    nba, nbb = xa_rows.shape[0] // ROW_TILES // tb, xb_rows.shape[0] // ROW_TILES // tb
    kern = functools.partial(_dispatch_kernel, tb=tb, nba=nba)
    grid_spec = pltpu.PrefetchScalarGridSpec(
        num_scalar_prefetch=3, grid=(nba + nbb,),
        in_specs=[pl.BlockSpec((tb,), lambda i, *_: (i,), memory_space=pltpu.SMEM),
                  pl.BlockSpec((tb,), lambda i, *_: (i,), memory_space=pltpu.SMEM),
                  pl.BlockSpec((tb * ROW_TILES, LANES), lambda i, *_: (jnp.minimum(i, nba - 1), 0)),
                  pl.BlockSpec((tb * ROW_TILES, LANES), lambda i, *_: (jnp.maximum(i - nba, 0), 0))],
        out_specs=pl.BlockSpec(memory_space=pl.ANY),
        scratch_shapes=[pltpu.VMEM((_PAD_PIECES[0] * ROW_TILES, LANES), F32),
                        pltpu.SemaphoreType.DMA(()), pltpu.SemaphoreType.DMA(())],
    )
    return pl.pallas_call(
        kern, grid_spec=grid_spec, out_shape=jax.ShapeDtypeStruct((n_slots * ROW_TILES, LANES), F32),
        name="dispatch",
        compiler_params=pltpu.CompilerParams(dimension_semantics=("arbitrary",)),
    )(zstart, zcount, tail, pos1, pos2, xa_rows, xb_rows)


DOWN_CHUNK = 2 * LANES
GU_CHUNKS = 2 * D_EXP // DOWN_CHUNK


def _expert_kernel(be_ref, src_ref, used_ref, xs_ref, wg_ref, wu_ref, wd_ref, yb_ref, wgu_b, wd_b, xin0, xin1):
    i = pl.program_id(0)
    prev = jnp.maximum(i - 1, 0)
    has_prev = i > 0

    @pl.when(jnp.logical_or(i <= 1, be_ref[prev] != be_ref[jnp.maximum(i - 2, 0)]))
    def _():
        wgu_b[:, 0:D_EXP] = wg_ref[...].astype(BF16)
        wgu_b[:, D_EXP:2 * D_EXP] = wu_ref[...].astype(BF16)
        wd_b[...] = wd_ref[...].astype(BF16)

    def convert(dst, piece=None):
        n = EXPERT_BLOCK if piece is None else EXPERT_BLOCK // GU_CHUNKS
        r0 = 0 if piece is None else piece * n
        tiles = [xs_ref[pl.ds(r0 * ROW_TILES + j, n, stride=ROW_TILES), :] for j in range(ROW_TILES)]
        dst[r0:r0 + n, :] = jnp.concatenate(tiles, axis=1).astype(BF16)

    def mlp(src, nxt):
        x = src[...]
        gu = []
        for c in range(GU_CHUNKS):
            gu.append(_dot(x, wgu_b[:, c * DOWN_CHUNK:(c + 1) * DOWN_CHUNK]))
            convert(nxt, c)
        gu = jnp.concatenate(gu, axis=1)
        g, u = gu[:, 0:D_EXP], gu[:, D_EXP:2 * D_EXP]
        hid = ((g / (1.0 + jnp.exp(-g))) * u).astype(BF16)
        for c in range(D_MODEL // DOWN_CHUNK):
            y = _dot(hid, wd_b[:, c * DOWN_CHUNK:(c + 1) * DOWN_CHUNK])
            for j in range(DOWN_CHUNK // LANES):
                tile = c * (DOWN_CHUNK // LANES) + j
                yb_ref[pl.ds(tile, EXPERT_BLOCK, stride=ROW_TILES), :] = y[:, j * LANES:(j + 1) * LANES]

    @pl.when(i == 0)
    def _():
        convert(xin0)

    for parity, (cur, nxt) in enumerate(((xin1, xin0), (xin0, xin1))):
        @pl.when(jnp.logical_and(jnp.logical_and(has_prev, used_ref[prev] != 0), i % 2 == parity))
        def _(cur=cur, nxt=nxt):
            mlp(cur, nxt)

    @pl.when(jnp.logical_and(has_prev, used_ref[prev] == 0))
    def _():
        yb_ref[...] = jnp.zeros_like(yb_ref)


def _experts(xs, blk_e, blk_src, blk_used, w_gate, w_up, w_down, layer):
    n_slots = xs.shape[0] // ROW_TILES
    nblk = n_slots // EXPERT_BLOCK

    def computed(i):
        return jnp.maximum(i - 1, 0)

    def wspec(shape):
        return pl.BlockSpec((None, None) + shape, lambda i, be, src, used: (layer, be[computed(i)], 0, 0))

    grid_spec = pltpu.PrefetchScalarGridSpec(
        num_scalar_prefetch=3, grid=(nblk + 1,),
        in_specs=[pl.BlockSpec((EXPERT_BLOCK * ROW_TILES, LANES),
                               lambda i, be, src, used: (src[jnp.minimum(i, nblk - 1)], 0)),
                  wspec((D_MODEL, D_EXP)), wspec((D_MODEL, D_EXP)), wspec((D_EXP, D_MODEL))],
        out_specs=pl.BlockSpec((EXPERT_BLOCK * ROW_TILES, LANES), lambda i, be, src, used: (computed(i), 0)),
        scratch_shapes=[pltpu.VMEM((D_MODEL, 2 * D_EXP), BF16), pltpu.VMEM((D_EXP, D_MODEL), BF16),
                        pltpu.VMEM((EXPERT_BLOCK, D_MODEL), BF16), pltpu.VMEM((EXPERT_BLOCK, D_MODEL), BF16)],
    )
    return pl.pallas_call(
        _expert_kernel, grid_spec=grid_spec, out_shape=jax.ShapeDtypeStruct((n_slots * ROW_TILES, LANES), F32),
        name="experts",
        compiler_params=pltpu.CompilerParams(dimension_semantics=("arbitrary",), vmem_limit_bytes=VMEM_LIMIT),
    )(blk_e, blk_src, blk_used, xs, w_gate, w_up, w_down)


def _combine_kernel(pos1_ref, pos2_ref, next1_ref, next2_ref, xa_ref, xb_ref, ga_ref, gb_ref, yb_hbm, g_ref, b_ref,
                    outa_ref, outb_ref, ybuf, sems, *, tb, alpha, nba, nsteps):
    step = pl.program_id(0)
    slot = step % 2

    def gather(p1_ref, p2_ref, to_slot):
        def issue(t, carry):
            _row_copy(yb_hbm, p1_ref[t], ybuf.at[to_slot].at[0], t, sems.at[to_slot]).start(priority=0)
            _row_copy(yb_hbm, p2_ref[t], ybuf.at[to_slot].at[1], t, sems.at[to_slot]).start(priority=1)
            return carry

        lax.fori_loop(0, tb, issue, 0, unroll=8)

    @pl.when(step == 0)
    def _():
        gather(pos1_ref, pos2_ref, 0)

    @pl.when(step + 1 < nsteps)
    def _():
        gather(next1_ref, next2_ref, 1 - slot)

    for k in range(TOP_K):
        pltpu.make_async_copy(yb_hbm.at[_tile_rows(0, tb), :], ybuf.at[slot].at[k], sems.at[slot]).wait()

    def finish(x1_ref, gcol_ref, out_ref):
        gates = gcol_ref[...]
        y = (gates[:, _R_G1:_R_G1 + 1] * _load_rows(ybuf.at[slot].at[0], tb)
             + gates[:, _R_G2:_R_G2 + 1] * _load_rows(ybuf.at[slot].at[1], tb))
        out_ref[...] = _layer_norm(alpha * _load_rows(x1_ref, tb) + y, g_ref[...], b_ref[...])

    @pl.when(step < nba)
    def _():
        finish(xa_ref, ga_ref, outa_ref)

    @pl.when(step >= nba)
    def _():
        finish(xb_ref, gb_ref, outb_ref)


def _combine(xa_rows, xb_rows, gcol_a, gcol_b, yb, pos1, pos2, ln2g, ln2b, *, tb, alpha):
    nba, nbb = xa_rows.shape[0] // ROW_TILES // tb, xb_rows.shape[0] // ROW_TILES // tb
    nsteps = nba + nbb
    kern = functools.partial(_combine_kernel, tb=tb, alpha=alpha, nba=nba, nsteps=nsteps)

    def first(i):
        return jnp.minimum(i, nba - 1)

    def second(i):
        return jnp.maximum(i - nba, 0)

    def nxt(i):
        return jnp.minimum(i + 1, nsteps - 1)

    smem = functools.partial(pl.BlockSpec, (tb,), memory_space=pltpu.SMEM)
    return pl.pallas_call(
        kern, grid=(nsteps,),
        in_specs=[smem(lambda i: (i,)), smem(lambda i: (i,)), smem(lambda i: (nxt(i),)), smem(lambda i: (nxt(i),)),
                  pl.BlockSpec((tb * ROW_TILES, LANES), lambda i: (first(i), 0)),
                  pl.BlockSpec((tb * ROW_TILES, LANES), lambda i: (second(i), 0)),
                  pl.BlockSpec((tb, LANES), lambda i: (first(i), 0)),
                  pl.BlockSpec((tb, LANES), lambda i: (second(i), 0)),
                  pl.BlockSpec(memory_space=pl.ANY),
                  pl.BlockSpec((1, D_MODEL), lambda i: (0, 0)),
                  pl.BlockSpec((1, D_MODEL), lambda i: (0, 0))],
        out_specs=[pl.BlockSpec((tb, D_MODEL), lambda i: (first(i), 0)),
                   pl.BlockSpec((tb, D_MODEL), lambda i: (second(i), 0))],
        out_shape=[jax.ShapeDtypeStruct((nba * tb, D_MODEL), F32), jax.ShapeDtypeStruct((nbb * tb, D_MODEL), F32)],
        scratch_shapes=[pltpu.VMEM((2, TOP_K, tb * ROW_TILES, LANES), F32), pltpu.SemaphoreType.DMA((2,))],
        name="combine",
        compiler_params=pltpu.CompilerParams(dimension_semantics=("arbitrary",), vmem_limit_bytes=VMEM_LIMIT),
    )(pos1, pos2, pos1, pos2, xa_rows, xb_rows, gcol_a, gcol_b, yb, ln2g, ln2b)


def _attn_column_order():
    g, kvh, d = np.meshgrid(np.arange(GROUP), np.arange(N_KV_HEADS), np.arange(HEAD_DIM), indexing='ij')
    return ((kvh * GROUP + g) * HEAD_DIM + d).reshape(-1)


def _alibi_bias():
    slopes = np.asarray([2.0 ** (-8.0 * (h + 1) / N_HEADS) for h in range(N_HEADS)], np.float32)
    qi = np.arange(CHUNK, dtype=np.int32)[:, None]
    sj = np.arange(KEYS, dtype=np.int32)[None, :]
    dist = np.abs(qi + WINDOW - sj).astype(np.float32)
    bias = -slopes.reshape(N_KV_HEADS, GROUP, 1, 1) * dist
    bias = np.transpose(bias, (1, 2, 0, 3)).reshape(GROUP * CHUNK, N_KV_HEADS * KEYS)
    return jnp.asarray(bias, F32)


def _key_selector():
    row_head = np.arange(N_KV_HEADS * KEYS)[:, None] // KEYS
    lane_head = np.arange(D_KV)[None, :] // HEAD_DIM
    return jnp.asarray((row_head == lane_head).astype(np.float32), BF16)


def _strict_upper():
    r = np.arange(RANK_CHUNK)
    return jnp.asarray((r[:, None] < r[None, :]).astype(np.float32), BF16)


def _slot_plan(route, counts, n_tok):
    nblk = (n_tok * TOP_K) // EXPERT_BLOCK + N_EXPERTS
    experts = route[_R_E1:_R_E2 + 1].astype(jnp.int32)
    ranks = route[_R_RANK1:_R_RANK2 + 1].astype(jnp.int32)
    cnt = counts[:, 0].astype(jnp.int32)
    padded = (cnt + EXPERT_BLOCK - 1) // EXPERT_BLOCK * EXPERT_BLOCK
    pad_end = jnp.cumsum(padded)
    pad_start = pad_end - padded
    ids = jnp.arange(N_EXPERTS, dtype=jnp.int32)
    pos = jnp.sum(jnp.where(experts[..., None] == ids, pad_start, 0), axis=-1) + ranks
    blk_first = jnp.arange(nblk, dtype=jnp.int32) * EXPERT_BLOCK
    blk_e = jnp.minimum(jnp.sum((pad_end[None, :] <= blk_first[:, None]).astype(jnp.int32), axis=-1), N_EXPERTS - 1)
    blk_used = (blk_first < pad_end[-1]).astype(jnp.int32)
    blk_src = jnp.minimum(jnp.arange(nblk, dtype=jnp.int32), pad_end[-1] // EXPERT_BLOCK - 1)
    tail = jnp.stack([pad_end[-1], nblk - pad_end[-1] // EXPERT_BLOCK]).astype(jnp.int32)
    return pos[0], pos[1], pad_start + cnt, padded - cnt, tail, blk_e, blk_src, blk_used, nblk * EXPERT_BLOCK


def _layer(xp, xs, cache, mix_prm, consts, moe_prm, layer, *, tl_prompt, nseq_sample, alpha):
    (bp, tp, _), (bs, ts, _) = xp.shape, xs.shape
    np_tok, ns_tok = bp * tp, bs * ts
    tb = ROWS
    assert np_tok % tb == 0 and ns_tok % tb == 0
    zeros_conv = jnp.zeros((bp, CONV_W - 1, D_CONV), F32)
    zeros_kv = jnp.zeros((bp, WINDOW, D_KV), F32)
    zero_counts = jnp.zeros((N_EXPERTS, LANES), F32)
    x1p, route_p, gcol_p, counts_p, *state_p = _mixer(
        xp, zeros_conv, zeros_kv, zeros_kv, zero_counts, mix_prm, consts,
        nseq=1, tl=tl_prompt, mask_history=True, alpha=alpha)
    x1s, route_s, gcol_s, counts, *state_s = _mixer(
        xs, *cache, counts_p, mix_prm, consts, nseq=nseq_sample, tl=ts, mask_history=False, alpha=alpha)
    route = jnp.concatenate([route_p, route_s], axis=1)
    pos1, pos2, zstart, zcount, tail, blk_e, blk_src, blk_used, n_slots = _slot_plan(route, counts, np_tok + ns_tok)
    sorted_rows = _dispatch(x1p, x1s, pos1, pos2, zstart, zcount, tail, n_slots, tb=tb)
    w_gate, w_up, w_down, ln2g, ln2b = moe_prm
    yb = _experts(sorted_rows, blk_e, blk_src, blk_used, w_gate, w_up, w_down, layer)
    out_p, out_s = _combine(x1p, x1s, gcol_p, gcol_s, yb, pos1, pos2, ln2g, ln2b, tb=tb, alpha=alpha)
    return out_p.reshape(bp, tp, D_MODEL), out_s.reshape(bs, ts, D_MODEL), state_p, state_s


def kernel(x_prompt, x_sample, cache_conv, cache_k, cache_v, w_in, conv_w, conv_b, attn_sinks, g_conv, g_attn,
           w_out, ln1_g, ln1_b, router_group_w, router_group_b, router_expert_w, router_expert_b,
           expert_w_gate, expert_w_up, expert_w_down, ln2_g, ln2_b):
    depth = w_in.shape[0]
    alpha = (2 * depth) ** 0.25
    batch, seq = x_prompt.shape[0], x_prompt.shape[1]
    dec_batch, dec_seq = x_sample.shape[0], x_sample.shape[1]
    assert dec_seq == CHUNK and seq % CHUNK == 0
    consts = (_alibi_bias(), _strict_upper(), _key_selector())
    attn_order = _attn_column_order()
    tl_prompt = min(MIXER_ROWS, seq)
    nseq_sample = min(MIXER_ROWS // dec_seq, dec_batch)
    assert seq % tl_prompt == 0 and dec_batch % nseq_sample == 0

    xp, xs = x_prompt, x_sample
    states = [[] for _ in range(6)]
    for l in range(depth):
        sink_col = jnp.broadcast_to(jnp.repeat(attn_sinks[l].astype(F32), CHUNK).reshape(N_KV_HEADS, QROWS, 1),
                                    (N_KV_HEADS, QROWS, LANES))
        wr = jnp.zeros((D_MODEL, LANES), F32)
        wr = wr.at[:, 0:N_EXPERTS].set(router_expert_w[l]).at[:, N_EXPERTS:N_EXPERTS + N_GROUPS].set(router_group_w[l])
        br = jnp.zeros((1, LANES), F32)
        br = br.at[0, 0:N_EXPERTS].set(router_expert_b[l]).at[0, N_EXPERTS:N_EXPERTS + N_GROUPS].set(router_group_b[l])
        w_in_l = w_in[l].at[:, _OFF_Q:_OFF_Q + D_ATTN].set(w_in[l][:, _OFF_Q + attn_order])
        w_out_l = w_out[l].at[D_CONV:D_CONV + D_ATTN, :].set(w_out[l][D_CONV + attn_order, :])
        mix_prm = (w_in_l.astype(BF16), conv_w[l], conv_b[l].reshape(1, D_CONV), sink_col,
                   g_conv[l].reshape(1, D_CONV), g_attn[l][attn_order].reshape(1, D_ATTN), w_out_l.astype(BF16),
                   ln1_g[l].reshape(1, D_MODEL), ln1_b[l].reshape(1, D_MODEL), wr.astype(BF16), br)
        moe_prm = (expert_w_gate, expert_w_up, expert_w_down,
                   ln2_g[l].reshape(1, D_MODEL), ln2_b[l].reshape(1, D_MODEL))
        cache = (cache_conv[l], cache_k[l].reshape(dec_batch, WINDOW, D_KV), cache_v[l].reshape(dec_batch, WINDOW, D_KV))
        xp, xs, state_p, state_s = _layer(xp, xs, cache, mix_prm, consts, moe_prm, l,
                                          tl_prompt=tl_prompt, nseq_sample=nseq_sample, alpha=alpha)
        for lst, val in zip(states, state_p + state_s):
            lst.append(val)

    def kv(lst, nb):
        return jnp.stack(lst).reshape(depth, nb, WINDOW, N_KV_HEADS, HEAD_DIM)

    return (xp, xs, jnp.stack(states[0]), kv(states[1], batch), kv(states[2], batch),
            jnp.stack(states[3]), kv(states[4], dec_batch), kv(states[5], dec_batch))
```

```python
import functools

import numpy as np
import jax
import jax.numpy as jnp
from jax import lax
from jax.experimental import pallas as pl
from jax.experimental.pallas import tpu as pltpu

D_MODEL = 1024
D_CONV = 512
CONV_W = 3
N_HEADS = 8
N_KV_HEADS = 2
GROUP = N_HEADS // N_KV_HEADS
HEAD_DIM = 64
D_ATTN = N_HEADS * HEAD_DIM
D_KV = N_KV_HEADS * HEAD_DIM
WINDOW = 128
CHUNK = 64
KEYS = WINDOW + CHUNK
QROWS = GROUP * CHUNK
N_GROUPS = 4
EXP_PER_GROUP = 8
N_EXPERTS = N_GROUPS * EXP_PER_GROUP
TOP_K = 2
D_EXP = 512
LN_EPS = 1e-5
NEG_INF = -1e30

SUBLANES = 8
LANES = 128
ROW_TILES = D_MODEL // LANES
assert ROW_TILES == SUBLANES
ROWS = 512
MIXER_ROWS = 512
EXPERT_BLOCK = 512
RANK_CHUNK = 256
SOFTMAX_ROWS = 64
PROJ_CHUNK = 2 * LANES
VMEM_LIMIT = 56 * 1024 * 1024

F32 = jnp.float32
BF16 = jnp.bfloat16

_OFF_B, _OFF_C, _OFF_H = 0, D_CONV, 2 * D_CONV
_OFF_Q = 3 * D_CONV
_OFF_K = _OFF_Q + D_ATTN
_OFF_V = _OFF_K + D_KV
D_IN = _OFF_V + D_KV

_R_E1, _R_E2, _R_RANK1, _R_RANK2, _R_G1, _R_G2 = 0, 1, 2, 3, 4, 5


def _load_rows(ref, n):
    return jnp.concatenate([ref[pl.ds(j, n, stride=ROW_TILES), :] for j in range(ROW_TILES)], axis=1)


def _store_rows(ref, val, n):
    for j in range(ROW_TILES):
        ref[pl.ds(j, n, stride=ROW_TILES), :] = val[:, j * LANES:(j + 1) * LANES]


def _dot(a, b):
    return jnp.dot(a, b, preferred_element_type=F32)


def _rms_norm(x, g):
    return x * lax.rsqrt(jnp.mean(jnp.square(x), -1, keepdims=True) + LN_EPS) * g


def _layer_norm(x, g, b):
    mu = jnp.mean(x, -1, keepdims=True)
    xc = x - mu
    var = jnp.mean(jnp.square(xc), -1, keepdims=True)
    return xc * lax.rsqrt(var + LN_EPS) * g + b


def _mixer_kernel(x_ref, cconv_ref, ck_ref, cv_ref, w_in_ref, convw_ref, convb_ref, abias_ref,
                  sink_ref, gconv_ref, gattn_ref, w_out_ref, ln1g_ref, ln1b_ref, wr_ref, br_ref,
                  tri_ref, keysel_ref, counts_in_ref,
                  x1_ref, route_ref, gcol_ref, counts_ref, sconv_ref, sk_ref, sv_ref,
                  uext, kext, vext, cnt_s, s_ref, e_ref, sinkden_ref, m0_ref, m1_ref, proj_ref,
                  *, nseq, tl, alpha, mask_history):
    b = pl.program_id(0)
    s = pl.program_id(1)
    rows = nseq * tl
    upitch = tl + SUBLANES
    kpitch = WINDOW + tl

    @pl.when(jnp.logical_and(b == 0, s == 0))
    def _():
        cnt_s[...] = counts_in_ref[:, 0:1]

    @pl.when(s == 0)
    def _():
        for j in range(nseq):
            uext[j * upitch + SUBLANES - 2:j * upitch + SUBLANES, :] = cconv_ref[j]
            kext[j * kpitch:j * kpitch + WINDOW, :] = ck_ref[j]
            vext[j * kpitch:j * kpitch + WINDOW, :] = cv_ref[j]

    x = x_ref[...].reshape(rows, D_MODEL)
    xb = x.astype(BF16)

    def conv_mixer():
        u = proj_ref[:, _OFF_C:_OFF_C + D_CONV] * proj_ref[:, _OFF_H:_OFF_H + D_CONV]
        w0, w1, w2 = convw_ref[0:1, :], convw_ref[1:2, :], convw_ref[2:3, :]
        ycs = []
        for j in range(nseq):
            base = j * upitch + SUBLANES
            uext[base:base + tl, :] = u[j * tl:(j + 1) * tl]
            yc = convb_ref[...] + uext[base - 2:base - 2 + tl, :] * w0
            yc = yc + uext[base - 1:base - 1 + tl, :] * w1
            yc = yc + uext[base:base + tl, :] * w2
            ycs.append(yc)
            tail = uext[base + tl - 2:base + tl, :]
            sconv_ref[j] = tail
            uext[base - 2:base, :] = tail
        yc = ycs[0] if nseq == 1 else jnp.concatenate(ycs, axis=0)
        y_conv = proj_ref[:, _OFF_B:_OFF_B + D_CONV] * yc
        return _rms_norm(y_conv, gconv_ref[...]).astype(BF16)

    q = (_dot(xb, w_in_ref[:, _OFF_Q:_OFF_Q + D_ATTN]) * (HEAD_DIM ** -0.5)).astype(BF16)
    k = _dot(xb, w_in_ref[:, _OFF_K:_OFF_K + D_KV])
    v = _dot(xb, w_in_ref[:, _OFF_V:_OFF_V + D_KV])
    for j in range(nseq):
        kext[j * kpitch + WINDOW:(j + 1) * kpitch, :] = k[j * tl:(j + 1) * tl]
        vext[j * kpitch + WINDOW:(j + 1) * kpitch, :] = v[j * tl:(j + 1) * tl]

    nchunk = rows // CHUNK
    assert nseq == 1 or tl == CHUNK
    key_stride = CHUNK if nseq == 1 else kpitch
    head0 = lax.broadcasted_iota(jnp.int32, (1, D_KV), 1) < HEAD_DIM

    def windows(ext_ref):
        ext = ext_ref[...].astype(BF16)
        heads = (jnp.where(head0, ext, jnp.zeros_like(ext)), jnp.where(head0, jnp.zeros_like(ext), ext))
        return jnp.stack([jnp.concatenate([h[c * key_stride:c * key_stride + KEYS] for h in heads], axis=0)
                          for c in range(nchunk)])

    q3 = jnp.stack([jnp.concatenate([q[c * CHUNK:(c + 1) * CHUNK, g * D_KV:(g + 1) * D_KV] for g in range(GROUP)], axis=0)
                    for c in range(nchunk)])
    logits = lax.dot_general(q3, windows(kext), (((2,), (2,)), ((0,), (0,))), preferred_element_type=F32)
    s_ref[...] = logits.reshape(nchunk * QROWS, 2 * KEYS)

    assert 2 * KEYS == 3 * LANES
    mid0 = lax.broadcasted_iota(jnp.int32, (1, LANES), 1) < KEYS - LANES
    colk = lax.broadcasted_iota(jnp.int32, (1, 2 * KEYS), 1)
    key = jnp.where(colk < KEYS, colk, colk - KEYS)

    def row_block(i):
        r = i * SOFTMAX_ROWS
        return pl.ds(r, SOFTMAX_ROWS), pl.ds(r % QROWS, SOFTMAX_ROWS), r // QROWS

    def tiles(lg):
        return lg[:, 0:LANES], lg[:, LANES:2 * LANES], lg[:, 2 * LANES:3 * LANES]

    def max_rows(i, masked):
        rows_i, qrows_i, chunk_i = row_block(i)
        lg = s_ref[rows_i, :] + abias_ref[qrows_i, :]
        if masked:
            lg = lg + jnp.where(s * tl + chunk_i * CHUNK - WINDOW + key < 0, NEG_INF, 0.0).astype(F32)
        s_ref[rows_i, :] = lg
        t0, t1, t2 = tiles(lg)
        m0_ref[rows_i, :] = jnp.maximum(jnp.max(jnp.maximum(t0, jnp.where(mid0, t1, NEG_INF)), -1, keepdims=True),
                                        sink_ref[0, qrows_i, :])
        m1_ref[rows_i, :] = jnp.maximum(jnp.max(jnp.maximum(t2, jnp.where(mid0, NEG_INF, t1)), -1, keepdims=True),
                                        sink_ref[1, qrows_i, :])

    def exp_rows(i):
        rows_i, qrows_i, _ = row_block(i)
        t0, t1, t2 = tiles(s_ref[rows_i, :])
        m0, m1 = m0_ref[rows_i, :], m1_ref[rows_i, :]
        e = jnp.concatenate([jnp.exp(t0 - m0), jnp.exp(t1 - jnp.where(mid0, m0, m1)), jnp.exp(t2 - m1)], axis=-1)
        e_ref[rows_i, :] = e.astype(BF16)
        sinkden_ref[rows_i, :] = jnp.where(head0, jnp.exp(sink_ref[0, qrows_i, :] - m0),
                                           jnp.exp(sink_ref[1, qrows_i, :] - m1))

    n_it = nchunk * QROWS // SOFTMAX_ROWS
    n_masked = min(WINDOW // CHUNK, nchunk) * QROWS // SOFTMAX_ROWS if mask_history else 0
    proj_chunks = [(c, min(c + PROJ_CHUNK, _OFF_Q)) for c in range(0, _OFF_Q, PROJ_CHUNK)]
    piece = -(-2 * n_it // len(proj_chunks))
    steps = [functools.partial(max_rows, i, i < n_masked) for i in range(n_it)]
    steps += [functools.partial(exp_rows, i) for i in range(n_it)]
    for p, (c0, c1) in enumerate(proj_chunks):
        for step_fn in steps[p * piece:(p + 1) * piece]:
            step_fn()
        proj_ref[:, c0:c1] = _dot(xb, w_in_ref[:, c0:c1])
    for step_fn in steps[len(proj_chunks) * piece:]:
        step_fn()
    n_conv = conv_mixer()

    vsel = jnp.concatenate([windows(vext), jnp.broadcast_to(keysel_ref[...], (nchunk, 2 * KEYS, D_KV))], axis=-1)
    o2 = lax.dot_general(e_ref[...].reshape(nchunk, QROWS, 2 * KEYS), vsel, (((2,), (1,)), ((0,), (0,))),
                         preferred_element_type=F32)
    o = o2[..., 0:D_KV] * (1.0 / (o2[..., D_KV:2 * D_KV] + sinkden_ref[...].reshape(nchunk, QROWS, D_KV)))
    y_attn = jnp.concatenate(
        [jnp.concatenate([o[c, g * CHUNK:(g + 1) * CHUNK, :] for g in range(GROUP)], axis=1) for c in range(nchunk)],
        axis=0)

    for j in range(nseq):
        sk_ref[j] = kext[j * kpitch + tl:(j + 1) * kpitch, :]
        sv_ref[j] = vext[j * kpitch + tl:(j + 1) * kpitch, :]
    if nseq == 1:
        kext[0:WINDOW, :] = kext[tl:tl + WINDOW, :]
        vext[0:WINDOW, :] = vext[tl:tl + WINDOW, :]

    n_attn = _rms_norm(y_attn, gattn_ref[...]).astype(BF16)
    mixed = _dot(n_conv, w_out_ref[0:D_CONV, :]) + _dot(n_attn, w_out_ref[D_CONV:D_CONV + D_ATTN, :])
    x1 = _layer_norm(alpha * x + mixed, ln1g_ref[...], ln1b_ref[...])
    _store_rows(x1_ref, x1, rows)

    logits_t = (_dot(x1.astype(BF16), wr_ref[...]) + br_ref[...]).T
    sub = lax.broadcasted_iota(jnp.int32, (SUBLANES, rows), 0)
    gl = jnp.where(sub < N_GROUPS, logits_t[N_EXPERTS:N_EXPERTS + SUBLANES, :], -jnp.inf)
    gmax = jnp.max(gl, axis=0, keepdims=True)
    grp = jnp.min(jnp.where(gl == gmax, sub, SUBLANES), axis=0, keepdims=True)
    p_grp = 1.0 / jnp.sum(jnp.exp(gl - gmax), axis=0, keepdims=True)
    el = logits_t[(N_GROUPS - 1) * EXP_PER_GROUP:N_GROUPS * EXP_PER_GROUP, :]
    for g in range(N_GROUPS - 2, -1, -1):
        el = jnp.where(grp == g, logits_t[g * EXP_PER_GROUP:(g + 1) * EXP_PER_GROUP, :], el)
    v1 = jnp.max(el, axis=0, keepdims=True)
    i1 = jnp.min(jnp.where(el == v1, sub, SUBLANES), axis=0, keepdims=True)
    el2 = jnp.where(sub == i1, -jnp.inf, el)
    v2 = jnp.max(el2, axis=0, keepdims=True)
    i2 = jnp.min(jnp.where(el2 == v2, sub, SUBLANES), axis=0, keepdims=True)
    e2 = jnp.exp(v2 - v1)
    gate1 = p_grp * (1.0 / (1.0 + e2))
    gate2 = p_grp * (e2 / (1.0 + e2))

    chosen = jnp.logical_or(sub == i1, sub == i2)
    onehot = jnp.concatenate(
        [jnp.where(jnp.logical_and(grp == g, chosen), 1.0, 0.0) for g in range(N_GROUPS)], axis=0)
    running = cnt_s[...]
    ranks = []
    for c in range(rows // RANK_CHUNK):
        oh = onehot[:, c * RANK_CHUNK:(c + 1) * RANK_CHUNK]
        ranks.append(_dot(oh.astype(BF16), tri_ref[...]) + running)
        running = running + jnp.sum(oh, axis=1, keepdims=True)
    rank = jnp.concatenate(ranks, axis=1)
    cnt_s[...] = running
    counts_ref[...] = jnp.broadcast_to(running, (N_EXPERTS, LANES))

    ex1 = grp * EXP_PER_GROUP + i1
    ex2 = grp * EXP_PER_GROUP + i2
    erow = lax.broadcasted_iota(jnp.int32, (N_EXPERTS, rows), 0)
    rank1 = jnp.sum(jnp.where(erow == ex1, rank, 0.0), axis=0, keepdims=True)
    rank2 = jnp.sum(jnp.where(erow == ex2, rank, 0.0), axis=0, keepdims=True)
    fields = {_R_E1: ex1.astype(F32), _R_E2: ex2.astype(F32), _R_RANK1: rank1, _R_RANK2: rank2,
              _R_G1: gate1, _R_G2: gate2}
    record = jnp.zeros((SUBLANES, rows), F32)
    for r, val in fields.items():
        record = jnp.where(sub == r, val, record)
    route_ref[...] = record
    padded = jnp.concatenate([record, jnp.zeros((LANES - SUBLANES, rows), F32)], axis=0)
    gcol_ref[...] = padded.T


def _mixer(x, cconv, ck, cv, counts_in, prm, consts, *, nseq, tl, mask_history, alpha):
    nb_total, t_total = x.shape[0], x.shape[1]
    nb, ns = nb_total // nseq, t_total // tl
    rows = nseq * tl
    n_tok = nb_total * t_total
    w_in, convw, convb, sinks, gconv, gattn, w_out, ln1g, ln1b, wr, br = prm
    abias, tri, keysel = consts

    def full(a):
        return pl.BlockSpec(a.shape, lambda b, s, _n=a.ndim: (0,) * _n)

    def seq_state(width, nrows):
        return pl.BlockSpec((nseq, nrows, width), lambda b, s: (b, 0, 0))

    in_specs = [
        pl.BlockSpec((nseq, tl, D_MODEL), lambda b, s: (b, s, 0)),
        seq_state(D_CONV, CONV_W - 1), seq_state(D_KV, WINDOW), seq_state(D_KV, WINDOW),
        full(w_in), full(convw), full(convb), full(abias), full(sinks), full(gconv), full(gattn),
        full(w_out), full(ln1g), full(ln1b), full(wr), full(br), full(tri), full(keysel), full(counts_in),
    ]
    out_shape = [
        jax.ShapeDtypeStruct((n_tok * ROW_TILES, LANES), F32),
        jax.ShapeDtypeStruct((SUBLANES, n_tok), F32),
        jax.ShapeDtypeStruct((n_tok, LANES), F32),
        jax.ShapeDtypeStruct((N_EXPERTS, LANES), F32),
        jax.ShapeDtypeStruct((nb_total, CONV_W - 1, D_CONV), F32),
        jax.ShapeDtypeStruct((nb_total, WINDOW, D_KV), F32),
        jax.ShapeDtypeStruct((nb_total, WINDOW, D_KV), F32),
    ]
    out_specs = [
        pl.BlockSpec((rows * ROW_TILES, LANES), lambda b, s: (b * ns + s, 0)),
        pl.BlockSpec((SUBLANES, rows), lambda b, s: (0, b * ns + s)),
        pl.BlockSpec((rows, LANES), lambda b, s: (b * ns + s, 0)),
        pl.BlockSpec((N_EXPERTS, LANES), lambda b, s: (0, 0)),
        seq_state(D_CONV, CONV_W - 1), seq_state(D_KV, WINDOW), seq_state(D_KV, WINDOW),
    ]
    scratch = [
        pltpu.VMEM((nseq * (tl + SUBLANES), D_CONV), F32),
        pltpu.VMEM((nseq * (WINDOW + tl), D_KV), F32),
        pltpu.VMEM((nseq * (WINDOW + tl), D_KV), F32),
        pltpu.VMEM((N_EXPERTS, 1), F32),
        pltpu.VMEM((rows // CHUNK * QROWS, 2 * KEYS), F32),
        pltpu.VMEM((rows // CHUNK * QROWS, 2 * KEYS), BF16),
        pltpu.VMEM((rows // CHUNK * QROWS, D_KV), F32),
        pltpu.VMEM((rows // CHUNK * QROWS, LANES), F32),
        pltpu.VMEM((rows // CHUNK * QROWS, LANES), F32),
        pltpu.VMEM((rows, _OFF_Q), F32),
    ]
    kern = functools.partial(_mixer_kernel, nseq=nseq, tl=tl, alpha=alpha, mask_history=mask_history)
    return pl.pallas_call(
        kern, grid=(nb, ns), in_specs=in_specs, out_specs=out_specs, out_shape=out_shape,
        scratch_shapes=scratch, name="mixer",
        compiler_params=pltpu.CompilerParams(dimension_semantics=("arbitrary", "arbitrary"),
                                             vmem_limit_bytes=VMEM_LIMIT),
    )(x, cconv, ck, cv, w_in, convw, convb, abias, sinks, gconv, gattn, w_out, ln1g, ln1b, wr, br, tri, keysel, counts_in)


_PAD_PIECES = tuple(2 ** k for k in range(EXPERT_BLOCK.bit_length() - 2, -1, -1))


def _tile_rows(row, n=1):
    return pl.ds(pl.multiple_of(row * ROW_TILES, ROW_TILES), n * ROW_TILES)


def _row_copy(src, src_row, dst, dst_row, sem):
    return pltpu.make_async_copy(src.at[_tile_rows(src_row), :], dst.at[_tile_rows(dst_row), :], sem)


def _dispatch_kernel(zstart_ref, zcount_ref, tail_ref, pos1_ref, pos2_ref, xa_ref, xb_ref, xs_hbm, zeros_v, sem, zsem,
                     *, tb, nba):
    step = pl.program_id(0)

    def zero_copy(start, piece):
        return pltpu.make_async_copy(zeros_v.at[_tile_rows(0, piece), :], xs_hbm.at[_tile_rows(start, piece), :], zsem)

    def pad_copy(e, piece, taken):
        return zero_copy(zstart_ref[e] + taken, piece)

    def for_each_pad_piece(fn):
        def per_tail_block(j, carry):
            for part in range(EXPERT_BLOCK // _PAD_PIECES[0]):
                fn(zero_copy(tail_ref[0] + j * EXPERT_BLOCK + part * _PAD_PIECES[0], _PAD_PIECES[0]))
            return carry

        lax.fori_loop(0, tail_ref[1], per_tail_block, 0)

        def per_expert(e, carry):
            count = zcount_ref[e]
            taken = 0
            for piece in _PAD_PIECES:
                present = (count & piece) != 0

                @pl.when(present)
                def _(piece=piece, taken=taken):
                    fn(pad_copy(e, piece, taken))

                taken = taken + jnp.where(present, piece, 0)
            return carry

        lax.fori_loop(0, N_EXPERTS, per_expert, 0)

    @pl.when(step == 0)
    def _():
        zeros_v[...] = jnp.zeros_like(zeros_v)
        for_each_pad_piece(lambda c: c.start())

    def issue_rows(x_ref):
        def issue(t, carry):
            _row_copy(x_ref, t, xs_hbm, pos1_ref[t], sem).start(priority=0)
            _row_copy(x_ref, t, xs_hbm, pos2_ref[t], sem).start(priority=1)
            return carry

        lax.fori_loop(0, tb, issue, 0, unroll=8)

    @pl.when(step < nba)
    def _():
        issue_rows(xa_ref)

    @pl.when(step >= nba)
    def _():
        issue_rows(xb_ref)

    pltpu.make_async_copy(xs_hbm.at[_tile_rows(0, 2 * tb), :], xs_hbm.at[_tile_rows(0, 2 * tb), :], sem).wait()

    @pl.when(step == 0)
    def _():
        for_each_pad_piece(lambda c: c.wait())


def _dispatch(xa_rows, xb_rows, pos1, pos2, zstart, zcount, tail, n_slots, *, tb):
    nba, nbb = xa_rows.shape[0] // ROW_TILES // tb, xb_rows.shape[0] // ROW_TILES // tb
    kern = functools.partial(_dispatch_kernel, tb=tb, nba=nba)
    grid_spec = pltpu.PrefetchScalarGridSpec(
        num_scalar_prefetch=3, grid=(nba + nbb,),
        in_specs=[pl.BlockSpec((tb,), lambda i, *_: (i,), memory_space=pltpu.SMEM),
                  pl.BlockSpec((tb,), lambda i, *_: (i,), memory_space=pltpu.SMEM),
                  pl.BlockSpec((tb * ROW_TILES, LANES), lambda i, *_: (jnp.minimum(i, nba - 1), 0)),
                  pl.BlockSpec((tb * ROW_TILES, LANES), lambda i, *_: (jnp.maximum(i - nba, 0), 0))],
        out_specs=pl.BlockSpec(memory_space=pl.ANY),
        scratch_shapes=[pltpu.VMEM((_PAD_PIECES[0] * ROW_TILES, LANES), F32),
                        pltpu.SemaphoreType.DMA(()), pltpu.SemaphoreType.DMA(())],
    )
    return pl.pallas_call(
        kern, grid_spec=grid_spec, out_shape=jax.ShapeDtypeStruct((n_slots * ROW_TILES, LANES), F32),
        name="dispatch",
        compiler_params=pltpu.CompilerParams(dimension_semantics=("arbitrary",)),
    )(zstart, zcount, tail, pos1, pos2, xa_rows, xb_rows)


def _expert_kernel(be_ref, src_ref, used_ref, xs_ref, wg_ref, wu_ref, wd_ref, yb_ref, wg_b, wu_b, wd_b):
    i = pl.program_id(0)
    e = be_ref[i]
    prev = be_ref[jnp.maximum(i - 1, 0)]

    @pl.when(jnp.logical_or(i == 0, e != prev))
    def _():
        wg_b[...] = wg_ref[...].astype(BF16)
        wu_b[...] = wu_ref[...].astype(BF16)
        wd_b[...] = wd_ref[...].astype(BF16)

    @pl.when(used_ref[i] != 0)
    def _():
        x = _load_rows(xs_ref, EXPERT_BLOCK).astype(BF16)
        g = _dot(x, wg_b[...])
        u = _dot(x, wu_b[...])
        hid = (g / (1.0 + jnp.exp(-g))) * u
        _store_rows(yb_ref, _dot(hid.astype(BF16), wd_b[...]), EXPERT_BLOCK)

    @pl.when(used_ref[i] == 0)
    def _():
        yb_ref[...] = jnp.zeros_like(yb_ref)


def _experts(xs, blk_e, blk_src, blk_used, w_gate, w_up, w_down, layer):
    n_slots = xs.shape[0] // ROW_TILES
    nblk = n_slots // EXPERT_BLOCK

    def wspec(shape):
        return pl.BlockSpec((None, None) + shape, lambda i, be, src, used: (layer, be[i], 0, 0))

    grid_spec = pltpu.PrefetchScalarGridSpec(
        num_scalar_prefetch=3, grid=(nblk,),
        in_specs=[pl.BlockSpec((EXPERT_BLOCK * ROW_TILES, LANES), lambda i, be, src, used: (src[i], 0)),
                  wspec((D_MODEL, D_EXP)), wspec((D_MODEL, D_EXP)), wspec((D_EXP, D_MODEL))],
        out_specs=pl.BlockSpec((EXPERT_BLOCK * ROW_TILES, LANES), lambda i, be, src, used: (i, 0)),
        scratch_shapes=[pltpu.VMEM((D_MODEL, D_EXP), BF16), pltpu.VMEM((D_MODEL, D_EXP), BF16),
                        pltpu.VMEM((D_EXP, D_MODEL), BF16)],
    )
    return pl.pallas_call(
        _expert_kernel, grid_spec=grid_spec, out_shape=jax.ShapeDtypeStruct((n_slots * ROW_TILES, LANES), F32),
        name="experts",
        compiler_params=pltpu.CompilerParams(dimension_semantics=("arbitrary",), vmem_limit_bytes=VMEM_LIMIT),
    )(blk_e, blk_src, blk_used, xs, w_gate, w_up, w_down)


def _combine_kernel(pos1_ref, pos2_ref, next1_ref, next2_ref, xa_ref, xb_ref, ga_ref, gb_ref, yb_hbm, g_ref, b_ref,
                    outa_ref, outb_ref, ybuf, sems, *, tb, alpha, nba, nsteps):
    step = pl.program_id(0)
    slot = step % 2

    def gather(p1_ref, p2_ref, to_slot):
        def issue(t, carry):
            _row_copy(yb_hbm, p1_ref[t], ybuf.at[to_slot].at[0], t, sems.at[to_slot]).start(priority=0)
            _row_copy(yb_hbm, p2_ref[t], ybuf.at[to_slot].at[1], t, sems.at[to_slot]).start(priority=1)
            return carry

        lax.fori_loop(0, tb, issue, 0, unroll=8)

    @pl.when(step == 0)
    def _():
        gather(pos1_ref, pos2_ref, 0)

    @pl.when(step + 1 < nsteps)
    def _():
        gather(next1_ref, next2_ref, 1 - slot)

    for k in range(TOP_K):
        pltpu.make_async_copy(yb_hbm.at[_tile_rows(0, tb), :], ybuf.at[slot].at[k], sems.at[slot]).wait()

    def finish(x1_ref, gcol_ref, out_ref):
        gates = gcol_ref[...]
        y = (gates[:, _R_G1:_R_G1 + 1] * _load_rows(ybuf.at[slot].at[0], tb)
             + gates[:, _R_G2:_R_G2 + 1] * _load_rows(ybuf.at[slot].at[1], tb))
        out_ref[...] = _layer_norm(alpha * _load_rows(x1_ref, tb) + y, g_ref[...], b_ref[...])

    @pl.when(step < nba)
    def _():
        finish(xa_ref, ga_ref, outa_ref)

    @pl.when(step >= nba)
    def _():
        finish(xb_ref, gb_ref, outb_ref)


def _combine(xa_rows, xb_rows, gcol_a, gcol_b, yb, pos1, pos2, ln2g, ln2b, *, tb, alpha):
    nba, nbb = xa_rows.shape[0] // ROW_TILES // tb, xb_rows.shape[0] // ROW_TILES // tb
    nsteps = nba + nbb
    kern = functools.partial(_combine_kernel, tb=tb, alpha=alpha, nba=nba, nsteps=nsteps)

    def first(i):
        return jnp.minimum(i, nba - 1)

    def second(i):
        return jnp.maximum(i - nba, 0)

    def nxt(i):
        return jnp.minimum(i + 1, nsteps - 1)

    smem = functools.partial(pl.BlockSpec, (tb,), memory_space=pltpu.SMEM)
    return pl.pallas_call(
        kern, grid=(nsteps,),
        in_specs=[smem(lambda i: (i,)), smem(lambda i: (i,)), smem(lambda i: (nxt(i),)), smem(lambda i: (nxt(i),)),
                  pl.BlockSpec((tb * ROW_TILES, LANES), lambda i: (first(i), 0)),
                  pl.BlockSpec((tb * ROW_TILES, LANES), lambda i: (second(i), 0)),
                  pl.BlockSpec((tb, LANES), lambda i: (first(i), 0)),
                  pl.BlockSpec((tb, LANES), lambda i: (second(i), 0)),
                  pl.BlockSpec(memory_space=pl.ANY),
                  pl.BlockSpec((1, D_MODEL), lambda i: (0, 0)),
                  pl.BlockSpec((1, D_MODEL), lambda i: (0, 0))],
        out_specs=[pl.BlockSpec((tb, D_MODEL), lambda i: (first(i), 0)),
                   pl.BlockSpec((tb, D_MODEL), lambda i: (second(i), 0))],
        out_shape=[jax.ShapeDtypeStruct((nba * tb, D_MODEL), F32), jax.ShapeDtypeStruct((nbb * tb, D_MODEL), F32)],
        scratch_shapes=[pltpu.VMEM((2, TOP_K, tb * ROW_TILES, LANES), F32), pltpu.SemaphoreType.DMA((2,))],
        name="combine",
        compiler_params=pltpu.CompilerParams(dimension_semantics=("arbitrary",), vmem_limit_bytes=VMEM_LIMIT),
    )(pos1, pos2, pos1, pos2, xa_rows, xb_rows, gcol_a, gcol_b, yb, ln2g, ln2b)


def _attn_column_order():
    g, kvh, d = np.meshgrid(np.arange(GROUP), np.arange(N_KV_HEADS), np.arange(HEAD_DIM), indexing='ij')
    return ((kvh * GROUP + g) * HEAD_DIM + d).reshape(-1)


def _alibi_bias():
    slopes = np.asarray([2.0 ** (-8.0 * (h + 1) / N_HEADS) for h in range(N_HEADS)], np.float32)
    qi = np.arange(CHUNK, dtype=np.int32)[:, None]
    sj = np.arange(KEYS, dtype=np.int32)[None, :]
    dist = np.abs(qi + WINDOW - sj).astype(np.float32)
    bias = -slopes.reshape(N_KV_HEADS, GROUP, 1, 1) * dist
    bias = np.transpose(bias, (1, 2, 0, 3)).reshape(GROUP * CHUNK, N_KV_HEADS * KEYS)
    return jnp.asarray(bias, F32)


def _key_selector():
    row_head = np.arange(N_KV_HEADS * KEYS)[:, None] // KEYS
    lane_head = np.arange(D_KV)[None, :] // HEAD_DIM
    return jnp.asarray((row_head == lane_head).astype(np.float32), BF16)


def _strict_upper():
    r = np.arange(RANK_CHUNK)
    return jnp.asarray((r[:, None] < r[None, :]).astype(np.float32), BF16)


def _slot_plan(route, counts, n_tok):
    nblk = (n_tok * TOP_K) // EXPERT_BLOCK + N_EXPERTS
    experts = route[_R_E1:_R_E2 + 1].astype(jnp.int32)
    ranks = route[_R_RANK1:_R_RANK2 + 1].astype(jnp.int32)
    cnt = counts[:, 0].astype(jnp.int32)
    padded = (cnt + EXPERT_BLOCK - 1) // EXPERT_BLOCK * EXPERT_BLOCK
    pad_end = jnp.cumsum(padded)
    pad_start = pad_end - padded
    ids = jnp.arange(N_EXPERTS, dtype=jnp.int32)
    pos = jnp.sum(jnp.where(experts[..., None] == ids, pad_start, 0), axis=-1) + ranks
    blk_first = jnp.arange(nblk, dtype=jnp.int32) * EXPERT_BLOCK
    blk_e = jnp.minimum(jnp.sum((pad_end[None, :] <= blk_first[:, None]).astype(jnp.int32), axis=-1), N_EXPERTS - 1)
    blk_used = (blk_first < pad_end[-1]).astype(jnp.int32)
    blk_src = jnp.minimum(jnp.arange(nblk, dtype=jnp.int32), pad_end[-1] // EXPERT_BLOCK - 1)
    tail = jnp.stack([pad_end[-1], nblk - pad_end[-1] // EXPERT_BLOCK]).astype(jnp.int32)
    return pos[0], pos[1], pad_start + cnt, padded - cnt, tail, blk_e, blk_src, blk_used, nblk * EXPERT_BLOCK


def _layer(xp, xs, cache, mix_prm, consts, moe_prm, layer, *, tl_prompt, nseq_sample, alpha):
    (bp, tp, _), (bs, ts, _) = xp.shape, xs.shape
    np_tok, ns_tok = bp * tp, bs * ts
    tb = ROWS
    assert np_tok % tb == 0 and ns_tok % tb == 0
    zeros_conv = jnp.zeros((bp, CONV_W - 1, D_CONV), F32)
    zeros_kv = jnp.zeros((bp, WINDOW, D_KV), F32)
    zero_counts = jnp.zeros((N_EXPERTS, LANES), F32)
    x1p, route_p, gcol_p, counts_p, *state_p = _mixer(
        xp, zeros_conv, zeros_kv, zeros_kv, zero_counts, mix_prm, consts,
        nseq=1, tl=tl_prompt, mask_history=True, alpha=alpha)
    x1s, route_s, gcol_s, counts, *state_s = _mixer(
        xs, *cache, counts_p, mix_prm, consts, nseq=nseq_sample, tl=ts, mask_history=False, alpha=alpha)
    route = jnp.concatenate([route_p, route_s], axis=1)
    pos1, pos2, zstart, zcount, tail, blk_e, blk_src, blk_used, n_slots = _slot_plan(route, counts, np_tok + ns_tok)
    sorted_rows = _dispatch(x1p, x1s, pos1, pos2, zstart, zcount, tail, n_slots, tb=tb)
    w_gate, w_up, w_down, ln2g, ln2b = moe_prm
    yb = _experts(sorted_rows, blk_e, blk_src, blk_used, w_gate, w_up, w_down, layer)
    out_p, out_s = _combine(x1p, x1s, gcol_p, gcol_s, yb, pos1, pos2, ln2g, ln2b, tb=tb, alpha=alpha)
    return out_p.reshape(bp, tp, D_MODEL), out_s.reshape(bs, ts, D_MODEL), state_p, state_s


def kernel(x_prompt, x_sample, cache_conv, cache_k, cache_v, w_in, conv_w, conv_b, attn_sinks, g_conv, g_attn,
           w_out, ln1_g, ln1_b, router_group_w, router_group_b, router_expert_w, router_expert_b,
           expert_w_gate, expert_w_up, expert_w_down, ln2_g, ln2_b):
    depth = w_in.shape[0]
    alpha = (2 * depth) ** 0.25
    batch, seq = x_prompt.shape[0], x_prompt.shape[1]
    dec_batch, dec_seq = x_sample.shape[0], x_sample.shape[1]
    assert dec_seq == CHUNK and seq % CHUNK == 0
    consts = (_alibi_bias(), _strict_upper(), _key_selector())
    attn_order = _attn_column_order()
    tl_prompt = min(MIXER_ROWS, seq)
    nseq_sample = min(MIXER_ROWS // dec_seq, dec_batch)
    assert seq % tl_prompt == 0 and dec_batch % nseq_sample == 0

    xp, xs = x_prompt, x_sample
    states = [[] for _ in range(6)]
    for l in range(depth):
        sink_col = jnp.broadcast_to(jnp.repeat(attn_sinks[l].astype(F32), CHUNK).reshape(N_KV_HEADS, QROWS, 1),
                                    (N_KV_HEADS, QROWS, LANES))
        wr = jnp.zeros((D_MODEL, LANES), F32)
        wr = wr.at[:, 0:N_EXPERTS].set(router_expert_w[l]).at[:, N_EXPERTS:N_EXPERTS + N_GROUPS].set(router_group_w[l])
        br = jnp.zeros((1, LANES), F32)
        br = br.at[0, 0:N_EXPERTS].set(router_expert_b[l]).at[0, N_EXPERTS:N_EXPERTS + N_GROUPS].set(router_group_b[l])
        w_in_l = w_in[l].at[:, _OFF_Q:_OFF_Q + D_ATTN].set(w_in[l][:, _OFF_Q + attn_order])
        w_out_l = w_out[l].at[D_CONV:D_CONV + D_ATTN, :].set(w_out[l][D_CONV + attn_order, :])
        mix_prm = (w_in_l.astype(BF16), conv_w[l], conv_b[l].reshape(1, D_CONV), sink_col,
                   g_conv[l].reshape(1, D_CONV), g_attn[l][attn_order].reshape(1, D_ATTN), w_out_l.astype(BF16),
                   ln1_g[l].reshape(1, D_MODEL), ln1_b[l].reshape(1, D_MODEL), wr.astype(BF16), br)
        moe_prm = (expert_w_gate, expert_w_up, expert_w_down,
                   ln2_g[l].reshape(1, D_MODEL), ln2_b[l].reshape(1, D_MODEL))
        cache = (cache_conv[l], cache_k[l].reshape(dec_batch, WINDOW, D_KV), cache_v[l].reshape(dec_batch, WINDOW, D_KV))
        xp, xs, state_p, state_s = _layer(xp, xs, cache, mix_prm, consts, moe_prm, l,
                                          tl_prompt=tl_prompt, nseq_sample=nseq_sample, alpha=alpha)
        for lst, val in zip(states, state_p + state_s):
            lst.append(val)

    def kv(lst, nb):
        return jnp.stack(lst).reshape(depth, nb, WINDOW, N_KV_HEADS, HEAD_DIM)

    return (xp, xs, jnp.stack(states[0]), kv(states[1], batch), kv(states[2], batch),
            jnp.stack(states[3]), kv(states[4], dec_batch), kv(states[5], dec_batch))
```

```python
import functools

import numpy as np
import jax
import jax.numpy as jnp
from jax import lax
from jax.experimental import pallas as pl
from jax.experimental.pallas import tpu as pltpu

D_MODEL = 1024
D_CONV = 512
CONV_W = 3
N_HEADS = 8
N_KV_HEADS = 2
GROUP = N_HEADS // N_KV_HEADS
HEAD_DIM = 64
D_ATTN = N_HEADS * HEAD_DIM
D_KV = N_KV_HEADS * HEAD_DIM
WINDOW = 128
CHUNK = 64
KEYS = WINDOW + CHUNK
QROWS = GROUP * CHUNK
N_GROUPS = 4
EXP_PER_GROUP = 8
N_EXPERTS = N_GROUPS * EXP_PER_GROUP
TOP_K = 2
D_EXP = 512
LN_EPS = 1e-5
NEG_INF = -1e30

SUBLANES = 8
LANES = 128
ROW_TILES = D_MODEL // LANES
assert ROW_TILES == SUBLANES
ROWS = 512
MIXER_ROWS = 512
EXPERT_BLOCK = 512
RANK_CHUNK = 256
SOFTMAX_ROWS = 64
PROJ_CHUNK = 2 * LANES
MIX_PIECES = 4
VMEM_LIMIT = 56 * 1024 * 1024

F32 = jnp.float32
BF16 = jnp.bfloat16

_OFF_B, _OFF_C, _OFF_H = 0, D_CONV, 2 * D_CONV
_OFF_Q = 3 * D_CONV
_OFF_K = _OFF_Q + D_ATTN
_OFF_V = _OFF_K + D_KV
D_IN = _OFF_V + D_KV

_R_E1, _R_E2, _R_RANK1, _R_RANK2, _R_G1, _R_G2 = 0, 1, 2, 3, 4, 5


def _load_rows(ref, n):
    return jnp.concatenate([ref[pl.ds(j, n, stride=ROW_TILES), :] for j in range(ROW_TILES)], axis=1)


def _store_rows(ref, val, n):
    for j in range(ROW_TILES):
        ref[pl.ds(j, n, stride=ROW_TILES), :] = val[:, j * LANES:(j + 1) * LANES]


def _dot(a, b):
    return jnp.dot(a, b, preferred_element_type=F32)


def _rms_norm(x, g):
    return x * lax.rsqrt(jnp.mean(jnp.square(x), -1, keepdims=True) + LN_EPS) * g


def _layer_norm(x, g, b):
    mu = jnp.mean(x, -1, keepdims=True)
    xc = x - mu
    var = jnp.mean(jnp.square(xc), -1, keepdims=True)
    return xc * lax.rsqrt(var + LN_EPS) * g + b


def _mixer_kernel(x_ref, cconv_ref, ck_ref, cv_ref, w_in_ref, convw_ref, convb_ref, abias_ref,
                  sink_ref, gconv_ref, gattn_ref, w_out_ref, ln1g_ref, ln1b_ref, wr_ref, br_ref,
                  tri_ref, keysel_ref, counts_in_ref,
                  x1_ref, route_ref, gcol_ref, counts_ref, sconv_ref, sk_ref, sv_ref,
                  uext, kext, vext, cnt_s, s_ref, e_ref, sinkden_ref, m0_ref, m1_ref, proj_ref,
                  *, nseq, tl, alpha, mask_history):
    b = pl.program_id(0)
    s = pl.program_id(1)
    rows = nseq * tl
    upitch = tl + SUBLANES
    kpitch = WINDOW + tl

    @pl.when(jnp.logical_and(b == 0, s == 0))
    def _():
        cnt_s[...] = counts_in_ref[:, 0:1]

    @pl.when(s == 0)
    def _():
        for j in range(nseq):
            uext[j * upitch + SUBLANES - 2:j * upitch + SUBLANES, :] = cconv_ref[j]
            kext[j * kpitch:j * kpitch + WINDOW, :] = ck_ref[j]
            vext[j * kpitch:j * kpitch + WINDOW, :] = cv_ref[j]

    x = x_ref[...].reshape(rows, D_MODEL)
    xb = x.astype(BF16)

    def conv_rows(r0, r1):
        w0, w1, w2 = convw_ref[0:1, :], convw_ref[1:2, :], convw_ref[2:3, :]
        out = []
        for j in range(r0 // tl, -(-r1 // tl)):
            a, b_ = max(r0, j * tl) - j * tl, min(r1, (j + 1) * tl) - j * tl
            base = j * upitch + SUBLANES
            src = pl.ds(j * tl + a, b_ - a)
            uext[base + a:base + b_, :] = proj_ref[src, _OFF_C:_OFF_C + D_CONV] * proj_ref[src, _OFF_H:_OFF_H + D_CONV]
            yc = convb_ref[...] + uext[base + a - 2:base + b_ - 2, :] * w0
            yc = yc + uext[base + a - 1:base + b_ - 1, :] * w1
            yc = yc + uext[base + a:base + b_, :] * w2
            out.append(_rms_norm(proj_ref[src, _OFF_B:_OFF_B + D_CONV] * yc, gconv_ref[...]).astype(BF16))
            if b_ == tl:
                tail = uext[base + tl - 2:base + tl, :]
                sconv_ref[j] = tail
                uext[base - 2:base, :] = tail
        return out[0] if len(out) == 1 else jnp.concatenate(out, axis=0)

    q = (_dot(xb, w_in_ref[:, _OFF_Q:_OFF_Q + D_ATTN]) * (HEAD_DIM ** -0.5)).astype(BF16)
    k = _dot(xb, w_in_ref[:, _OFF_K:_OFF_K + D_KV])
    v = _dot(xb, w_in_ref[:, _OFF_V:_OFF_V + D_KV])
    for j in range(nseq):
        kext[j * kpitch + WINDOW:(j + 1) * kpitch, :] = k[j * tl:(j + 1) * tl]
        vext[j * kpitch + WINDOW:(j + 1) * kpitch, :] = v[j * tl:(j + 1) * tl]

    nchunk = rows // CHUNK
    assert nseq == 1 or tl == CHUNK
    key_stride = CHUNK if nseq == 1 else kpitch
    head0 = lax.broadcasted_iota(jnp.int32, (1, D_KV), 1) < HEAD_DIM

    def windows(ext_ref):
        ext = ext_ref[...].astype(BF16)
        heads = (jnp.where(head0, ext, jnp.zeros_like(ext)), jnp.where(head0, jnp.zeros_like(ext), ext))
        return jnp.stack([jnp.concatenate([h[c * key_stride:c * key_stride + KEYS] for h in heads], axis=0)
                          for c in range(nchunk)])

    q3 = jnp.stack([jnp.concatenate([q[c * CHUNK:(c + 1) * CHUNK, g * D_KV:(g + 1) * D_KV] for g in range(GROUP)], axis=0)
                    for c in range(nchunk)])
    logits = lax.dot_general(q3, windows(kext), (((2,), (2,)), ((0,), (0,))), preferred_element_type=F32)
    s_ref[...] = logits.reshape(nchunk * QROWS, 2 * KEYS)

    assert 2 * KEYS == 3 * LANES
    mid0 = lax.broadcasted_iota(jnp.int32, (1, LANES), 1) < KEYS - LANES
    colk = lax.broadcasted_iota(jnp.int32, (1, 2 * KEYS), 1)
    key = jnp.where(colk < KEYS, colk, colk - KEYS)

    def row_block(i):
        r = i * SOFTMAX_ROWS
        return pl.ds(r, SOFTMAX_ROWS), pl.ds(r % QROWS, SOFTMAX_ROWS), r // QROWS

    def tiles(lg):
        return lg[:, 0:LANES], lg[:, LANES:2 * LANES], lg[:, 2 * LANES:3 * LANES]

    def max_rows(i, masked):
        rows_i, qrows_i, chunk_i = row_block(i)
        lg = s_ref[rows_i, :] + abias_ref[qrows_i, :]
        if masked:
            lg = lg + jnp.where(s * tl + chunk_i * CHUNK - WINDOW + key < 0, NEG_INF, 0.0).astype(F32)
        s_ref[rows_i, :] = lg
        t0, t1, t2 = tiles(lg)
        m0_ref[rows_i, :] = jnp.maximum(jnp.max(jnp.maximum(t0, jnp.where(mid0, t1, NEG_INF)), -1, keepdims=True),
                                        sink_ref[0, qrows_i, :])
        m1_ref[rows_i, :] = jnp.maximum(jnp.max(jnp.maximum(t2, jnp.where(mid0, NEG_INF, t1)), -1, keepdims=True),
                                        sink_ref[1, qrows_i, :])

    def exp_rows(i):
        rows_i, qrows_i, _ = row_block(i)
        t0, t1, t2 = tiles(s_ref[rows_i, :])
        m0, m1 = m0_ref[rows_i, :], m1_ref[rows_i, :]
        e = jnp.concatenate([jnp.exp(t0 - m0), jnp.exp(t1 - jnp.where(mid0, m0, m1)), jnp.exp(t2 - m1)], axis=-1)
        e_ref[rows_i, :] = e.astype(BF16)
        sinkden_ref[rows_i, :] = jnp.where(head0, jnp.exp(sink_ref[0, qrows_i, :] - m0),
                                           jnp.exp(sink_ref[1, qrows_i, :] - m1))

    n_it = nchunk * QROWS // SOFTMAX_ROWS
    n_masked = min(WINDOW // CHUNK, nchunk) * QROWS // SOFTMAX_ROWS if mask_history else 0
    proj_chunks = [(c, min(c + PROJ_CHUNK, _OFF_Q)) for c in range(0, _OFF_Q, PROJ_CHUNK)]
    piece = -(-2 * n_it // len(proj_chunks))
    steps = [functools.partial(max_rows, i, i < n_masked) for i in range(n_it)]
    steps += [functools.partial(exp_rows, i) for i in range(n_it)]
    for p, (c0, c1) in enumerate(proj_chunks):
        for step_fn in steps[p * piece:(p + 1) * piece]:
            step_fn()
        proj_ref[:, c0:c1] = _dot(xb, w_in_ref[:, c0:c1])
    for step_fn in steps[len(proj_chunks) * piece:]:
        step_fn()
    vsel = jnp.concatenate([windows(vext), jnp.broadcast_to(keysel_ref[...], (nchunk, 2 * KEYS, D_KV))], axis=-1)
    n_pieces = min(MIX_PIECES, nchunk)
    cpp = nchunk // n_pieces
    conv_parts, attn_parts = [], []
    for p in range(n_pieces):
        cs = slice(p * cpp, (p + 1) * cpp)
        o2 = lax.dot_general(e_ref[p * cpp * QROWS:(p + 1) * cpp * QROWS, :].reshape(cpp, QROWS, 2 * KEYS), vsel[cs],
                             (((2,), (1,)), ((0,), (0,))), preferred_element_type=F32)
        sinkden = sinkden_ref[p * cpp * QROWS:(p + 1) * cpp * QROWS, :].reshape(cpp, QROWS, D_KV)
        o = o2[..., 0:D_KV] * (1.0 / (o2[..., D_KV:2 * D_KV] + sinkden))
        attn_parts += [jnp.concatenate([o[c, g * CHUNK:(g + 1) * CHUNK, :] for g in range(GROUP)], axis=1)
                       for c in range(cpp)]
        conv_parts.append(conv_rows(p * cpp * CHUNK, (p + 1) * cpp * CHUNK))
    y_attn = jnp.concatenate(attn_parts, axis=0)
    n_conv = jnp.concatenate(conv_parts, axis=0)

    for j in range(nseq):
        sk_ref[j] = kext[j * kpitch + tl:(j + 1) * kpitch, :]
        sv_ref[j] = vext[j * kpitch + tl:(j + 1) * kpitch, :]
    if nseq == 1:
        kext[0:WINDOW, :] = kext[tl:tl + WINDOW, :]
        vext[0:WINDOW, :] = vext[tl:tl + WINDOW, :]

    n_attn = _rms_norm(y_attn, gattn_ref[...]).astype(BF16)
    mixed = _dot(n_attn, w_out_ref[D_CONV:D_CONV + D_ATTN, :]) + _dot(n_conv, w_out_ref[0:D_CONV, :])
    x1 = _layer_norm(alpha * x + mixed, ln1g_ref[...], ln1b_ref[...])
    _store_rows(x1_ref, x1, rows)

    logits_t = (_dot(x1.astype(BF16), wr_ref[...]) + br_ref[...]).T
    sub = lax.broadcasted_iota(jnp.int32, (SUBLANES, rows), 0)
    gl = jnp.where(sub < N_GROUPS, logits_t[N_EXPERTS:N_EXPERTS + SUBLANES, :], -jnp.inf)
    gmax = jnp.max(gl, axis=0, keepdims=True)
    grp = jnp.min(jnp.where(gl == gmax, sub, SUBLANES), axis=0, keepdims=True)
    p_grp = 1.0 / jnp.sum(jnp.exp(gl - gmax), axis=0, keepdims=True)
    el = logits_t[(N_GROUPS - 1) * EXP_PER_GROUP:N_GROUPS * EXP_PER_GROUP, :]
    for g in range(N_GROUPS - 2, -1, -1):
        el = jnp.where(grp == g, logits_t[g * EXP_PER_GROUP:(g + 1) * EXP_PER_GROUP, :], el)
    v1 = jnp.max(el, axis=0, keepdims=True)
    i1 = jnp.min(jnp.where(el == v1, sub, SUBLANES), axis=0, keepdims=True)
    el2 = jnp.where(sub == i1, -jnp.inf, el)
    v2 = jnp.max(el2, axis=0, keepdims=True)
    i2 = jnp.min(jnp.where(el2 == v2, sub, SUBLANES), axis=0, keepdims=True)
    e2 = jnp.exp(v2 - v1)
    gate1 = p_grp * (1.0 / (1.0 + e2))
    gate2 = p_grp * (e2 / (1.0 + e2))

    chosen = jnp.logical_or(sub == i1, sub == i2)
    onehot = jnp.concatenate(
        [jnp.where(jnp.logical_and(grp == g, chosen), 1.0, 0.0) for g in range(N_GROUPS)], axis=0)
    running = cnt_s[...]
    ranks = []
    for c in range(rows // RANK_CHUNK):
        oh = onehot[:, c * RANK_CHUNK:(c + 1) * RANK_CHUNK]
        ranks.append(_dot(oh.astype(BF16), tri_ref[...]) + running)
        running = running + jnp.sum(oh, axis=1, keepdims=True)
    rank = jnp.concatenate(ranks, axis=1)
    cnt_s[...] = running
    counts_ref[...] = jnp.broadcast_to(running, (N_EXPERTS, LANES))

    ex1 = grp * EXP_PER_GROUP + i1
    ex2 = grp * EXP_PER_GROUP + i2
    erow = lax.broadcasted_iota(jnp.int32, (N_EXPERTS, rows), 0)
    rank1 = jnp.sum(jnp.where(erow == ex1, rank, 0.0), axis=0, keepdims=True)
    rank2 = jnp.sum(jnp.where(erow == ex2, rank, 0.0), axis=0, keepdims=True)
    fields = {_R_E1: ex1.astype(F32), _R_E2: ex2.astype(F32), _R_RANK1: rank1, _R_RANK2: rank2,
              _R_G1: gate1, _R_G2: gate2}
    record = jnp.zeros((SUBLANES, rows), F32)
    for r, val in fields.items():
        record = jnp.where(sub == r, val, record)
    route_ref[...] = record
    padded = jnp.concatenate([record, jnp.zeros((LANES - SUBLANES, rows), F32)], axis=0)
    gcol_ref[...] = padded.T


def _mixer(x, cconv, ck, cv, counts_in, prm, consts, *, nseq, tl, mask_history, alpha):
    nb_total, t_total = x.shape[0], x.shape[1]
    nb, ns = nb_total // nseq, t_total // tl
    rows = nseq * tl
    n_tok = nb_total * t_total
    w_in, convw, convb, sinks, gconv, gattn, w_out, ln1g, ln1b, wr, br = prm
    abias, tri, keysel = consts

    def full(a):
        return pl.BlockSpec(a.shape, lambda b, s, _n=a.ndim: (0,) * _n)

    def seq_state(width, nrows):
        return pl.BlockSpec((nseq, nrows, width), lambda b, s: (b, 0, 0))

    in_specs = [
        pl.BlockSpec((nseq, tl, D_MODEL), lambda b, s: (b, s, 0)),
        seq_state(D_CONV, CONV_W - 1), seq_state(D_KV, WINDOW), seq_state(D_KV, WINDOW),
        full(w_in), full(convw), full(convb), full(abias), full(sinks), full(gconv), full(gattn),
        full(w_out), full(ln1g), full(ln1b), full(wr), full(br), full(tri), full(keysel), full(counts_in),
    ]
    out_shape = [
        jax.ShapeDtypeStruct((n_tok * ROW_TILES, LANES), F32),
        jax.ShapeDtypeStruct((SUBLANES, n_tok), F32),
        jax.ShapeDtypeStruct((n_tok, LANES), F32),
        jax.ShapeDtypeStruct((N_EXPERTS, LANES), F32),
        jax.ShapeDtypeStruct((nb_total, CONV_W - 1, D_CONV), F32),
        jax.ShapeDtypeStruct((nb_total, WINDOW, D_KV), F32),
        jax.ShapeDtypeStruct((nb_total, WINDOW, D_KV), F32),
    ]
    out_specs = [
        pl.BlockSpec((rows * ROW_TILES, LANES), lambda b, s: (b * ns + s, 0)),
        pl.BlockSpec((SUBLANES, rows), lambda b, s: (0, b * ns + s)),
        pl.BlockSpec((rows, LANES), lambda b, s: (b * ns + s, 0)),
        pl.BlockSpec((N_EXPERTS, LANES), lambda b, s: (0, 0)),
        seq_state(D_CONV, CONV_W - 1), seq_state(D_KV, WINDOW), seq_state(D_KV, WINDOW),
    ]
    scratch = [
        pltpu.VMEM((nseq * (tl + SUBLANES), D_CONV), F32),
        pltpu.VMEM((nseq * (WINDOW + tl), D_KV), F32),
        pltpu.VMEM((nseq * (WINDOW + tl), D_KV), F32),
        pltpu.VMEM((N_EXPERTS, 1), F32),
        pltpu.VMEM((rows // CHUNK * QROWS, 2 * KEYS), F32),
        pltpu.VMEM((rows // CHUNK * QROWS, 2 * KEYS), BF16),
        pltpu.VMEM((rows // CHUNK * QROWS, D_KV), F32),
        pltpu.VMEM((rows // CHUNK * QROWS, LANES), F32),
        pltpu.VMEM((rows // CHUNK * QROWS, LANES), F32),
        pltpu.VMEM((rows, _OFF_Q), F32),
    ]
    kern = functools.partial(_mixer_kernel, nseq=nseq, tl=tl, alpha=alpha, mask_history=mask_history)
    return pl.pallas_call(
        kern, grid=(nb, ns), in_specs=in_specs, out_specs=out_specs, out_shape=out_shape,
        scratch_shapes=scratch, name="mixer",
        compiler_params=pltpu.CompilerParams(dimension_semantics=("arbitrary", "arbitrary"),
                                             vmem_limit_bytes=VMEM_LIMIT),
    )(x, cconv, ck, cv, w_in, convw, convb, abias, sinks, gconv, gattn, w_out, ln1g, ln1b, wr, br, tri, keysel, counts_in)


_PAD_PIECES = tuple(2 ** k for k in range(EXPERT_BLOCK.bit_length() - 2, -1, -1))


def _tile_rows(row, n=1):
    return pl.ds(pl.multiple_of(row * ROW_TILES, ROW_TILES), n * ROW_TILES)


def _row_copy(src, src_row, dst, dst_row, sem):
    return pltpu.make_async_copy(src.at[_tile_rows(src_row), :], dst.at[_tile_rows(dst_row), :], sem)


def _dispatch_kernel(zstart_ref, zcount_ref, tail_ref, pos1_ref, pos2_ref, xa_ref, xb_ref, xs_hbm, zeros_v, sem, zsem,
                     *, tb, nba):
    step = pl.program_id(0)

    def zero_copy(start, piece):
        return pltpu.make_async_copy(zeros_v.at[_tile_rows(0, piece), :], xs_hbm.at[_tile_rows(start, piece), :], zsem)

    def pad_copy(e, piece, taken):
        return zero_copy(zstart_ref[e] + taken, piece)

    def for_each_pad_piece(fn):
        def per_tail_block(j, carry):
            for part in range(EXPERT_BLOCK // _PAD_PIECES[0]):
                fn(zero_copy(tail_ref[0] + j * EXPERT_BLOCK + part * _PAD_PIECES[0], _PAD_PIECES[0]))
            return carry

        lax.fori_loop(0, tail_ref[1], per_tail_block, 0)

        def per_expert(e, carry):
            count = zcount_ref[e]
            taken = 0
            for piece in _PAD_PIECES:
                present = (count & piece) != 0

                @pl.when(present)
                def _(piece=piece, taken=taken):
                    fn(pad_copy(e, piece, taken))

                taken = taken + jnp.where(present, piece, 0)
            return carry

        lax.fori_loop(0, N_EXPERTS, per_expert, 0)

    @pl.when(step == 0)
    def _():
        zeros_v[...] = jnp.zeros_like(zeros_v)
        for_each_pad_piece(lambda c: c.start())

    def issue_rows(x_ref):
        def issue(t, carry):
            _row_copy(x_ref, t, xs_hbm, pos1_ref[t], sem).start(priority=0)
            _row_copy(x_ref, t, xs_hbm, pos2_ref[t], sem).start(priority=1)
            return carry

        lax.fori_loop(0, tb, issue, 0, unroll=8)

    @pl.when(step < nba)
    def _():
        issue_rows(xa_ref)

    @pl.when(step >= nba)
    def _():
        issue_rows(xb_ref)

    pltpu.make_async_copy(xs_hbm.at[_tile_rows(0, 2 * tb), :], xs_hbm.at[_tile_rows(0, 2 * tb), :], sem).wait()

    @pl.when(step == 0)
    def _():
        for_each_pad_piece(lambda c: c.wait())


def _dispatch(xa_rows, xb_rows, pos1, pos2, zstart, zcount, tail, n_slots, *, tb):
    nba, nbb = xa_rows.shape[0] // ROW_TILES // tb, xb_rows.shape[0] // ROW_TILES // tb
    kern = functools.partial(_dispatch_kernel, tb=tb, nba=nba)
    grid_spec = pltpu.PrefetchScalarGridSpec(
        num_scalar_prefetch=3, grid=(nba + nbb,),
        in_specs=[pl.BlockSpec((tb,), lambda i, *_: (i,), memory_space=pltpu.SMEM),
                  pl.BlockSpec((tb,), lambda i, *_: (i,), memory_space=pltpu.SMEM),
                  pl.BlockSpec((tb * ROW_TILES, LANES), lambda i, *_: (jnp.minimum(i, nba - 1), 0)),
                  pl.BlockSpec((tb * ROW_TILES, LANES), lambda i, *_: (jnp.maximum(i - nba, 0), 0))],
        out_specs=pl.BlockSpec(memory_space=pl.ANY),
        scratch_shapes=[pltpu.VMEM((_PAD_PIECES[0] * ROW_TILES, LANES), F32),
                        pltpu.SemaphoreType.DMA(()), pltpu.SemaphoreType.DMA(())],
    )
    return pl.pallas_call(
        kern, grid_spec=grid_spec, out_shape=jax.ShapeDtypeStruct((n_slots * ROW_TILES, LANES), F32),
        name="dispatch",
        compiler_params=pltpu.CompilerParams(dimension_semantics=("arbitrary",)),
    )(zstart, zcount, tail, pos1, pos2, xa_rows, xb_rows)


def _expert_kernel(be_ref, src_ref, used_ref, xs_ref, wg_ref, wu_ref, wd_ref, yb_ref, wg_b, wu_b, wd_b):
    i = pl.program_id(0)
    e = be_ref[i]
    prev = be_ref[jnp.maximum(i - 1, 0)]

    @pl.when(jnp.logical_or(i == 0, e != prev))
    def _():
        wg_b[...] = wg_ref[...].astype(BF16)
        wu_b[...] = wu_ref[...].astype(BF16)
        wd_b[...] = wd_ref[...].astype(BF16)

    @pl.when(used_ref[i] != 0)
    def _():
        x = _load_rows(xs_ref, EXPERT_BLOCK).astype(BF16)
        g = _dot(x, wg_b[...])
        u = _dot(x, wu_b[...])
        hid = (g / (1.0 + jnp.exp(-g))) * u
        _store_rows(yb_ref, _dot(hid.astype(BF16), wd_b[...]), EXPERT_BLOCK)

    @pl.when(used_ref[i] == 0)
    def _():
        yb_ref[...] = jnp.zeros_like(yb_ref)


def _experts(xs, blk_e, blk_src, blk_used, w_gate, w_up, w_down, layer):
    n_slots = xs.shape[0] // ROW_TILES
    nblk = n_slots // EXPERT_BLOCK

    def wspec(shape):
        return pl.BlockSpec((None, None) + shape, lambda i, be, src, used: (layer, be[i], 0, 0))

    grid_spec = pltpu.PrefetchScalarGridSpec(
        num_scalar_prefetch=3, grid=(nblk,),
        in_specs=[pl.BlockSpec((EXPERT_BLOCK * ROW_TILES, LANES), lambda i, be, src, used: (src[i], 0)),
                  wspec((D_MODEL, D_EXP)), wspec((D_MODEL, D_EXP)), wspec((D_EXP, D_MODEL))],
        out_specs=pl.BlockSpec((EXPERT_BLOCK * ROW_TILES, LANES), lambda i, be, src, used: (i, 0)),
        scratch_shapes=[pltpu.VMEM((D_MODEL, D_EXP), BF16), pltpu.VMEM((D_MODEL, D_EXP), BF16),
                        pltpu.VMEM((D_EXP, D_MODEL), BF16)],
    )
    return pl.pallas_call(
        _expert_kernel, grid_spec=grid_spec, out_shape=jax.ShapeDtypeStruct((n_slots * ROW_TILES, LANES), F32),
        name="experts",
        compiler_params=pltpu.CompilerParams(dimension_semantics=("arbitrary",), vmem_limit_bytes=VMEM_LIMIT),
    )(blk_e, blk_src, blk_used, xs, w_gate, w_up, w_down)


def _combine_kernel(pos1_ref, pos2_ref, next1_ref, next2_ref, xa_ref, xb_ref, ga_ref, gb_ref, yb_hbm, g_ref, b_ref,
                    outa_ref, outb_ref, ybuf, sems, *, tb, alpha, nba, nsteps):
    step = pl.program_id(0)
    slot = step % 2

    def gather(p1_ref, p2_ref, to_slot):
        def issue(t, carry):
            _row_copy(yb_hbm, p1_ref[t], ybuf.at[to_slot].at[0], t, sems.at[to_slot]).start(priority=0)
            _row_copy(yb_hbm, p2_ref[t], ybuf.at[to_slot].at[1], t, sems.at[to_slot]).start(priority=1)
            return carry

        lax.fori_loop(0, tb, issue, 0, unroll=8)

    @pl.when(step == 0)
    def _():
        gather(pos1_ref, pos2_ref, 0)

    @pl.when(step + 1 < nsteps)
    def _():
        gather(next1_ref, next2_ref, 1 - slot)

    for k in range(TOP_K):
        pltpu.make_async_copy(yb_hbm.at[_tile_rows(0, tb), :], ybuf.at[slot].at[k], sems.at[slot]).wait()

    def finish(x1_ref, gcol_ref, out_ref):
        gates = gcol_ref[...]
        y = (gates[:, _R_G1:_R_G1 + 1] * _load_rows(ybuf.at[slot].at[0], tb)
             + gates[:, _R_G2:_R_G2 + 1] * _load_rows(ybuf.at[slot].at[1], tb))
        out_ref[...] = _layer_norm(alpha * _load_rows(x1_ref, tb) + y, g_ref[...], b_ref[...])

    @pl.when(step < nba)
    def _():
        finish(xa_ref, ga_ref, outa_ref)

    @pl.when(step >= nba)
    def _():
        finish(xb_ref, gb_ref, outb_ref)


def _combine(xa_rows, xb_rows, gcol_a, gcol_b, yb, pos1, pos2, ln2g, ln2b, *, tb, alpha):
    nba, nbb = xa_rows.shape[0] // ROW_TILES // tb, xb_rows.shape[0] // ROW_TILES // tb
    nsteps = nba + nbb
    kern = functools.partial(_combine_kernel, tb=tb, alpha=alpha, nba=nba, nsteps=nsteps)

    def first(i):
        return jnp.minimum(i, nba - 1)

    def second(i):
        return jnp.maximum(i - nba, 0)

    def nxt(i):
        return jnp.minimum(i + 1, nsteps - 1)

    smem = functools.partial(pl.BlockSpec, (tb,), memory_space=pltpu.SMEM)
    return pl.pallas_call(
        kern, grid=(nsteps,),
        in_specs=[smem(lambda i: (i,)), smem(lambda i: (i,)), smem(lambda i: (nxt(i),)), smem(lambda i: (nxt(i),)),
                  pl.BlockSpec((tb * ROW_TILES, LANES), lambda i: (first(i), 0)),
                  pl.BlockSpec((tb * ROW_TILES, LANES), lambda i: (second(i), 0)),
                  pl.BlockSpec((tb, LANES), lambda i: (first(i), 0)),
                  pl.BlockSpec((tb, LANES), lambda i: (second(i), 0)),
                  pl.BlockSpec(memory_space=pl.ANY),
                  pl.BlockSpec((1, D_MODEL), lambda i: (0, 0)),
                  pl.BlockSpec((1, D_MODEL), lambda i: (0, 0))],
        out_specs=[pl.BlockSpec((tb, D_MODEL), lambda i: (first(i), 0)),
                   pl.BlockSpec((tb, D_MODEL), lambda i: (second(i), 0))],
        out_shape=[jax.ShapeDtypeStruct((nba * tb, D_MODEL), F32), jax.ShapeDtypeStruct((nbb * tb, D_MODEL), F32)],
        scratch_shapes=[pltpu.VMEM((2, TOP_K, tb * ROW_TILES, LANES), F32), pltpu.SemaphoreType.DMA((2,))],
        name="combine",
        compiler_params=pltpu.CompilerParams(dimension_semantics=("arbitrary",), vmem_limit_bytes=VMEM_LIMIT),
    )(pos1, pos2, pos1, pos2, xa_rows, xb_rows, gcol_a, gcol_b, yb, ln2g, ln2b)


def _attn_column_order():
    g, kvh, d = np.meshgrid(np.arange(GROUP), np.arange(N_KV_HEADS), np.arange(HEAD_DIM), indexing='ij')
    return ((kvh * GROUP + g) * HEAD_DIM + d).reshape(-1)


def _alibi_bias():
    slopes = np.asarray([2.0 ** (-8.0 * (h + 1) / N_HEADS) for h in range(N_HEADS)], np.float32)
    qi = np.arange(CHUNK, dtype=np.int32)[:, None]
    sj = np.arange(KEYS, dtype=np.int32)[None, :]
    dist = np.abs(qi + WINDOW - sj).astype(np.float32)
    bias = -slopes.reshape(N_KV_HEADS, GROUP, 1, 1) * dist
    bias = np.transpose(bias, (1, 2, 0, 3)).reshape(GROUP * CHUNK, N_KV_HEADS * KEYS)
    return jnp.asarray(bias, F32)


def _key_selector():
    row_head = np.arange(N_KV_HEADS * KEYS)[:, None] // KEYS
    lane_head = np.arange(D_KV)[None, :] // HEAD_DIM
    return jnp.asarray((row_head == lane_head).astype(np.float32), BF16)


def _strict_upper():
    r = np.arange(RANK_CHUNK)
    return jnp.asarray((r[:, None] < r[None, :]).astype(np.float32), BF16)


def _slot_plan(route, counts, n_tok):
    nblk = (n_tok * TOP_K) // EXPERT_BLOCK + N_EXPERTS
    experts = route[_R_E1:_R_E2 + 1].astype(jnp.int32)
    ranks = route[_R_RANK1:_R_RANK2 + 1].astype(jnp.int32)
    cnt = counts[:, 0].astype(jnp.int32)
    padded = (cnt + EXPERT_BLOCK - 1) // EXPERT_BLOCK * EXPERT_BLOCK
    pad_end = jnp.cumsum(padded)
    pad_start = pad_end - padded
    ids = jnp.arange(N_EXPERTS, dtype=jnp.int32)
    pos = jnp.sum(jnp.where(experts[..., None] == ids, pad_start, 0), axis=-1) + ranks
    blk_first = jnp.arange(nblk, dtype=jnp.int32) * EXPERT_BLOCK
    blk_e = jnp.minimum(jnp.sum((pad_end[None, :] <= blk_first[:, None]).astype(jnp.int32), axis=-1), N_EXPERTS - 1)
    blk_used = (blk_first < pad_end[-1]).astype(jnp.int32)
    blk_src = jnp.minimum(jnp.arange(nblk, dtype=jnp.int32), pad_end[-1] // EXPERT_BLOCK - 1)
    tail = jnp.stack([pad_end[-1], nblk - pad_end[-1] // EXPERT_BLOCK]).astype(jnp.int32)
    return pos[0], pos[1], pad_start + cnt, padded - cnt, tail, blk_e, blk_src, blk_used, nblk * EXPERT_BLOCK


def _layer(xp, xs, cache, mix_prm, consts, moe_prm, layer, *, tl_prompt, nseq_sample, alpha):
    (bp, tp, _), (bs, ts, _) = xp.shape, xs.shape
    np_tok, ns_tok = bp * tp, bs * ts
    tb = ROWS
    assert np_tok % tb == 0 and ns_tok % tb == 0
    zeros_conv = jnp.zeros((bp, CONV_W - 1, D_CONV), F32)
    zeros_kv = jnp.zeros((bp, WINDOW, D_KV), F32)
    zero_counts = jnp.zeros((N_EXPERTS, LANES), F32)
    x1p, route_p, gcol_p, counts_p, *state_p = _mixer(
        xp, zeros_conv, zeros_kv, zeros_kv, zero_counts, mix_prm, consts,
        nseq=1, tl=tl_prompt, mask_history=True, alpha=alpha)
    x1s, route_s, gcol_s, counts, *state_s = _mixer(
        xs, *cache, counts_p, mix_prm, consts, nseq=nseq_sample, tl=ts, mask_history=False, alpha=alpha)
    route = jnp.concatenate([route_p, route_s], axis=1)
    pos1, pos2, zstart, zcount, tail, blk_e, blk_src, blk_used, n_slots = _slot_plan(route, counts, np_tok + ns_tok)
    sorted_rows = _dispatch(x1p, x1s, pos1, pos2, zstart, zcount, tail, n_slots, tb=tb)
    w_gate, w_up, w_down, ln2g, ln2b = moe_prm
    yb = _experts(sorted_rows, blk_e, blk_src, blk_used, w_gate, w_up, w_down, layer)
    out_p, out_s = _combine(x1p, x1s, gcol_p, gcol_s, yb, pos1, pos2, ln2g, ln2b, tb=tb, alpha=alpha)
    return out_p.reshape(bp, tp, D_MODEL), out_s.reshape(bs, ts, D_MODEL), state_p, state_s


def kernel(x_prompt, x_sample, cache_conv, cache_k, cache_v, w_in, conv_w, conv_b, attn_sinks, g_conv, g_attn,
           w_out, ln1_g, ln1_b, router_group_w, router_group_b, router_expert_w, router_expert_b,
           expert_w_gate, expert_w_up, expert_w_down, ln2_g, ln2_b):
    depth = w_in.shape[0]
    alpha = (2 * depth) ** 0.25
    batch, seq = x_prompt.shape[0], x_prompt.shape[1]
    dec_batch, dec_seq = x_sample.shape[0], x_sample.shape[1]
    assert dec_seq == CHUNK and seq % CHUNK == 0
    consts = (_alibi_bias(), _strict_upper(), _key_selector())
    attn_order = _attn_column_order()
    tl_prompt = min(MIXER_ROWS, seq)
    nseq_sample = min(MIXER_ROWS // dec_seq, dec_batch)
    assert seq % tl_prompt == 0 and dec_batch % nseq_sample == 0

    xp, xs = x_prompt, x_sample
    states = [[] for _ in range(6)]
    for l in range(depth):
        sink_col = jnp.broadcast_to(jnp.repeat(attn_sinks[l].astype(F32), CHUNK).reshape(N_KV_HEADS, QROWS, 1),
                                    (N_KV_HEADS, QROWS, LANES))
        wr = jnp.zeros((D_MODEL, LANES), F32)
        wr = wr.at[:, 0:N_EXPERTS].set(router_expert_w[l]).at[:, N_EXPERTS:N_EXPERTS + N_GROUPS].set(router_group_w[l])
        br = jnp.zeros((1, LANES), F32)
        br = br.at[0, 0:N_EXPERTS].set(router_expert_b[l]).at[0, N_EXPERTS:N_EXPERTS + N_GROUPS].set(router_group_b[l])
        w_in_l = w_in[l].at[:, _OFF_Q:_OFF_Q + D_ATTN].set(w_in[l][:, _OFF_Q + attn_order])
        w_out_l = w_out[l].at[D_CONV:D_CONV + D_ATTN, :].set(w_out[l][D_CONV + attn_order, :])
        mix_prm = (w_in_l.astype(BF16), conv_w[l], conv_b[l].reshape(1, D_CONV), sink_col,
                   g_conv[l].reshape(1, D_CONV), g_attn[l][attn_order].reshape(1, D_ATTN), w_out_l.astype(BF16),
                   ln1_g[l].reshape(1, D_MODEL), ln1_b[l].reshape(1, D_MODEL), wr.astype(BF16), br)
        moe_prm = (expert_w_gate, expert_w_up, expert_w_down,
                   ln2_g[l].reshape(1, D_MODEL), ln2_b[l].reshape(1, D_MODEL))
        cache = (cache_conv[l], cache_k[l].reshape(dec_batch, WINDOW, D_KV), cache_v[l].reshape(dec_batch, WINDOW, D_KV))
        xp, xs, state_p, state_s = _layer(xp, xs, cache, mix_prm, consts, moe_prm, l,
                                          tl_prompt=tl_prompt, nseq_sample=nseq_sample, alpha=alpha)
        for lst, val in zip(states, state_p + state_s):
            lst.append(val)

    def kv(lst, nb):
        return jnp.stack(lst).reshape(depth, nb, WINDOW, N_KV_HEADS, HEAD_DIM)

    return (xp, xs, jnp.stack(states[0]), kv(states[1], batch), kv(states[2], batch),
            jnp.stack(states[3]), kv(states[4], dec_batch), kv(states[5], dec_batch))
```

```python
import functools

import numpy as np
import jax
import jax.numpy as jnp
from jax import lax
from jax.experimental import pallas as pl
from jax.experimental.pallas import tpu as pltpu

D_MODEL = 1024
D_CONV = 512
CONV_W = 3
N_HEADS = 8
N_KV_HEADS = 2
GROUP = N_HEADS // N_KV_HEADS
HEAD_DIM = 64
D_ATTN = N_HEADS * HEAD_DIM
D_KV = N_KV_HEADS * HEAD_DIM
WINDOW = 128
CHUNK = 64
KEYS = WINDOW + CHUNK
QROWS = GROUP * CHUNK
N_GROUPS = 4
EXP_PER_GROUP = 8
N_EXPERTS = N_GROUPS * EXP_PER_GROUP
TOP_K = 2
D_EXP = 512
LN_EPS = 1e-5
NEG_INF = -1e30

SUBLANES = 8
LANES = 128
ROW_TILES = D_MODEL // LANES
assert ROW_TILES == SUBLANES
ROWS = 512
MIXER_ROWS = 512
EXPERT_BLOCK = 512
RANK_CHUNK = 256
SOFTMAX_ROWS = 64
PROJ_CHUNK = 2 * LANES
MIX_PIECES = 4
VMEM_LIMIT = 56 * 1024 * 1024

F32 = jnp.float32
BF16 = jnp.bfloat16

_OFF_B, _OFF_C, _OFF_H = 0, D_CONV, 2 * D_CONV
_OFF_Q = 3 * D_CONV
_OFF_K = _OFF_Q + D_ATTN
_OFF_V = _OFF_K + D_KV
D_IN = _OFF_V + D_KV

_R_E1, _R_E2, _R_RANK1, _R_RANK2, _R_G1, _R_G2 = 0, 1, 2, 3, 4, 5


def _load_rows(ref, n):
    return jnp.concatenate([ref[pl.ds(j, n, stride=ROW_TILES), :] for j in range(ROW_TILES)], axis=1)


def _store_rows(ref, val, n):
    for j in range(ROW_TILES):
        ref[pl.ds(j, n, stride=ROW_TILES), :] = val[:, j * LANES:(j + 1) * LANES]


def _dot(a, b):
    return jnp.dot(a, b, preferred_element_type=F32)


def _rms_norm(x, g):
    return x * lax.rsqrt(jnp.mean(jnp.square(x), -1, keepdims=True) + LN_EPS) * g


def _layer_norm(x, g, b):
    mu = jnp.mean(x, -1, keepdims=True)
    xc = x - mu
    var = jnp.mean(jnp.square(xc), -1, keepdims=True)
    return xc * lax.rsqrt(var + LN_EPS) * g + b


def _mixer_kernel(x_ref, cconv_ref, ck_ref, cv_ref, w_in_ref, convw_ref, convb_ref, abias_ref,
                  sink_ref, gconv_ref, gattn_ref, w_out_ref, ln1g_ref, ln1b_ref, wr_ref, br_ref,
                  tri_ref, keysel_ref, counts_in_ref,
                  x1_ref, route_ref, gcol_ref, counts_ref, sconv_ref, sk_ref, sv_ref,
                  uext, kext, vext, cnt_s, s_ref, e_ref, sinkden_ref, m0_ref, m1_ref, proj_ref, x1b_s, xprev_s,
                  mixed_s, *, nseq, tl, alpha, mask_history, defer_tail):
    b = pl.program_id(0)
    s = pl.program_id(1)
    rows = nseq * tl
    upitch = tl + SUBLANES
    kpitch = WINDOW + tl

    @pl.when(jnp.logical_and(b == 0, s == 0))
    def _():
        cnt_s[...] = counts_in_ref[:, 0:1]
        if defer_tail:
            xprev_s[...] = jnp.zeros_like(xprev_s)
            mixed_s[...] = jnp.zeros_like(mixed_s)

    @pl.when(s == 0)
    def _():
        for j in range(nseq):
            uext[j * upitch + SUBLANES - 2:j * upitch + SUBLANES, :] = cconv_ref[j]
            kext[j * kpitch:j * kpitch + WINDOW, :] = ck_ref[j]
            vext[j * kpitch:j * kpitch + WINDOW, :] = cv_ref[j]

    def norm_rows(r0, r1, x_rows, mixed_rows):
        x1 = _layer_norm(alpha * x_rows + mixed_rows, ln1g_ref[...], ln1b_ref[...])
        for j in range(ROW_TILES):
            x1_ref[pl.ds(r0 * ROW_TILES + j, r1 - r0, stride=ROW_TILES), :] = x1[:, j * LANES:(j + 1) * LANES]
        x1b_s[r0:r1, :] = x1.astype(BF16)

    def route(valid):
        logits_t = (_dot(x1b_s[...], wr_ref[...]) + br_ref[...]).T
        sub = lax.broadcasted_iota(jnp.int32, (SUBLANES, rows), 0)
        gl = jnp.where(sub < N_GROUPS, logits_t[N_EXPERTS:N_EXPERTS + SUBLANES, :], -jnp.inf)
        gmax = jnp.max(gl, axis=0, keepdims=True)
        grp = jnp.min(jnp.where(gl == gmax, sub, SUBLANES), axis=0, keepdims=True)
        p_grp = 1.0 / jnp.sum(jnp.exp(gl - gmax), axis=0, keepdims=True)
        el = logits_t[(N_GROUPS - 1) * EXP_PER_GROUP:N_GROUPS * EXP_PER_GROUP, :]
        for g in range(N_GROUPS - 2, -1, -1):
            el = jnp.where(grp == g, logits_t[g * EXP_PER_GROUP:(g + 1) * EXP_PER_GROUP, :], el)
        v1 = jnp.max(el, axis=0, keepdims=True)
        i1 = jnp.min(jnp.where(el == v1, sub, SUBLANES), axis=0, keepdims=True)
        el2 = jnp.where(sub == i1, -jnp.inf, el)
        v2 = jnp.max(el2, axis=0, keepdims=True)
        i2 = jnp.min(jnp.where(el2 == v2, sub, SUBLANES), axis=0, keepdims=True)
        e2 = jnp.exp(v2 - v1)
        gate1 = p_grp * (1.0 / (1.0 + e2))
        gate2 = p_grp * (e2 / (1.0 + e2))

        chosen = jnp.logical_or(sub == i1, sub == i2)
        onehot = jnp.concatenate(
            [jnp.where(jnp.logical_and(grp == g, chosen), 1.0, 0.0) for g in range(N_GROUPS)], axis=0)
        running = cnt_s[...]
        ranks = []
        for c in range(rows // RANK_CHUNK):
            oh = onehot[:, c * RANK_CHUNK:(c + 1) * RANK_CHUNK]
            ranks.append(_dot(oh.astype(BF16), tri_ref[...]) + running)
            running = running + jnp.sum(oh, axis=1, keepdims=True) * valid
        rank = jnp.concatenate(ranks, axis=1)
        cnt_s[...] = running
        counts_ref[...] = jnp.broadcast_to(running, (N_EXPERTS, LANES))

        ex1 = grp * EXP_PER_GROUP + i1
        ex2 = grp * EXP_PER_GROUP + i2
        erow = lax.broadcasted_iota(jnp.int32, (N_EXPERTS, rows), 0)
        rank1 = jnp.sum(jnp.where(erow == ex1, rank, 0.0), axis=0, keepdims=True)
        rank2 = jnp.sum(jnp.where(erow == ex2, rank, 0.0), axis=0, keepdims=True)
        fields = {_R_E1: ex1.astype(F32), _R_E2: ex2.astype(F32), _R_RANK1: rank1, _R_RANK2: rank2,
                  _R_G1: gate1, _R_G2: gate2}
        record = jnp.zeros((SUBLANES, rows), F32)
        for r, val in fields.items():
            record = jnp.where(sub == r, val, record)
        route_ref[...] = record
        padded = jnp.concatenate([record, jnp.zeros((LANES - SUBLANES, rows), F32)], axis=0)
        gcol_ref[...] = padded.T

    x = x_ref[...].reshape(rows, D_MODEL)
    xb = x.astype(BF16)

    def conv_rows(r0, r1):
        w0, w1, w2 = convw_ref[0:1, :], convw_ref[1:2, :], convw_ref[2:3, :]
        out = []
        for j in range(r0 // tl, -(-r1 // tl)):
            a, b_ = max(r0, j * tl) - j * tl, min(r1, (j + 1) * tl) - j * tl
            base = j * upitch + SUBLANES
            src = pl.ds(j * tl + a, b_ - a)
            uext[base + a:base + b_, :] = proj_ref[src, _OFF_C:_OFF_C + D_CONV] * proj_ref[src, _OFF_H:_OFF_H + D_CONV]
            yc = convb_ref[...] + uext[base + a - 2:base + b_ - 2, :] * w0
            yc = yc + uext[base + a - 1:base + b_ - 1, :] * w1
            yc = yc + uext[base + a:base + b_, :] * w2
            out.append(_rms_norm(proj_ref[src, _OFF_B:_OFF_B + D_CONV] * yc, gconv_ref[...]).astype(BF16))
            if b_ == tl:
                tail = uext[base + tl - 2:base + tl, :]
                sconv_ref[j] = tail
                uext[base - 2:base, :] = tail
        return out[0] if len(out) == 1 else jnp.concatenate(out, axis=0)

    qkv = []
    qkv_chunks = [(c, c + PROJ_CHUNK) for c in range(_OFF_Q, D_IN, PROJ_CHUNK)]
    tail_rows = -(-rows // len(qkv_chunks) // SUBLANES) * SUBLANES
    for p, (c0, c1) in enumerate(qkv_chunks):
        r0, r1 = min(p * tail_rows, rows), min((p + 1) * tail_rows, rows)
        if defer_tail and r1 > r0:
            norm_rows(r0, r1, xprev_s[r0:r1, :], mixed_s[r0:r1, :])
        qkv.append(_dot(xb, w_in_ref[:, c0:c1]))
    qkv = jnp.concatenate(qkv, axis=1)
    if defer_tail:
        route((s > 0).astype(F32))
    q = (qkv[:, 0:D_ATTN] * (HEAD_DIM ** -0.5)).astype(BF16)
    k = qkv[:, D_ATTN:D_ATTN + D_KV]
    v = qkv[:, D_ATTN + D_KV:D_ATTN + 2 * D_KV]
    for j in range(nseq):
        kext[j * kpitch + WINDOW:(j + 1) * kpitch, :] = k[j * tl:(j + 1) * tl]
        vext[j * kpitch + WINDOW:(j + 1) * kpitch, :] = v[j * tl:(j + 1) * tl]

    nchunk = rows // CHUNK
    assert nseq == 1 or tl == CHUNK
    key_stride = CHUNK if nseq == 1 else kpitch
    head0 = lax.broadcasted_iota(jnp.int32, (1, D_KV), 1) < HEAD_DIM

    def windows(ext_ref):
        ext = ext_ref[...].astype(BF16)
        heads = (jnp.where(head0, ext, jnp.zeros_like(ext)), jnp.where(head0, jnp.zeros_like(ext), ext))
        return jnp.stack([jnp.concatenate([h[c * key_stride:c * key_stride + KEYS] for h in heads], axis=0)
                          for c in range(nchunk)])

    q3 = jnp.stack([jnp.concatenate([q[c * CHUNK:(c + 1) * CHUNK, g * D_KV:(g + 1) * D_KV] for g in range(GROUP)], axis=0)
                    for c in range(nchunk)])
    logits = lax.dot_general(q3, windows(kext), (((2,), (2,)), ((0,), (0,))), preferred_element_type=F32)
    s_ref[...] = logits.reshape(nchunk * QROWS, 2 * KEYS)

    assert 2 * KEYS == 3 * LANES
    mid0 = lax.broadcasted_iota(jnp.int32, (1, LANES), 1) < KEYS - LANES
    colk = lax.broadcasted_iota(jnp.int32, (1, 2 * KEYS), 1)
    key = jnp.where(colk < KEYS, colk, colk - KEYS)

    def row_block(i):
        r = i * SOFTMAX_ROWS
        return pl.ds(r, SOFTMAX_ROWS), pl.ds(r % QROWS, SOFTMAX_ROWS), r // QROWS

    def tiles(lg):
        return lg[:, 0:LANES], lg[:, LANES:2 * LANES], lg[:, 2 * LANES:3 * LANES]

    def max_rows(i, masked):
        rows_i, qrows_i, chunk_i = row_block(i)
        lg = s_ref[rows_i, :] + abias_ref[qrows_i, :]
        if masked:
            lg = lg + jnp.where(s * tl + chunk_i * CHUNK - WINDOW + key < 0, NEG_INF, 0.0).astype(F32)
        s_ref[rows_i, :] = lg
        t0, t1, t2 = tiles(lg)
        m0_ref[rows_i, :] = jnp.maximum(jnp.max(jnp.maximum(t0, jnp.where(mid0, t1, NEG_INF)), -1, keepdims=True),
                                        sink_ref[0, qrows_i, :])
        m1_ref[rows_i, :] = jnp.maximum(jnp.max(jnp.maximum(t2, jnp.where(mid0, NEG_INF, t1)), -1, keepdims=True),
                                        sink_ref[1, qrows_i, :])

    def exp_rows(i):
        rows_i, qrows_i, _ = row_block(i)
        t0, t1, t2 = tiles(s_ref[rows_i, :])
        m0, m1 = m0_ref[rows_i, :], m1_ref[rows_i, :]
        e = jnp.concatenate([jnp.exp(t0 - m0), jnp.exp(t1 - jnp.where(mid0, m0, m1)), jnp.exp(t2 - m1)], axis=-1)
        e_ref[rows_i, :] = e.astype(BF16)
        sinkden_ref[rows_i, :] = jnp.where(head0, jnp.exp(sink_ref[0, qrows_i, :] - m0),
                                           jnp.exp(sink_ref[1, qrows_i, :] - m1))

    n_it = nchunk * QROWS // SOFTMAX_ROWS
    n_masked = min(WINDOW // CHUNK, nchunk) * QROWS // SOFTMAX_ROWS if mask_history else 0
    proj_chunks = [(c, min(c + PROJ_CHUNK, _OFF_Q)) for c in range(0, _OFF_Q, PROJ_CHUNK)]
    piece = -(-2 * n_it // len(proj_chunks))
    steps = [functools.partial(max_rows, i, i < n_masked) for i in range(n_it)]
    steps += [functools.partial(exp_rows, i) for i in range(n_it)]
    for p, (c0, c1) in enumerate(proj_chunks):
        for step_fn in steps[p * piece:(p + 1) * piece]:
            step_fn()
        proj_ref[:, c0:c1] = _dot(xb, w_in_ref[:, c0:c1])
    for step_fn in steps[len(proj_chunks) * piece:]:
        step_fn()
    vsel = jnp.concatenate([windows(vext), jnp.broadcast_to(keysel_ref[...], (nchunk, 2 * KEYS, D_KV))], axis=-1)
    n_pieces = min(MIX_PIECES, nchunk)
    cpp = nchunk // n_pieces
    conv_parts, attn_parts = [], []
    for p in range(n_pieces):
        cs = slice(p * cpp, (p + 1) * cpp)
        o2 = lax.dot_general(e_ref[p * cpp * QROWS:(p + 1) * cpp * QROWS, :].reshape(cpp, QROWS, 2 * KEYS), vsel[cs],
                             (((2,), (1,)), ((0,), (0,))), preferred_element_type=F32)
        sinkden = sinkden_ref[p * cpp * QROWS:(p + 1) * cpp * QROWS, :].reshape(cpp, QROWS, D_KV)
        o = o2[..., 0:D_KV] * (1.0 / (o2[..., D_KV:2 * D_KV] + sinkden))
        attn_parts += [jnp.concatenate([o[c, g * CHUNK:(g + 1) * CHUNK, :] for g in range(GROUP)], axis=1)
                       for c in range(cpp)]
        conv_parts.append(conv_rows(p * cpp * CHUNK, (p + 1) * cpp * CHUNK))
    y_attn = jnp.concatenate(attn_parts, axis=0)
    n_conv = jnp.concatenate(conv_parts, axis=0)

    for j in range(nseq):
        sk_ref[j] = kext[j * kpitch + tl:(j + 1) * kpitch, :]
        sv_ref[j] = vext[j * kpitch + tl:(j + 1) * kpitch, :]
    if nseq == 1:
        kext[0:WINDOW, :] = kext[tl:tl + WINDOW, :]
        vext[0:WINDOW, :] = vext[tl:tl + WINDOW, :]

    n_attn = _rms_norm(y_attn, gattn_ref[...]).astype(BF16)
    mixed = _dot(n_attn, w_out_ref[D_CONV:D_CONV + D_ATTN, :]) + _dot(n_conv, w_out_ref[0:D_CONV, :])
    if defer_tail:
        xprev_s[...] = x
        mixed_s[...] = mixed
    else:
        norm_rows(0, rows, x, mixed)
        route(1.0)


def _mixer(x, cconv, ck, cv, counts_in, prm, consts, *, nseq, tl, mask_history, alpha):
    nb_total, t_total = x.shape[0], x.shape[1]
    nb, ns = nb_total // nseq, t_total // tl
    defer_tail = ns > 1
    steps = ns + 1 if defer_tail else ns

    def in_blk(s):
        return jnp.minimum(s, ns - 1)

    def out_blk(b, s):
        return b * ns + (jnp.maximum(s - 1, 0) if defer_tail else s)

    rows = nseq * tl
    n_tok = nb_total * t_total
    w_in, convw, convb, sinks, gconv, gattn, w_out, ln1g, ln1b, wr, br = prm
    abias, tri, keysel = consts

    def full(a):
        return pl.BlockSpec(a.shape, lambda b, s, _n=a.ndim: (0,) * _n)

    def seq_state(width, nrows):
        return pl.BlockSpec((nseq, nrows, width), lambda b, s: (b, 0, 0))

    in_specs = [
        pl.BlockSpec((nseq, tl, D_MODEL), lambda b, s: (b, in_blk(s), 0)),
        seq_state(D_CONV, CONV_W - 1), seq_state(D_KV, WINDOW), seq_state(D_KV, WINDOW),
        full(w_in), full(convw), full(convb), full(abias), full(sinks), full(gconv), full(gattn),
        full(w_out), full(ln1g), full(ln1b), full(wr), full(br), full(tri), full(keysel), full(counts_in),
    ]
    out_shape = [
        jax.ShapeDtypeStruct((n_tok * ROW_TILES, LANES), F32),
        jax.ShapeDtypeStruct((SUBLANES, n_tok), F32),
        jax.ShapeDtypeStruct((n_tok, LANES), F32),
        jax.ShapeDtypeStruct((N_EXPERTS, LANES), F32),
        jax.ShapeDtypeStruct((nb_total, CONV_W - 1, D_CONV), F32),
        jax.ShapeDtypeStruct((nb_total, WINDOW, D_KV), F32),
        jax.ShapeDtypeStruct((nb_total, WINDOW, D_KV), F32),
    ]
    out_specs = [
        pl.BlockSpec((rows * ROW_TILES, LANES), lambda b, s: (out_blk(b, s), 0)),
        pl.BlockSpec((SUBLANES, rows), lambda b, s: (0, out_blk(b, s))),
        pl.BlockSpec((rows, LANES), lambda b, s: (out_blk(b, s), 0)),
        pl.BlockSpec((N_EXPERTS, LANES), lambda b, s: (0, 0)),
        seq_state(D_CONV, CONV_W - 1), seq_state(D_KV, WINDOW), seq_state(D_KV, WINDOW),
    ]
    scratch = [
        pltpu.VMEM((nseq * (tl + SUBLANES), D_CONV), F32),
        pltpu.VMEM((nseq * (WINDOW + tl), D_KV), F32),
        pltpu.VMEM((nseq * (WINDOW + tl), D_KV), F32),
        pltpu.VMEM((N_EXPERTS, 1), F32),
        pltpu.VMEM((rows // CHUNK * QROWS, 2 * KEYS), F32),
        pltpu.VMEM((rows // CHUNK * QROWS, 2 * KEYS), BF16),
        pltpu.VMEM((rows // CHUNK * QROWS, D_KV), F32),
        pltpu.VMEM((rows // CHUNK * QROWS, LANES), F32),
        pltpu.VMEM((rows // CHUNK * QROWS, LANES), F32),
        pltpu.VMEM((rows, _OFF_Q), F32),
        pltpu.VMEM((rows, D_MODEL), BF16),
        pltpu.VMEM((rows, D_MODEL) if defer_tail else (SUBLANES, LANES), F32),
        pltpu.VMEM((rows, D_MODEL) if defer_tail else (SUBLANES, LANES), F32),
    ]
    kern = functools.partial(_mixer_kernel, nseq=nseq, tl=tl, alpha=alpha, mask_history=mask_history,
                             defer_tail=defer_tail)
    return pl.pallas_call(
        kern, grid=(nb, steps), in_specs=in_specs, out_specs=out_specs, out_shape=out_shape,
        scratch_shapes=scratch, name="mixer",
        compiler_params=pltpu.CompilerParams(dimension_semantics=("arbitrary", "arbitrary"),
                                             vmem_limit_bytes=VMEM_LIMIT),
    )(x, cconv, ck, cv, w_in, convw, convb, abias, sinks, gconv, gattn, w_out, ln1g, ln1b, wr, br, tri, keysel, counts_in)


_PAD_PIECES = tuple(2 ** k for k in range(EXPERT_BLOCK.bit_length() - 2, -1, -1))


def _tile_rows(row, n=1):
    return pl.ds(pl.multiple_of(row * ROW_TILES, ROW_TILES), n * ROW_TILES)


def _row_copy(src, src_row, dst, dst_row, sem):
    return pltpu.make_async_copy(src.at[_tile_rows(src_row), :], dst.at[_tile_rows(dst_row), :], sem)


def _dispatch_kernel(zstart_ref, zcount_ref, tail_ref, pos1_ref, pos2_ref, xa_ref, xb_ref, xs_hbm, zeros_v, sem, zsem,
                     *, tb, nba):
    step = pl.program_id(0)

    def zero_copy(start, piece):
        return pltpu.make_async_copy(zeros_v.at[_tile_rows(0, piece), :], xs_hbm.at[_tile_rows(start, piece), :], zsem)

    def pad_copy(e, piece, taken):
        return zero_copy(zstart_ref[e] + taken, piece)

    def for_each_pad_piece(fn):
        def per_tail_block(j, carry):
            for part in range(EXPERT_BLOCK // _PAD_PIECES[0]):
                fn(zero_copy(tail_ref[0] + j * EXPERT_BLOCK + part * _PAD_PIECES[0], _PAD_PIECES[0]))
            return carry

        lax.fori_loop(0, tail_ref[1], per_tail_block, 0)

        def per_expert(e, carry):
            count = zcount_ref[e]
            taken = 0
            for piece in _PAD_PIECES:
                present = (count & piece) != 0

                @pl.when(present)
                def _(piece=piece, taken=taken):
                    fn(pad_copy(e, piece, taken))

                taken = taken + jnp.where(present, piece, 0)
            return carry

        lax.fori_loop(0, N_EXPERTS, per_expert, 0)

    @pl.when(step == 0)
    def _():
        zeros_v[...] = jnp.zeros_like(zeros_v)
        for_each_pad_piece(lambda c: c.start())

    def issue_rows(x_ref):
        def issue(t, carry):
            _row_copy(x_ref, t, xs_hbm, pos1_ref[t], sem).start(priority=0)
            _row_copy(x_ref, t, xs_hbm, pos2_ref[t], sem).start(priority=1)
            return carry

        lax.fori_loop(0, tb, issue, 0, unroll=8)

    @pl.when(step < nba)
    def _():
        issue_rows(xa_ref)

    @pl.when(step >= nba)
    def _():
        issue_rows(xb_ref)

    pltpu.make_async_copy(xs_hbm.at[_tile_rows(0, 2 * tb), :], xs_hbm.at[_tile_rows(0, 2 * tb), :], sem).wait()

    @pl.when(step == 0)
    def _():
        for_each_pad_piece(lambda c: c.wait())


def _dispatch(xa_rows, xb_rows, pos1, pos2, zstart, zcount, tail, n_slots, *, tb):
    nba, nbb = xa_rows.shape[0] // ROW_TILES // tb, xb_rows.shape[0] // ROW_TILES // tb
    kern = functools.partial(_dispatch_kernel, tb=tb, nba=nba)
    grid_spec = pltpu.PrefetchScalarGridSpec(
        num_scalar_prefetch=3, grid=(nba + nbb,),
        in_specs=[pl.BlockSpec((tb,), lambda i, *_: (i,), memory_space=pltpu.SMEM),
                  pl.BlockSpec((tb,), lambda i, *_: (i,), memory_space=pltpu.SMEM),
                  pl.BlockSpec((tb * ROW_TILES, LANES), lambda i, *_: (jnp.minimum(i, nba - 1), 0)),
                  pl.BlockSpec((tb * ROW_TILES, LANES), lambda i, *_: (jnp.maximum(i - nba, 0), 0))],
        out_specs=pl.BlockSpec(memory_space=pl.ANY),
        scratch_shapes=[pltpu.VMEM((_PAD_PIECES[0] * ROW_TILES, LANES), F32),
                        pltpu.SemaphoreType.DMA(()), pltpu.SemaphoreType.DMA(())],
    )
    return pl.pallas_call(
        kern, grid_spec=grid_spec, out_shape=jax.ShapeDtypeStruct((n_slots * ROW_TILES, LANES), F32),
        name="dispatch",
        compiler_params=pltpu.CompilerParams(dimension_semantics=("arbitrary",)),
    )(zstart, zcount, tail, pos1, pos2, xa_rows, xb_rows)


def _expert_kernel(be_ref, src_ref, used_ref, xs_ref, wg_ref, wu_ref, wd_ref, yb_ref, wg_b, wu_b, wd_b):
    i = pl.program_id(0)
    e = be_ref[i]
    prev = be_ref[jnp.maximum(i - 1, 0)]

    @pl.when(jnp.logical_or(i == 0, e != prev))
    def _():
        wg_b[...] = wg_ref[...].astype(BF16)
        wu_b[...] = wu_ref[...].astype(BF16)
        wd_b[...] = wd_ref[...].astype(BF16)

    @pl.when(used_ref[i] != 0)
    def _():
        x = _load_rows(xs_ref, EXPERT_BLOCK).astype(BF16)
        g = _dot(x, wg_b[...])
        u = _dot(x, wu_b[...])
        hid = (g / (1.0 + jnp.exp(-g))) * u
        _store_rows(yb_ref, _dot(hid.astype(BF16), wd_b[...]), EXPERT_BLOCK)

    @pl.when(used_ref[i] == 0)
    def _():
        yb_ref[...] = jnp.zeros_like(yb_ref)


def _experts(xs, blk_e, blk_src, blk_used, w_gate, w_up, w_down, layer):
    n_slots = xs.shape[0] // ROW_TILES
    nblk = n_slots // EXPERT_BLOCK

    def wspec(shape):
        return pl.BlockSpec((None, None) + shape, lambda i, be, src, used: (layer, be[i], 0, 0))

    grid_spec = pltpu.PrefetchScalarGridSpec(
        num_scalar_prefetch=3, grid=(nblk,),
        in_specs=[pl.BlockSpec((EXPERT_BLOCK * ROW_TILES, LANES), lambda i, be, src, used: (src[i], 0)),
                  wspec((D_MODEL, D_EXP)), wspec((D_MODEL, D_EXP)), wspec((D_EXP, D_MODEL))],
        out_specs=pl.BlockSpec((EXPERT_BLOCK * ROW_TILES, LANES), lambda i, be, src, used: (i, 0)),
        scratch_shapes=[pltpu.VMEM((D_MODEL, D_EXP), BF16), pltpu.VMEM((D_MODEL, D_EXP), BF16),
                        pltpu.VMEM((D_EXP, D_MODEL), BF16)],
    )
    return pl.pallas_call(
        _expert_kernel, grid_spec=grid_spec, out_shape=jax.ShapeDtypeStruct((n_slots * ROW_TILES, LANES), F32),
        name="experts",
        compiler_params=pltpu.CompilerParams(dimension_semantics=("arbitrary",), vmem_limit_bytes=VMEM_LIMIT),
    )(blk_e, blk_src, blk_used, xs, w_gate, w_up, w_down)


def _combine_kernel(pos1_ref, pos2_ref, next1_ref, next2_ref, xa_ref, xb_ref, ga_ref, gb_ref, yb_hbm, g_ref, b_ref,
                    outa_ref, outb_ref, ybuf, sems, *, tb, alpha, nba, nsteps):
    step = pl.program_id(0)
    slot = step % 2

    def gather(p1_ref, p2_ref, to_slot):
        def issue(t, carry):
            _row_copy(yb_hbm, p1_ref[t], ybuf.at[to_slot].at[0], t, sems.at[to_slot]).start(priority=0)
            _row_copy(yb_hbm, p2_ref[t], ybuf.at[to_slot].at[1], t, sems.at[to_slot]).start(priority=1)
            return carry

        lax.fori_loop(0, tb, issue, 0, unroll=8)

    @pl.when(step == 0)
    def _():
        gather(pos1_ref, pos2_ref, 0)

    @pl.when(step + 1 < nsteps)
    def _():
        gather(next1_ref, next2_ref, 1 - slot)

    for k in range(TOP_K):
        pltpu.make_async_copy(yb_hbm.at[_tile_rows(0, tb), :], ybuf.at[slot].at[k], sems.at[slot]).wait()

    def finish(x1_ref, gcol_ref, out_ref):
        gates = gcol_ref[...]
        y = (gates[:, _R_G1:_R_G1 + 1] * _load_rows(ybuf.at[slot].at[0], tb)
             + gates[:, _R_G2:_R_G2 + 1] * _load_rows(ybuf.at[slot].at[1], tb))
        out_ref[...] = _layer_norm(alpha * _load_rows(x1_ref, tb) + y, g_ref[...], b_ref[...])

    @pl.when(step < nba)
    def _():
        finish(xa_ref, ga_ref, outa_ref)

    @pl.when(step >= nba)
    def _():
        finish(xb_ref, gb_ref, outb_ref)


def _combine(xa_rows, xb_rows, gcol_a, gcol_b, yb, pos1, pos2, ln2g, ln2b, *, tb, alpha):
    nba, nbb = xa_rows.shape[0] // ROW_TILES // tb, xb_rows.shape[0] // ROW_TILES // tb
    nsteps = nba + nbb
    kern = functools.partial(_combine_kernel, tb=tb, alpha=alpha, nba=nba, nsteps=nsteps)

    def first(i):
        return jnp.minimum(i, nba - 1)

    def second(i):
        return jnp.maximum(i - nba, 0)

    def nxt(i):
        return jnp.minimum(i + 1, nsteps - 1)

    smem = functools.partial(pl.BlockSpec, (tb,), memory_space=pltpu.SMEM)
    return pl.pallas_call(
        kern, grid=(nsteps,),
        in_specs=[smem(lambda i: (i,)), smem(lambda i: (i,)), smem(lambda i: (nxt(i),)), smem(lambda i: (nxt(i),)),
                  pl.BlockSpec((tb * ROW_TILES, LANES), lambda i: (first(i), 0)),
                  pl.BlockSpec((tb * ROW_TILES, LANES), lambda i: (second(i), 0)),
                  pl.BlockSpec((tb, LANES), lambda i: (first(i), 0)),
                  pl.BlockSpec((tb, LANES), lambda i: (second(i), 0)),
                  pl.BlockSpec(memory_space=pl.ANY),
                  pl.BlockSpec((1, D_MODEL), lambda i: (0, 0)),
                  pl.BlockSpec((1, D_MODEL), lambda i: (0, 0))],
        out_specs=[pl.BlockSpec((tb, D_MODEL), lambda i: (first(i), 0)),
                   pl.BlockSpec((tb, D_MODEL), lambda i: (second(i), 0))],
        out_shape=[jax.ShapeDtypeStruct((nba * tb, D_MODEL), F32), jax.ShapeDtypeStruct((nbb * tb, D_MODEL), F32)],
        scratch_shapes=[pltpu.VMEM((2, TOP_K, tb * ROW_TILES, LANES), F32), pltpu.SemaphoreType.DMA((2,))],
        name="combine",
        compiler_params=pltpu.CompilerParams(dimension_semantics=("arbitrary",), vmem_limit_bytes=VMEM_LIMIT),
    )(pos1, pos2, pos1, pos2, xa_rows, xb_rows, gcol_a, gcol_b, yb, ln2g, ln2b)


def _attn_column_order():
    g, kvh, d = np.meshgrid(np.arange(GROUP), np.arange(N_KV_HEADS), np.arange(HEAD_DIM), indexing='ij')
    return ((kvh * GROUP + g) * HEAD_DIM + d).reshape(-1)


def _alibi_bias():
    slopes = np.asarray([2.0 ** (-8.0 * (h + 1) / N_HEADS) for h in range(N_HEADS)], np.float32)
    qi = np.arange(CHUNK, dtype=np.int32)[:, None]
    sj = np.arange(KEYS, dtype=np.int32)[None, :]
    dist = np.abs(qi + WINDOW - sj).astype(np.float32)
    bias = -slopes.reshape(N_KV_HEADS, GROUP, 1, 1) * dist
    bias = np.transpose(bias, (1, 2, 0, 3)).reshape(GROUP * CHUNK, N_KV_HEADS * KEYS)
    return jnp.asarray(bias, F32)


def _key_selector():
    row_head = np.arange(N_KV_HEADS * KEYS)[:, None] // KEYS
    lane_head = np.arange(D_KV)[None, :] // HEAD_DIM
    return jnp.asarray((row_head == lane_head).astype(np.float32), BF16)


def _strict_upper():
    r = np.arange(RANK_CHUNK)
    return jnp.asarray((r[:, None] < r[None, :]).astype(np.float32), BF16)


def _slot_plan(route, counts, n_tok):
    nblk = (n_tok * TOP_K) // EXPERT_BLOCK + N_EXPERTS
    experts = route[_R_E1:_R_E2 + 1].astype(jnp.int32)
    ranks = route[_R_RANK1:_R_RANK2 + 1].astype(jnp.int32)
    cnt = counts[:, 0].astype(jnp.int32)
    padded = (cnt + EXPERT_BLOCK - 1) // EXPERT_BLOCK * EXPERT_BLOCK
    pad_end = jnp.cumsum(padded)
    pad_start = pad_end - padded
    ids = jnp.arange(N_EXPERTS, dtype=jnp.int32)
    pos = jnp.sum(jnp.where(experts[..., None] == ids, pad_start, 0), axis=-1) + ranks
    blk_first = jnp.arange(nblk, dtype=jnp.int32) * EXPERT_BLOCK
    blk_e = jnp.minimum(jnp.sum((pad_end[None, :] <= blk_first[:, None]).astype(jnp.int32), axis=-1), N_EXPERTS - 1)
    blk_used = (blk_first < pad_end[-1]).astype(jnp.int32)
    blk_src = jnp.minimum(jnp.arange(nblk, dtype=jnp.int32), pad_end[-1] // EXPERT_BLOCK - 1)
    tail = jnp.stack([pad_end[-1], nblk - pad_end[-1] // EXPERT_BLOCK]).astype(jnp.int32)
    return pos[0], pos[1], pad_start + cnt, padded - cnt, tail, blk_e, blk_src, blk_used, nblk * EXPERT_BLOCK


def _layer(xp, xs, cache, mix_prm, consts, moe_prm, layer, *, tl_prompt, nseq_sample, alpha):
    (bp, tp, _), (bs, ts, _) = xp.shape, xs.shape
    np_tok, ns_tok = bp * tp, bs * ts
    tb = ROWS
    assert np_tok % tb == 0 and ns_tok % tb == 0
    zeros_conv = jnp.zeros((bp, CONV_W - 1, D_CONV), F32)
    zeros_kv = jnp.zeros((bp, WINDOW, D_KV), F32)
    zero_counts = jnp.zeros((N_EXPERTS, LANES), F32)
    x1p, route_p, gcol_p, counts_p, *state_p = _mixer(
        xp, zeros_conv, zeros_kv, zeros_kv, zero_counts, mix_prm, consts,
        nseq=1, tl=tl_prompt, mask_history=True, alpha=alpha)
    x1s, route_s, gcol_s, counts, *state_s = _mixer(
        xs, *cache, counts_p, mix_prm, consts, nseq=nseq_sample, tl=ts, mask_history=False, alpha=alpha)
    route = jnp.concatenate([route_p, route_s], axis=1)
    pos1, pos2, zstart, zcount, tail, blk_e, blk_src, blk_used, n_slots = _slot_plan(route, counts, np_tok + ns_tok)
    sorted_rows = _dispatch(x1p, x1s, pos1, pos2, zstart, zcount, tail, n_slots, tb=tb)
    w_gate, w_up, w_down, ln2g, ln2b = moe_prm
    yb = _experts(sorted_rows, blk_e, blk_src, blk_used, w_gate, w_up, w_down, layer)
    out_p, out_s = _combine(x1p, x1s, gcol_p, gcol_s, yb, pos1, pos2, ln2g, ln2b, tb=tb, alpha=alpha)
    return out_p.reshape(bp, tp, D_MODEL), out_s.reshape(bs, ts, D_MODEL), state_p, state_s


def kernel(x_prompt, x_sample, cache_conv, cache_k, cache_v, w_in, conv_w, conv_b, attn_sinks, g_conv, g_attn,
           w_out, ln1_g, ln1_b, router_group_w, router_group_b, router_expert_w, router_expert_b,
           expert_w_gate, expert_w_up, expert_w_down, ln2_g, ln2_b):
    depth = w_in.shape[0]
    alpha = (2 * depth) ** 0.25
    batch, seq = x_prompt.shape[0], x_prompt.shape[1]
    dec_batch, dec_seq = x_sample.shape[0], x_sample.shape[1]
    assert dec_seq == CHUNK and seq % CHUNK == 0
    consts = (_alibi_bias(), _strict_upper(), _key_selector())
    attn_order = _attn_column_order()
    tl_prompt = min(MIXER_ROWS, seq)
    nseq_sample = min(MIXER_ROWS // dec_seq, dec_batch)
    assert seq % tl_prompt == 0 and dec_batch % nseq_sample == 0

    xp, xs = x_prompt, x_sample
    states = [[] for _ in range(6)]
    for l in range(depth):
        sink_col = jnp.broadcast_to(jnp.repeat(attn_sinks[l].astype(F32), CHUNK).reshape(N_KV_HEADS, QROWS, 1),
                                    (N_KV_HEADS, QROWS, LANES))
        wr = jnp.zeros((D_MODEL, LANES), F32)
        wr = wr.at[:, 0:N_EXPERTS].set(router_expert_w[l]).at[:, N_EXPERTS:N_EXPERTS + N_GROUPS].set(router_group_w[l])
        br = jnp.zeros((1, LANES), F32)
        br = br.at[0, 0:N_EXPERTS].set(router_expert_b[l]).at[0, N_EXPERTS:N_EXPERTS + N_GROUPS].set(router_group_b[l])
        w_in_l = w_in[l].at[:, _OFF_Q:_OFF_Q + D_ATTN].set(w_in[l][:, _OFF_Q + attn_order])
        w_out_l = w_out[l].at[D_CONV:D_CONV + D_ATTN, :].set(w_out[l][D_CONV + attn_order, :])
        mix_prm = (w_in_l.astype(BF16), conv_w[l], conv_b[l].reshape(1, D_CONV), sink_col,
                   g_conv[l].reshape(1, D_CONV), g_attn[l][attn_order].reshape(1, D_ATTN), w_out_l.astype(BF16),
                   ln1_g[l].reshape(1, D_MODEL), ln1_b[l].reshape(1, D_MODEL), wr.astype(BF16), br)
        moe_prm = (expert_w_gate, expert_w_up, expert_w_down,
                   ln2_g[l].reshape(1, D_MODEL), ln2_b[l].reshape(1, D_MODEL))
        cache = (cache_conv[l], cache_k[l].reshape(dec_batch, WINDOW, D_KV), cache_v[l].reshape(dec_batch, WINDOW, D_KV))
        xp, xs, state_p, state_s = _layer(xp, xs, cache, mix_prm, consts, moe_prm, l,
                                          tl_prompt=tl_prompt, nseq_sample=nseq_sample, alpha=alpha)
        for lst, val in zip(states, state_p + state_s):
            lst.append(val)

    def kv(lst, nb):
        return jnp.stack(lst).reshape(depth, nb, WINDOW, N_KV_HEADS, HEAD_DIM)

    return (xp, xs, jnp.stack(states[0]), kv(states[1], batch), kv(states[2], batch),
            jnp.stack(states[3]), kv(states[4], dec_batch), kv(states[5], dec_batch))
```

```python
import functools

import numpy as np
import jax
import jax.numpy as jnp
from jax import lax
from jax.experimental import pallas as pl
from jax.experimental.pallas import tpu as pltpu

D_MODEL = 1024
D_CONV = 512
CONV_W = 3
N_HEADS = 8
N_KV_HEADS = 2
GROUP = N_HEADS // N_KV_HEADS
HEAD_DIM = 64
D_ATTN = N_HEADS * HEAD_DIM
D_KV = N_KV_HEADS * HEAD_DIM
WINDOW = 128
CHUNK = 64
KEYS = WINDOW + CHUNK
QROWS = GROUP * CHUNK
N_GROUPS = 4
EXP_PER_GROUP = 8
N_EXPERTS = N_GROUPS * EXP_PER_GROUP
TOP_K = 2
D_EXP = 512
LN_EPS = 1e-5
NEG_INF = -1e30

SUBLANES = 8
LANES = 128
ROW_TILES = D_MODEL // LANES
assert ROW_TILES == SUBLANES
ROWS = 512
MIXER_ROWS = 512
EXPERT_BLOCK = 1024
RANK_CHUNK = 256
SOFTMAX_ROWS = 64
PROJ_CHUNK = 2 * LANES
MIX_PIECES = 4
VMEM_LIMIT = 56 * 1024 * 1024

F32 = jnp.float32
BF16 = jnp.bfloat16

_OFF_B, _OFF_C, _OFF_H = 0, D_CONV, 2 * D_CONV
_OFF_Q = 3 * D_CONV
_OFF_K = _OFF_Q + D_ATTN
_OFF_V = _OFF_K + D_KV
D_IN = _OFF_V + D_KV

_R_E1, _R_E2, _R_RANK1, _R_RANK2, _R_G1, _R_G2 = 0, 1, 2, 3, 4, 5


def _load_rows(ref, n):
    return jnp.concatenate([ref[pl.ds(j, n, stride=ROW_TILES), :] for j in range(ROW_TILES)], axis=1)


def _store_rows(ref, val, n):
    for j in range(ROW_TILES):
        ref[pl.ds(j, n, stride=ROW_TILES), :] = val[:, j * LANES:(j + 1) * LANES]


def _dot(a, b):
    return jnp.dot(a, b, preferred_element_type=F32)


def _rms_norm(x, g):
    return x * lax.rsqrt(jnp.mean(jnp.square(x), -1, keepdims=True) + LN_EPS) * g


def _layer_norm(x, g, b):
    mu = jnp.mean(x, -1, keepdims=True)
    xc = x - mu
    var = jnp.mean(jnp.square(xc), -1, keepdims=True)
    return xc * lax.rsqrt(var + LN_EPS) * g + b


def _mixer_kernel(x_ref, cconv_ref, ck_ref, cv_ref, w_in_ref, convw_ref, convb_ref, abias_ref,
                  sink_ref, gconv_ref, gattn_ref, w_out_ref, ln1g_ref, ln1b_ref, wr_ref, br_ref,
                  tri_ref, keysel_ref, counts_in_ref,
                  x1_ref, route_ref, gcol_ref, counts_ref, sconv_ref, sk_ref, sv_ref,
                  uext, kext, vext, cnt_s, s_ref, e_ref, sinkden_ref, m0_ref, m1_ref, proj_ref, x1b_s, xprev_s,
                  mixed_s, *, nseq, tl, alpha, mask_history, defer_tail):
    b = pl.program_id(0)
    s = pl.program_id(1)
    rows = nseq * tl
    upitch = tl + SUBLANES
    kpitch = WINDOW + tl

    @pl.when(jnp.logical_and(b == 0, s == 0))
    def _():
        cnt_s[...] = counts_in_ref[:, 0:1]
        if defer_tail:
            xprev_s[...] = jnp.zeros_like(xprev_s)
            mixed_s[...] = jnp.zeros_like(mixed_s)

    @pl.when(s == 0)
    def _():
        for j in range(nseq):
            uext[j * upitch + SUBLANES - 2:j * upitch + SUBLANES, :] = cconv_ref[j]
            kext[j * kpitch:j * kpitch + WINDOW, :] = ck_ref[j]
            vext[j * kpitch:j * kpitch + WINDOW, :] = cv_ref[j]

    def norm_rows(r0, r1, x_rows, mixed_rows):
        x1 = _layer_norm(alpha * x_rows + mixed_rows, ln1g_ref[...], ln1b_ref[...])
        for j in range(ROW_TILES):
            x1_ref[pl.ds(r0 * ROW_TILES + j, r1 - r0, stride=ROW_TILES), :] = x1[:, j * LANES:(j + 1) * LANES]
        x1b_s[r0:r1, :] = x1.astype(BF16)

    def route(valid):
        logits_t = (_dot(x1b_s[...], wr_ref[...]) + br_ref[...]).T
        sub = lax.broadcasted_iota(jnp.int32, (SUBLANES, rows), 0)
        gl = jnp.where(sub < N_GROUPS, logits_t[N_EXPERTS:N_EXPERTS + SUBLANES, :], -jnp.inf)
        gmax = jnp.max(gl, axis=0, keepdims=True)
        grp = jnp.min(jnp.where(gl == gmax, sub, SUBLANES), axis=0, keepdims=True)
        p_grp = 1.0 / jnp.sum(jnp.exp(gl - gmax), axis=0, keepdims=True)
        el = logits_t[(N_GROUPS - 1) * EXP_PER_GROUP:N_GROUPS * EXP_PER_GROUP, :]
        for g in range(N_GROUPS - 2, -1, -1):
            el = jnp.where(grp == g, logits_t[g * EXP_PER_GROUP:(g + 1) * EXP_PER_GROUP, :], el)
        v1 = jnp.max(el, axis=0, keepdims=True)
        i1 = jnp.min(jnp.where(el == v1, sub, SUBLANES), axis=0, keepdims=True)
        el2 = jnp.where(sub == i1, -jnp.inf, el)
        v2 = jnp.max(el2, axis=0, keepdims=True)
        i2 = jnp.min(jnp.where(el2 == v2, sub, SUBLANES), axis=0, keepdims=True)
        e2 = jnp.exp(v2 - v1)
        gate1 = p_grp * (1.0 / (1.0 + e2))
        gate2 = p_grp * (e2 / (1.0 + e2))

        chosen = jnp.logical_or(sub == i1, sub == i2)
        onehot = jnp.concatenate(
            [jnp.where(jnp.logical_and(grp == g, chosen), 1.0, 0.0) for g in range(N_GROUPS)], axis=0)
        running = cnt_s[...]
        ranks = []
        for c in range(rows // RANK_CHUNK):
            oh = onehot[:, c * RANK_CHUNK:(c + 1) * RANK_CHUNK]
            ranks.append(_dot(oh.astype(BF16), tri_ref[...]) + running)
            running = running + jnp.sum(oh, axis=1, keepdims=True) * valid
        rank = jnp.concatenate(ranks, axis=1)
        cnt_s[...] = running
        counts_ref[...] = jnp.broadcast_to(running, (N_EXPERTS, LANES))

        ex1 = grp * EXP_PER_GROUP + i1
        ex2 = grp * EXP_PER_GROUP + i2
        erow = lax.broadcasted_iota(jnp.int32, (N_EXPERTS, rows), 0)
        rank1 = jnp.sum(jnp.where(erow == ex1, rank, 0.0), axis=0, keepdims=True)
        rank2 = jnp.sum(jnp.where(erow == ex2, rank, 0.0), axis=0, keepdims=True)
        fields = {_R_E1: ex1.astype(F32), _R_E2: ex2.astype(F32), _R_RANK1: rank1, _R_RANK2: rank2,
                  _R_G1: gate1, _R_G2: gate2}
        record = jnp.zeros((SUBLANES, rows), F32)
        for r, val in fields.items():
            record = jnp.where(sub == r, val, record)
        route_ref[...] = record
        padded = jnp.concatenate([record, jnp.zeros((LANES - SUBLANES, rows), F32)], axis=0)
        gcol_ref[...] = padded.T

    x = x_ref[...].reshape(rows, D_MODEL)
    xb = x.astype(BF16)

    def conv_rows(r0, r1):
        w0, w1, w2 = convw_ref[0:1, :], convw_ref[1:2, :], convw_ref[2:3, :]
        out = []
        for j in range(r0 // tl, -(-r1 // tl)):
            a, b_ = max(r0, j * tl) - j * tl, min(r1, (j + 1) * tl) - j * tl
            base = j * upitch + SUBLANES
            src = pl.ds(j * tl + a, b_ - a)
            uext[base + a:base + b_, :] = proj_ref[src, _OFF_C:_OFF_C + D_CONV] * proj_ref[src, _OFF_H:_OFF_H + D_CONV]
            yc = convb_ref[...] + uext[base + a - 2:base + b_ - 2, :] * w0
            yc = yc + uext[base + a - 1:base + b_ - 1, :] * w1
            yc = yc + uext[base + a:base + b_, :] * w2
            out.append(_rms_norm(proj_ref[src, _OFF_B:_OFF_B + D_CONV] * yc, gconv_ref[...]).astype(BF16))
            if b_ == tl:
                tail = uext[base + tl - 2:base + tl, :]
                sconv_ref[j] = tail
                uext[base - 2:base, :] = tail
        return out[0] if len(out) == 1 else jnp.concatenate(out, axis=0)

    qkv = []
    qkv_chunks = [(c, c + PROJ_CHUNK) for c in range(_OFF_Q, D_IN, PROJ_CHUNK)]
    tail_rows = -(-rows // len(qkv_chunks) // SUBLANES) * SUBLANES
    for p, (c0, c1) in enumerate(qkv_chunks):
        r0, r1 = min(p * tail_rows, rows), min((p + 1) * tail_rows, rows)
        if defer_tail and r1 > r0:
            norm_rows(r0, r1, xprev_s[r0:r1, :], mixed_s[r0:r1, :])
        qkv.append(_dot(xb, w_in_ref[:, c0:c1]))
    qkv = jnp.concatenate(qkv, axis=1)
    if defer_tail:
        route((s > 0).astype(F32))
    q = (qkv[:, 0:D_ATTN] * (HEAD_DIM ** -0.5)).astype(BF16)
    k = qkv[:, D_ATTN:D_ATTN + D_KV]
    v = qkv[:, D_ATTN + D_KV:D_ATTN + 2 * D_KV]
    for j in range(nseq):
        kext[j * kpitch + WINDOW:(j + 1) * kpitch, :] = k[j * tl:(j + 1) * tl]
        vext[j * kpitch + WINDOW:(j + 1) * kpitch, :] = v[j * tl:(j + 1) * tl]

    nchunk = rows // CHUNK
    assert nseq == 1 or tl == CHUNK
    key_stride = CHUNK if nseq == 1 else kpitch
    head0 = lax.broadcasted_iota(jnp.int32, (1, D_KV), 1) < HEAD_DIM

    def windows(ext_ref):
        ext = ext_ref[...].astype(BF16)
        heads = (jnp.where(head0, ext, jnp.zeros_like(ext)), jnp.where(head0, jnp.zeros_like(ext), ext))
        return jnp.stack([jnp.concatenate([h[c * key_stride:c * key_stride + KEYS] for h in heads], axis=0)
                          for c in range(nchunk)])

    q3 = jnp.stack([jnp.concatenate([q[c * CHUNK:(c + 1) * CHUNK, g * D_KV:(g + 1) * D_KV] for g in range(GROUP)], axis=0)
                    for c in range(nchunk)])
    logits = lax.dot_general(q3, windows(kext), (((2,), (2,)), ((0,), (0,))), preferred_element_type=F32)
    s_ref[...] = logits.reshape(nchunk * QROWS, 2 * KEYS)

    assert 2 * KEYS == 3 * LANES
    mid0 = lax.broadcasted_iota(jnp.int32, (1, LANES), 1) < KEYS - LANES
    colk = lax.broadcasted_iota(jnp.int32, (1, 2 * KEYS), 1)
    key = jnp.where(colk < KEYS, colk, colk - KEYS)

    def row_block(i):
        r = i * SOFTMAX_ROWS
        return pl.ds(r, SOFTMAX_ROWS), pl.ds(r % QROWS, SOFTMAX_ROWS), r // QROWS

    def tiles(lg):
        return lg[:, 0:LANES], lg[:, LANES:2 * LANES], lg[:, 2 * LANES:3 * LANES]

    def max_rows(i, masked):
        rows_i, qrows_i, chunk_i = row_block(i)
        lg = s_ref[rows_i, :] + abias_ref[qrows_i, :]
        if masked:
            lg = lg + jnp.where(s * tl + chunk_i * CHUNK - WINDOW + key < 0, NEG_INF, 0.0).astype(F32)
        s_ref[rows_i, :] = lg
        t0, t1, t2 = tiles(lg)
        m0_ref[rows_i, :] = jnp.maximum(jnp.max(jnp.maximum(t0, jnp.where(mid0, t1, NEG_INF)), -1, keepdims=True),
                                        sink_ref[0, qrows_i, :])
        m1_ref[rows_i, :] = jnp.maximum(jnp.max(jnp.maximum(t2, jnp.where(mid0, NEG_INF, t1)), -1, keepdims=True),
                                        sink_ref[1, qrows_i, :])

    def exp_rows(i):
        rows_i, qrows_i, _ = row_block(i)
        t0, t1, t2 = tiles(s_ref[rows_i, :])
        m0, m1 = m0_ref[rows_i, :], m1_ref[rows_i, :]
        e = jnp.concatenate([jnp.exp(t0 - m0), jnp.exp(t1 - jnp.where(mid0, m0, m1)), jnp.exp(t2 - m1)], axis=-1)
        e_ref[rows_i, :] = e.astype(BF16)
        sinkden_ref[rows_i, :] = jnp.where(head0, jnp.exp(sink_ref[0, qrows_i, :] - m0),
                                           jnp.exp(sink_ref[1, qrows_i, :] - m1))

    n_it = nchunk * QROWS // SOFTMAX_ROWS
    n_masked = min(WINDOW // CHUNK, nchunk) * QROWS // SOFTMAX_ROWS if mask_history else 0
    proj_chunks = [(c, min(c + PROJ_CHUNK, _OFF_Q)) for c in range(0, _OFF_Q, PROJ_CHUNK)]
    piece = -(-2 * n_it // len(proj_chunks))
    steps = [functools.partial(max_rows, i, i < n_masked) for i in range(n_it)]
    steps += [functools.partial(exp_rows, i) for i in range(n_it)]
    for p, (c0, c1) in enumerate(proj_chunks):
        for step_fn in steps[p * piece:(p + 1) * piece]:
            step_fn()
        proj_ref[:, c0:c1] = _dot(xb, w_in_ref[:, c0:c1])
    for step_fn in steps[len(proj_chunks) * piece:]:
        step_fn()
    vsel = jnp.concatenate([windows(vext), jnp.broadcast_to(keysel_ref[...], (nchunk, 2 * KEYS, D_KV))], axis=-1)
    n_pieces = min(MIX_PIECES, nchunk)
    cpp = nchunk // n_pieces
    conv_parts, attn_parts = [], []
    for p in range(n_pieces):
        cs = slice(p * cpp, (p + 1) * cpp)
        o2 = lax.dot_general(e_ref[p * cpp * QROWS:(p + 1) * cpp * QROWS, :].reshape(cpp, QROWS, 2 * KEYS), vsel[cs],
                             (((2,), (1,)), ((0,), (0,))), preferred_element_type=F32)
        sinkden = sinkden_ref[p * cpp * QROWS:(p + 1) * cpp * QROWS, :].reshape(cpp, QROWS, D_KV)
        o = o2[..., 0:D_KV] * (1.0 / (o2[..., D_KV:2 * D_KV] + sinkden))
        attn_parts += [jnp.concatenate([o[c, g * CHUNK:(g + 1) * CHUNK, :] for g in range(GROUP)], axis=1)
                       for c in range(cpp)]
        conv_parts.append(conv_rows(p * cpp * CHUNK, (p + 1) * cpp * CHUNK))
    y_attn = jnp.concatenate(attn_parts, axis=0)
    n_conv = jnp.concatenate(conv_parts, axis=0)

    for j in range(nseq):
        sk_ref[j] = kext[j * kpitch + tl:(j + 1) * kpitch, :]
        sv_ref[j] = vext[j * kpitch + tl:(j + 1) * kpitch, :]
    if nseq == 1:
        kext[0:WINDOW, :] = kext[tl:tl + WINDOW, :]
        vext[0:WINDOW, :] = vext[tl:tl + WINDOW, :]

    n_attn = _rms_norm(y_attn, gattn_ref[...]).astype(BF16)
    mixed = _dot(n_attn, w_out_ref[D_CONV:D_CONV + D_ATTN, :]) + _dot(n_conv, w_out_ref[0:D_CONV, :])
    if defer_tail:
        xprev_s[...] = x
        mixed_s[...] = mixed
    else:
        norm_rows(0, rows, x, mixed)
        route(1.0)


def _mixer(x, cconv, ck, cv, counts_in, prm, consts, *, nseq, tl, mask_history, alpha):
    nb_total, t_total = x.shape[0], x.shape[1]
    nb, ns = nb_total // nseq, t_total // tl
    defer_tail = ns > 1
    steps = ns + 1 if defer_tail else ns

    def in_blk(s):
        return jnp.minimum(s, ns - 1)

    def out_blk(b, s):
        return b * ns + (jnp.maximum(s - 1, 0) if defer_tail else s)

    rows = nseq * tl
    n_tok = nb_total * t_total
    w_in, convw, convb, sinks, gconv, gattn, w_out, ln1g, ln1b, wr, br = prm
    abias, tri, keysel = consts

    def full(a):
        return pl.BlockSpec(a.shape, lambda b, s, _n=a.ndim: (0,) * _n)

    def seq_state(width, nrows):
        return pl.BlockSpec((nseq, nrows, width), lambda b, s: (b, 0, 0))

    in_specs = [
        pl.BlockSpec((nseq, tl, D_MODEL), lambda b, s: (b, in_blk(s), 0)),
        seq_state(D_CONV, CONV_W - 1), seq_state(D_KV, WINDOW), seq_state(D_KV, WINDOW),
        full(w_in), full(convw), full(convb), full(abias), full(sinks), full(gconv), full(gattn),
        full(w_out), full(ln1g), full(ln1b), full(wr), full(br), full(tri), full(keysel), full(counts_in),
    ]
    out_shape = [
        jax.ShapeDtypeStruct((n_tok * ROW_TILES, LANES), F32),
        jax.ShapeDtypeStruct((SUBLANES, n_tok), F32),
        jax.ShapeDtypeStruct((n_tok, LANES), F32),
        jax.ShapeDtypeStruct((N_EXPERTS, LANES), F32),
        jax.ShapeDtypeStruct((nb_total, CONV_W - 1, D_CONV), F32),
        jax.ShapeDtypeStruct((nb_total, WINDOW, D_KV), F32),
        jax.ShapeDtypeStruct((nb_total, WINDOW, D_KV), F32),
    ]
    out_specs = [
        pl.BlockSpec((rows * ROW_TILES, LANES), lambda b, s: (out_blk(b, s), 0)),
        pl.BlockSpec((SUBLANES, rows), lambda b, s: (0, out_blk(b, s))),
        pl.BlockSpec((rows, LANES), lambda b, s: (out_blk(b, s), 0)),
        pl.BlockSpec((N_EXPERTS, LANES), lambda b, s: (0, 0)),
        seq_state(D_CONV, CONV_W - 1), seq_state(D_KV, WINDOW), seq_state(D_KV, WINDOW),
    ]
    scratch = [
        pltpu.VMEM((nseq * (tl + SUBLANES), D_CONV), F32),
        pltpu.VMEM((nseq * (WINDOW + tl), D_KV), F32),
        pltpu.VMEM((nseq * (WINDOW + tl), D_KV), F32),
        pltpu.VMEM((N_EXPERTS, 1), F32),
        pltpu.VMEM((rows // CHUNK * QROWS, 2 * KEYS), F32),
        pltpu.VMEM((rows // CHUNK * QROWS, 2 * KEYS), BF16),
        pltpu.VMEM((rows // CHUNK * QROWS, D_KV), F32),
        pltpu.VMEM((rows // CHUNK * QROWS, LANES), F32),
        pltpu.VMEM((rows // CHUNK * QROWS, LANES), F32),
        pltpu.VMEM((rows, _OFF_Q), F32),
        pltpu.VMEM((rows, D_MODEL), BF16),
        pltpu.VMEM((rows, D_MODEL) if defer_tail else (SUBLANES, LANES), F32),
        pltpu.VMEM((rows, D_MODEL) if defer_tail else (SUBLANES, LANES), F32),
    ]
    kern = functools.partial(_mixer_kernel, nseq=nseq, tl=tl, alpha=alpha, mask_history=mask_history,
                             defer_tail=defer_tail)
    return pl.pallas_call(
        kern, grid=(nb, steps), in_specs=in_specs, out_specs=out_specs, out_shape=out_shape,
        scratch_shapes=scratch, name="mixer",
        compiler_params=pltpu.CompilerParams(dimension_semantics=("arbitrary", "arbitrary"),
                                             vmem_limit_bytes=VMEM_LIMIT),
    )(x, cconv, ck, cv, w_in, convw, convb, abias, sinks, gconv, gattn, w_out, ln1g, ln1b, wr, br, tri, keysel, counts_in)


_PAD_PIECES = tuple(2 ** k for k in range(EXPERT_BLOCK.bit_length() - 2, -1, -1))


def _tile_rows(row, n=1):
    return pl.ds(pl.multiple_of(row * ROW_TILES, ROW_TILES), n * ROW_TILES)


def _row_copy(src, src_row, dst, dst_row, sem):
    return pltpu.make_async_copy(src.at[_tile_rows(src_row), :], dst.at[_tile_rows(dst_row), :], sem)


def _dispatch_kernel(zstart_ref, zcount_ref, tail_ref, pos1_ref, pos2_ref, xa_ref, xb_ref, xs_hbm, zeros_v, sem, zsem,
                     *, tb, nba):
    step = pl.program_id(0)

    def zero_copy(start, piece):
        return pltpu.make_async_copy(zeros_v.at[_tile_rows(0, piece), :], xs_hbm.at[_tile_rows(start, piece), :], zsem)

    def pad_copy(e, piece, taken):
        return zero_copy(zstart_ref[e] + taken, piece)

    def for_each_pad_piece(fn):
        def per_tail_block(j, carry):
            for part in range(EXPERT_BLOCK // _PAD_PIECES[0]):
                fn(zero_copy(tail_ref[0] + j * EXPERT_BLOCK + part * _PAD_PIECES[0], _PAD_PIECES[0]))
            return carry

        lax.fori_loop(0, tail_ref[1], per_tail_block, 0)

        def per_expert(e, carry):
            count = zcount_ref[e]
            taken = 0
            for piece in _PAD_PIECES:
                present = (count & piece) != 0

                @pl.when(present)
                def _(piece=piece, taken=taken):
                    fn(pad_copy(e, piece, taken))

                taken = taken + jnp.where(present, piece, 0)
            return carry

        lax.fori_loop(0, N_EXPERTS, per_expert, 0)

    @pl.when(step == 0)
    def _():
        zeros_v[...] = jnp.zeros_like(zeros_v)
        for_each_pad_piece(lambda c: c.start())

    def issue_rows(x_ref):
        def issue(t, carry):
            _row_copy(x_ref, t, xs_hbm, pos1_ref[t], sem).start(priority=0)
            _row_copy(x_ref, t, xs_hbm, pos2_ref[t], sem).start(priority=1)
            return carry

        lax.fori_loop(0, tb, issue, 0, unroll=8)

    @pl.when(step < nba)
    def _():
        issue_rows(xa_ref)

    @pl.when(step >= nba)
    def _():
        issue_rows(xb_ref)

    pltpu.make_async_copy(xs_hbm.at[_tile_rows(0, 2 * tb), :], xs_hbm.at[_tile_rows(0, 2 * tb), :], sem).wait()

    @pl.when(step == 0)
    def _():
        for_each_pad_piece(lambda c: c.wait())


def _dispatch(xa_rows, xb_rows, pos1, pos2, zstart, zcount, tail, n_slots, *, tb):
    nba, nbb = xa_rows.shape[0] // ROW_TILES // tb, xb_rows.shape[0] // ROW_TILES // tb
    kern = functools.partial(_dispatch_kernel, tb=tb, nba=nba)
    grid_spec = pltpu.PrefetchScalarGridSpec(
        num_scalar_prefetch=3, grid=(nba + nbb,),
        in_specs=[pl.BlockSpec((tb,), lambda i, *_: (i,), memory_space=pltpu.SMEM),
                  pl.BlockSpec((tb,), lambda i, *_: (i,), memory_space=pltpu.SMEM),
                  pl.BlockSpec((tb * ROW_TILES, LANES), lambda i, *_: (jnp.minimum(i, nba - 1), 0)),
                  pl.BlockSpec((tb * ROW_TILES, LANES), lambda i, *_: (jnp.maximum(i - nba, 0), 0))],
        out_specs=pl.BlockSpec(memory_space=pl.ANY),
        scratch_shapes=[pltpu.VMEM((_PAD_PIECES[0] * ROW_TILES, LANES), F32),
                        pltpu.SemaphoreType.DMA(()), pltpu.SemaphoreType.DMA(())],
    )
    return pl.pallas_call(
        kern, grid_spec=grid_spec, out_shape=jax.ShapeDtypeStruct((n_slots * ROW_TILES, LANES), F32),
        name="dispatch",
        compiler_params=pltpu.CompilerParams(dimension_semantics=("arbitrary",)),
    )(zstart, zcount, tail, pos1, pos2, xa_rows, xb_rows)


def _expert_kernel(be_ref, src_ref, used_ref, xs_ref, wg_ref, wu_ref, wd_ref, yb_ref, wg_b, wu_b, wd_b):
    i = pl.program_id(0)
    e = be_ref[i]
    prev = be_ref[jnp.maximum(i - 1, 0)]

    @pl.when(jnp.logical_or(i == 0, e != prev))
    def _():
        wg_b[...] = wg_ref[...].astype(BF16)
        wu_b[...] = wu_ref[...].astype(BF16)
        wd_b[...] = wd_ref[...].astype(BF16)

    @pl.when(used_ref[i] != 0)
    def _():
        x = _load_rows(xs_ref, EXPERT_BLOCK).astype(BF16)
        g = _dot(x, wg_b[...])
        u = _dot(x, wu_b[...])
        hid = (g / (1.0 + jnp.exp(-g))) * u
        _store_rows(yb_ref, _dot(hid.astype(BF16), wd_b[...]), EXPERT_BLOCK)

    @pl.when(used_ref[i] == 0)
    def _():
        yb_ref[...] = jnp.zeros_like(yb_ref)


def _experts(xs, blk_e, blk_src, blk_used, w_gate, w_up, w_down, layer):
    n_slots = xs.shape[0] // ROW_TILES
    nblk = n_slots // EXPERT_BLOCK

    def wspec(shape):
        return pl.BlockSpec((None, None) + shape, lambda i, be, src, used: (layer, be[i], 0, 0))

    grid_spec = pltpu.PrefetchScalarGridSpec(
        num_scalar_prefetch=3, grid=(nblk,),
        in_specs=[pl.BlockSpec((EXPERT_BLOCK * ROW_TILES, LANES), lambda i, be, src, used: (src[i], 0)),
                  wspec((D_MODEL, D_EXP)), wspec((D_MODEL, D_EXP)), wspec((D_EXP, D_MODEL))],
        out_specs=pl.BlockSpec((EXPERT_BLOCK * ROW_TILES, LANES), lambda i, be, src, used: (i, 0)),
        scratch_shapes=[pltpu.VMEM((D_MODEL, D_EXP), BF16), pltpu.VMEM((D_MODEL, D_EXP), BF16),
                        pltpu.VMEM((D_EXP, D_MODEL), BF16)],
    )
    return pl.pallas_call(
        _expert_kernel, grid_spec=grid_spec, out_shape=jax.ShapeDtypeStruct((n_slots * ROW_TILES, LANES), F32),
        name="experts",
        compiler_params=pltpu.CompilerParams(dimension_semantics=("arbitrary",), vmem_limit_bytes=VMEM_LIMIT),
    )(blk_e, blk_src, blk_used, xs, w_gate, w_up, w_down)


def _combine_kernel(pos1_ref, pos2_ref, next1_ref, next2_ref, xa_ref, xb_ref, ga_ref, gb_ref, yb_hbm, g_ref, b_ref,
                    outa_ref, outb_ref, ybuf, sems, *, tb, alpha, nba, nsteps):
    step = pl.program_id(0)
    slot = step % 2

    def gather(p1_ref, p2_ref, to_slot):
        def issue(t, carry):
            _row_copy(yb_hbm, p1_ref[t], ybuf.at[to_slot].at[0], t, sems.at[to_slot]).start(priority=0)
            _row_copy(yb_hbm, p2_ref[t], ybuf.at[to_slot].at[1], t, sems.at[to_slot]).start(priority=1)
            return carry

        lax.fori_loop(0, tb, issue, 0, unroll=8)

    @pl.when(step == 0)
    def _():
        gather(pos1_ref, pos2_ref, 0)

    @pl.when(step + 1 < nsteps)
    def _():
        gather(next1_ref, next2_ref, 1 - slot)

    for k in range(TOP_K):
        pltpu.make_async_copy(yb_hbm.at[_tile_rows(0, tb), :], ybuf.at[slot].at[k], sems.at[slot]).wait()

    def finish(x1_ref, gcol_ref, out_ref):
        gates = gcol_ref[...]
        y = (gates[:, _R_G1:_R_G1 + 1] * _load_rows(ybuf.at[slot].at[0], tb)
             + gates[:, _R_G2:_R_G2 + 1] * _load_rows(ybuf.at[slot].at[1], tb))
        out_ref[...] = _layer_norm(alpha * _load_rows(x1_ref, tb) + y, g_ref[...], b_ref[...])

    @pl.when(step < nba)
    def _():
        finish(xa_ref, ga_ref, outa_ref)

    @pl.when(step >= nba)
    def _():
        finish(xb_ref, gb_ref, outb_ref)


def _combine(xa_rows, xb_rows, gcol_a, gcol_b, yb, pos1, pos2, ln2g, ln2b, *, tb, alpha):
    nba, nbb = xa_rows.shape[0] // ROW_TILES // tb, xb_rows.shape[0] // ROW_TILES // tb
    nsteps = nba + nbb
    kern = functools.partial(_combine_kernel, tb=tb, alpha=alpha, nba=nba, nsteps=nsteps)

    def first(i):
        return jnp.minimum(i, nba - 1)

    def second(i):
        return jnp.maximum(i - nba, 0)

    def nxt(i):
        return jnp.minimum(i + 1, nsteps - 1)

    smem = functools.partial(pl.BlockSpec, (tb,), memory_space=pltpu.SMEM)
    return pl.pallas_call(
        kern, grid=(nsteps,),
        in_specs=[smem(lambda i: (i,)), smem(lambda i: (i,)), smem(lambda i: (nxt(i),)), smem(lambda i: (nxt(i),)),
                  pl.BlockSpec((tb * ROW_TILES, LANES), lambda i: (first(i), 0)),
                  pl.BlockSpec((tb * ROW_TILES, LANES), lambda i: (second(i), 0)),
                  pl.BlockSpec((tb, LANES), lambda i: (first(i), 0)),
                  pl.BlockSpec((tb, LANES), lambda i: (second(i), 0)),
                  pl.BlockSpec(memory_space=pl.ANY),
                  pl.BlockSpec((1, D_MODEL), lambda i: (0, 0)),
                  pl.BlockSpec((1, D_MODEL), lambda i: (0, 0))],
        out_specs=[pl.BlockSpec((tb, D_MODEL), lambda i: (first(i), 0)),
                   pl.BlockSpec((tb, D_MODEL), lambda i: (second(i), 0))],
        out_shape=[jax.ShapeDtypeStruct((nba * tb, D_MODEL), F32), jax.ShapeDtypeStruct((nbb * tb, D_MODEL), F32)],
        scratch_shapes=[pltpu.VMEM((2, TOP_K, tb * ROW_TILES, LANES), F32), pltpu.SemaphoreType.DMA((2,))],
        name="combine",
        compiler_params=pltpu.CompilerParams(dimension_semantics=("arbitrary",), vmem_limit_bytes=VMEM_LIMIT),
    )(pos1, pos2, pos1, pos2, xa_rows, xb_rows, gcol_a, gcol_b, yb, ln2g, ln2b)


def _attn_column_order():
    g, kvh, d = np.meshgrid(np.arange(GROUP), np.arange(N_KV_HEADS), np.arange(HEAD_DIM), indexing='ij')
    return ((kvh * GROUP + g) * HEAD_DIM + d).reshape(-1)


def _alibi_bias():
    slopes = np.asarray([2.0 ** (-8.0 * (h + 1) / N_HEADS) for h in range(N_HEADS)], np.float32)
    qi = np.arange(CHUNK, dtype=np.int32)[:, None]
    sj = np.arange(KEYS, dtype=np.int32)[None, :]
    dist = np.abs(qi + WINDOW - sj).astype(np.float32)
    bias = -slopes.reshape(N_KV_HEADS, GROUP, 1, 1) * dist
    bias = np.transpose(bias, (1, 2, 0, 3)).reshape(GROUP * CHUNK, N_KV_HEADS * KEYS)
    return jnp.asarray(bias, F32)


def _key_selector():
    row_head = np.arange(N_KV_HEADS * KEYS)[:, None] // KEYS
    lane_head = np.arange(D_KV)[None, :] // HEAD_DIM
    return jnp.asarray((row_head == lane_head).astype(np.float32), BF16)


def _strict_upper():
    r = np.arange(RANK_CHUNK)
    return jnp.asarray((r[:, None] < r[None, :]).astype(np.float32), BF16)


def _slot_plan(route, counts, n_tok):
    nblk = (n_tok * TOP_K) // EXPERT_BLOCK + N_EXPERTS
    experts = route[_R_E1:_R_E2 + 1].astype(jnp.int32)
    ranks = route[_R_RANK1:_R_RANK2 + 1].astype(jnp.int32)
    cnt = counts[:, 0].astype(jnp.int32)
    padded = (cnt + EXPERT_BLOCK - 1) // EXPERT_BLOCK * EXPERT_BLOCK
    pad_end = jnp.cumsum(padded)
    pad_start = pad_end - padded
    ids = jnp.arange(N_EXPERTS, dtype=jnp.int32)
    pos = jnp.sum(jnp.where(experts[..., None] == ids, pad_start, 0), axis=-1) + ranks
    blk_first = jnp.arange(nblk, dtype=jnp.int32) * EXPERT_BLOCK
    blk_e = jnp.minimum(jnp.sum((pad_end[None, :] <= blk_first[:, None]).astype(jnp.int32), axis=-1), N_EXPERTS - 1)
    blk_used = (blk_first < pad_end[-1]).astype(jnp.int32)
    blk_src = jnp.minimum(jnp.arange(nblk, dtype=jnp.int32), pad_end[-1] // EXPERT_BLOCK - 1)
    tail = jnp.stack([pad_end[-1], nblk - pad_end[-1] // EXPERT_BLOCK]).astype(jnp.int32)
    return pos[0], pos[1], pad_start + cnt, padded - cnt, tail, blk_e, blk_src, blk_used, nblk * EXPERT_BLOCK


def _layer(xp, xs, cache, mix_prm, consts, moe_prm, layer, *, tl_prompt, nseq_sample, alpha):
    (bp, tp, _), (bs, ts, _) = xp.shape, xs.shape
    np_tok, ns_tok = bp * tp, bs * ts
    tb = ROWS
    assert np_tok % tb == 0 and ns_tok % tb == 0
    zeros_conv = jnp.zeros((bp, CONV_W - 1, D_CONV), F32)
    zeros_kv = jnp.zeros((bp, WINDOW, D_KV), F32)
    zero_counts = jnp.zeros((N_EXPERTS, LANES), F32)
    x1p, route_p, gcol_p, counts_p, *state_p = _mixer(
        xp, zeros_conv, zeros_kv, zeros_kv, zero_counts, mix_prm, consts,
        nseq=1, tl=tl_prompt, mask_history=True, alpha=alpha)
    x1s, route_s, gcol_s, counts, *state_s = _mixer(
        xs, *cache, counts_p, mix_prm, consts, nseq=nseq_sample, tl=ts, mask_history=False, alpha=alpha)
    route = jnp.concatenate([route_p, route_s], axis=1)
    pos1, pos2, zstart, zcount, tail, blk_e, blk_src, blk_used, n_slots = _slot_plan(route, counts, np_tok + ns_tok)
    sorted_rows = _dispatch(x1p, x1s, pos1, pos2, zstart, zcount, tail, n_slots, tb=tb)
    w_gate, w_up, w_down, ln2g, ln2b = moe_prm
    yb = _experts(sorted_rows, blk_e, blk_src, blk_used, w_gate, w_up, w_down, layer)
    out_p, out_s = _combine(x1p, x1s, gcol_p, gcol_s, yb, pos1, pos2, ln2g, ln2b, tb=tb, alpha=alpha)
    return out_p.reshape(bp, tp, D_MODEL), out_s.reshape(bs, ts, D_MODEL), state_p, state_s


def kernel(x_prompt, x_sample, cache_conv, cache_k, cache_v, w_in, conv_w, conv_b, attn_sinks, g_conv, g_attn,
           w_out, ln1_g, ln1_b, router_group_w, router_group_b, router_expert_w, router_expert_b,
           expert_w_gate, expert_w_up, expert_w_down, ln2_g, ln2_b):
    depth = w_in.shape[0]
    alpha = (2 * depth) ** 0.25
    batch, seq = x_prompt.shape[0], x_prompt.shape[1]
    dec_batch, dec_seq = x_sample.shape[0], x_sample.shape[1]
    assert dec_seq == CHUNK and seq % CHUNK == 0
    consts = (_alibi_bias(), _strict_upper(), _key_selector())
    attn_order = _attn_column_order()
    tl_prompt = min(MIXER_ROWS, seq)
    nseq_sample = min(MIXER_ROWS // dec_seq, dec_batch)
    assert seq % tl_prompt == 0 and dec_batch % nseq_sample == 0

    xp, xs = x_prompt, x_sample
    states = [[] for _ in range(6)]
    for l in range(depth):
        sink_col = jnp.broadcast_to(jnp.repeat(attn_sinks[l].astype(F32), CHUNK).reshape(N_KV_HEADS, QROWS, 1),
                                    (N_KV_HEADS, QROWS, LANES))
        wr = jnp.zeros((D_MODEL, LANES), F32)
        wr = wr.at[:, 0:N_EXPERTS].set(router_expert_w[l]).at[:, N_EXPERTS:N_EXPERTS + N_GROUPS].set(router_group_w[l])
        br = jnp.zeros((1, LANES), F32)
        br = br.at[0, 0:N_EXPERTS].set(router_expert_b[l]).at[0, N_EXPERTS:N_EXPERTS + N_GROUPS].set(router_group_b[l])
        w_in_l = w_in[l].at[:, _OFF_Q:_OFF_Q + D_ATTN].set(w_in[l][:, _OFF_Q + attn_order])
        w_out_l = w_out[l].at[D_CONV:D_CONV + D_ATTN, :].set(w_out[l][D_CONV + attn_order, :])
        mix_prm = (w_in_l.astype(BF16), conv_w[l], conv_b[l].reshape(1, D_CONV), sink_col,
                   g_conv[l].reshape(1, D_CONV), g_attn[l][attn_order].reshape(1, D_ATTN), w_out_l.astype(BF16),
                   ln1_g[l].reshape(1, D_MODEL), ln1_b[l].reshape(1, D_MODEL), wr.astype(BF16), br)
        moe_prm = (expert_w_gate, expert_w_up, expert_w_down,
                   ln2_g[l].reshape(1, D_MODEL), ln2_b[l].reshape(1, D_MODEL))
        cache = (cache_conv[l], cache_k[l].reshape(dec_batch, WINDOW, D_KV), cache_v[l].reshape(dec_batch, WINDOW, D_KV))
        xp, xs, state_p, state_s = _layer(xp, xs, cache, mix_prm, consts, moe_prm, l,
                                          tl_prompt=tl_prompt, nseq_sample=nseq_sample, alpha=alpha)
        for lst, val in zip(states, state_p + state_s):
            lst.append(val)

    def kv(lst, nb):
        return jnp.stack(lst).reshape(depth, nb, WINDOW, N_KV_HEADS, HEAD_DIM)

    return (xp, xs, jnp.stack(states[0]), kv(states[1], batch), kv(states[2], batch),
            jnp.stack(states[3]), kv(states[4], dec_batch), kv(states[5], dec_batch))
```

```python
import functools

import numpy as np
import jax
import jax.numpy as jnp
from jax import lax
from jax.experimental import pallas as pl
from jax.experimental.pallas import tpu as pltpu

D_MODEL = 1024
D_CONV = 512
CONV_W = 3
N_HEADS = 8
N_KV_HEADS = 2
GROUP = N_HEADS // N_KV_HEADS
HEAD_DIM = 64
D_ATTN = N_HEADS * HEAD_DIM
D_KV = N_KV_HEADS * HEAD_DIM
WINDOW = 128
CHUNK = 64
KEYS = WINDOW + CHUNK
QROWS = GROUP * CHUNK
N_GROUPS = 4
EXP_PER_GROUP = 8
N_EXPERTS = N_GROUPS * EXP_PER_GROUP
TOP_K = 2
D_EXP = 512
LN_EPS = 1e-5
NEG_INF = -1e30

SUBLANES = 8
LANES = 128
ROW_TILES = D_MODEL // LANES
assert ROW_TILES == SUBLANES
ROWS = 512
MIXER_ROWS = 512
EXPERT_BLOCK = 512
RANK_CHUNK = 256
SOFTMAX_ROWS = 64
PROJ_CHUNK = 2 * LANES
MIX_PIECES = 4
VMEM_LIMIT = 56 * 1024 * 1024

F32 = jnp.float32
BF16 = jnp.bfloat16

_OFF_B, _OFF_C, _OFF_H = 0, D_CONV, 2 * D_CONV
_OFF_Q = 3 * D_CONV
_OFF_K = _OFF_Q + D_ATTN
_OFF_V = _OFF_K + D_KV
D_IN = _OFF_V + D_KV

_R_E1, _R_E2, _R_RANK1, _R_RANK2, _R_G1, _R_G2 = 0, 1, 2, 3, 4, 5


def _load_rows(ref, n):
    return jnp.concatenate([ref[pl.ds(j, n, stride=ROW_TILES), :] for j in range(ROW_TILES)], axis=1)


def _store_rows(ref, val, n):
    for j in range(ROW_TILES):
        ref[pl.ds(j, n, stride=ROW_TILES), :] = val[:, j * LANES:(j + 1) * LANES]


def _dot(a, b):
    return jnp.dot(a, b, preferred_element_type=F32)


def _rms_norm(x, g):
    return x * lax.rsqrt(jnp.mean(jnp.square(x), -1, keepdims=True) + LN_EPS) * g


def _layer_norm(x, g, b):
    mu = jnp.mean(x, -1, keepdims=True)
    xc = x - mu
    var = jnp.mean(jnp.square(xc), -1, keepdims=True)
    return xc * lax.rsqrt(var + LN_EPS) * g + b


def _mixer_kernel(x_ref, cconv_ref, ck_ref, cv_ref, w_in_ref, convw_ref, convb_ref, abias_ref,
                  sink_ref, gconv_ref, gattn_ref, w_out_ref, ln1g_ref, ln1b_ref, wr_ref, br_ref,
                  tri_ref, keysel_ref, counts_in_ref,
                  x1_ref, route_ref, gcol_ref, counts_ref, sconv_ref, sk_ref, sv_ref,
                  uext, kext, vext, cnt_s, s_ref, e_ref, sinkden_ref, m0_ref, m1_ref, proj_ref, x1b_s, xprev_s,
                  mixed_s, *, nseq, tl, alpha, mask_history, defer_tail):
    b = pl.program_id(0)
    s = pl.program_id(1)
    rows = nseq * tl
    upitch = tl + SUBLANES
    kpitch = WINDOW + tl

    @pl.when(jnp.logical_and(b == 0, s == 0))
    def _():
        cnt_s[...] = counts_in_ref[:, 0:1]
        if defer_tail:
            xprev_s[...] = jnp.zeros_like(xprev_s)
            mixed_s[...] = jnp.zeros_like(mixed_s)

    @pl.when(s == 0)
    def _():
        for j in range(nseq):
            uext[j * upitch + SUBLANES - 2:j * upitch + SUBLANES, :] = cconv_ref[j]
            kext[j * kpitch:j * kpitch + WINDOW, :] = ck_ref[j]
            vext[j * kpitch:j * kpitch + WINDOW, :] = cv_ref[j]

    def norm_rows(r0, r1, x_rows, mixed_rows):
        x1 = _layer_norm(alpha * x_rows + mixed_rows, ln1g_ref[...], ln1b_ref[...])
        for j in range(ROW_TILES):
            x1_ref[pl.ds(r0 * ROW_TILES + j, r1 - r0, stride=ROW_TILES), :] = x1[:, j * LANES:(j + 1) * LANES]
        x1b_s[r0:r1, :] = x1.astype(BF16)

    def route(valid):
        logits_t = (_dot(x1b_s[...], wr_ref[...]) + br_ref[...]).T
        sub = lax.broadcasted_iota(jnp.int32, (SUBLANES, rows), 0)
        gl = jnp.where(sub < N_GROUPS, logits_t[N_EXPERTS:N_EXPERTS + SUBLANES, :], -jnp.inf)
        gmax = jnp.max(gl, axis=0, keepdims=True)
        grp = jnp.min(jnp.where(gl == gmax, sub, SUBLANES), axis=0, keepdims=True)
        p_grp = 1.0 / jnp.sum(jnp.exp(gl - gmax), axis=0, keepdims=True)
        el = logits_t[(N_GROUPS - 1) * EXP_PER_GROUP:N_GROUPS * EXP_PER_GROUP, :]
        for g in range(N_GROUPS - 2, -1, -1):
            el = jnp.where(grp == g, logits_t[g * EXP_PER_GROUP:(g + 1) * EXP_PER_GROUP, :], el)
        v1 = jnp.max(el, axis=0, keepdims=True)
        i1 = jnp.min(jnp.where(el == v1, sub, SUBLANES), axis=0, keepdims=True)
        el2 = jnp.where(sub == i1, -jnp.inf, el)
        v2 = jnp.max(el2, axis=0, keepdims=True)
        i2 = jnp.min(jnp.where(el2 == v2, sub, SUBLANES), axis=0, keepdims=True)
        e2 = jnp.exp(v2 - v1)
        gate1 = p_grp * (1.0 / (1.0 + e2))
        gate2 = p_grp * (e2 / (1.0 + e2))

        chosen = jnp.logical_or(sub == i1, sub == i2)
        onehot = jnp.concatenate(
            [jnp.where(jnp.logical_and(grp == g, chosen), 1.0, 0.0) for g in range(N_GROUPS)], axis=0)
        running = cnt_s[...]
        ranks = []
        for c in range(rows // RANK_CHUNK):
            oh = onehot[:, c * RANK_CHUNK:(c + 1) * RANK_CHUNK]
            ranks.append(_dot(oh.astype(BF16), tri_ref[...]) + running)
            running = running + jnp.sum(oh, axis=1, keepdims=True) * valid
        rank = jnp.concatenate(ranks, axis=1)
        cnt_s[...] = running
        counts_ref[...] = jnp.broadcast_to(running, (N_EXPERTS, LANES))

        ex1 = grp * EXP_PER_GROUP + i1
        ex2 = grp * EXP_PER_GROUP + i2
        erow = lax.broadcasted_iota(jnp.int32, (N_EXPERTS, rows), 0)
        rank1 = jnp.sum(jnp.where(erow == ex1, rank, 0.0), axis=0, keepdims=True)
        rank2 = jnp.sum(jnp.where(erow == ex2, rank, 0.0), axis=0, keepdims=True)
        fields = {_R_E1: ex1.astype(F32), _R_E2: ex2.astype(F32), _R_RANK1: rank1, _R_RANK2: rank2,
                  _R_G1: gate1, _R_G2: gate2}
        record = jnp.zeros((SUBLANES, rows), F32)
        for r, val in fields.items():
            record = jnp.where(sub == r, val, record)
        route_ref[...] = record
        padded = jnp.concatenate([record, jnp.zeros((LANES - SUBLANES, rows), F32)], axis=0)
        gcol_ref[...] = padded.T

    x = x_ref[...].reshape(rows, D_MODEL)
    xb = x.astype(BF16)

    def conv_rows(r0, r1):
        w0, w1, w2 = convw_ref[0:1, :], convw_ref[1:2, :], convw_ref[2:3, :]
        out = []
        for j in range(r0 // tl, -(-r1 // tl)):
            a, b_ = max(r0, j * tl) - j * tl, min(r1, (j + 1) * tl) - j * tl
            base = j * upitch + SUBLANES
            src = pl.ds(j * tl + a, b_ - a)
            uext[base + a:base + b_, :] = proj_ref[src, _OFF_C:_OFF_C + D_CONV] * proj_ref[src, _OFF_H:_OFF_H + D_CONV]
            yc = convb_ref[...] + uext[base + a - 2:base + b_ - 2, :] * w0
            yc = yc + uext[base + a - 1:base + b_ - 1, :] * w1
            yc = yc + uext[base + a:base + b_, :] * w2
            out.append(_rms_norm(proj_ref[src, _OFF_B:_OFF_B + D_CONV] * yc, gconv_ref[...]).astype(BF16))
            if b_ == tl:
                tail = uext[base + tl - 2:base + tl, :]
                sconv_ref[j] = tail
                uext[base - 2:base, :] = tail
        return out[0] if len(out) == 1 else jnp.concatenate(out, axis=0)

    qkv = []
    qkv_chunks = [(c, c + PROJ_CHUNK) for c in range(_OFF_Q, D_IN, PROJ_CHUNK)]
    tail_rows = -(-rows // len(qkv_chunks) // SUBLANES) * SUBLANES
    for p, (c0, c1) in enumerate(qkv_chunks):
        r0, r1 = min(p * tail_rows, rows), min((p + 1) * tail_rows, rows)
        if defer_tail and r1 > r0:
            norm_rows(r0, r1, xprev_s[r0:r1, :], mixed_s[r0:r1, :])
        qkv.append(_dot(xb, w_in_ref[:, c0:c1]))
    qkv = jnp.concatenate(qkv, axis=1)
    if defer_tail:
        route((s > 0).astype(F32))
    q = (qkv[:, 0:D_ATTN] * (HEAD_DIM ** -0.5)).astype(BF16)
    k = qkv[:, D_ATTN:D_ATTN + D_KV]
    v = qkv[:, D_ATTN + D_KV:D_ATTN + 2 * D_KV]
    for j in range(nseq):
        kext[j * kpitch + WINDOW:(j + 1) * kpitch, :] = k[j * tl:(j + 1) * tl]
        vext[j * kpitch + WINDOW:(j + 1) * kpitch, :] = v[j * tl:(j + 1) * tl]

    nchunk = rows // CHUNK
    assert nseq == 1 or tl == CHUNK
    key_stride = CHUNK if nseq == 1 else kpitch
    head0 = lax.broadcasted_iota(jnp.int32, (1, D_KV), 1) < HEAD_DIM

    def windows(ext_ref):
        ext = ext_ref[...].astype(BF16)
        heads = (jnp.where(head0, ext, jnp.zeros_like(ext)), jnp.where(head0, jnp.zeros_like(ext), ext))
        return jnp.stack([jnp.concatenate([h[c * key_stride:c * key_stride + KEYS] for h in heads], axis=0)
                          for c in range(nchunk)])

    q3 = jnp.stack([jnp.concatenate([q[c * CHUNK:(c + 1) * CHUNK, g * D_KV:(g + 1) * D_KV] for g in range(GROUP)], axis=0)
                    for c in range(nchunk)])
    logits = lax.dot_general(q3, windows(kext), (((2,), (2,)), ((0,), (0,))), preferred_element_type=F32)
    s_ref[...] = logits.reshape(nchunk * QROWS, 2 * KEYS)

    assert 2 * KEYS == 3 * LANES
    mid0 = lax.broadcasted_iota(jnp.int32, (1, LANES), 1) < KEYS - LANES
    colk = lax.broadcasted_iota(jnp.int32, (1, 2 * KEYS), 1)
    key = jnp.where(colk < KEYS, colk, colk - KEYS)

    def row_block(i):
        r = i * SOFTMAX_ROWS
        return pl.ds(r, SOFTMAX_ROWS), pl.ds(r % QROWS, SOFTMAX_ROWS), r // QROWS

    def tiles(lg):
        return lg[:, 0:LANES], lg[:, LANES:2 * LANES], lg[:, 2 * LANES:3 * LANES]

    def max_rows(i, masked):
        rows_i, qrows_i, chunk_i = row_block(i)
        lg = s_ref[rows_i, :] + abias_ref[qrows_i, :]
        if masked:
            lg = lg + jnp.where(s * tl + chunk_i * CHUNK - WINDOW + key < 0, NEG_INF, 0.0).astype(F32)
        s_ref[rows_i, :] = lg
        t0, t1, t2 = tiles(lg)
        m0_ref[rows_i, :] = jnp.maximum(jnp.max(jnp.maximum(t0, jnp.where(mid0, t1, NEG_INF)), -1, keepdims=True),
                                        sink_ref[0, qrows_i, :])
        m1_ref[rows_i, :] = jnp.maximum(jnp.max(jnp.maximum(t2, jnp.where(mid0, NEG_INF, t1)), -1, keepdims=True),
                                        sink_ref[1, qrows_i, :])

    def exp_rows(i):
        rows_i, qrows_i, _ = row_block(i)
        t0, t1, t2 = tiles(s_ref[rows_i, :])
        m0, m1 = m0_ref[rows_i, :], m1_ref[rows_i, :]
        e = jnp.concatenate([jnp.exp(t0 - m0), jnp.exp(t1 - jnp.where(mid0, m0, m1)), jnp.exp(t2 - m1)], axis=-1)
        e_ref[rows_i, :] = e.astype(BF16)
        sinkden_ref[rows_i, :] = jnp.where(head0, jnp.exp(sink_ref[0, qrows_i, :] - m0),
                                           jnp.exp(sink_ref[1, qrows_i, :] - m1))

    n_it = nchunk * QROWS // SOFTMAX_ROWS
    n_masked = min(WINDOW // CHUNK, nchunk) * QROWS // SOFTMAX_ROWS if mask_history else 0
    proj_chunks = [(c, min(c + PROJ_CHUNK, _OFF_Q)) for c in range(0, _OFF_Q, PROJ_CHUNK)]
    piece = -(-2 * n_it // len(proj_chunks))
    steps = [functools.partial(max_rows, i, i < n_masked) for i in range(n_it)]
    steps += [functools.partial(exp_rows, i) for i in range(n_it)]
    for p, (c0, c1) in enumerate(proj_chunks):
        for step_fn in steps[p * piece:(p + 1) * piece]:
            step_fn()
        proj_ref[:, c0:c1] = _dot(xb, w_in_ref[:, c0:c1])
    for step_fn in steps[len(proj_chunks) * piece:]:
        step_fn()
    vsel = jnp.concatenate([windows(vext), jnp.broadcast_to(keysel_ref[...], (nchunk, 2 * KEYS, D_KV))], axis=-1)
    n_pieces = min(MIX_PIECES, nchunk)
    cpp = nchunk // n_pieces
    conv_parts, attn_parts = [], []
    for p in range(n_pieces):
        cs = slice(p * cpp, (p + 1) * cpp)
        o2 = lax.dot_general(e_ref[p * cpp * QROWS:(p + 1) * cpp * QROWS, :].reshape(cpp, QROWS, 2 * KEYS), vsel[cs],
                             (((2,), (1,)), ((0,), (0,))), preferred_element_type=F32)
        sinkden = sinkden_ref[p * cpp * QROWS:(p + 1) * cpp * QROWS, :].reshape(cpp, QROWS, D_KV)
        o = o2[..., 0:D_KV] * (1.0 / (o2[..., D_KV:2 * D_KV] + sinkden))
        attn_parts += [jnp.concatenate([o[c, g * CHUNK:(g + 1) * CHUNK, :] for g in range(GROUP)], axis=1)
                       for c in range(cpp)]
        conv_parts.append(conv_rows(p * cpp * CHUNK, (p + 1) * cpp * CHUNK))
    y_attn = jnp.concatenate(attn_parts, axis=0)
    n_conv = jnp.concatenate(conv_parts, axis=0)

    for j in range(nseq):
        sk_ref[j] = kext[j * kpitch + tl:(j + 1) * kpitch, :]
        sv_ref[j] = vext[j * kpitch + tl:(j + 1) * kpitch, :]
    if nseq == 1:
        kext[0:WINDOW, :] = kext[tl:tl + WINDOW, :]
        vext[0:WINDOW, :] = vext[tl:tl + WINDOW, :]

    n_attn = _rms_norm(y_attn, gattn_ref[...]).astype(BF16)
    mixed = _dot(n_attn, w_out_ref[D_CONV:D_CONV + D_ATTN, :]) + _dot(n_conv, w_out_ref[0:D_CONV, :])
    if defer_tail:
        xprev_s[...] = x
        mixed_s[...] = mixed
    else:
        norm_rows(0, rows, x, mixed)
        route(1.0)


def _mixer(x, cconv, ck, cv, counts_in, prm, consts, *, nseq, tl, mask_history, alpha):
    nb_total, t_total = x.shape[0], x.shape[1]
    nb, ns = nb_total // nseq, t_total // tl
    defer_tail = ns > 1
    steps = ns + 1 if defer_tail else ns

    def in_blk(s):
        return jnp.minimum(s, ns - 1)

    def out_blk(b, s):
        return b * ns + (jnp.maximum(s - 1, 0) if defer_tail else s)

    rows = nseq * tl
    n_tok = nb_total * t_total
    w_in, convw, convb, sinks, gconv, gattn, w_out, ln1g, ln1b, wr, br = prm
    abias, tri, keysel = consts

    def full(a):
        return pl.BlockSpec(a.shape, lambda b, s, _n=a.ndim: (0,) * _n)

    def seq_state(width, nrows):
        return pl.BlockSpec((nseq, nrows, width), lambda b, s: (b, 0, 0))

    in_specs = [
        pl.BlockSpec((nseq, tl, D_MODEL), lambda b, s: (b, in_blk(s), 0)),
        seq_state(D_CONV, CONV_W - 1), seq_state(D_KV, WINDOW), seq_state(D_KV, WINDOW),
        full(w_in), full(convw), full(convb), full(abias), full(sinks), full(gconv), full(gattn),
        full(w_out), full(ln1g), full(ln1b), full(wr), full(br), full(tri), full(keysel), full(counts_in),
    ]
    out_shape = [
        jax.ShapeDtypeStruct((n_tok * ROW_TILES, LANES), F32),
        jax.ShapeDtypeStruct((SUBLANES, n_tok), F32),
        jax.ShapeDtypeStruct((n_tok, LANES), F32),
        jax.ShapeDtypeStruct((N_EXPERTS, LANES), F32),
        jax.ShapeDtypeStruct((nb_total, CONV_W - 1, D_CONV), F32),
        jax.ShapeDtypeStruct((nb_total, WINDOW, D_KV), F32),
        jax.ShapeDtypeStruct((nb_total, WINDOW, D_KV), F32),
    ]
    out_specs = [
        pl.BlockSpec((rows * ROW_TILES, LANES), lambda b, s: (out_blk(b, s), 0)),
        pl.BlockSpec((SUBLANES, rows), lambda b, s: (0, out_blk(b, s))),
        pl.BlockSpec((rows, LANES), lambda b, s: (out_blk(b, s), 0)),
        pl.BlockSpec((N_EXPERTS, LANES), lambda b, s: (0, 0)),
        seq_state(D_CONV, CONV_W - 1), seq_state(D_KV, WINDOW), seq_state(D_KV, WINDOW),
    ]
    scratch = [
        pltpu.VMEM((nseq * (tl + SUBLANES), D_CONV), F32),
        pltpu.VMEM((nseq * (WINDOW + tl), D_KV), F32),
        pltpu.VMEM((nseq * (WINDOW + tl), D_KV), F32),
        pltpu.VMEM((N_EXPERTS, 1), F32),
        pltpu.VMEM((rows // CHUNK * QROWS, 2 * KEYS), F32),
        pltpu.VMEM((rows // CHUNK * QROWS, 2 * KEYS), BF16),
        pltpu.VMEM((rows // CHUNK * QROWS, D_KV), F32),
        pltpu.VMEM((rows // CHUNK * QROWS, LANES), F32),
        pltpu.VMEM((rows // CHUNK * QROWS, LANES), F32),
        pltpu.VMEM((rows, _OFF_Q), F32),
        pltpu.VMEM((rows, D_MODEL), BF16),
        pltpu.VMEM((rows, D_MODEL) if defer_tail else (SUBLANES, LANES), F32),
        pltpu.VMEM((rows, D_MODEL) if defer_tail else (SUBLANES, LANES), F32),
    ]
    kern = functools.partial(_mixer_kernel, nseq=nseq, tl=tl, alpha=alpha, mask_history=mask_history,
                             defer_tail=defer_tail)
    return pl.pallas_call(
        kern, grid=(nb, steps), in_specs=in_specs, out_specs=out_specs, out_shape=out_shape,
        scratch_shapes=scratch, name="mixer",
        compiler_params=pltpu.CompilerParams(dimension_semantics=("arbitrary", "arbitrary"),
                                             vmem_limit_bytes=VMEM_LIMIT),
    )(x, cconv, ck, cv, w_in, convw, convb, abias, sinks, gconv, gattn, w_out, ln1g, ln1b, wr, br, tri, keysel, counts_in)


_PAD_PIECES = tuple(2 ** k for k in range(EXPERT_BLOCK.bit_length() - 2, -1, -1))


def _tile_rows(row, n=1):
    return pl.ds(pl.multiple_of(row * ROW_TILES, ROW_TILES), n * ROW_TILES)


def _row_copy(src, src_row, dst, dst_row, sem):
    return pltpu.make_async_copy(src.at[_tile_rows(src_row), :], dst.at[_tile_rows(dst_row), :], sem)


def _dispatch_kernel(zstart_ref, zcount_ref, tail_ref, pos1_ref, pos2_ref, xa_ref, xb_ref, xs_hbm, zeros_v, sem, zsem,
                     *, tb, nba):
    step = pl.program_id(0)

    def zero_copy(start, piece):
        return pltpu.make_async_copy(zeros_v.at[_tile_rows(0, piece), :], xs_hbm.at[_tile_rows(start, piece), :], zsem)

    def pad_copy(e, piece, taken):
        return zero_copy(zstart_ref[e] + taken, piece)

    def for_each_pad_piece(fn):
        def per_tail_block(j, carry):
            for part in range(EXPERT_BLOCK // _PAD_PIECES[0]):
                fn(zero_copy(tail_ref[0] + j * EXPERT_BLOCK + part * _PAD_PIECES[0], _PAD_PIECES[0]))
            return carry

        lax.fori_loop(0, tail_ref[1], per_tail_block, 0)

        def per_expert(e, carry):
            count = zcount_ref[e]
            taken = 0
            for piece in _PAD_PIECES:
                present = (count & piece) != 0

                @pl.when(present)
                def _(piece=piece, taken=taken):
                    fn(pad_copy(e, piece, taken))

                taken = taken + jnp.where(present, piece, 0)
            return carry

        lax.fori_loop(0, N_EXPERTS, per_expert, 0)

    @pl.when(step == 0)
    def _():
        zeros_v[...] = jnp.zeros_like(zeros_v)
        for_each_pad_piece(lambda c: c.start())

    def issue_rows(x_ref):
        def issue(t, carry):
            _row_copy(x_ref, t, xs_hbm, pos1_ref[t], sem).start(priority=0)
            _row_copy(x_ref, t, xs_hbm, pos2_ref[t], sem).start(priority=1)
            return carry

        lax.fori_loop(0, tb, issue, 0, unroll=8)

    @pl.when(step < nba)
    def _():
        issue_rows(xa_ref)

    @pl.when(step >= nba)
    def _():
        issue_rows(xb_ref)

    pltpu.make_async_copy(xs_hbm.at[_tile_rows(0, 2 * tb), :], xs_hbm.at[_tile_rows(0, 2 * tb), :], sem).wait()

    @pl.when(step == 0)
    def _():
        for_each_pad_piece(lambda c: c.wait())


def _dispatch(xa_rows, xb_rows, pos1, pos2, zstart, zcount, tail, n_slots, *, tb):
    nba, nbb = xa_rows.shape[0] // ROW_TILES // tb, xb_rows.shape[0] // ROW_TILES // tb
    kern = functools.partial(_dispatch_kernel, tb=tb, nba=nba)
    grid_spec = pltpu.PrefetchScalarGridSpec(
        num_scalar_prefetch=3, grid=(nba + nbb,),
        in_specs=[pl.BlockSpec((tb,), lambda i, *_: (i,), memory_space=pltpu.SMEM),
                  pl.BlockSpec((tb,), lambda i, *_: (i,), memory_space=pltpu.SMEM),
                  pl.BlockSpec((tb * ROW_TILES, LANES), lambda i, *_: (jnp.minimum(i, nba - 1), 0)),
                  pl.BlockSpec((tb * ROW_TILES, LANES), lambda i, *_: (jnp.maximum(i - nba, 0), 0))],
        out_specs=pl.BlockSpec(memory_space=pl.ANY),
        scratch_shapes=[pltpu.VMEM((_PAD_PIECES[0] * ROW_TILES, LANES), F32),
                        pltpu.SemaphoreType.DMA(()), pltpu.SemaphoreType.DMA(())],
    )
    return pl.pallas_call(
        kern, grid_spec=grid_spec, out_shape=jax.ShapeDtypeStruct((n_slots * ROW_TILES, LANES), F32),
        name="dispatch",
        compiler_params=pltpu.CompilerParams(dimension_semantics=("arbitrary",)),
    )(zstart, zcount, tail, pos1, pos2, xa_rows, xb_rows)


def _expert_kernel(be_ref, src_ref, used_ref, xs_ref, wg_ref, wu_ref, wd_ref, yb_ref, wg_b, wu_b, wd_b):
    i = pl.program_id(0)
    e = be_ref[i]
    prev = be_ref[jnp.maximum(i - 1, 0)]

    @pl.when(jnp.logical_or(i == 0, e != prev))
    def _():
        wg_b[...] = wg_ref[...].astype(BF16)
        wu_b[...] = wu_ref[...].astype(BF16)
        wd_b[...] = wd_ref[...].astype(BF16)

    @pl.when(used_ref[i] != 0)
    def _():
        x = _load_rows(xs_ref, EXPERT_BLOCK).astype(BF16)
        g = _dot(x, wg_b[...])
        u = _dot(x, wu_b[...])
        hid = (g / (1.0 + jnp.exp(-g))) * u
        _store_rows(yb_ref, _dot(hid.astype(BF16), wd_b[...]), EXPERT_BLOCK)

    @pl.when(used_ref[i] == 0)
    def _():
        yb_ref[...] = jnp.zeros_like(yb_ref)


def _experts(xs, blk_e, blk_src, blk_used, w_gate, w_up, w_down, layer):
    n_slots = xs.shape[0] // ROW_TILES
    nblk = n_slots // EXPERT_BLOCK

    def wspec(shape):
        return pl.BlockSpec((None, None) + shape, lambda i, be, src, used: (layer, be[i], 0, 0))

    grid_spec = pltpu.PrefetchScalarGridSpec(
        num_scalar_prefetch=3, grid=(nblk,),
        in_specs=[pl.BlockSpec((EXPERT_BLOCK * ROW_TILES, LANES), lambda i, be, src, used: (src[i], 0)),
                  wspec((D_MODEL, D_EXP)), wspec((D_MODEL, D_EXP)), wspec((D_EXP, D_MODEL))],
        out_specs=pl.BlockSpec((EXPERT_BLOCK * ROW_TILES, LANES), lambda i, be, src, used: (i, 0)),
        scratch_shapes=[pltpu.VMEM((D_MODEL, D_EXP), BF16), pltpu.VMEM((D_MODEL, D_EXP), BF16),
                        pltpu.VMEM((D_EXP, D_MODEL), BF16)],
    )
    return pl.pallas_call(
        _expert_kernel, grid_spec=grid_spec, out_shape=jax.ShapeDtypeStruct((n_slots * ROW_TILES, LANES), F32),
        name="experts",
        compiler_params=pltpu.CompilerParams(dimension_semantics=("arbitrary",), vmem_limit_bytes=VMEM_LIMIT),
    )(blk_e, blk_src, blk_used, xs, w_gate, w_up, w_down)


def _combine_kernel(pos1_ref, pos2_ref, next1_ref, next2_ref, xa_ref, xb_ref, ga_ref, gb_ref, yb_hbm, g_ref, b_ref,
                    outa_ref, outb_ref, ybuf, sems, *, tb, alpha, nba, nsteps):
    step = pl.program_id(0)
    slot = step % 2

    def gather(p1_ref, p2_ref, to_slot):
        def issue(t, carry):
            _row_copy(yb_hbm, p1_ref[t], ybuf.at[to_slot].at[0], t, sems.at[to_slot]).start(priority=0)
            _row_copy(yb_hbm, p2_ref[t], ybuf.at[to_slot].at[1], t, sems.at[to_slot]).start(priority=1)
            return carry

        lax.fori_loop(0, tb, issue, 0, unroll=8)

    @pl.when(step == 0)
    def _():
        gather(pos1_ref, pos2_ref, 0)

    @pl.when(step + 1 < nsteps)
    def _():
        gather(next1_ref, next2_ref, 1 - slot)

    for k in range(TOP_K):
        pltpu.make_async_copy(yb_hbm.at[_tile_rows(0, tb), :], ybuf.at[slot].at[k], sems.at[slot]).wait()

    def finish(x1_ref, gcol_ref, out_ref):
        gates = gcol_ref[...]
        y = (gates[:, _R_G1:_R_G1 + 1] * _load_rows(ybuf.at[slot].at[0], tb)
             + gates[:, _R_G2:_R_G2 + 1] * _load_rows(ybuf.at[slot].at[1], tb))
        out_ref[...] = _layer_norm(alpha * _load_rows(x1_ref, tb) + y, g_ref[...], b_ref[...])

    @pl.when(step < nba)
    def _():
        finish(xa_ref, ga_ref, outa_ref)

    @pl.when(step >= nba)
    def _():
        finish(xb_ref, gb_ref, outb_ref)


def _combine(xa_rows, xb_rows, gcol_a, gcol_b, yb, pos1, pos2, ln2g, ln2b, *, tb, alpha):
    nba, nbb = xa_rows.shape[0] // ROW_TILES // tb, xb_rows.shape[0] // ROW_TILES // tb
    nsteps = nba + nbb
    kern = functools.partial(_combine_kernel, tb=tb, alpha=alpha, nba=nba, nsteps=nsteps)

    def first(i):
        return jnp.minimum(i, nba - 1)

    def second(i):
        return jnp.maximum(i - nba, 0)

    def nxt(i):
        return jnp.minimum(i + 1, nsteps - 1)

    smem = functools.partial(pl.BlockSpec, (tb,), memory_space=pltpu.SMEM)
    return pl.pallas_call(
        kern, grid=(nsteps,),
        in_specs=[smem(lambda i: (i,)), smem(lambda i: (i,)), smem(lambda i: (nxt(i),)), smem(lambda i: (nxt(i),)),
                  pl.BlockSpec((tb * ROW_TILES, LANES), lambda i: (first(i), 0)),
                  pl.BlockSpec((tb * ROW_TILES, LANES), lambda i: (second(i), 0)),
                  pl.BlockSpec((tb, LANES), lambda i: (first(i), 0)),
                  pl.BlockSpec((tb, LANES), lambda i: (second(i), 0)),
                  pl.BlockSpec(memory_space=pl.ANY),
                  pl.BlockSpec((1, D_MODEL), lambda i: (0, 0)),
                  pl.BlockSpec((1, D_MODEL), lambda i: (0, 0))],
        out_specs=[pl.BlockSpec((tb, D_MODEL), lambda i: (first(i), 0)),
                   pl.BlockSpec((tb, D_MODEL), lambda i: (second(i), 0))],
        out_shape=[jax.ShapeDtypeStruct((nba * tb, D_MODEL), F32), jax.ShapeDtypeStruct((nbb * tb, D_MODEL), F32)],
        scratch_shapes=[pltpu.VMEM((2, TOP_K, tb * ROW_TILES, LANES), F32), pltpu.SemaphoreType.DMA((2,))],
        name="combine",
        compiler_params=pltpu.CompilerParams(dimension_semantics=("arbitrary",), vmem_limit_bytes=VMEM_LIMIT),
    )(pos1, pos2, pos1, pos2, xa_rows, xb_rows, gcol_a, gcol_b, yb, ln2g, ln2b)


def _alibi_bias():
    slopes = np.asarray([2.0 ** (-8.0 * (h + 1) / N_HEADS) for h in range(N_HEADS)], np.float32)
    qi = np.arange(CHUNK, dtype=np.int32)[:, None]
    sj = np.arange(KEYS, dtype=np.int32)[None, :]
    dist = np.abs(qi + WINDOW - sj).astype(np.float32)
    bias = -slopes.reshape(N_KV_HEADS, GROUP, 1, 1) * dist
    bias = np.transpose(bias, (1, 2, 0, 3)).reshape(GROUP * CHUNK, N_KV_HEADS * KEYS)
    return jnp.asarray(bias, F32)


def _key_selector():
    row_head = np.arange(N_KV_HEADS * KEYS)[:, None] // KEYS
    lane_head = np.arange(D_KV)[None, :] // HEAD_DIM
    return jnp.asarray((row_head == lane_head).astype(np.float32), BF16)


def _strict_upper():
    r = np.arange(RANK_CHUNK)
    return jnp.asarray((r[:, None] < r[None, :]).astype(np.float32), BF16)


def _slot_plan(route, counts, n_tok):
    nblk = (n_tok * TOP_K) // EXPERT_BLOCK + N_EXPERTS
    experts = route[_R_E1:_R_E2 + 1].astype(jnp.int32)
    ranks = route[_R_RANK1:_R_RANK2 + 1].astype(jnp.int32)
    cnt = counts[:, 0].astype(jnp.int32)
    padded = (cnt + EXPERT_BLOCK - 1) // EXPERT_BLOCK * EXPERT_BLOCK
    pad_end = jnp.cumsum(padded)
    pad_start = pad_end - padded
    ids = jnp.arange(N_EXPERTS, dtype=jnp.int32)
    pos = jnp.sum(jnp.where(experts[..., None] == ids, pad_start, 0), axis=-1) + ranks
    blk_first = jnp.arange(nblk, dtype=jnp.int32) * EXPERT_BLOCK
    blk_e = jnp.minimum(jnp.sum((pad_end[None, :] <= blk_first[:, None]).astype(jnp.int32), axis=-1), N_EXPERTS - 1)
    blk_used = (blk_first < pad_end[-1]).astype(jnp.int32)
    blk_src = jnp.minimum(jnp.arange(nblk, dtype=jnp.int32), pad_end[-1] // EXPERT_BLOCK - 1)
    tail = jnp.stack([pad_end[-1], nblk - pad_end[-1] // EXPERT_BLOCK]).astype(jnp.int32)
    return pos[0], pos[1], pad_start + cnt, padded - cnt, tail, blk_e, blk_src, blk_used, nblk * EXPERT_BLOCK


def _layer(xp, xs, cache, mix_prm, consts, moe_prm, layer, *, tl_prompt, nseq_sample, alpha):
    (bp, tp, _), (bs, ts, _) = xp.shape, xs.shape
    np_tok, ns_tok = bp * tp, bs * ts
    tb = ROWS
    assert np_tok % tb == 0 and ns_tok % tb == 0
    zeros_conv = jnp.zeros((bp, CONV_W - 1, D_CONV), F32)
    zeros_kv = jnp.zeros((bp, WINDOW, D_KV), F32)
    zero_counts = jnp.zeros((N_EXPERTS, LANES), F32)
    x1p, route_p, gcol_p, counts_p, *state_p = _mixer(
        xp, zeros_conv, zeros_kv, zeros_kv, zero_counts, mix_prm, consts,
        nseq=1, tl=tl_prompt, mask_history=True, alpha=alpha)
    x1s, route_s, gcol_s, counts, *state_s = _mixer(
        xs, *cache, counts_p, mix_prm, consts, nseq=nseq_sample, tl=ts, mask_history=False, alpha=alpha)
    route = jnp.concatenate([route_p, route_s], axis=1)
    pos1, pos2, zstart, zcount, tail, blk_e, blk_src, blk_used, n_slots = _slot_plan(route, counts, np_tok + ns_tok)
    sorted_rows = _dispatch(x1p, x1s, pos1, pos2, zstart, zcount, tail, n_slots, tb=tb)
    w_gate, w_up, w_down, ln2g, ln2b = moe_prm
    yb = _experts(sorted_rows, blk_e, blk_src, blk_used, w_gate, w_up, w_down, layer)
    out_p, out_s = _combine(x1p, x1s, gcol_p, gcol_s, yb, pos1, pos2, ln2g, ln2b, tb=tb, alpha=alpha)
    return out_p.reshape(bp, tp, D_MODEL), out_s.reshape(bs, ts, D_MODEL), state_p, state_s


def kernel(x_prompt, x_sample, cache_conv, cache_k, cache_v, w_in, conv_w, conv_b, attn_sinks, g_conv, g_attn,
           w_out, ln1_g, ln1_b, router_group_w, router_group_b, router_expert_w, router_expert_b,
           expert_w_gate, expert_w_up, expert_w_down, ln2_g, ln2_b):
    depth = w_in.shape[0]
    alpha = (2 * depth) ** 0.25
    batch, seq = x_prompt.shape[0], x_prompt.shape[1]
    dec_batch, dec_seq = x_sample.shape[0], x_sample.shape[1]
    assert dec_seq == CHUNK and seq % CHUNK == 0
    consts = (_alibi_bias(), _strict_upper(), _key_selector())
    tl_prompt = min(MIXER_ROWS, seq)
    nseq_sample = min(MIXER_ROWS // dec_seq, dec_batch)
    assert seq % tl_prompt == 0 and dec_batch % nseq_sample == 0

    def regroup_heads(a, axis):
        shape = a.shape
        a = a.reshape(shape[:axis] + (N_KV_HEADS, GROUP, HEAD_DIM) + shape[axis + 1:])
        return jnp.swapaxes(a, axis, axis + 1).reshape(shape)

    w_in_all = jnp.concatenate([w_in[:, :, :_OFF_Q], regroup_heads(w_in[:, :, _OFF_Q:_OFF_K], 2), w_in[:, :, _OFF_K:]],
                               axis=2).astype(BF16)
    w_out_all = jnp.concatenate([w_out[:, :D_CONV], regroup_heads(w_out[:, D_CONV:], 1)], axis=1).astype(BF16)
    g_attn_all = regroup_heads(g_attn, 1)
    router_pad = LANES - N_EXPERTS - N_GROUPS
    wr_all = jnp.concatenate([router_expert_w, router_group_w, jnp.zeros((depth, D_MODEL, router_pad), F32)],
                             axis=2).astype(BF16)
    br_all = jnp.concatenate([router_expert_b, router_group_b, jnp.zeros((depth, router_pad), F32)], axis=1)
    sink_all = jnp.broadcast_to(jnp.repeat(attn_sinks.astype(F32), CHUNK, axis=1).reshape(depth, N_KV_HEADS, QROWS, 1),
                                (depth, N_KV_HEADS, QROWS, LANES))

    xp, xs = x_prompt, x_sample
    states = [[] for _ in range(6)]
    for l in range(depth):
        mix_prm = (w_in_all[l], conv_w[l], conv_b[l].reshape(1, D_CONV), sink_all[l],
                   g_conv[l].reshape(1, D_CONV), g_attn_all[l].reshape(1, D_ATTN), w_out_all[l],
                   ln1_g[l].reshape(1, D_MODEL), ln1_b[l].reshape(1, D_MODEL), wr_all[l], br_all[l].reshape(1, LANES))
        moe_prm = (expert_w_gate, expert_w_up, expert_w_down,
                   ln2_g[l].reshape(1, D_MODEL), ln2_b[l].reshape(1, D_MODEL))
        cache = (cache_conv[l], cache_k[l].reshape(dec_batch, WINDOW, D_KV), cache_v[l].reshape(dec_batch, WINDOW, D_KV))
        xp, xs, state_p, state_s = _layer(xp, xs, cache, mix_prm, consts, moe_prm, l,
                                          tl_prompt=tl_prompt, nseq_sample=nseq_sample, alpha=alpha)
        for lst, val in zip(states, state_p + state_s):
            lst.append(val)

    def kv(lst, nb):
        return jnp.stack(lst).reshape(depth, nb, WINDOW, N_KV_HEADS, HEAD_DIM)

    return (xp, xs, jnp.stack(states[0]), kv(states[1], batch), kv(states[2], batch),
            jnp.stack(states[3]), kv(states[4], dec_batch), kv(states[5], dec_batch))
```

```python
import functools

import numpy as np
import jax
import jax.numpy as jnp
from jax import lax
from jax.experimental import pallas as pl
from jax.experimental.pallas import tpu as pltpu

D_MODEL = 1024
D_CONV = 512
CONV_W = 3
N_HEADS = 8
N_KV_HEADS = 2
GROUP = N_HEADS // N_KV_HEADS
HEAD_DIM = 64
D_ATTN = N_HEADS * HEAD_DIM
D_KV = N_KV_HEADS * HEAD_DIM
WINDOW = 128
CHUNK = 64
KEYS = WINDOW + CHUNK
QROWS = GROUP * CHUNK
N_GROUPS = 4
EXP_PER_GROUP = 8
N_EXPERTS = N_GROUPS * EXP_PER_GROUP
TOP_K = 2
D_EXP = 512
LN_EPS = 1e-5
NEG_INF = -1e30

SUBLANES = 8
LANES = 128
ROW_TILES = D_MODEL // LANES
assert ROW_TILES == SUBLANES
ROWS = 512
MIXER_ROWS = 512
DISPATCH_ROWS = 1024
EXPERT_BLOCK = 512
RANK_CHUNK = 256
SOFTMAX_ROWS = 64
PROJ_CHUNK = 2 * LANES
MIX_PIECES = 4
VMEM_LIMIT = 56 * 1024 * 1024

F32 = jnp.float32
BF16 = jnp.bfloat16

_OFF_B, _OFF_C, _OFF_H = 0, D_CONV, 2 * D_CONV
_OFF_Q = 3 * D_CONV
_OFF_K = _OFF_Q + D_ATTN
_OFF_V = _OFF_K + D_KV
D_IN = _OFF_V + D_KV

_R_E1, _R_E2, _R_RANK1, _R_RANK2, _R_G1, _R_G2 = 0, 1, 2, 3, 4, 5


def _load_rows(ref, n):
    return jnp.concatenate([ref[pl.ds(j, n, stride=ROW_TILES), :] for j in range(ROW_TILES)], axis=1)


def _store_rows(ref, val, n):
    for j in range(ROW_TILES):
        ref[pl.ds(j, n, stride=ROW_TILES), :] = val[:, j * LANES:(j + 1) * LANES]


def _dot(a, b):
    return jnp.dot(a, b, preferred_element_type=F32)


def _rms_norm(x, g):
    return x * lax.rsqrt(jnp.mean(jnp.square(x), -1, keepdims=True) + LN_EPS) * g


def _layer_norm(x, g, b):
    mu = jnp.mean(x, -1, keepdims=True)
    xc = x - mu
    var = jnp.mean(jnp.square(xc), -1, keepdims=True)
    return xc * lax.rsqrt(var + LN_EPS) * g + b


def _mixer_kernel(x_ref, cconv_ref, ck_ref, cv_ref, w_in_ref, convw_ref, convb_ref, abias_ref,
                  sink_ref, gconv_ref, gattn_ref, w_out_ref, ln1g_ref, ln1b_ref, wr_ref, br_ref,
                  tri_ref, keysel_ref, counts_in_ref,
                  x1_ref, route_ref, gcol_ref, counts_ref, sconv_ref, sk_ref, sv_ref,
                  uext, kext, vext, cnt_s, s_ref, e_ref, sinkden_ref, m0_ref, m1_ref, proj_ref, x1b_s, xprev_s,
                  mixed_s, *, nseq, tl, alpha, mask_history, defer_tail):
    b = pl.program_id(0)
    s = pl.program_id(1)
    rows = nseq * tl
    upitch = tl + SUBLANES
    kpitch = WINDOW + tl

    @pl.when(jnp.logical_and(b == 0, s == 0))
    def _():
        cnt_s[...] = counts_in_ref[:, 0:1]
        if defer_tail:
            xprev_s[...] = jnp.zeros_like(xprev_s)
            mixed_s[...] = jnp.zeros_like(mixed_s)

    @pl.when(s == 0)
    def _():
        for j in range(nseq):
            uext[j * upitch + SUBLANES - 2:j * upitch + SUBLANES, :] = cconv_ref[j]
            kext[j * kpitch:j * kpitch + WINDOW, :] = ck_ref[j]
            vext[j * kpitch:j * kpitch + WINDOW, :] = cv_ref[j]

    def norm_rows(r0, r1, x_rows, mixed_rows):
        x1 = _layer_norm(alpha * x_rows + mixed_rows, ln1g_ref[...], ln1b_ref[...])
        for j in range(ROW_TILES):
            x1_ref[pl.ds(r0 * ROW_TILES + j, r1 - r0, stride=ROW_TILES), :] = x1[:, j * LANES:(j + 1) * LANES]
        x1b_s[r0:r1, :] = x1.astype(BF16)

    def route(valid):
        logits_t = (_dot(x1b_s[...], wr_ref[...]) + br_ref[...]).T
        sub = lax.broadcasted_iota(jnp.int32, (SUBLANES, rows), 0)
        gl = jnp.where(sub < N_GROUPS, logits_t[N_EXPERTS:N_EXPERTS + SUBLANES, :], -jnp.inf)
        gmax = jnp.max(gl, axis=0, keepdims=True)
        grp = jnp.min(jnp.where(gl == gmax, sub, SUBLANES), axis=0, keepdims=True)
        p_grp = 1.0 / jnp.sum(jnp.exp(gl - gmax), axis=0, keepdims=True)
        el = logits_t[(N_GROUPS - 1) * EXP_PER_GROUP:N_GROUPS * EXP_PER_GROUP, :]
        for g in range(N_GROUPS - 2, -1, -1):
            el = jnp.where(grp == g, logits_t[g * EXP_PER_GROUP:(g + 1) * EXP_PER_GROUP, :], el)
        v1 = jnp.max(el, axis=0, keepdims=True)
        i1 = jnp.min(jnp.where(el == v1, sub, SUBLANES), axis=0, keepdims=True)
        el2 = jnp.where(sub == i1, -jnp.inf, el)
        v2 = jnp.max(el2, axis=0, keepdims=True)
        i2 = jnp.min(jnp.where(el2 == v2, sub, SUBLANES), axis=0, keepdims=True)
        e2 = jnp.exp(v2 - v1)
        gate1 = p_grp * (1.0 / (1.0 + e2))
        gate2 = p_grp * (e2 / (1.0 + e2))

        chosen = jnp.logical_or(sub == i1, sub == i2)
        onehot = jnp.concatenate(
            [jnp.where(jnp.logical_and(grp == g, chosen), 1.0, 0.0) for g in range(N_GROUPS)], axis=0)
        running = cnt_s[...]
        ranks = []
        for c in range(rows // RANK_CHUNK):
            oh = onehot[:, c * RANK_CHUNK:(c + 1) * RANK_CHUNK]
            ranks.append(_dot(oh.astype(BF16), tri_ref[...]) + running)
            running = running + jnp.sum(oh, axis=1, keepdims=True) * valid
        rank = jnp.concatenate(ranks, axis=1)
        cnt_s[...] = running
        counts_ref[...] = jnp.broadcast_to(running, (N_EXPERTS, LANES))

        ex1 = grp * EXP_PER_GROUP + i1
        ex2 = grp * EXP_PER_GROUP + i2
        erow = lax.broadcasted_iota(jnp.int32, (N_EXPERTS, rows), 0)
        rank1 = jnp.sum(jnp.where(erow == ex1, rank, 0.0), axis=0, keepdims=True)
        rank2 = jnp.sum(jnp.where(erow == ex2, rank, 0.0), axis=0, keepdims=True)
        fields = {_R_E1: ex1.astype(F32), _R_E2: ex2.astype(F32), _R_RANK1: rank1, _R_RANK2: rank2,
                  _R_G1: gate1, _R_G2: gate2}
        record = jnp.zeros((SUBLANES, rows), F32)
        for r, val in fields.items():
            record = jnp.where(sub == r, val, record)
        route_ref[...] = record
        padded = jnp.concatenate([record, jnp.zeros((LANES - SUBLANES, rows), F32)], axis=0)
        gcol_ref[...] = padded.T

    x = x_ref[...].reshape(rows, D_MODEL)
    xb = x.astype(BF16)

    def conv_rows(r0, r1):
        w0, w1, w2 = convw_ref[0:1, :], convw_ref[1:2, :], convw_ref[2:3, :]
        out = []
        for j in range(r0 // tl, -(-r1 // tl)):
            a, b_ = max(r0, j * tl) - j * tl, min(r1, (j + 1) * tl) - j * tl
            base = j * upitch + SUBLANES
            src = pl.ds(j * tl + a, b_ - a)
            uext[base + a:base + b_, :] = proj_ref[src, _OFF_C:_OFF_C + D_CONV] * proj_ref[src, _OFF_H:_OFF_H + D_CONV]
            yc = convb_ref[...] + uext[base + a - 2:base + b_ - 2, :] * w0
            yc = yc + uext[base + a - 1:base + b_ - 1, :] * w1
            yc = yc + uext[base + a:base + b_, :] * w2
            out.append(_rms_norm(proj_ref[src, _OFF_B:_OFF_B + D_CONV] * yc, gconv_ref[...]).astype(BF16))
            if b_ == tl:
                tail = uext[base + tl - 2:base + tl, :]
                sconv_ref[j] = tail
                uext[base - 2:base, :] = tail
        return out[0] if len(out) == 1 else jnp.concatenate(out, axis=0)

    qkv = []
    qkv_chunks = [(c, c + PROJ_CHUNK) for c in range(_OFF_Q, D_IN, PROJ_CHUNK)]
    tail_rows = -(-rows // len(qkv_chunks) // SUBLANES) * SUBLANES
    for p, (c0, c1) in enumerate(qkv_chunks):
        r0, r1 = min(p * tail_rows, rows), min((p + 1) * tail_rows, rows)
        if defer_tail and r1 > r0:
            norm_rows(r0, r1, xprev_s[r0:r1, :], mixed_s[r0:r1, :])
        qkv.append(_dot(xb, w_in_ref[:, c0:c1]))
    qkv = jnp.concatenate(qkv, axis=1)
    if defer_tail:
        route((s > 0).astype(F32))
    q = (qkv[:, 0:D_ATTN] * (HEAD_DIM ** -0.5)).astype(BF16)
    k = qkv[:, D_ATTN:D_ATTN + D_KV]
    v = qkv[:, D_ATTN + D_KV:D_ATTN + 2 * D_KV]
    for j in range(nseq):
        kext[j * kpitch + WINDOW:(j + 1) * kpitch, :] = k[j * tl:(j + 1) * tl]
        vext[j * kpitch + WINDOW:(j + 1) * kpitch, :] = v[j * tl:(j + 1) * tl]

    nchunk = rows // CHUNK
    assert nseq == 1 or tl == CHUNK
    key_stride = CHUNK if nseq == 1 else kpitch
    head0 = lax.broadcasted_iota(jnp.int32, (1, D_KV), 1) < HEAD_DIM

    def windows(ext_ref):
        ext = ext_ref[...].astype(BF16)
        heads = (jnp.where(head0, ext, jnp.zeros_like(ext)), jnp.where(head0, jnp.zeros_like(ext), ext))
        return jnp.stack([jnp.concatenate([h[c * key_stride:c * key_stride + KEYS] for h in heads], axis=0)
                          for c in range(nchunk)])

    q3 = jnp.stack([jnp.concatenate([q[c * CHUNK:(c + 1) * CHUNK, g * D_KV:(g + 1) * D_KV] for g in range(GROUP)], axis=0)
                    for c in range(nchunk)])
    logits = lax.dot_general(q3, windows(kext), (((2,), (2,)), ((0,), (0,))), preferred_element_type=F32)
    s_ref[...] = logits.reshape(nchunk * QROWS, 2 * KEYS)

    assert 2 * KEYS == 3 * LANES
    mid0 = lax.broadcasted_iota(jnp.int32, (1, LANES), 1) < KEYS - LANES
    colk = lax.broadcasted_iota(jnp.int32, (1, 2 * KEYS), 1)
    key = jnp.where(colk < KEYS, colk, colk - KEYS)

    def row_block(i):
        r = i * SOFTMAX_ROWS
        return pl.ds(r, SOFTMAX_ROWS), pl.ds(r % QROWS, SOFTMAX_ROWS), r // QROWS

    def tiles(lg):
        return lg[:, 0:LANES], lg[:, LANES:2 * LANES], lg[:, 2 * LANES:3 * LANES]

    def max_rows(i, masked):
        rows_i, qrows_i, chunk_i = row_block(i)
        lg = s_ref[rows_i, :] + abias_ref[qrows_i, :]
        if masked:
            lg = lg + jnp.where(s * tl + chunk_i * CHUNK - WINDOW + key < 0, NEG_INF, 0.0).astype(F32)
        s_ref[rows_i, :] = lg
        t0, t1, t2 = tiles(lg)
        m0_ref[rows_i, :] = jnp.maximum(jnp.max(jnp.maximum(t0, jnp.where(mid0, t1, NEG_INF)), -1, keepdims=True),
                                        sink_ref[0, qrows_i, :])
        m1_ref[rows_i, :] = jnp.maximum(jnp.max(jnp.maximum(t2, jnp.where(mid0, NEG_INF, t1)), -1, keepdims=True),
                                        sink_ref[1, qrows_i, :])

    def exp_rows(i):
        rows_i, qrows_i, _ = row_block(i)
        t0, t1, t2 = tiles(s_ref[rows_i, :])
        m0, m1 = m0_ref[rows_i, :], m1_ref[rows_i, :]
        e = jnp.concatenate([jnp.exp(t0 - m0), jnp.exp(t1 - jnp.where(mid0, m0, m1)), jnp.exp(t2 - m1)], axis=-1)
        e_ref[rows_i, :] = e.astype(BF16)
        sinkden_ref[rows_i, :] = jnp.where(head0, jnp.exp(sink_ref[0, qrows_i, :] - m0),
                                           jnp.exp(sink_ref[1, qrows_i, :] - m1))

    n_it = nchunk * QROWS // SOFTMAX_ROWS
    n_masked = min(WINDOW // CHUNK, nchunk) * QROWS // SOFTMAX_ROWS if mask_history else 0
    proj_chunks = [(c, min(c + PROJ_CHUNK, _OFF_Q)) for c in range(0, _OFF_Q, PROJ_CHUNK)]
    piece = -(-2 * n_it // len(proj_chunks))
    steps = [functools.partial(max_rows, i, i < n_masked) for i in range(n_it)]
    steps += [functools.partial(exp_rows, i) for i in range(n_it)]
    for p, (c0, c1) in enumerate(proj_chunks):
        for step_fn in steps[p * piece:(p + 1) * piece]:
            step_fn()
        proj_ref[:, c0:c1] = _dot(xb, w_in_ref[:, c0:c1])
    for step_fn in steps[len(proj_chunks) * piece:]:
        step_fn()
    vsel = jnp.concatenate([windows(vext), jnp.broadcast_to(keysel_ref[...], (nchunk, 2 * KEYS, D_KV))], axis=-1)
    n_pieces = min(MIX_PIECES, nchunk)
    cpp = nchunk // n_pieces
    conv_parts, attn_parts = [], []
    for p in range(n_pieces):
        cs = slice(p * cpp, (p + 1) * cpp)
        o2 = lax.dot_general(e_ref[p * cpp * QROWS:(p + 1) * cpp * QROWS, :].reshape(cpp, QROWS, 2 * KEYS), vsel[cs],
                             (((2,), (1,)), ((0,), (0,))), preferred_element_type=F32)
        sinkden = sinkden_ref[p * cpp * QROWS:(p + 1) * cpp * QROWS, :].reshape(cpp, QROWS, D_KV)
        o = o2[..., 0:D_KV] * (1.0 / (o2[..., D_KV:2 * D_KV] + sinkden))
        attn_parts += [jnp.concatenate([o[c, g * CHUNK:(g + 1) * CHUNK, :] for g in range(GROUP)], axis=1)
                       for c in range(cpp)]
        conv_parts.append(conv_rows(p * cpp * CHUNK, (p + 1) * cpp * CHUNK))
    y_attn = jnp.concatenate(attn_parts, axis=0)
    n_conv = jnp.concatenate(conv_parts, axis=0)

    for j in range(nseq):
        sk_ref[j] = kext[j * kpitch + tl:(j + 1) * kpitch, :]
        sv_ref[j] = vext[j * kpitch + tl:(j + 1) * kpitch, :]
    if nseq == 1:
        kext[0:WINDOW, :] = kext[tl:tl + WINDOW, :]
        vext[0:WINDOW, :] = vext[tl:tl + WINDOW, :]

    n_attn = _rms_norm(y_attn, gattn_ref[...]).astype(BF16)
    mixed = _dot(n_attn, w_out_ref[D_CONV:D_CONV + D_ATTN, :]) + _dot(n_conv, w_out_ref[0:D_CONV, :])
    if defer_tail:
        xprev_s[...] = x
        mixed_s[...] = mixed
    else:
        norm_rows(0, rows, x, mixed)
        route(1.0)


def _mixer(x, cconv, ck, cv, counts_in, prm, consts, *, nseq, tl, mask_history, alpha):
    nb_total, t_total = x.shape[0], x.shape[1]
    nb, ns = nb_total // nseq, t_total // tl
    defer_tail = ns > 1
    steps = ns + 1 if defer_tail else ns

    def in_blk(s):
        return jnp.minimum(s, ns - 1)

    def out_blk(b, s):
        return b * ns + (jnp.maximum(s - 1, 0) if defer_tail else s)

    rows = nseq * tl
    n_tok = nb_total * t_total
    w_in, convw, convb, sinks, gconv, gattn, w_out, ln1g, ln1b, wr, br = prm
    abias, tri, keysel = consts

    def full(a):
        return pl.BlockSpec(a.shape, lambda b, s, _n=a.ndim: (0,) * _n)

    def seq_state(width, nrows):
        return pl.BlockSpec((nseq, nrows, width), lambda b, s: (b, 0, 0))

    in_specs = [
        pl.BlockSpec((nseq, tl, D_MODEL), lambda b, s: (b, in_blk(s), 0)),
        seq_state(D_CONV, CONV_W - 1), seq_state(D_KV, WINDOW), seq_state(D_KV, WINDOW),
        full(w_in), full(convw), full(convb), full(abias), full(sinks), full(gconv), full(gattn),
        full(w_out), full(ln1g), full(ln1b), full(wr), full(br), full(tri), full(keysel), full(counts_in),
    ]
    out_shape = [
        jax.ShapeDtypeStruct((n_tok * ROW_TILES, LANES), F32),
        jax.ShapeDtypeStruct((SUBLANES, n_tok), F32),
        jax.ShapeDtypeStruct((n_tok, LANES), F32),
        jax.ShapeDtypeStruct((N_EXPERTS, LANES), F32),
        jax.ShapeDtypeStruct((nb_total, CONV_W - 1, D_CONV), F32),
        jax.ShapeDtypeStruct((nb_total, WINDOW, D_KV), F32),
        jax.ShapeDtypeStruct((nb_total, WINDOW, D_KV), F32),
    ]
    out_specs = [
        pl.BlockSpec((rows * ROW_TILES, LANES), lambda b, s: (out_blk(b, s), 0)),
        pl.BlockSpec((SUBLANES, rows), lambda b, s: (0, out_blk(b, s))),
        pl.BlockSpec((rows, LANES), lambda b, s: (out_blk(b, s), 0)),
        pl.BlockSpec((N_EXPERTS, LANES), lambda b, s: (0, 0)),
        seq_state(D_CONV, CONV_W - 1), seq_state(D_KV, WINDOW), seq_state(D_KV, WINDOW),
    ]
    scratch = [
        pltpu.VMEM((nseq * (tl + SUBLANES), D_CONV), F32),
        pltpu.VMEM((nseq * (WINDOW + tl), D_KV), F32),
        pltpu.VMEM((nseq * (WINDOW + tl), D_KV), F32),
        pltpu.VMEM((N_EXPERTS, 1), F32),
        pltpu.VMEM((rows // CHUNK * QROWS, 2 * KEYS), F32),
        pltpu.VMEM((rows // CHUNK * QROWS, 2 * KEYS), BF16),
        pltpu.VMEM((rows // CHUNK * QROWS, D_KV), F32),
        pltpu.VMEM((rows // CHUNK * QROWS, LANES), F32),
        pltpu.VMEM((rows // CHUNK * QROWS, LANES), F32),
        pltpu.VMEM((rows, _OFF_Q), F32),
        pltpu.VMEM((rows, D_MODEL), BF16),
        pltpu.VMEM((rows, D_MODEL) if defer_tail else (SUBLANES, LANES), F32),
        pltpu.VMEM((rows, D_MODEL) if defer_tail else (SUBLANES, LANES), F32),
    ]
    kern = functools.partial(_mixer_kernel, nseq=nseq, tl=tl, alpha=alpha, mask_history=mask_history,
                             defer_tail=defer_tail)
    return pl.pallas_call(
        kern, grid=(nb, steps), in_specs=in_specs, out_specs=out_specs, out_shape=out_shape,
        scratch_shapes=scratch, name="mixer",
        compiler_params=pltpu.CompilerParams(dimension_semantics=("arbitrary", "arbitrary"),
                                             vmem_limit_bytes=VMEM_LIMIT),
    )(x, cconv, ck, cv, w_in, convw, convb, abias, sinks, gconv, gattn, w_out, ln1g, ln1b, wr, br, tri, keysel, counts_in)


_PAD_PIECES = tuple(2 ** k for k in range(EXPERT_BLOCK.bit_length() - 2, -1, -1))


def _tile_rows(row, n=1):
    return pl.ds(pl.multiple_of(row * ROW_TILES, ROW_TILES), n * ROW_TILES)


def _row_copy(src, src_row, dst, dst_row, sem):
    return pltpu.make_async_copy(src.at[_tile_rows(src_row), :], dst.at[_tile_rows(dst_row), :], sem)


def _dispatch_kernel(zstart_ref, zcount_ref, tail_ref, pos1_ref, pos2_ref, xa_ref, xb_ref, xs_hbm, zeros_v, sem, zsem,
                     *, tb, nba):
    step = pl.program_id(0)

    def zero_copy(start, piece):
        return pltpu.make_async_copy(zeros_v.at[_tile_rows(0, piece), :], xs_hbm.at[_tile_rows(start, piece), :], zsem)

    def pad_copy(e, piece, taken):
        return zero_copy(zstart_ref[e] + taken, piece)

    def for_each_pad_piece(fn):
        def per_tail_block(j, carry):
            for part in range(EXPERT_BLOCK // _PAD_PIECES[0]):
                fn(zero_copy(tail_ref[0] + j * EXPERT_BLOCK + part * _PAD_PIECES[0], _PAD_PIECES[0]))
            return carry

        lax.fori_loop(0, tail_ref[1], per_tail_block, 0)

        def per_expert(e, carry):
            count = zcount_ref[e]
            taken = 0
            for piece in _PAD_PIECES:
                present = (count & piece) != 0

                @pl.when(present)
                def _(piece=piece, taken=taken):
                    fn(pad_copy(e, piece, taken))

                taken = taken + jnp.where(present, piece, 0)
            return carry

        lax.fori_loop(0, N_EXPERTS, per_expert, 0)

    @pl.when(step == 0)
    def _():
        zeros_v[...] = jnp.zeros_like(zeros_v)
        for_each_pad_piece(lambda c: c.start())

    def issue_rows(x_ref):
        def issue(t, carry):
            _row_copy(x_ref, t, xs_hbm, pos1_ref[t], sem).start(priority=0)
            _row_copy(x_ref, t, xs_hbm, pos2_ref[t], sem).start(priority=1)
            return carry

        lax.fori_loop(0, tb, issue, 0, unroll=8)

    @pl.when(step < nba)
    def _():
        issue_rows(xa_ref)

    @pl.when(step >= nba)
    def _():
        issue_rows(xb_ref)

    pltpu.make_async_copy(xs_hbm.at[_tile_rows(0, 2 * tb), :], xs_hbm.at[_tile_rows(0, 2 * tb), :], sem).wait()

    @pl.when(step == 0)
    def _():
        for_each_pad_piece(lambda c: c.wait())


def _dispatch(xa_rows, xb_rows, pos1, pos2, zstart, zcount, tail, n_slots, *, tb):
    nba, nbb = xa_rows.shape[0] // ROW_TILES // tb, xb_rows.shape[0] // ROW_TILES // tb
    kern = functools.partial(_dispatch_kernel, tb=tb, nba=nba)
    grid_spec = pltpu.PrefetchScalarGridSpec(
        num_scalar_prefetch=3, grid=(nba + nbb,),
        in_specs=[pl.BlockSpec((tb,), lambda i, *_: (i,), memory_space=pltpu.SMEM),
                  pl.BlockSpec((tb,), lambda i, *_: (i,), memory_space=pltpu.SMEM),
                  pl.BlockSpec((tb * ROW_TILES, LANES), lambda i, *_: (jnp.minimum(i, nba - 1), 0)),
                  pl.BlockSpec((tb * ROW_TILES, LANES), lambda i, *_: (jnp.maximum(i - nba, 0), 0))],
        out_specs=pl.BlockSpec(memory_space=pl.ANY),
        scratch_shapes=[pltpu.VMEM((_PAD_PIECES[0] * ROW_TILES, LANES), F32),
                        pltpu.SemaphoreType.DMA(()), pltpu.SemaphoreType.DMA(())],
    )
    return pl.pallas_call(
        kern, grid_spec=grid_spec, out_shape=jax.ShapeDtypeStruct((n_slots * ROW_TILES, LANES), F32),
        name="dispatch",
        compiler_params=pltpu.CompilerParams(dimension_semantics=("arbitrary",)),
    )(zstart, zcount, tail, pos1, pos2, xa_rows, xb_rows)


def _expert_kernel(be_ref, src_ref, used_ref, xs_ref, wg_ref, wu_ref, wd_ref, yb_ref, wg_b, wu_b, wd_b):
    i = pl.program_id(0)
    e = be_ref[i]
    prev = be_ref[jnp.maximum(i - 1, 0)]

    @pl.when(jnp.logical_or(i == 0, e != prev))
    def _():
        wg_b[...] = wg_ref[...].astype(BF16)
        wu_b[...] = wu_ref[...].astype(BF16)
        wd_b[...] = wd_ref[...].astype(BF16)

    @pl.when(used_ref[i] != 0)
    def _():
        x = _load_rows(xs_ref, EXPERT_BLOCK).astype(BF16)
        g = _dot(x, wg_b[...])
        u = _dot(x, wu_b[...])
        hid = (g / (1.0 + jnp.exp(-g))) * u
        _store_rows(yb_ref, _dot(hid.astype(BF16), wd_b[...]), EXPERT_BLOCK)

    @pl.when(used_ref[i] == 0)
    def _():
        yb_ref[...] = jnp.zeros_like(yb_ref)


def _experts(xs, blk_e, blk_src, blk_used, w_gate, w_up, w_down, layer):
    n_slots = xs.shape[0] // ROW_TILES
    nblk = n_slots // EXPERT_BLOCK

    def wspec(shape):
        return pl.BlockSpec((None, None) + shape, lambda i, be, src, used: (layer, be[i], 0, 0))

    grid_spec = pltpu.PrefetchScalarGridSpec(
        num_scalar_prefetch=3, grid=(nblk,),
        in_specs=[pl.BlockSpec((EXPERT_BLOCK * ROW_TILES, LANES), lambda i, be, src, used: (src[i], 0)),
                  wspec((D_MODEL, D_EXP)), wspec((D_MODEL, D_EXP)), wspec((D_EXP, D_MODEL))],
        out_specs=pl.BlockSpec((EXPERT_BLOCK * ROW_TILES, LANES), lambda i, be, src, used: (i, 0)),
        scratch_shapes=[pltpu.VMEM((D_MODEL, D_EXP), BF16), pltpu.VMEM((D_MODEL, D_EXP), BF16),
                        pltpu.VMEM((D_EXP, D_MODEL), BF16)],
    )
    return pl.pallas_call(
        _expert_kernel, grid_spec=grid_spec, out_shape=jax.ShapeDtypeStruct((n_slots * ROW_TILES, LANES), F32),
        name="experts",
        compiler_params=pltpu.CompilerParams(dimension_semantics=("arbitrary",), vmem_limit_bytes=VMEM_LIMIT),
    )(blk_e, blk_src, blk_used, xs, w_gate, w_up, w_down)


def _combine_kernel(pos1_ref, pos2_ref, next1_ref, next2_ref, xa_ref, xb_ref, ga_ref, gb_ref, yb_hbm, g_ref, b_ref,
                    outa_ref, outb_ref, ybuf, sems, *, tb, alpha, nba, nsteps):
    step = pl.program_id(0)
    slot = step % 2

    def gather(p1_ref, p2_ref, to_slot):
        def issue(t, carry):
            _row_copy(yb_hbm, p1_ref[t], ybuf.at[to_slot].at[0], t, sems.at[to_slot]).start(priority=0)
            _row_copy(yb_hbm, p2_ref[t], ybuf.at[to_slot].at[1], t, sems.at[to_slot]).start(priority=1)
            return carry

        lax.fori_loop(0, tb, issue, 0, unroll=8)

    @pl.when(step == 0)
    def _():
        gather(pos1_ref, pos2_ref, 0)

    @pl.when(step + 1 < nsteps)
    def _():
        gather(next1_ref, next2_ref, 1 - slot)

    for k in range(TOP_K):
        pltpu.make_async_copy(yb_hbm.at[_tile_rows(0, tb), :], ybuf.at[slot].at[k], sems.at[slot]).wait()

    def finish(x1_ref, gcol_ref, out_ref):
        gates = gcol_ref[...]
        y = (gates[:, _R_G1:_R_G1 + 1] * _load_rows(ybuf.at[slot].at[0], tb)
             + gates[:, _R_G2:_R_G2 + 1] * _load_rows(ybuf.at[slot].at[1], tb))
        out_ref[...] = _layer_norm(alpha * _load_rows(x1_ref, tb) + y, g_ref[...], b_ref[...])

    @pl.when(step < nba)
    def _():
        finish(xa_ref, ga_ref, outa_ref)

    @pl.when(step >= nba)
    def _():
        finish(xb_ref, gb_ref, outb_ref)


def _combine(xa_rows, xb_rows, gcol_a, gcol_b, yb, pos1, pos2, ln2g, ln2b, *, tb, alpha):
    nba, nbb = xa_rows.shape[0] // ROW_TILES // tb, xb_rows.shape[0] // ROW_TILES // tb
    nsteps = nba + nbb
    kern = functools.partial(_combine_kernel, tb=tb, alpha=alpha, nba=nba, nsteps=nsteps)

    def first(i):
        return jnp.minimum(i, nba - 1)

    def second(i):
        return jnp.maximum(i - nba, 0)

    def nxt(i):
        return jnp.minimum(i + 1, nsteps - 1)

    smem = functools.partial(pl.BlockSpec, (tb,), memory_space=pltpu.SMEM)
    return pl.pallas_call(
        kern, grid=(nsteps,),
        in_specs=[smem(lambda i: (i,)), smem(lambda i: (i,)), smem(lambda i: (nxt(i),)), smem(lambda i: (nxt(i),)),
                  pl.BlockSpec((tb * ROW_TILES, LANES), lambda i: (first(i), 0)),
                  pl.BlockSpec((tb * ROW_TILES, LANES), lambda i: (second(i), 0)),
                  pl.BlockSpec((tb, LANES), lambda i: (first(i), 0)),
                  pl.BlockSpec((tb, LANES), lambda i: (second(i), 0)),
                  pl.BlockSpec(memory_space=pl.ANY),
                  pl.BlockSpec((1, D_MODEL), lambda i: (0, 0)),
                  pl.BlockSpec((1, D_MODEL), lambda i: (0, 0))],
        out_specs=[pl.BlockSpec((tb, D_MODEL), lambda i: (first(i), 0)),
                   pl.BlockSpec((tb, D_MODEL), lambda i: (second(i), 0))],
        out_shape=[jax.ShapeDtypeStruct((nba * tb, D_MODEL), F32), jax.ShapeDtypeStruct((nbb * tb, D_MODEL), F32)],
        scratch_shapes=[pltpu.VMEM((2, TOP_K, tb * ROW_TILES, LANES), F32), pltpu.SemaphoreType.DMA((2,))],
        name="combine",
        compiler_params=pltpu.CompilerParams(dimension_semantics=("arbitrary",), vmem_limit_bytes=VMEM_LIMIT),
    )(pos1, pos2, pos1, pos2, xa_rows, xb_rows, gcol_a, gcol_b, yb, ln2g, ln2b)


def _alibi_bias():
    slopes = np.asarray([2.0 ** (-8.0 * (h + 1) / N_HEADS) for h in range(N_HEADS)], np.float32)
    qi = np.arange(CHUNK, dtype=np.int32)[:, None]
    sj = np.arange(KEYS, dtype=np.int32)[None, :]
    dist = np.abs(qi + WINDOW - sj).astype(np.float32)
    bias = -slopes.reshape(N_KV_HEADS, GROUP, 1, 1) * dist
    bias = np.transpose(bias, (1, 2, 0, 3)).reshape(GROUP * CHUNK, N_KV_HEADS * KEYS)
    return jnp.asarray(bias, F32)


def _key_selector():
    row_head = np.arange(N_KV_HEADS * KEYS)[:, None] // KEYS
    lane_head = np.arange(D_KV)[None, :] // HEAD_DIM
    return jnp.asarray((row_head == lane_head).astype(np.float32), BF16)


def _strict_upper():
    r = np.arange(RANK_CHUNK)
    return jnp.asarray((r[:, None] < r[None, :]).astype(np.float32), BF16)


def _slot_plan(route, counts, n_tok):
    nblk = (n_tok * TOP_K) // EXPERT_BLOCK + N_EXPERTS
    experts = route[_R_E1:_R_E2 + 1].astype(jnp.int32)
    ranks = route[_R_RANK1:_R_RANK2 + 1].astype(jnp.int32)
    cnt = counts[:, 0].astype(jnp.int32)
    padded = (cnt + EXPERT_BLOCK - 1) // EXPERT_BLOCK * EXPERT_BLOCK
    pad_end = jnp.cumsum(padded)
    pad_start = pad_end - padded
    ids = jnp.arange(N_EXPERTS, dtype=jnp.int32)
    pos = jnp.sum(jnp.where(experts[..., None] == ids, pad_start, 0), axis=-1) + ranks
    blk_first = jnp.arange(nblk, dtype=jnp.int32) * EXPERT_BLOCK
    blk_e = jnp.minimum(jnp.sum((pad_end[None, :] <= blk_first[:, None]).astype(jnp.int32), axis=-1), N_EXPERTS - 1)
    blk_used = (blk_first < pad_end[-1]).astype(jnp.int32)
    blk_src = jnp.minimum(jnp.arange(nblk, dtype=jnp.int32), pad_end[-1] // EXPERT_BLOCK - 1)
    tail = jnp.stack([pad_end[-1], nblk - pad_end[-1] // EXPERT_BLOCK]).astype(jnp.int32)
    return pos[0], pos[1], pad_start + cnt, padded - cnt, tail, blk_e, blk_src, blk_used, nblk * EXPERT_BLOCK


def _layer(xp, xs, cache, mix_prm, consts, moe_prm, layer, *, tl_prompt, nseq_sample, alpha):
    (bp, tp, _), (bs, ts, _) = xp.shape, xs.shape
    np_tok, ns_tok = bp * tp, bs * ts
    tb = ROWS
    assert np_tok % tb == 0 and ns_tok % tb == 0
    zeros_conv = jnp.zeros((bp, CONV_W - 1, D_CONV), F32)
    zeros_kv = jnp.zeros((bp, WINDOW, D_KV), F32)
    zero_counts = jnp.zeros((N_EXPERTS, LANES), F32)
    x1p, route_p, gcol_p, counts_p, *state_p = _mixer(
        xp, zeros_conv, zeros_kv, zeros_kv, zero_counts, mix_prm, consts,
        nseq=1, tl=tl_prompt, mask_history=True, alpha=alpha)
    x1s, route_s, gcol_s, counts, *state_s = _mixer(
        xs, *cache, counts_p, mix_prm, consts, nseq=nseq_sample, tl=ts, mask_history=False, alpha=alpha)
    route = jnp.concatenate([route_p, route_s], axis=1)
    pos1, pos2, zstart, zcount, tail, blk_e, blk_src, blk_used, n_slots = _slot_plan(route, counts, np_tok + ns_tok)
    tb_dispatch = DISPATCH_ROWS if np_tok % DISPATCH_ROWS == 0 and ns_tok % DISPATCH_ROWS == 0 else tb
    sorted_rows = _dispatch(x1p, x1s, pos1, pos2, zstart, zcount, tail, n_slots, tb=tb_dispatch)
    w_gate, w_up, w_down, ln2g, ln2b = moe_prm
    yb = _experts(sorted_rows, blk_e, blk_src, blk_used, w_gate, w_up, w_down, layer)
    out_p, out_s = _combine(x1p, x1s, gcol_p, gcol_s, yb, pos1, pos2, ln2g, ln2b, tb=tb, alpha=alpha)
    return out_p.reshape(bp, tp, D_MODEL), out_s.reshape(bs, ts, D_MODEL), state_p, state_s


def kernel(x_prompt, x_sample, cache_conv, cache_k, cache_v, w_in, conv_w, conv_b, attn_sinks, g_conv, g_attn,
           w_out, ln1_g, ln1_b, router_group_w, router_group_b, router_expert_w, router_expert_b,
           expert_w_gate, expert_w_up, expert_w_down, ln2_g, ln2_b):
    depth = w_in.shape[0]
    alpha = (2 * depth) ** 0.25
    batch, seq = x_prompt.shape[0], x_prompt.shape[1]
    dec_batch, dec_seq = x_sample.shape[0], x_sample.shape[1]
    assert dec_seq == CHUNK and seq % CHUNK == 0
    consts = (_alibi_bias(), _strict_upper(), _key_selector())
    tl_prompt = min(MIXER_ROWS, seq)
    nseq_sample = min(MIXER_ROWS // dec_seq, dec_batch)
    assert seq % tl_prompt == 0 and dec_batch % nseq_sample == 0

    def regroup_heads(a, axis):
        shape = a.shape
        a = a.reshape(shape[:axis] + (N_KV_HEADS, GROUP, HEAD_DIM) + shape[axis + 1:])
        return jnp.swapaxes(a, axis, axis + 1).reshape(shape)

    w_in_all = jnp.concatenate([w_in[:, :, :_OFF_Q], regroup_heads(w_in[:, :, _OFF_Q:_OFF_K], 2), w_in[:, :, _OFF_K:]],
                               axis=2).astype(BF16)
    w_out_all = jnp.concatenate([w_out[:, :D_CONV], regroup_heads(w_out[:, D_CONV:], 1)], axis=1).astype(BF16)
    g_attn_all = regroup_heads(g_attn, 1)
    router_pad = LANES - N_EXPERTS - N_GROUPS
    wr_all = jnp.concatenate([router_expert_w, router_group_w, jnp.zeros((depth, D_MODEL, router_pad), F32)],
                             axis=2).astype(BF16)
    br_all = jnp.concatenate([router_expert_b, router_group_b, jnp.zeros((depth, router_pad), F32)], axis=1)
    sink_all = jnp.broadcast_to(jnp.repeat(attn_sinks.astype(F32), CHUNK, axis=1).reshape(depth, N_KV_HEADS, QROWS, 1),
                                (depth, N_KV_HEADS, QROWS, LANES))

    xp, xs = x_prompt, x_sample
    states = [[] for _ in range(6)]
    for l in range(depth):
        mix_prm = (w_in_all[l], conv_w[l], conv_b[l].reshape(1, D_CONV), sink_all[l],
                   g_conv[l].reshape(1, D_CONV), g_attn_all[l].reshape(1, D_ATTN), w_out_all[l],
                   ln1_g[l].reshape(1, D_MODEL), ln1_b[l].reshape(1, D_MODEL), wr_all[l], br_all[l].reshape(1, LANES))
        moe_prm = (expert_w_gate, expert_w_up, expert_w_down,
                   ln2_g[l].reshape(1, D_MODEL), ln2_b[l].reshape(1, D_MODEL))
        cache = (cache_conv[l], cache_k[l].reshape(dec_batch, WINDOW, D_KV), cache_v[l].reshape(dec_batch, WINDOW, D_KV))
        xp, xs, state_p, state_s = _layer(xp, xs, cache, mix_prm, consts, moe_prm, l,
                                          tl_prompt=tl_prompt, nseq_sample=nseq_sample, alpha=alpha)
        for lst, val in zip(states, state_p + state_s):
            lst.append(val)

    def kv(lst, nb):
        return jnp.stack(lst).reshape(depth, nb, WINDOW, N_KV_HEADS, HEAD_DIM)

    return (xp, xs, jnp.stack(states[0]), kv(states[1], batch), kv(states[2], batch),
            jnp.stack(states[3]), kv(states[4], dec_batch), kv(states[5], dec_batch))
```

```python
import functools

import numpy as np
import jax
import jax.numpy as jnp
from jax import lax
from jax.experimental import pallas as pl
from jax.experimental.pallas import tpu as pltpu

D_MODEL = 1024
D_CONV = 512
CONV_W = 3
N_HEADS = 8
N_KV_HEADS = 2
GROUP = N_HEADS // N_KV_HEADS
HEAD_DIM = 64
D_ATTN = N_HEADS * HEAD_DIM
D_KV = N_KV_HEADS * HEAD_DIM
WINDOW = 128
CHUNK = 64
KEYS = WINDOW + CHUNK
QROWS = GROUP * CHUNK
N_GROUPS = 4
EXP_PER_GROUP = 8
N_EXPERTS = N_GROUPS * EXP_PER_GROUP
TOP_K = 2
D_EXP = 512
LN_EPS = 1e-5
NEG_INF = -1e30

SUBLANES = 8
LANES = 128
ROW_TILES = D_MODEL // LANES
assert ROW_TILES == SUBLANES
ROWS = 512
MIXER_ROWS = 512
DISPATCH_ROWS = 1024
COMBINE_PIECES = 8
EXPERT_BLOCK = 512
RANK_CHUNK = 256
SOFTMAX_ROWS = 64
PROJ_CHUNK = 2 * LANES
MIX_PIECES = 4
VMEM_LIMIT = 56 * 1024 * 1024

F32 = jnp.float32
BF16 = jnp.bfloat16

_OFF_B, _OFF_C, _OFF_H = 0, D_CONV, 2 * D_CONV
_OFF_Q = 3 * D_CONV
_OFF_K = _OFF_Q + D_ATTN
_OFF_V = _OFF_K + D_KV
D_IN = _OFF_V + D_KV

_R_E1, _R_E2, _R_RANK1, _R_RANK2, _R_G1, _R_G2 = 0, 1, 2, 3, 4, 5


def _load_rows(ref, n, first=0):
    return jnp.concatenate([ref[pl.ds(first * ROW_TILES + j, n, stride=ROW_TILES), :] for j in range(ROW_TILES)],
                           axis=1)


def _store_rows(ref, val, n):
    for j in range(ROW_TILES):
        ref[pl.ds(j, n, stride=ROW_TILES), :] = val[:, j * LANES:(j + 1) * LANES]


def _dot(a, b):
    return jnp.dot(a, b, preferred_element_type=F32)


def _rms_norm(x, g):
    return x * lax.rsqrt(jnp.mean(jnp.square(x), -1, keepdims=True) + LN_EPS) * g


def _layer_norm(x, g, b):
    mu = jnp.mean(x, -1, keepdims=True)
    xc = x - mu
    var = jnp.mean(jnp.square(xc), -1, keepdims=True)
    return xc * lax.rsqrt(var + LN_EPS) * g + b


def _mixer_kernel(x_ref, cconv_ref, ck_ref, cv_ref, w_in_ref, convw_ref, convb_ref, abias_ref,
                  sink_ref, gconv_ref, gattn_ref, w_out_ref, ln1g_ref, ln1b_ref, wr_ref, br_ref,
                  tri_ref, keysel_ref, counts_in_ref,
                  x1_ref, route_ref, gcol_ref, counts_ref, sconv_ref, sk_ref, sv_ref,
                  uext, kext, vext, cnt_s, s_ref, e_ref, sinkden_ref, m0_ref, m1_ref, proj_ref, x1b_s, xprev_s,
                  mixed_s, *, nseq, tl, alpha, mask_history, defer_tail):
    b = pl.program_id(0)
    s = pl.program_id(1)
    rows = nseq * tl
    upitch = tl + SUBLANES
    kpitch = WINDOW + tl

    @pl.when(jnp.logical_and(b == 0, s == 0))
    def _():
        cnt_s[...] = counts_in_ref[:, 0:1]
        if defer_tail:
            xprev_s[...] = jnp.zeros_like(xprev_s)
            mixed_s[...] = jnp.zeros_like(mixed_s)

    @pl.when(s == 0)
    def _():
        for j in range(nseq):
            uext[j * upitch + SUBLANES - 2:j * upitch + SUBLANES, :] = cconv_ref[j]
            kext[j * kpitch:j * kpitch + WINDOW, :] = ck_ref[j]
            vext[j * kpitch:j * kpitch + WINDOW, :] = cv_ref[j]

    def norm_rows(r0, r1, x_rows, mixed_rows):
        x1 = _layer_norm(alpha * x_rows + mixed_rows, ln1g_ref[...], ln1b_ref[...])
        for j in range(ROW_TILES):
            x1_ref[pl.ds(r0 * ROW_TILES + j, r1 - r0, stride=ROW_TILES), :] = x1[:, j * LANES:(j + 1) * LANES]
        x1b_s[r0:r1, :] = x1.astype(BF16)

    def route(valid):
        logits_t = (_dot(x1b_s[...], wr_ref[...]) + br_ref[...]).T
        sub = lax.broadcasted_iota(jnp.int32, (SUBLANES, rows), 0)
        gl = jnp.where(sub < N_GROUPS, logits_t[N_EXPERTS:N_EXPERTS + SUBLANES, :], -jnp.inf)
        gmax = jnp.max(gl, axis=0, keepdims=True)
        grp = jnp.min(jnp.where(gl == gmax, sub, SUBLANES), axis=0, keepdims=True)
        p_grp = 1.0 / jnp.sum(jnp.exp(gl - gmax), axis=0, keepdims=True)
        el = logits_t[(N_GROUPS - 1) * EXP_PER_GROUP:N_GROUPS * EXP_PER_GROUP, :]
        for g in range(N_GROUPS - 2, -1, -1):
            el = jnp.where(grp == g, logits_t[g * EXP_PER_GROUP:(g + 1) * EXP_PER_GROUP, :], el)
        v1 = jnp.max(el, axis=0, keepdims=True)
        i1 = jnp.min(jnp.where(el == v1, sub, SUBLANES), axis=0, keepdims=True)
        el2 = jnp.where(sub == i1, -jnp.inf, el)
        v2 = jnp.max(el2, axis=0, keepdims=True)
        i2 = jnp.min(jnp.where(el2 == v2, sub, SUBLANES), axis=0, keepdims=True)
        e2 = jnp.exp(v2 - v1)
        gate1 = p_grp * (1.0 / (1.0 + e2))
        gate2 = p_grp * (e2 / (1.0 + e2))

        chosen = jnp.logical_or(sub == i1, sub == i2)
        onehot = jnp.concatenate(
            [jnp.where(jnp.logical_and(grp == g, chosen), 1.0, 0.0) for g in range(N_GROUPS)], axis=0)
        running = cnt_s[...]
        ranks = []
        for c in range(rows // RANK_CHUNK):
            oh = onehot[:, c * RANK_CHUNK:(c + 1) * RANK_CHUNK]
            ranks.append(_dot(oh.astype(BF16), tri_ref[...]) + running)
            running = running + jnp.sum(oh, axis=1, keepdims=True) * valid
        rank = jnp.concatenate(ranks, axis=1)
        cnt_s[...] = running
        counts_ref[...] = jnp.broadcast_to(running, (N_EXPERTS, LANES))

        ex1 = grp * EXP_PER_GROUP + i1
        ex2 = grp * EXP_PER_GROUP + i2
        erow = lax.broadcasted_iota(jnp.int32, (N_EXPERTS, rows), 0)
        rank1 = jnp.sum(jnp.where(erow == ex1, rank, 0.0), axis=0, keepdims=True)
        rank2 = jnp.sum(jnp.where(erow == ex2, rank, 0.0), axis=0, keepdims=True)
        fields = {_R_E1: ex1.astype(F32), _R_E2: ex2.astype(F32), _R_RANK1: rank1, _R_RANK2: rank2,
                  _R_G1: gate1, _R_G2: gate2}
        record = jnp.zeros((SUBLANES, rows), F32)
        for r, val in fields.items():
            record = jnp.where(sub == r, val, record)
        route_ref[...] = record
        padded = jnp.concatenate([record, jnp.zeros((LANES - SUBLANES, rows), F32)], axis=0)
        gcol_ref[...] = padded.T

    x = x_ref[...].reshape(rows, D_MODEL)
    xb = x.astype(BF16)

    def conv_rows(r0, r1):
        w0, w1, w2 = convw_ref[0:1, :], convw_ref[1:2, :], convw_ref[2:3, :]
        out = []
        for j in range(r0 // tl, -(-r1 // tl)):
            a, b_ = max(r0, j * tl) - j * tl, min(r1, (j + 1) * tl) - j * tl
            base = j * upitch + SUBLANES
            src = pl.ds(j * tl + a, b_ - a)
            uext[base + a:base + b_, :] = proj_ref[src, _OFF_C:_OFF_C + D_CONV] * proj_ref[src, _OFF_H:_OFF_H + D_CONV]
            yc = convb_ref[...] + uext[base + a - 2:base + b_ - 2, :] * w0
            yc = yc + uext[base + a - 1:base + b_ - 1, :] * w1
            yc = yc + uext[base + a:base + b_, :] * w2
            out.append(_rms_norm(proj_ref[src, _OFF_B:_OFF_B + D_CONV] * yc, gconv_ref[...]).astype(BF16))
            if b_ == tl:
                tail = uext[base + tl - 2:base + tl, :]
                sconv_ref[j] = tail
                uext[base - 2:base, :] = tail
        return out[0] if len(out) == 1 else jnp.concatenate(out, axis=0)

    qkv = []
    qkv_chunks = [(c, c + PROJ_CHUNK) for c in range(_OFF_Q, D_IN, PROJ_CHUNK)]
    tail_rows = -(-rows // len(qkv_chunks) // SUBLANES) * SUBLANES
    for p, (c0, c1) in enumerate(qkv_chunks):
        r0, r1 = min(p * tail_rows, rows), min((p + 1) * tail_rows, rows)
        if defer_tail and r1 > r0:
            norm_rows(r0, r1, xprev_s[r0:r1, :], mixed_s[r0:r1, :])
        qkv.append(_dot(xb, w_in_ref[:, c0:c1]))
    qkv = jnp.concatenate(qkv, axis=1)
    if defer_tail:
        route((s > 0).astype(F32))
    q = (qkv[:, 0:D_ATTN] * (HEAD_DIM ** -0.5)).astype(BF16)
    k = qkv[:, D_ATTN:D_ATTN + D_KV]
    v = qkv[:, D_ATTN + D_KV:D_ATTN + 2 * D_KV]
    for j in range(nseq):
        kext[j * kpitch + WINDOW:(j + 1) * kpitch, :] = k[j * tl:(j + 1) * tl]
        vext[j * kpitch + WINDOW:(j + 1) * kpitch, :] = v[j * tl:(j + 1) * tl]

    nchunk = rows // CHUNK
    assert nseq == 1 or tl == CHUNK
    key_stride = CHUNK if nseq == 1 else kpitch
    head0 = lax.broadcasted_iota(jnp.int32, (1, D_KV), 1) < HEAD_DIM

    def windows(ext_ref):
        ext = ext_ref[...].astype(BF16)
        heads = (jnp.where(head0, ext, jnp.zeros_like(ext)), jnp.where(head0, jnp.zeros_like(ext), ext))
        return jnp.stack([jnp.concatenate([h[c * key_stride:c * key_stride + KEYS] for h in heads], axis=0)
                          for c in range(nchunk)])

    q3 = jnp.stack([jnp.concatenate([q[c * CHUNK:(c + 1) * CHUNK, g * D_KV:(g + 1) * D_KV] for g in range(GROUP)], axis=0)
                    for c in range(nchunk)])
    logits = lax.dot_general(q3, windows(kext), (((2,), (2,)), ((0,), (0,))), preferred_element_type=F32)
    s_ref[...] = logits.reshape(nchunk * QROWS, 2 * KEYS)

    assert 2 * KEYS == 3 * LANES
    mid0 = lax.broadcasted_iota(jnp.int32, (1, LANES), 1) < KEYS - LANES
    colk = lax.broadcasted_iota(jnp.int32, (1, 2 * KEYS), 1)
    key = jnp.where(colk < KEYS, colk, colk - KEYS)

    def row_block(i):
        r = i * SOFTMAX_ROWS
        return pl.ds(r, SOFTMAX_ROWS), pl.ds(r % QROWS, SOFTMAX_ROWS), r // QROWS

    def tiles(lg):
        return lg[:, 0:LANES], lg[:, LANES:2 * LANES], lg[:, 2 * LANES:3 * LANES]

    def max_rows(i, masked):
        rows_i, qrows_i, chunk_i = row_block(i)
        lg = s_ref[rows_i, :] + abias_ref[qrows_i, :]
        if masked:
            lg = lg + jnp.where(s * tl + chunk_i * CHUNK - WINDOW + key < 0, NEG_INF, 0.0).astype(F32)
        s_ref[rows_i, :] = lg
        t0, t1, t2 = tiles(lg)
        m0_ref[rows_i, :] = jnp.maximum(jnp.max(jnp.maximum(t0, jnp.where(mid0, t1, NEG_INF)), -1, keepdims=True),
                                        sink_ref[0, qrows_i, :])
        m1_ref[rows_i, :] = jnp.maximum(jnp.max(jnp.maximum(t2, jnp.where(mid0, NEG_INF, t1)), -1, keepdims=True),
                                        sink_ref[1, qrows_i, :])

    def exp_rows(i):
        rows_i, qrows_i, _ = row_block(i)
        t0, t1, t2 = tiles(s_ref[rows_i, :])
        m0, m1 = m0_ref[rows_i, :], m1_ref[rows_i, :]
        e = jnp.concatenate([jnp.exp(t0 - m0), jnp.exp(t1 - jnp.where(mid0, m0, m1)), jnp.exp(t2 - m1)], axis=-1)
        e_ref[rows_i, :] = e.astype(BF16)
        sinkden_ref[rows_i, :] = jnp.where(head0, jnp.exp(sink_ref[0, qrows_i, :] - m0),
                                           jnp.exp(sink_ref[1, qrows_i, :] - m1))

    n_it = nchunk * QROWS // SOFTMAX_ROWS
    n_masked = min(WINDOW // CHUNK, nchunk) * QROWS // SOFTMAX_ROWS if mask_history else 0
    proj_chunks = [(c, min(c + PROJ_CHUNK, _OFF_Q)) for c in range(0, _OFF_Q, PROJ_CHUNK)]
    piece = -(-2 * n_it // len(proj_chunks))
    steps = [functools.partial(max_rows, i, i < n_masked) for i in range(n_it)]
    steps += [functools.partial(exp_rows, i) for i in range(n_it)]
    for p, (c0, c1) in enumerate(proj_chunks):
        for step_fn in steps[p * piece:(p + 1) * piece]:
            step_fn()
        proj_ref[:, c0:c1] = _dot(xb, w_in_ref[:, c0:c1])
    for step_fn in steps[len(proj_chunks) * piece:]:
        step_fn()
    vsel = jnp.concatenate([windows(vext), jnp.broadcast_to(keysel_ref[...], (nchunk, 2 * KEYS, D_KV))], axis=-1)
    n_pieces = min(MIX_PIECES, nchunk)
    cpp = nchunk // n_pieces
    conv_parts, attn_parts = [], []
    for p in range(n_pieces):
        cs = slice(p * cpp, (p + 1) * cpp)
        o2 = lax.dot_general(e_ref[p * cpp * QROWS:(p + 1) * cpp * QROWS, :].reshape(cpp, QROWS, 2 * KEYS), vsel[cs],
                             (((2,), (1,)), ((0,), (0,))), preferred_element_type=F32)
        sinkden = sinkden_ref[p * cpp * QROWS:(p + 1) * cpp * QROWS, :].reshape(cpp, QROWS, D_KV)
        o = o2[..., 0:D_KV] * (1.0 / (o2[..., D_KV:2 * D_KV] + sinkden))
        attn_parts += [jnp.concatenate([o[c, g * CHUNK:(g + 1) * CHUNK, :] for g in range(GROUP)], axis=1)
                       for c in range(cpp)]
        conv_parts.append(conv_rows(p * cpp * CHUNK, (p + 1) * cpp * CHUNK))
    y_attn = jnp.concatenate(attn_parts, axis=0)
    n_conv = jnp.concatenate(conv_parts, axis=0)

    for j in range(nseq):
        sk_ref[j] = kext[j * kpitch + tl:(j + 1) * kpitch, :]
        sv_ref[j] = vext[j * kpitch + tl:(j + 1) * kpitch, :]
    if nseq == 1:
        kext[0:WINDOW, :] = kext[tl:tl + WINDOW, :]
        vext[0:WINDOW, :] = vext[tl:tl + WINDOW, :]

    n_attn = _rms_norm(y_attn, gattn_ref[...]).astype(BF16)
    mixed = _dot(n_attn, w_out_ref[D_CONV:D_CONV + D_ATTN, :]) + _dot(n_conv, w_out_ref[0:D_CONV, :])
    if defer_tail:
        xprev_s[...] = x
        mixed_s[...] = mixed
    else:
        norm_rows(0, rows, x, mixed)
        route(1.0)


def _mixer(x, cconv, ck, cv, counts_in, prm, consts, *, nseq, tl, mask_history, alpha):
    nb_total, t_total = x.shape[0], x.shape[1]
    nb, ns = nb_total // nseq, t_total // tl
    defer_tail = ns > 1
    steps = ns + 1 if defer_tail else ns

    def in_blk(s):
        return jnp.minimum(s, ns - 1)

    def out_blk(b, s):
        return b * ns + (jnp.maximum(s - 1, 0) if defer_tail else s)

    rows = nseq * tl
    n_tok = nb_total * t_total
    w_in, convw, convb, sinks, gconv, gattn, w_out, ln1g, ln1b, wr, br = prm
    abias, tri, keysel = consts

    def full(a):
        return pl.BlockSpec(a.shape, lambda b, s, _n=a.ndim: (0,) * _n)

    def seq_state(width, nrows):
        return pl.BlockSpec((nseq, nrows, width), lambda b, s: (b, 0, 0))

    in_specs = [
        pl.BlockSpec((nseq, tl, D_MODEL), lambda b, s: (b, in_blk(s), 0)),
        seq_state(D_CONV, CONV_W - 1), seq_state(D_KV, WINDOW), seq_state(D_KV, WINDOW),
        full(w_in), full(convw), full(convb), full(abias), full(sinks), full(gconv), full(gattn),
        full(w_out), full(ln1g), full(ln1b), full(wr), full(br), full(tri), full(keysel), full(counts_in),
    ]
    out_shape = [
        jax.ShapeDtypeStruct((n_tok * ROW_TILES, LANES), F32),
        jax.ShapeDtypeStruct((SUBLANES, n_tok), F32),
        jax.ShapeDtypeStruct((n_tok, LANES), F32),
        jax.ShapeDtypeStruct((N_EXPERTS, LANES), F32),
        jax.ShapeDtypeStruct((nb_total, CONV_W - 1, D_CONV), F32),
        jax.ShapeDtypeStruct((nb_total, WINDOW, D_KV), F32),
        jax.ShapeDtypeStruct((nb_total, WINDOW, D_KV), F32),
    ]
    out_specs = [
        pl.BlockSpec((rows * ROW_TILES, LANES), lambda b, s: (out_blk(b, s), 0)),
        pl.BlockSpec((SUBLANES, rows), lambda b, s: (0, out_blk(b, s))),
        pl.BlockSpec((rows, LANES), lambda b, s: (out_blk(b, s), 0)),
        pl.BlockSpec((N_EXPERTS, LANES), lambda b, s: (0, 0)),
        seq_state(D_CONV, CONV_W - 1), seq_state(D_KV, WINDOW), seq_state(D_KV, WINDOW),
    ]
    scratch = [
        pltpu.VMEM((nseq * (tl + SUBLANES), D_CONV), F32),
        pltpu.VMEM((nseq * (WINDOW + tl), D_KV), F32),
        pltpu.VMEM((nseq * (WINDOW + tl), D_KV), F32),
        pltpu.VMEM((N_EXPERTS, 1), F32),
        pltpu.VMEM((rows // CHUNK * QROWS, 2 * KEYS), F32),
        pltpu.VMEM((rows // CHUNK * QROWS, 2 * KEYS), BF16),
        pltpu.VMEM((rows // CHUNK * QROWS, D_KV), F32),
        pltpu.VMEM((rows // CHUNK * QROWS, LANES), F32),
        pltpu.VMEM((rows // CHUNK * QROWS, LANES), F32),
        pltpu.VMEM((rows, _OFF_Q), F32),
        pltpu.VMEM((rows, D_MODEL), BF16),
        pltpu.VMEM((rows, D_MODEL) if defer_tail else (SUBLANES, LANES), F32),
        pltpu.VMEM((rows, D_MODEL) if defer_tail else (SUBLANES, LANES), F32),
    ]
    kern = functools.partial(_mixer_kernel, nseq=nseq, tl=tl, alpha=alpha, mask_history=mask_history,
                             defer_tail=defer_tail)
    return pl.pallas_call(
        kern, grid=(nb, steps), in_specs=in_specs, out_specs=out_specs, out_shape=out_shape,
        scratch_shapes=scratch, name="mixer",
        compiler_params=pltpu.CompilerParams(dimension_semantics=("arbitrary", "arbitrary"),
                                             vmem_limit_bytes=VMEM_LIMIT),
    )(x, cconv, ck, cv, w_in, convw, convb, abias, sinks, gconv, gattn, w_out, ln1g, ln1b, wr, br, tri, keysel, counts_in)


_PAD_PIECES = tuple(2 ** k for k in range(EXPERT_BLOCK.bit_length() - 2, -1, -1))


def _tile_rows(row, n=1):
    if isinstance(row, int):
        return pl.ds(row * ROW_TILES, n * ROW_TILES)
    return pl.ds(pl.multiple_of(row * ROW_TILES, ROW_TILES), n * ROW_TILES)


def _row_copy(src, src_row, dst, dst_row, sem):
    return pltpu.make_async_copy(src.at[_tile_rows(src_row), :], dst.at[_tile_rows(dst_row), :], sem)


def _dispatch_kernel(zstart_ref, zcount_ref, tail_ref, pos1_ref, pos2_ref, xa_ref, xb_ref, xs_hbm, zeros_v, sem, zsem,
                     *, tb, nba):
    step = pl.program_id(0)

    def zero_copy(start, piece):
        return pltpu.make_async_copy(zeros_v.at[_tile_rows(0, piece), :], xs_hbm.at[_tile_rows(start, piece), :], zsem)

    def pad_copy(e, piece, taken):
        return zero_copy(zstart_ref[e] + taken, piece)

    def for_each_pad_piece(fn):
        def per_tail_block(j, carry):
            for part in range(EXPERT_BLOCK // _PAD_PIECES[0]):
                fn(zero_copy(tail_ref[0] + j * EXPERT_BLOCK + part * _PAD_PIECES[0], _PAD_PIECES[0]))
            return carry

        lax.fori_loop(0, tail_ref[1], per_tail_block, 0)

        def per_expert(e, carry):
            count = zcount_ref[e]
            taken = 0
            for piece in _PAD_PIECES:
                present = (count & piece) != 0

                @pl.when(present)
                def _(piece=piece, taken=taken):
                    fn(pad_copy(e, piece, taken))

                taken = taken + jnp.where(present, piece, 0)
            return carry

        lax.fori_loop(0, N_EXPERTS, per_expert, 0)

    @pl.when(step == 0)
    def _():
        zeros_v[...] = jnp.zeros_like(zeros_v)
        for_each_pad_piece(lambda c: c.start())

    def issue_rows(x_ref):
        def issue(t, carry):
            _row_copy(x_ref, t, xs_hbm, pos1_ref[t], sem).start(priority=0)
            _row_copy(x_ref, t, xs_hbm, pos2_ref[t], sem).start(priority=1)
            return carry

        lax.fori_loop(0, tb, issue, 0, unroll=8)

    @pl.when(step < nba)
    def _():
        issue_rows(xa_ref)

    @pl.when(step >= nba)
    def _():
        issue_rows(xb_ref)

    pltpu.make_async_copy(xs_hbm.at[_tile_rows(0, 2 * tb), :], xs_hbm.at[_tile_rows(0, 2 * tb), :], sem).wait()

    @pl.when(step == 0)
    def _():
        for_each_pad_piece(lambda c: c.wait())


def _dispatch(xa_rows, xb_rows, pos1, pos2, zstart, zcount, tail, n_slots, *, tb):
    nba, nbb = xa_rows.shape[0] // ROW_TILES // tb, xb_rows.shape[0] // ROW_TILES // tb
    kern = functools.partial(_dispatch_kernel, tb=tb, nba=nba)
    grid_spec = pltpu.PrefetchScalarGridSpec(
        num_scalar_prefetch=3, grid=(nba + nbb,),
        in_specs=[pl.BlockSpec((tb,), lambda i, *_: (i,), memory_space=pltpu.SMEM),
                  pl.BlockSpec((tb,), lambda i, *_: (i,), memory_space=pltpu.SMEM),
                  pl.BlockSpec((tb * ROW_TILES, LANES), lambda i, *_: (jnp.minimum(i, nba - 1), 0)),
                  pl.BlockSpec((tb * ROW_TILES, LANES), lambda i, *_: (jnp.maximum(i - nba, 0), 0))],
        out_specs=pl.BlockSpec(memory_space=pl.ANY),
        scratch_shapes=[pltpu.VMEM((_PAD_PIECES[0] * ROW_TILES, LANES), F32),
                        pltpu.SemaphoreType.DMA(()), pltpu.SemaphoreType.DMA(())],
    )
    return pl.pallas_call(
        kern, grid_spec=grid_spec, out_shape=jax.ShapeDtypeStruct((n_slots * ROW_TILES, LANES), F32),
        name="dispatch",
        compiler_params=pltpu.CompilerParams(dimension_semantics=("arbitrary",)),
    )(zstart, zcount, tail, pos1, pos2, xa_rows, xb_rows)


def _expert_kernel(be_ref, src_ref, used_ref, xs_ref, wg_ref, wu_ref, wd_ref, yb_ref, wg_b, wu_b, wd_b):
    i = pl.program_id(0)
    e = be_ref[i]
    prev = be_ref[jnp.maximum(i - 1, 0)]

    @pl.when(jnp.logical_or(i == 0, e != prev))
    def _():
        wg_b[...] = wg_ref[...].astype(BF16)
        wu_b[...] = wu_ref[...].astype(BF16)
        wd_b[...] = wd_ref[...].astype(BF16)

    @pl.when(used_ref[i] != 0)
    def _():
        x = _load_rows(xs_ref, EXPERT_BLOCK).astype(BF16)
        g = _dot(x, wg_b[...])
        u = _dot(x, wu_b[...])
        hid = (g / (1.0 + jnp.exp(-g))) * u
        _store_rows(yb_ref, _dot(hid.astype(BF16), wd_b[...]), EXPERT_BLOCK)

    @pl.when(used_ref[i] == 0)
    def _():
        yb_ref[...] = jnp.zeros_like(yb_ref)


def _experts(xs, blk_e, blk_src, blk_used, w_gate, w_up, w_down, layer):
    n_slots = xs.shape[0] // ROW_TILES
    nblk = n_slots // EXPERT_BLOCK

    def wspec(shape):
        return pl.BlockSpec((None, None) + shape, lambda i, be, src, used: (layer, be[i], 0, 0))

    grid_spec = pltpu.PrefetchScalarGridSpec(
        num_scalar_prefetch=3, grid=(nblk,),
        in_specs=[pl.BlockSpec((EXPERT_BLOCK * ROW_TILES, LANES), lambda i, be, src, used: (src[i], 0)),
                  wspec((D_MODEL, D_EXP)), wspec((D_MODEL, D_EXP)), wspec((D_EXP, D_MODEL))],
        out_specs=pl.BlockSpec((EXPERT_BLOCK * ROW_TILES, LANES), lambda i, be, src, used: (i, 0)),
        scratch_shapes=[pltpu.VMEM((D_MODEL, D_EXP), BF16), pltpu.VMEM((D_MODEL, D_EXP), BF16),
                        pltpu.VMEM((D_EXP, D_MODEL), BF16)],
    )
    return pl.pallas_call(
        _expert_kernel, grid_spec=grid_spec, out_shape=jax.ShapeDtypeStruct((n_slots * ROW_TILES, LANES), F32),
        name="experts",
        compiler_params=pltpu.CompilerParams(dimension_semantics=("arbitrary",), vmem_limit_bytes=VMEM_LIMIT),
    )(blk_e, blk_src, blk_used, xs, w_gate, w_up, w_down)


def _combine_kernel(pos1_ref, pos2_ref, next1_ref, next2_ref, xa_ref, xb_ref, ga_ref, gb_ref, yb_hbm, g_ref, b_ref,
                    outa_ref, outb_ref, ybuf, sems, *, tb, alpha, nba, nsteps):
    step = pl.program_id(0)
    slot = step % 2
    other = 1 - slot

    def issue(p1_ref, p2_ref, to_slot, t):
        _row_copy(yb_hbm, p1_ref[t], ybuf.at[to_slot].at[0], t, sems.at[to_slot]).start(priority=0)
        _row_copy(yb_hbm, p2_ref[t], ybuf.at[to_slot].at[1], t, sems.at[to_slot]).start(priority=1)

    def wait_slot(which):
        for k in range(TOP_K):
            pltpu.make_async_copy(yb_hbm.at[_tile_rows(0, tb), :], ybuf.at[which].at[k], sems.at[which]).wait()

    @pl.when(step == 0)
    def _():
        def body(t, carry):
            issue(pos1_ref, pos2_ref, 0, t)
            return carry
        lax.fori_loop(0, tb, body, 0, unroll=8)

    wait_slot(slot)

    def finish(x1_ref, gcol_ref, out_ref):
        n = tb // COMBINE_PIECES
        for p in range(COMBINE_PIECES):
            for t in range(p * n, (p + 1) * n):
                issue(next1_ref, next2_ref, other, t)
            gates = gcol_ref[p * n:(p + 1) * n, :]
            y = (gates[:, _R_G1:_R_G1 + 1] * _load_rows(ybuf.at[slot].at[0], n, p * n)
                 + gates[:, _R_G2:_R_G2 + 1] * _load_rows(ybuf.at[slot].at[1], n, p * n))
            out_ref[p * n:(p + 1) * n, :] = _layer_norm(alpha * _load_rows(x1_ref, n, p * n) + y, g_ref[...], b_ref[...])

    @pl.when(step < nba)
    def _():
        finish(xa_ref, ga_ref, outa_ref)

    @pl.when(step >= nba)
    def _():
        finish(xb_ref, gb_ref, outb_ref)

    @pl.when(step == nsteps - 1)
    def _():
        wait_slot(other)


def _combine(xa_rows, xb_rows, gcol_a, gcol_b, yb, pos1, pos2, ln2g, ln2b, *, tb, alpha):
    nba, nbb = xa_rows.shape[0] // ROW_TILES // tb, xb_rows.shape[0] // ROW_TILES // tb
    nsteps = nba + nbb
    kern = functools.partial(_combine_kernel, tb=tb, alpha=alpha, nba=nba, nsteps=nsteps)

    def first(i):
        return jnp.minimum(i, nba - 1)

    def second(i):
        return jnp.maximum(i - nba, 0)

    def nxt(i):
        return jnp.minimum(i + 1, nsteps - 1)

    smem = functools.partial(pl.BlockSpec, (tb,), memory_space=pltpu.SMEM)
    return pl.pallas_call(
        kern, grid=(nsteps,),
        in_specs=[smem(lambda i: (i,)), smem(lambda i: (i,)), smem(lambda i: (nxt(i),)), smem(lambda i: (nxt(i),)),
                  pl.BlockSpec((tb * ROW_TILES, LANES), lambda i: (first(i), 0)),
                  pl.BlockSpec((tb * ROW_TILES, LANES), lambda i: (second(i), 0)),
                  pl.BlockSpec((tb, LANES), lambda i: (first(i), 0)),
                  pl.BlockSpec((tb, LANES), lambda i: (second(i), 0)),
                  pl.BlockSpec(memory_space=pl.ANY),
                  pl.BlockSpec((1, D_MODEL), lambda i: (0, 0)),
                  pl.BlockSpec((1, D_MODEL), lambda i: (0, 0))],
        out_specs=[pl.BlockSpec((tb, D_MODEL), lambda i: (first(i), 0)),
                   pl.BlockSpec((tb, D_MODEL), lambda i: (second(i), 0))],
        out_shape=[jax.ShapeDtypeStruct((nba * tb, D_MODEL), F32), jax.ShapeDtypeStruct((nbb * tb, D_MODEL), F32)],
        scratch_shapes=[pltpu.VMEM((2, TOP_K, tb * ROW_TILES, LANES), F32), pltpu.SemaphoreType.DMA((2,))],
        name="combine",
        compiler_params=pltpu.CompilerParams(dimension_semantics=("arbitrary",), vmem_limit_bytes=VMEM_LIMIT),
    )(pos1, pos2, pos1, pos2, xa_rows, xb_rows, gcol_a, gcol_b, yb, ln2g, ln2b)


def _alibi_bias():
    slopes = np.asarray([2.0 ** (-8.0 * (h + 1) / N_HEADS) for h in range(N_HEADS)], np.float32)
    qi = np.arange(CHUNK, dtype=np.int32)[:, None]
    sj = np.arange(KEYS, dtype=np.int32)[None, :]
    dist = np.abs(qi + WINDOW - sj).astype(np.float32)
    bias = -slopes.reshape(N_KV_HEADS, GROUP, 1, 1) * dist
    bias = np.transpose(bias, (1, 2, 0, 3)).reshape(GROUP * CHUNK, N_KV_HEADS * KEYS)
    return jnp.asarray(bias, F32)


def _key_selector():
    row_head = np.arange(N_KV_HEADS * KEYS)[:, None] // KEYS
    lane_head = np.arange(D_KV)[None, :] // HEAD_DIM
    return jnp.asarray((row_head == lane_head).astype(np.float32), BF16)


def _strict_upper():
    r = np.arange(RANK_CHUNK)
    return jnp.asarray((r[:, None] < r[None, :]).astype(np.float32), BF16)


def _slot_plan(route, counts, n_tok):
    nblk = (n_tok * TOP_K) // EXPERT_BLOCK + N_EXPERTS
    experts = route[_R_E1:_R_E2 + 1].astype(jnp.int32)
    ranks = route[_R_RANK1:_R_RANK2 + 1].astype(jnp.int32)
    cnt = counts[:, 0].astype(jnp.int32)
    padded = (cnt + EXPERT_BLOCK - 1) // EXPERT_BLOCK * EXPERT_BLOCK
    pad_end = jnp.cumsum(padded)
    pad_start = pad_end - padded
    ids = jnp.arange(N_EXPERTS, dtype=jnp.int32)
    pos = jnp.sum(jnp.where(experts[..., None] == ids, pad_start, 0), axis=-1) + ranks
    blk_first = jnp.arange(nblk, dtype=jnp.int32) * EXPERT_BLOCK
    blk_e = jnp.minimum(jnp.sum((pad_end[None, :] <= blk_first[:, None]).astype(jnp.int32), axis=-1), N_EXPERTS - 1)
    blk_used = (blk_first < pad_end[-1]).astype(jnp.int32)
    blk_src = jnp.minimum(jnp.arange(nblk, dtype=jnp.int32), pad_end[-1] // EXPERT_BLOCK - 1)
    tail = jnp.stack([pad_end[-1], nblk - pad_end[-1] // EXPERT_BLOCK]).astype(jnp.int32)
    return pos[0], pos[1], pad_start + cnt, padded - cnt, tail, blk_e, blk_src, blk_used, nblk * EXPERT_BLOCK


def _layer(xp, xs, cache, mix_prm, consts, moe_prm, layer, *, tl_prompt, nseq_sample, alpha):
    (bp, tp, _), (bs, ts, _) = xp.shape, xs.shape
    np_tok, ns_tok = bp * tp, bs * ts
    tb = ROWS
    assert np_tok % tb == 0 and ns_tok % tb == 0
    zeros_conv = jnp.zeros((bp, CONV_W - 1, D_CONV), F32)
    zeros_kv = jnp.zeros((bp, WINDOW, D_KV), F32)
    zero_counts = jnp.zeros((N_EXPERTS, LANES), F32)
    x1p, route_p, gcol_p, counts_p, *state_p = _mixer(
        xp, zeros_conv, zeros_kv, zeros_kv, zero_counts, mix_prm, consts,
        nseq=1, tl=tl_prompt, mask_history=True, alpha=alpha)
    x1s, route_s, gcol_s, counts, *state_s = _mixer(
        xs, *cache, counts_p, mix_prm, consts, nseq=nseq_sample, tl=ts, mask_history=False, alpha=alpha)
    route = jnp.concatenate([route_p, route_s], axis=1)
    pos1, pos2, zstart, zcount, tail, blk_e, blk_src, blk_used, n_slots = _slot_plan(route, counts, np_tok + ns_tok)
    tb_dispatch = DISPATCH_ROWS if np_tok % DISPATCH_ROWS == 0 and ns_tok % DISPATCH_ROWS == 0 else tb
    sorted_rows = _dispatch(x1p, x1s, pos1, pos2, zstart, zcount, tail, n_slots, tb=tb_dispatch)
    w_gate, w_up, w_down, ln2g, ln2b = moe_prm
    yb = _experts(sorted_rows, blk_e, blk_src, blk_used, w_gate, w_up, w_down, layer)
    out_p, out_s = _combine(x1p, x1s, gcol_p, gcol_s, yb, pos1, pos2, ln2g, ln2b, tb=tb, alpha=alpha)
    return out_p.reshape(bp, tp, D_MODEL), out_s.reshape(bs, ts, D_MODEL), state_p, state_s


def kernel(x_prompt, x_sample, cache_conv, cache_k, cache_v, w_in, conv_w, conv_b, attn_sinks, g_conv, g_attn,
           w_out, ln1_g, ln1_b, router_group_w, router_group_b, router_expert_w, router_expert_b,
           expert_w_gate, expert_w_up, expert_w_down, ln2_g, ln2_b):
    depth = w_in.shape[0]
    alpha = (2 * depth) ** 0.25
    batch, seq = x_prompt.shape[0], x_prompt.shape[1]
    dec_batch, dec_seq = x_sample.shape[0], x_sample.shape[1]
    assert dec_seq == CHUNK and seq % CHUNK == 0
    consts = (_alibi_bias(), _strict_upper(), _key_selector())
    tl_prompt = min(MIXER_ROWS, seq)
    nseq_sample = min(MIXER_ROWS // dec_seq, dec_batch)
    assert seq % tl_prompt == 0 and dec_batch % nseq_sample == 0

    def regroup_heads(a, axis):
        shape = a.shape
        a = a.reshape(shape[:axis] + (N_KV_HEADS, GROUP, HEAD_DIM) + shape[axis + 1:])
        return jnp.swapaxes(a, axis, axis + 1).reshape(shape)

    w_in_all = jnp.concatenate([w_in[:, :, :_OFF_Q], regroup_heads(w_in[:, :, _OFF_Q:_OFF_K], 2), w_in[:, :, _OFF_K:]],
                               axis=2).astype(BF16)
    w_out_all = jnp.concatenate([w_out[:, :D_CONV], regroup_heads(w_out[:, D_CONV:], 1)], axis=1).astype(BF16)
    g_attn_all = regroup_heads(g_attn, 1)
    router_pad = LANES - N_EXPERTS - N_GROUPS
    wr_all = jnp.concatenate([router_expert_w, router_group_w, jnp.zeros((depth, D_MODEL, router_pad), F32)],
                             axis=2).astype(BF16)
    br_all = jnp.concatenate([router_expert_b, router_group_b, jnp.zeros((depth, router_pad), F32)], axis=1)
    sink_all = jnp.broadcast_to(jnp.repeat(attn_sinks.astype(F32), CHUNK, axis=1).reshape(depth, N_KV_HEADS, QROWS, 1),
                                (depth, N_KV_HEADS, QROWS, LANES))

    xp, xs = x_prompt, x_sample
    states = [[] for _ in range(6)]
    for l in range(depth):
        mix_prm = (w_in_all[l], conv_w[l], conv_b[l].reshape(1, D_CONV), sink_all[l],
                   g_conv[l].reshape(1, D_CONV), g_attn_all[l].reshape(1, D_ATTN), w_out_all[l],
                   ln1_g[l].reshape(1, D_MODEL), ln1_b[l].reshape(1, D_MODEL), wr_all[l], br_all[l].reshape(1, LANES))
        moe_prm = (expert_w_gate, expert_w_up, expert_w_down,
                   ln2_g[l].reshape(1, D_MODEL), ln2_b[l].reshape(1, D_MODEL))
        cache = (cache_conv[l], cache_k[l].reshape(dec_batch, WINDOW, D_KV), cache_v[l].reshape(dec_batch, WINDOW, D_KV))
        xp, xs, state_p, state_s = _layer(xp, xs, cache, mix_prm, consts, moe_prm, l,
                                          tl_prompt=tl_prompt, nseq_sample=nseq_sample, alpha=alpha)
        for lst, val in zip(states, state_p + state_s):
            lst.append(val)

    def kv(lst, nb):
        return jnp.stack(lst).reshape(depth, nb, WINDOW, N_KV_HEADS, HEAD_DIM)

    return (xp, xs, jnp.stack(states[0]), kv(states[1], batch), kv(states[2], batch),
            jnp.stack(states[3]), kv(states[4], dec_batch), kv(states[5], dec_batch))
```

```python
import functools

import numpy as np
import jax
import jax.numpy as jnp
from jax import lax
from jax.experimental import pallas as pl
from jax.experimental.pallas import tpu as pltpu

D_MODEL = 1024
D_CONV = 512
CONV_W = 3
N_HEADS = 8
N_KV_HEADS = 2
GROUP = N_HEADS // N_KV_HEADS
HEAD_DIM = 64
D_ATTN = N_HEADS * HEAD_DIM
D_KV = N_KV_HEADS * HEAD_DIM
WINDOW = 128
CHUNK = 64
KEYS = WINDOW + CHUNK
QROWS = GROUP * CHUNK
N_GROUPS = 4
EXP_PER_GROUP = 8
N_EXPERTS = N_GROUPS * EXP_PER_GROUP
TOP_K = 2
D_EXP = 512
LN_EPS = 1e-5
NEG_INF = -1e30

SUBLANES = 8
LANES = 128
ROW_TILES = D_MODEL // LANES
assert ROW_TILES == SUBLANES
ROWS = 512
MIXER_ROWS = 512
DISPATCH_ROWS = 1024
COMBINE_PIECES = 16
EXPERT_BLOCK = 512
RANK_CHUNK = 256
SOFTMAX_ROWS = 64
PROJ_CHUNK = 2 * LANES
MIX_PIECES = 4
VMEM_LIMIT = 56 * 1024 * 1024

F32 = jnp.float32
BF16 = jnp.bfloat16

_OFF_B, _OFF_C, _OFF_H = 0, D_CONV, 2 * D_CONV
_OFF_Q = 3 * D_CONV
_OFF_K = _OFF_Q + D_ATTN
_OFF_V = _OFF_K + D_KV
D_IN = _OFF_V + D_KV

_R_E1, _R_E2, _R_RANK1, _R_RANK2, _R_G1, _R_G2 = 0, 1, 2, 3, 4, 5


def _load_rows(ref, n, first=0):
    return jnp.concatenate([ref[pl.ds(first * ROW_TILES + j, n, stride=ROW_TILES), :] for j in range(ROW_TILES)],
                           axis=1)


def _store_rows(ref, val, n):
    for j in range(ROW_TILES):
        ref[pl.ds(j, n, stride=ROW_TILES), :] = val[:, j * LANES:(j + 1) * LANES]


def _dot(a, b):
    return jnp.dot(a, b, preferred_element_type=F32)


def _rms_norm(x, g):
    return x * lax.rsqrt(jnp.mean(jnp.square(x), -1, keepdims=True) + LN_EPS) * g


def _layer_norm(x, g, b):
    mu = jnp.mean(x, -1, keepdims=True)
    xc = x - mu
    var = jnp.mean(jnp.square(xc), -1, keepdims=True)
    return xc * lax.rsqrt(var + LN_EPS) * g + b


def _mixer_kernel(x_ref, cconv_ref, ck_ref, cv_ref, w_in_ref, convw_ref, convb_ref, abias_ref,
                  sink_ref, gconv_ref, gattn_ref, w_out_ref, ln1g_ref, ln1b_ref, wr_ref, br_ref,
                  tri_ref, keysel_ref, counts_in_ref,
                  x1_ref, route_ref, gcol_ref, counts_ref, sconv_ref, sk_ref, sv_ref,
                  uext, kext, vext, cnt_s, s_ref, e_ref, sinkden_ref, m0_ref, m1_ref, proj_ref, x1b_s, xprev_s,
                  mixed_s, *, nseq, tl, alpha, mask_history, defer_tail):
    b = pl.program_id(0)
    s = pl.program_id(1)
    rows = nseq * tl
    upitch = tl + SUBLANES
    kpitch = WINDOW + tl

    @pl.when(jnp.logical_and(b == 0, s == 0))
    def _():
        cnt_s[...] = counts_in_ref[:, 0:1]
        if defer_tail:
            xprev_s[...] = jnp.zeros_like(xprev_s)
            mixed_s[...] = jnp.zeros_like(mixed_s)

    @pl.when(s == 0)
    def _():
        for j in range(nseq):
            uext[j * upitch + SUBLANES - 2:j * upitch + SUBLANES, :] = cconv_ref[j]
            kext[j * kpitch:j * kpitch + WINDOW, :] = ck_ref[j]
            vext[j * kpitch:j * kpitch + WINDOW, :] = cv_ref[j]

    def norm_rows(r0, r1, x_rows, mixed_rows):
        x1 = _layer_norm(alpha * x_rows + mixed_rows, ln1g_ref[...], ln1b_ref[...])
        for j in range(ROW_TILES):
            x1_ref[pl.ds(r0 * ROW_TILES + j, r1 - r0, stride=ROW_TILES), :] = x1[:, j * LANES:(j + 1) * LANES]
        x1b_s[r0:r1, :] = x1.astype(BF16)

    def route(valid):
        logits_t = (_dot(x1b_s[...], wr_ref[...]) + br_ref[...]).T
        sub = lax.broadcasted_iota(jnp.int32, (SUBLANES, rows), 0)
        gl = jnp.where(sub < N_GROUPS, logits_t[N_EXPERTS:N_EXPERTS + SUBLANES, :], -jnp.inf)
        gmax = jnp.max(gl, axis=0, keepdims=True)
        grp = jnp.min(jnp.where(gl == gmax, sub, SUBLANES), axis=0, keepdims=True)
        p_grp = 1.0 / jnp.sum(jnp.exp(gl - gmax), axis=0, keepdims=True)
        el = logits_t[(N_GROUPS - 1) * EXP_PER_GROUP:N_GROUPS * EXP_PER_GROUP, :]
        for g in range(N_GROUPS - 2, -1, -1):
            el = jnp.where(grp == g, logits_t[g * EXP_PER_GROUP:(g + 1) * EXP_PER_GROUP, :], el)
        v1 = jnp.max(el, axis=0, keepdims=True)
        i1 = jnp.min(jnp.where(el == v1, sub, SUBLANES), axis=0, keepdims=True)
        el2 = jnp.where(sub == i1, -jnp.inf, el)
        v2 = jnp.max(el2, axis=0, keepdims=True)
        i2 = jnp.min(jnp.where(el2 == v2, sub, SUBLANES), axis=0, keepdims=True)
        e2 = jnp.exp(v2 - v1)
        gate1 = p_grp * (1.0 / (1.0 + e2))
        gate2 = p_grp * (e2 / (1.0 + e2))

        chosen = jnp.logical_or(sub == i1, sub == i2)
        onehot = jnp.concatenate(
            [jnp.where(jnp.logical_and(grp == g, chosen), 1.0, 0.0) for g in range(N_GROUPS)], axis=0)
        running = cnt_s[...]
        ranks = []
        for c in range(rows // RANK_CHUNK):
            oh = onehot[:, c * RANK_CHUNK:(c + 1) * RANK_CHUNK]
            ranks.append(_dot(oh.astype(BF16), tri_ref[...]) + running)
            running = running + jnp.sum(oh, axis=1, keepdims=True) * valid
        rank = jnp.concatenate(ranks, axis=1)
        cnt_s[...] = running
        counts_ref[...] = jnp.broadcast_to(running, (N_EXPERTS, LANES))

        ex1 = grp * EXP_PER_GROUP + i1
        ex2 = grp * EXP_PER_GROUP + i2
        erow = lax.broadcasted_iota(jnp.int32, (N_EXPERTS, rows), 0)
        rank1 = jnp.sum(jnp.where(erow == ex1, rank, 0.0), axis=0, keepdims=True)
        rank2 = jnp.sum(jnp.where(erow == ex2, rank, 0.0), axis=0, keepdims=True)
        fields = {_R_E1: ex1.astype(F32), _R_E2: ex2.astype(F32), _R_RANK1: rank1, _R_RANK2: rank2,
                  _R_G1: gate1, _R_G2: gate2}
        record = jnp.zeros((SUBLANES, rows), F32)
        for r, val in fields.items():
            record = jnp.where(sub == r, val, record)
        route_ref[...] = record
        padded = jnp.concatenate([record, jnp.zeros((LANES - SUBLANES, rows), F32)], axis=0)
        gcol_ref[...] = padded.T

    x = x_ref[...].reshape(rows, D_MODEL)
    xb = x.astype(BF16)

    def conv_rows(r0, r1):
        w0, w1, w2 = convw_ref[0:1, :], convw_ref[1:2, :], convw_ref[2:3, :]
        out = []
        for j in range(r0 // tl, -(-r1 // tl)):
            a, b_ = max(r0, j * tl) - j * tl, min(r1, (j + 1) * tl) - j * tl
            base = j * upitch + SUBLANES
            src = pl.ds(j * tl + a, b_ - a)
            uext[base + a:base + b_, :] = proj_ref[src, _OFF_C:_OFF_C + D_CONV] * proj_ref[src, _OFF_H:_OFF_H + D_CONV]
            yc = convb_ref[...] + uext[base + a - 2:base + b_ - 2, :] * w0
            yc = yc + uext[base + a - 1:base + b_ - 1, :] * w1
            yc = yc + uext[base + a:base + b_, :] * w2
            out.append(_rms_norm(proj_ref[src, _OFF_B:_OFF_B + D_CONV] * yc, gconv_ref[...]).astype(BF16))
            if b_ == tl:
                tail = uext[base + tl - 2:base + tl, :]
                sconv_ref[j] = tail
                uext[base - 2:base, :] = tail
        return out[0] if len(out) == 1 else jnp.concatenate(out, axis=0)

    qkv = []
    qkv_chunks = [(c, c + PROJ_CHUNK) for c in range(_OFF_Q, D_IN, PROJ_CHUNK)]
    tail_rows = -(-rows // len(qkv_chunks) // SUBLANES) * SUBLANES
    for p, (c0, c1) in enumerate(qkv_chunks):
        r0, r1 = min(p * tail_rows, rows), min((p + 1) * tail_rows, rows)
        if defer_tail and r1 > r0:
            norm_rows(r0, r1, xprev_s[r0:r1, :], mixed_s[r0:r1, :])
        qkv.append(_dot(xb, w_in_ref[:, c0:c1]))
    qkv = jnp.concatenate(qkv, axis=1)
    if defer_tail:
        route((s > 0).astype(F32))
    q = (qkv[:, 0:D_ATTN] * (HEAD_DIM ** -0.5)).astype(BF16)
    k = qkv[:, D_ATTN:D_ATTN + D_KV]
    v = qkv[:, D_ATTN + D_KV:D_ATTN + 2 * D_KV]
    for j in range(nseq):
        kext[j * kpitch + WINDOW:(j + 1) * kpitch, :] = k[j * tl:(j + 1) * tl]
        vext[j * kpitch + WINDOW:(j + 1) * kpitch, :] = v[j * tl:(j + 1) * tl]

    nchunk = rows // CHUNK
    assert nseq == 1 or tl == CHUNK
    key_stride = CHUNK if nseq == 1 else kpitch
    head0 = lax.broadcasted_iota(jnp.int32, (1, D_KV), 1) < HEAD_DIM

    def windows(ext_ref):
        ext = ext_ref[...].astype(BF16)
        heads = (jnp.where(head0, ext, jnp.zeros_like(ext)), jnp.where(head0, jnp.zeros_like(ext), ext))
        return jnp.stack([jnp.concatenate([h[c * key_stride:c * key_stride + KEYS] for h in heads], axis=0)
                          for c in range(nchunk)])

    q3 = jnp.stack([jnp.concatenate([q[c * CHUNK:(c + 1) * CHUNK, g * D_KV:(g + 1) * D_KV] for g in range(GROUP)], axis=0)
                    for c in range(nchunk)])
    logits = lax.dot_general(q3, windows(kext), (((2,), (2,)), ((0,), (0,))), preferred_element_type=F32)
    s_ref[...] = logits.reshape(nchunk * QROWS, 2 * KEYS)

    assert 2 * KEYS == 3 * LANES
    mid0 = lax.broadcasted_iota(jnp.int32, (1, LANES), 1) < KEYS - LANES
    colk = lax.broadcasted_iota(jnp.int32, (1, 2 * KEYS), 1)
    key = jnp.where(colk < KEYS, colk, colk - KEYS)

    def row_block(i):
        r = i * SOFTMAX_ROWS
        return pl.ds(r, SOFTMAX_ROWS), pl.ds(r % QROWS, SOFTMAX_ROWS), r // QROWS

    def tiles(lg):
        return lg[:, 0:LANES], lg[:, LANES:2 * LANES], lg[:, 2 * LANES:3 * LANES]

    def max_rows(i, masked):
        rows_i, qrows_i, chunk_i = row_block(i)
        lg = s_ref[rows_i, :] + abias_ref[qrows_i, :]
        if masked:
            lg = lg + jnp.where(s * tl + chunk_i * CHUNK - WINDOW + key < 0, NEG_INF, 0.0).astype(F32)
        s_ref[rows_i, :] = lg
        t0, t1, t2 = tiles(lg)
        m0_ref[rows_i, :] = jnp.maximum(jnp.max(jnp.maximum(t0, jnp.where(mid0, t1, NEG_INF)), -1, keepdims=True),
                                        sink_ref[0, qrows_i, :])
        m1_ref[rows_i, :] = jnp.maximum(jnp.max(jnp.maximum(t2, jnp.where(mid0, NEG_INF, t1)), -1, keepdims=True),
                                        sink_ref[1, qrows_i, :])

    def exp_rows(i):
        rows_i, qrows_i, _ = row_block(i)
        t0, t1, t2 = tiles(s_ref[rows_i, :])
        m0, m1 = m0_ref[rows_i, :], m1_ref[rows_i, :]
        e = jnp.concatenate([jnp.exp(t0 - m0), jnp.exp(t1 - jnp.where(mid0, m0, m1)), jnp.exp(t2 - m1)], axis=-1)
        e_ref[rows_i, :] = e.astype(BF16)
        sinkden_ref[rows_i, :] = jnp.where(head0, jnp.exp(sink_ref[0, qrows_i, :] - m0),
                                           jnp.exp(sink_ref[1, qrows_i, :] - m1))

    n_it = nchunk * QROWS // SOFTMAX_ROWS
    n_masked = min(WINDOW // CHUNK, nchunk) * QROWS // SOFTMAX_ROWS if mask_history else 0
    proj_chunks = [(c, min(c + PROJ_CHUNK, _OFF_Q)) for c in range(0, _OFF_Q, PROJ_CHUNK)]
    piece = -(-2 * n_it // len(proj_chunks))
    steps = [functools.partial(max_rows, i, i < n_masked) for i in range(n_it)]
    steps += [functools.partial(exp_rows, i) for i in range(n_it)]
    for p, (c0, c1) in enumerate(proj_chunks):
        for step_fn in steps[p * piece:(p + 1) * piece]:
            step_fn()
        proj_ref[:, c0:c1] = _dot(xb, w_in_ref[:, c0:c1])
    for step_fn in steps[len(proj_chunks) * piece:]:
        step_fn()
    vsel = jnp.concatenate([windows(vext), jnp.broadcast_to(keysel_ref[...], (nchunk, 2 * KEYS, D_KV))], axis=-1)
    n_pieces = min(MIX_PIECES, nchunk)
    cpp = nchunk // n_pieces
    conv_parts, attn_parts = [], []
    for p in range(n_pieces):
        cs = slice(p * cpp, (p + 1) * cpp)
        o2 = lax.dot_general(e_ref[p * cpp * QROWS:(p + 1) * cpp * QROWS, :].reshape(cpp, QROWS, 2 * KEYS), vsel[cs],
                             (((2,), (1,)), ((0,), (0,))), preferred_element_type=F32)
        sinkden = sinkden_ref[p * cpp * QROWS:(p + 1) * cpp * QROWS, :].reshape(cpp, QROWS, D_KV)
        o = o2[..., 0:D_KV] * (1.0 / (o2[..., D_KV:2 * D_KV] + sinkden))
        attn_parts += [jnp.concatenate([o[c, g * CHUNK:(g + 1) * CHUNK, :] for g in range(GROUP)], axis=1)
                       for c in range(cpp)]
        conv_parts.append(conv_rows(p * cpp * CHUNK, (p + 1) * cpp * CHUNK))
    y_attn = jnp.concatenate(attn_parts, axis=0)
    n_conv = jnp.concatenate(conv_parts, axis=0)

    for j in range(nseq):
        sk_ref[j] = kext[j * kpitch + tl:(j + 1) * kpitch, :]
        sv_ref[j] = vext[j * kpitch + tl:(j + 1) * kpitch, :]
    if nseq == 1:
        kext[0:WINDOW, :] = kext[tl:tl + WINDOW, :]
        vext[0:WINDOW, :] = vext[tl:tl + WINDOW, :]

    n_attn = _rms_norm(y_attn, gattn_ref[...]).astype(BF16)
    mixed = _dot(n_attn, w_out_ref[D_CONV:D_CONV + D_ATTN, :]) + _dot(n_conv, w_out_ref[0:D_CONV, :])
    if defer_tail:
        xprev_s[...] = x
        mixed_s[...] = mixed
    else:
        norm_rows(0, rows, x, mixed)
        route(1.0)


def _mixer(x, cconv, ck, cv, counts_in, prm, consts, *, nseq, tl, mask_history, alpha):
    nb_total, t_total = x.shape[0], x.shape[1]
    nb, ns = nb_total // nseq, t_total // tl
    defer_tail = ns > 1
    steps = ns + 1 if defer_tail else ns

    def in_blk(s):
        return jnp.minimum(s, ns - 1)

    def out_blk(b, s):
        return b * ns + (jnp.maximum(s - 1, 0) if defer_tail else s)

    rows = nseq * tl
    n_tok = nb_total * t_total
    w_in, convw, convb, sinks, gconv, gattn, w_out, ln1g, ln1b, wr, br = prm
    abias, tri, keysel = consts

    def full(a):
        return pl.BlockSpec(a.shape, lambda b, s, _n=a.ndim: (0,) * _n)

    def seq_state(width, nrows):
        return pl.BlockSpec((nseq, nrows, width), lambda b, s: (b, 0, 0))

    in_specs = [
        pl.BlockSpec((nseq, tl, D_MODEL), lambda b, s: (b, in_blk(s), 0)),
        seq_state(D_CONV, CONV_W - 1), seq_state(D_KV, WINDOW), seq_state(D_KV, WINDOW),
        full(w_in), full(convw), full(convb), full(abias), full(sinks), full(gconv), full(gattn),
        full(w_out), full(ln1g), full(ln1b), full(wr), full(br), full(tri), full(keysel), full(counts_in),
    ]
    out_shape = [
        jax.ShapeDtypeStruct((n_tok * ROW_TILES, LANES), F32),
        jax.ShapeDtypeStruct((SUBLANES, n_tok), F32),
        jax.ShapeDtypeStruct((n_tok, LANES), F32),
        jax.ShapeDtypeStruct((N_EXPERTS, LANES), F32),
        jax.ShapeDtypeStruct((nb_total, CONV_W - 1, D_CONV), F32),
        jax.ShapeDtypeStruct((nb_total, WINDOW, D_KV), F32),
        jax.ShapeDtypeStruct((nb_total, WINDOW, D_KV), F32),
    ]
    out_specs = [
        pl.BlockSpec((rows * ROW_TILES, LANES), lambda b, s: (out_blk(b, s), 0)),
        pl.BlockSpec((SUBLANES, rows), lambda b, s: (0, out_blk(b, s))),
        pl.BlockSpec((rows, LANES), lambda b, s: (out_blk(b, s), 0)),
        pl.BlockSpec((N_EXPERTS, LANES), lambda b, s: (0, 0)),
        seq_state(D_CONV, CONV_W - 1), seq_state(D_KV, WINDOW), seq_state(D_KV, WINDOW),
    ]
    scratch = [
        pltpu.VMEM((nseq * (tl + SUBLANES), D_CONV), F32),
        pltpu.VMEM((nseq * (WINDOW + tl), D_KV), F32),
        pltpu.VMEM((nseq * (WINDOW + tl), D_KV), F32),
        pltpu.VMEM((N_EXPERTS, 1), F32),
        pltpu.VMEM((rows // CHUNK * QROWS, 2 * KEYS), F32),
        pltpu.VMEM((rows // CHUNK * QROWS, 2 * KEYS), BF16),
        pltpu.VMEM((rows // CHUNK * QROWS, D_KV), F32),
        pltpu.VMEM((rows // CHUNK * QROWS, LANES), F32),
        pltpu.VMEM((rows // CHUNK * QROWS, LANES), F32),
        pltpu.VMEM((rows, _OFF_Q), F32),
        pltpu.VMEM((rows, D_MODEL), BF16),
        pltpu.VMEM((rows, D_MODEL) if defer_tail else (SUBLANES, LANES), F32),
        pltpu.VMEM((rows, D_MODEL) if defer_tail else (SUBLANES, LANES), F32),
    ]
    kern = functools.partial(_mixer_kernel, nseq=nseq, tl=tl, alpha=alpha, mask_history=mask_history,
                             defer_tail=defer_tail)
    return pl.pallas_call(
        kern, grid=(nb, steps), in_specs=in_specs, out_specs=out_specs, out_shape=out_shape,
        scratch_shapes=scratch, name="mixer",
        compiler_params=pltpu.CompilerParams(dimension_semantics=("arbitrary", "arbitrary"),
                                             vmem_limit_bytes=VMEM_LIMIT),
    )(x, cconv, ck, cv, w_in, convw, convb, abias, sinks, gconv, gattn, w_out, ln1g, ln1b, wr, br, tri, keysel, counts_in)


_PAD_PIECES = tuple(2 ** k for k in range(EXPERT_BLOCK.bit_length() - 2, -1, -1))


def _tile_rows(row, n=1):
    if isinstance(row, int):
        return pl.ds(row * ROW_TILES, n * ROW_TILES)
    return pl.ds(pl.multiple_of(row * ROW_TILES, ROW_TILES), n * ROW_TILES)


def _row_copy(src, src_row, dst, dst_row, sem):
    return pltpu.make_async_copy(src.at[_tile_rows(src_row), :], dst.at[_tile_rows(dst_row), :], sem)


def _dispatch_kernel(zstart_ref, zcount_ref, tail_ref, pos1_ref, pos2_ref, xa_ref, xb_ref, xs_hbm, zeros_v, sem, zsem,
                     *, tb, nba):
    step = pl.program_id(0)

    def zero_copy(start, piece):
        return pltpu.make_async_copy(zeros_v.at[_tile_rows(0, piece), :], xs_hbm.at[_tile_rows(start, piece), :], zsem)

    def pad_copy(e, piece, taken):
        return zero_copy(zstart_ref[e] + taken, piece)

    def for_each_pad_piece(fn):
        def per_tail_block(j, carry):
            for part in range(EXPERT_BLOCK // _PAD_PIECES[0]):
                fn(zero_copy(tail_ref[0] + j * EXPERT_BLOCK + part * _PAD_PIECES[0], _PAD_PIECES[0]))
            return carry

        lax.fori_loop(0, tail_ref[1], per_tail_block, 0)

        def per_expert(e, carry):
            count = zcount_ref[e]
            taken = 0
            for piece in _PAD_PIECES:
                present = (count & piece) != 0

                @pl.when(present)
                def _(piece=piece, taken=taken):
                    fn(pad_copy(e, piece, taken))

                taken = taken + jnp.where(present, piece, 0)
            return carry

        lax.fori_loop(0, N_EXPERTS, per_expert, 0)

    @pl.when(step == 0)
    def _():
        zeros_v[...] = jnp.zeros_like(zeros_v)
        for_each_pad_piece(lambda c: c.start())

    def issue_rows(x_ref):
        def issue(t, carry):
            _row_copy(x_ref, t, xs_hbm, pos1_ref[t], sem).start(priority=0)
            _row_copy(x_ref, t, xs_hbm, pos2_ref[t], sem).start(priority=1)
            return carry

        lax.fori_loop(0, tb, issue, 0, unroll=8)

    @pl.when(step < nba)
    def _():
        issue_rows(xa_ref)

    @pl.when(step >= nba)
    def _():
        issue_rows(xb_ref)

    pltpu.make_async_copy(xs_hbm.at[_tile_rows(0, 2 * tb), :], xs_hbm.at[_tile_rows(0, 2 * tb), :], sem).wait()

    @pl.when(step == 0)
    def _():
        for_each_pad_piece(lambda c: c.wait())


def _dispatch(xa_rows, xb_rows, pos1, pos2, zstart, zcount, tail, n_slots, *, tb):
    nba, nbb = xa_rows.shape[0] // ROW_TILES // tb, xb_rows.shape[0] // ROW_TILES // tb
    kern = functools.partial(_dispatch_kernel, tb=tb, nba=nba)
    grid_spec = pltpu.PrefetchScalarGridSpec(
        num_scalar_prefetch=3, grid=(nba + nbb,),
        in_specs=[pl.BlockSpec((tb,), lambda i, *_: (i,), memory_space=pltpu.SMEM),
                  pl.BlockSpec((tb,), lambda i, *_: (i,), memory_space=pltpu.SMEM),
                  pl.BlockSpec((tb * ROW_TILES, LANES), lambda i, *_: (jnp.minimum(i, nba - 1), 0)),
                  pl.BlockSpec((tb * ROW_TILES, LANES), lambda i, *_: (jnp.maximum(i - nba, 0), 0))],
        out_specs=pl.BlockSpec(memory_space=pl.ANY),
        scratch_shapes=[pltpu.VMEM((_PAD_PIECES[0] * ROW_TILES, LANES), F32),
                        pltpu.SemaphoreType.DMA(()), pltpu.SemaphoreType.DMA(())],
    )
    return pl.pallas_call(
        kern, grid_spec=grid_spec, out_shape=jax.ShapeDtypeStruct((n_slots * ROW_TILES, LANES), F32),
        name="dispatch",
        compiler_params=pltpu.CompilerParams(dimension_semantics=("arbitrary",)),
    )(zstart, zcount, tail, pos1, pos2, xa_rows, xb_rows)


def _expert_kernel(be_ref, src_ref, used_ref, xs_ref, wg_ref, wu_ref, wd_ref, yb_ref, wg_b, wu_b, wd_b):
    i = pl.program_id(0)
    e = be_ref[i]
    prev = be_ref[jnp.maximum(i - 1, 0)]

    @pl.when(jnp.logical_or(i == 0, e != prev))
    def _():
        wg_b[...] = wg_ref[...].astype(BF16)
        wu_b[...] = wu_ref[...].astype(BF16)
        wd_b[...] = wd_ref[...].astype(BF16)

    @pl.when(used_ref[i] != 0)
    def _():
        x = _load_rows(xs_ref, EXPERT_BLOCK).astype(BF16)
        g = _dot(x, wg_b[...])
        u = _dot(x, wu_b[...])
        hid = (g / (1.0 + jnp.exp(-g))) * u
        _store_rows(yb_ref, _dot(hid.astype(BF16), wd_b[...]), EXPERT_BLOCK)

    @pl.when(used_ref[i] == 0)
    def _():
        yb_ref[...] = jnp.zeros_like(yb_ref)


def _experts(xs, blk_e, blk_src, blk_used, w_gate, w_up, w_down, layer):
    n_slots = xs.shape[0] // ROW_TILES
    nblk = n_slots // EXPERT_BLOCK

    def wspec(shape):
        return pl.BlockSpec((None, None) + shape, lambda i, be, src, used: (layer, be[i], 0, 0))

    grid_spec = pltpu.PrefetchScalarGridSpec(
        num_scalar_prefetch=3, grid=(nblk,),
        in_specs=[pl.BlockSpec((EXPERT_BLOCK * ROW_TILES, LANES), lambda i, be, src, used: (src[i], 0)),
                  wspec((D_MODEL, D_EXP)), wspec((D_MODEL, D_EXP)), wspec((D_EXP, D_MODEL))],
        out_specs=pl.BlockSpec((EXPERT_BLOCK * ROW_TILES, LANES), lambda i, be, src, used: (i, 0)),
        scratch_shapes=[pltpu.VMEM((D_MODEL, D_EXP), BF16), pltpu.VMEM((D_MODEL, D_EXP), BF16),
                        pltpu.VMEM((D_EXP, D_MODEL), BF16)],
    )
    return pl.pallas_call(
        _expert_kernel, grid_spec=grid_spec, out_shape=jax.ShapeDtypeStruct((n_slots * ROW_TILES, LANES), F32),
        name="experts",
        compiler_params=pltpu.CompilerParams(dimension_semantics=("arbitrary",), vmem_limit_bytes=VMEM_LIMIT),
    )(blk_e, blk_src, blk_used, xs, w_gate, w_up, w_down)


def _combine_kernel(pos1_ref, pos2_ref, next1_ref, next2_ref, xa_ref, xb_ref, ga_ref, gb_ref, yb_hbm, g_ref, b_ref,
                    outa_ref, outb_ref, ybuf, sems, *, tb, alpha, nba, nsteps):
    step = pl.program_id(0)
    slot = step % 2
    other = 1 - slot

    def issue(p1_ref, p2_ref, to_slot, t):
        _row_copy(yb_hbm, p1_ref[t], ybuf.at[to_slot].at[0], t, sems.at[to_slot]).start(priority=0)
        _row_copy(yb_hbm, p2_ref[t], ybuf.at[to_slot].at[1], t, sems.at[to_slot]).start(priority=1)

    def wait_slot(which):
        for k in range(TOP_K):
            pltpu.make_async_copy(yb_hbm.at[_tile_rows(0, tb), :], ybuf.at[which].at[k], sems.at[which]).wait()

    @pl.when(step == 0)
    def _():
        def body(t, carry):
            issue(pos1_ref, pos2_ref, 0, t)
            return carry
        lax.fori_loop(0, tb, body, 0, unroll=8)

    wait_slot(slot)

    def finish(x1_ref, gcol_ref, out_ref):
        n = tb // COMBINE_PIECES
        for p in range(COMBINE_PIECES):
            for t in range(p * n, (p + 1) * n):
                issue(next1_ref, next2_ref, other, t)
            gates = gcol_ref[p * n:(p + 1) * n, :]
            y = (gates[:, _R_G1:_R_G1 + 1] * _load_rows(ybuf.at[slot].at[0], n, p * n)
                 + gates[:, _R_G2:_R_G2 + 1] * _load_rows(ybuf.at[slot].at[1], n, p * n))
            out_ref[p * n:(p + 1) * n, :] = _layer_norm(alpha * _load_rows(x1_ref, n, p * n) + y, g_ref[...], b_ref[...])

    @pl.when(step < nba)
    def _():
        finish(xa_ref, ga_ref, outa_ref)

    @pl.when(step >= nba)
    def _():
        finish(xb_ref, gb_ref, outb_ref)

    @pl.when(step == nsteps - 1)
    def _():
        wait_slot(other)


def _combine(xa_rows, xb_rows, gcol_a, gcol_b, yb, pos1, pos2, ln2g, ln2b, *, tb, alpha):
    nba, nbb = xa_rows.shape[0] // ROW_TILES // tb, xb_rows.shape[0] // ROW_TILES // tb
    nsteps = nba + nbb
    kern = functools.partial(_combine_kernel, tb=tb, alpha=alpha, nba=nba, nsteps=nsteps)

    def first(i):
        return jnp.minimum(i, nba - 1)

    def second(i):
        return jnp.maximum(i - nba, 0)

    def nxt(i):
        return jnp.minimum(i + 1, nsteps - 1)

    smem = functools.partial(pl.BlockSpec, (tb,), memory_space=pltpu.SMEM)
    return pl.pallas_call(
        kern, grid=(nsteps,),
        in_specs=[smem(lambda i: (i,)), smem(lambda i: (i,)), smem(lambda i: (nxt(i),)), smem(lambda i: (nxt(i),)),
                  pl.BlockSpec((tb * ROW_TILES, LANES), lambda i: (first(i), 0)),
                  pl.BlockSpec((tb * ROW_TILES, LANES), lambda i: (second(i), 0)),
                  pl.BlockSpec((tb, LANES), lambda i: (first(i), 0)),
                  pl.BlockSpec((tb, LANES), lambda i: (second(i), 0)),
                  pl.BlockSpec(memory_space=pl.ANY),
                  pl.BlockSpec((1, D_MODEL), lambda i: (0, 0)),
                  pl.BlockSpec((1, D_MODEL), lambda i: (0, 0))],
        out_specs=[pl.BlockSpec((tb, D_MODEL), lambda i: (first(i), 0)),
                   pl.BlockSpec((tb, D_MODEL), lambda i: (second(i), 0))],
        out_shape=[jax.ShapeDtypeStruct((nba * tb, D_MODEL), F32), jax.ShapeDtypeStruct((nbb * tb, D_MODEL), F32)],
        scratch_shapes=[pltpu.VMEM((2, TOP_K, tb * ROW_TILES, LANES), F32), pltpu.SemaphoreType.DMA((2,))],
        name="combine",
        compiler_params=pltpu.CompilerParams(dimension_semantics=("arbitrary",), vmem_limit_bytes=VMEM_LIMIT),
    )(pos1, pos2, pos1, pos2, xa_rows, xb_rows, gcol_a, gcol_b, yb, ln2g, ln2b)


def _alibi_bias():
    slopes = np.asarray([2.0 ** (-8.0 * (h + 1) / N_HEADS) for h in range(N_HEADS)], np.float32)
    qi = np.arange(CHUNK, dtype=np.int32)[:, None]
    sj = np.arange(KEYS, dtype=np.int32)[None, :]
    dist = np.abs(qi + WINDOW - sj).astype(np.float32)
    bias = -slopes.reshape(N_KV_HEADS, GROUP, 1, 1) * dist
    bias = np.transpose(bias, (1, 2, 0, 3)).reshape(GROUP * CHUNK, N_KV_HEADS * KEYS)
    return jnp.asarray(bias, F32)


def _key_selector():
    row_head = np.arange(N_KV_HEADS * KEYS)[:, None] // KEYS
    lane_head = np.arange(D_KV)[None, :] // HEAD_DIM
    return jnp.asarray((row_head == lane_head).astype(np.float32), BF16)


def _strict_upper():
    r = np.arange(RANK_CHUNK)
    return jnp.asarray((r[:, None] < r[None, :]).astype(np.float32), BF16)


def _slot_plan(route, counts, n_tok):
    nblk = (n_tok * TOP_K) // EXPERT_BLOCK + N_EXPERTS
    experts = route[_R_E1:_R_E2 + 1].astype(jnp.int32)
    ranks = route[_R_RANK1:_R_RANK2 + 1].astype(jnp.int32)
    cnt = counts[:, 0].astype(jnp.int32)
    padded = (cnt + EXPERT_BLOCK - 1) // EXPERT_BLOCK * EXPERT_BLOCK
    pad_end = jnp.cumsum(padded)
    pad_start = pad_end - padded
    ids = jnp.arange(N_EXPERTS, dtype=jnp.int32)
    pos = jnp.sum(jnp.where(experts[..., None] == ids, pad_start, 0), axis=-1) + ranks
    blk_first = jnp.arange(nblk, dtype=jnp.int32) * EXPERT_BLOCK
    blk_e = jnp.minimum(jnp.sum((pad_end[None, :] <= blk_first[:, None]).astype(jnp.int32), axis=-1), N_EXPERTS - 1)
    blk_used = (blk_first < pad_end[-1]).astype(jnp.int32)
    blk_src = jnp.minimum(jnp.arange(nblk, dtype=jnp.int32), pad_end[-1] // EXPERT_BLOCK - 1)
    tail = jnp.stack([pad_end[-1], nblk - pad_end[-1] // EXPERT_BLOCK]).astype(jnp.int32)
    return pos[0], pos[1], pad_start + cnt, padded - cnt, tail, blk_e, blk_src, blk_used, nblk * EXPERT_BLOCK


def _layer(xp, xs, cache, mix_prm, consts, moe_prm, layer, *, tl_prompt, nseq_sample, alpha):
    (bp, tp, _), (bs, ts, _) = xp.shape, xs.shape
    np_tok, ns_tok = bp * tp, bs * ts
    tb = ROWS
    assert np_tok % tb == 0 and ns_tok % tb == 0
    zeros_conv = jnp.zeros((bp, CONV_W - 1, D_CONV), F32)
    zeros_kv = jnp.zeros((bp, WINDOW, D_KV), F32)
    zero_counts = jnp.zeros((N_EXPERTS, LANES), F32)
    x1p, route_p, gcol_p, counts_p, *state_p = _mixer(
        xp, zeros_conv, zeros_kv, zeros_kv, zero_counts, mix_prm, consts,
        nseq=1, tl=tl_prompt, mask_history=True, alpha=alpha)
    x1s, route_s, gcol_s, counts, *state_s = _mixer(
        xs, *cache, counts_p, mix_prm, consts, nseq=nseq_sample, tl=ts, mask_history=False, alpha=alpha)
    route = jnp.concatenate([route_p, route_s], axis=1)
    pos1, pos2, zstart, zcount, tail, blk_e, blk_src, blk_used, n_slots = _slot_plan(route, counts, np_tok + ns_tok)
    tb_dispatch = DISPATCH_ROWS if np_tok % DISPATCH_ROWS == 0 and ns_tok % DISPATCH_ROWS == 0 else tb
    sorted_rows = _dispatch(x1p, x1s, pos1, pos2, zstart, zcount, tail, n_slots, tb=tb_dispatch)
    w_gate, w_up, w_down, ln2g, ln2b = moe_prm
    yb = _experts(sorted_rows, blk_e, blk_src, blk_used, w_gate, w_up, w_down, layer)
    out_p, out_s = _combine(x1p, x1s, gcol_p, gcol_s, yb, pos1, pos2, ln2g, ln2b, tb=tb, alpha=alpha)
    return out_p.reshape(bp, tp, D_MODEL), out_s.reshape(bs, ts, D_MODEL), state_p, state_s


def kernel(x_prompt, x_sample, cache_conv, cache_k, cache_v, w_in, conv_w, conv_b, attn_sinks, g_conv, g_attn,
           w_out, ln1_g, ln1_b, router_group_w, router_group_b, router_expert_w, router_expert_b,
           expert_w_gate, expert_w_up, expert_w_down, ln2_g, ln2_b):
    depth = w_in.shape[0]
    alpha = (2 * depth) ** 0.25
    batch, seq = x_prompt.shape[0], x_prompt.shape[1]
    dec_batch, dec_seq = x_sample.shape[0], x_sample.shape[1]
    assert dec_seq == CHUNK and seq % CHUNK == 0
    consts = (_alibi_bias(), _strict_upper(), _key_selector())
    tl_prompt = min(MIXER_ROWS, seq)
    nseq_sample = min(MIXER_ROWS // dec_seq, dec_batch)
    assert seq % tl_prompt == 0 and dec_batch % nseq_sample == 0

    def regroup_heads(a, axis):
        shape = a.shape
        a = a.reshape(shape[:axis] + (N_KV_HEADS, GROUP, HEAD_DIM) + shape[axis + 1:])
        return jnp.swapaxes(a, axis, axis + 1).reshape(shape)

    w_in_all = jnp.concatenate([w_in[:, :, :_OFF_Q], regroup_heads(w_in[:, :, _OFF_Q:_OFF_K], 2), w_in[:, :, _OFF_K:]],
                               axis=2).astype(BF16)
    w_out_all = jnp.concatenate([w_out[:, :D_CONV], regroup_heads(w_out[:, D_CONV:], 1)], axis=1).astype(BF16)
    g_attn_all = regroup_heads(g_attn, 1)
    router_pad = LANES - N_EXPERTS - N_GROUPS
    wr_all = jnp.concatenate([router_expert_w, router_group_w, jnp.zeros((depth, D_MODEL, router_pad), F32)],
                             axis=2).astype(BF16)
    br_all = jnp.concatenate([router_expert_b, router_group_b, jnp.zeros((depth, router_pad), F32)], axis=1)
    sink_all = jnp.broadcast_to(jnp.repeat(attn_sinks.astype(F32), CHUNK, axis=1).reshape(depth, N_KV_HEADS, QROWS, 1),
                                (depth, N_KV_HEADS, QROWS, LANES))

    xp, xs = x_prompt, x_sample
    states = [[] for _ in range(6)]
    for l in range(depth):
        mix_prm = (w_in_all[l], conv_w[l], conv_b[l].reshape(1, D_CONV), sink_all[l],
                   g_conv[l].reshape(1, D_CONV), g_attn_all[l].reshape(1, D_ATTN), w_out_all[l],
                   ln1_g[l].reshape(1, D_MODEL), ln1_b[l].reshape(1, D_MODEL), wr_all[l], br_all[l].reshape(1, LANES))
        moe_prm = (expert_w_gate, expert_w_up, expert_w_down,
                   ln2_g[l].reshape(1, D_MODEL), ln2_b[l].reshape(1, D_MODEL))
        cache = (cache_conv[l], cache_k[l].reshape(dec_batch, WINDOW, D_KV), cache_v[l].reshape(dec_batch, WINDOW, D_KV))
        xp, xs, state_p, state_s = _layer(xp, xs, cache, mix_prm, consts, moe_prm, l,
                                          tl_prompt=tl_prompt, nseq_sample=nseq_sample, alpha=alpha)
        for lst, val in zip(states, state_p + state_s):
            lst.append(val)

    def kv(lst, nb):
        return jnp.stack(lst).reshape(depth, nb, WINDOW, N_KV_HEADS, HEAD_DIM)

    return (xp, xs, jnp.stack(states[0]), kv(states[1], batch), kv(states[2], batch),
            jnp.stack(states[3]), kv(states[4], dec_batch), kv(states[5], dec_batch))
```

```python
import functools

import numpy as np
import jax
import jax.numpy as jnp
from jax import lax
from jax.experimental import pallas as pl
from jax.experimental.pallas import tpu as pltpu

D_MODEL = 1024
D_CONV = 512
CONV_W = 3
N_HEADS = 8
N_KV_HEADS = 2
GROUP = N_HEADS // N_KV_HEADS
HEAD_DIM = 64
D_ATTN = N_HEADS * HEAD_DIM
D_KV = N_KV_HEADS * HEAD_DIM
WINDOW = 128
CHUNK = 64
KEYS = WINDOW + CHUNK
QROWS = GROUP * CHUNK
N_GROUPS = 4
EXP_PER_GROUP = 8
N_EXPERTS = N_GROUPS * EXP_PER_GROUP
TOP_K = 2
D_EXP = 512
LN_EPS = 1e-5
NEG_INF = -1e30

SUBLANES = 8
LANES = 128
ROW_TILES = D_MODEL // LANES
assert ROW_TILES == SUBLANES
ROWS = 512
MIXER_ROWS = 512
DISPATCH_ROWS = 1024
COMBINE_PIECES = 16
DISPATCH_PIECES = 8
EXPERT_BLOCK = 512
RANK_CHUNK = 256
SOFTMAX_ROWS = 64
PROJ_CHUNK = 2 * LANES
MIX_PIECES = 4
VMEM_LIMIT = 56 * 1024 * 1024

F32 = jnp.float32
BF16 = jnp.bfloat16

_OFF_B, _OFF_C, _OFF_H = 0, D_CONV, 2 * D_CONV
_OFF_Q = 3 * D_CONV
_OFF_K = _OFF_Q + D_ATTN
_OFF_V = _OFF_K + D_KV
D_IN = _OFF_V + D_KV

_R_E1, _R_E2, _R_RANK1, _R_RANK2, _R_G1, _R_G2 = 0, 1, 2, 3, 4, 5


def _load_rows(ref, n, first=0):
    return jnp.concatenate([ref[pl.ds(first * ROW_TILES + j, n, stride=ROW_TILES), :] for j in range(ROW_TILES)],
                           axis=1)


def _store_rows(ref, val, n):
    for j in range(ROW_TILES):
        ref[pl.ds(j, n, stride=ROW_TILES), :] = val[:, j * LANES:(j + 1) * LANES]


def _dot(a, b):
    return jnp.dot(a, b, preferred_element_type=F32)


def _rms_norm(x, g):
    return x * lax.rsqrt(jnp.mean(jnp.square(x), -1, keepdims=True) + LN_EPS) * g


def _layer_norm(x, g, b):
    mu = jnp.mean(x, -1, keepdims=True)
    xc = x - mu
    var = jnp.mean(jnp.square(xc), -1, keepdims=True)
    return xc * lax.rsqrt(var + LN_EPS) * g + b


def _mixer_kernel(x_ref, cconv_ref, ck_ref, cv_ref, w_in_ref, convw_ref, convb_ref, abias_ref,
                  sink_ref, gconv_ref, gattn_ref, w_out_ref, ln1g_ref, ln1b_ref, wr_ref, br_ref,
                  tri_ref, keysel_ref, counts_in_ref,
                  x1_ref, route_ref, gcol_ref, counts_ref, sconv_ref, sk_ref, sv_ref,
                  uext, kext, vext, cnt_s, s_ref, e_ref, sinkden_ref, m0_ref, m1_ref, proj_ref, x1b_s, xprev_s,
                  mixed_s, *, nseq, tl, alpha, mask_history, defer_tail):
    b = pl.program_id(0)
    s = pl.program_id(1)
    rows = nseq * tl
    upitch = tl + SUBLANES
    kpitch = WINDOW + tl

    @pl.when(jnp.logical_and(b == 0, s == 0))
    def _():
        cnt_s[...] = counts_in_ref[:, 0:1]
        if defer_tail:
            xprev_s[...] = jnp.zeros_like(xprev_s)
            mixed_s[...] = jnp.zeros_like(mixed_s)

    @pl.when(s == 0)
    def _():
        for j in range(nseq):
            uext[j * upitch + SUBLANES - 2:j * upitch + SUBLANES, :] = cconv_ref[j]
            kext[j * kpitch:j * kpitch + WINDOW, :] = ck_ref[j]
            vext[j * kpitch:j * kpitch + WINDOW, :] = cv_ref[j]

    def norm_rows(r0, r1, x_rows, mixed_rows):
        x1 = _layer_norm(alpha * x_rows + mixed_rows, ln1g_ref[...], ln1b_ref[...])
        x1_ref[r0:r1, :] = x1
        x1b_s[r0:r1, :] = x1.astype(BF16)

    def route(valid):
        logits_t = (_dot(x1b_s[...], wr_ref[...]) + br_ref[...]).T
        sub = lax.broadcasted_iota(jnp.int32, (SUBLANES, rows), 0)
        gl = jnp.where(sub < N_GROUPS, logits_t[N_EXPERTS:N_EXPERTS + SUBLANES, :], -jnp.inf)
        gmax = jnp.max(gl, axis=0, keepdims=True)
        grp = jnp.min(jnp.where(gl == gmax, sub, SUBLANES), axis=0, keepdims=True)
        p_grp = 1.0 / jnp.sum(jnp.exp(gl - gmax), axis=0, keepdims=True)
        el = logits_t[(N_GROUPS - 1) * EXP_PER_GROUP:N_GROUPS * EXP_PER_GROUP, :]
        for g in range(N_GROUPS - 2, -1, -1):
            el = jnp.where(grp == g, logits_t[g * EXP_PER_GROUP:(g + 1) * EXP_PER_GROUP, :], el)
        v1 = jnp.max(el, axis=0, keepdims=True)
        i1 = jnp.min(jnp.where(el == v1, sub, SUBLANES), axis=0, keepdims=True)
        el2 = jnp.where(sub == i1, -jnp.inf, el)
        v2 = jnp.max(el2, axis=0, keepdims=True)
        i2 = jnp.min(jnp.where(el2 == v2, sub, SUBLANES), axis=0, keepdims=True)
        e2 = jnp.exp(v2 - v1)
        gate1 = p_grp * (1.0 / (1.0 + e2))
        gate2 = p_grp * (e2 / (1.0 + e2))

        chosen = jnp.logical_or(sub == i1, sub == i2)
        onehot = jnp.concatenate(
            [jnp.where(jnp.logical_and(grp == g, chosen), 1.0, 0.0) for g in range(N_GROUPS)], axis=0)
        running = cnt_s[...]
        ranks = []
        for c in range(rows // RANK_CHUNK):
            oh = onehot[:, c * RANK_CHUNK:(c + 1) * RANK_CHUNK]
            ranks.append(_dot(oh.astype(BF16), tri_ref[...]) + running)
            running = running + jnp.sum(oh, axis=1, keepdims=True) * valid
        rank = jnp.concatenate(ranks, axis=1)
        cnt_s[...] = running
        counts_ref[...] = jnp.broadcast_to(running, (N_EXPERTS, LANES))

        ex1 = grp * EXP_PER_GROUP + i1
        ex2 = grp * EXP_PER_GROUP + i2
        erow = lax.broadcasted_iota(jnp.int32, (N_EXPERTS, rows), 0)
        rank1 = jnp.sum(jnp.where(erow == ex1, rank, 0.0), axis=0, keepdims=True)
        rank2 = jnp.sum(jnp.where(erow == ex2, rank, 0.0), axis=0, keepdims=True)
        fields = {_R_E1: ex1.astype(F32), _R_E2: ex2.astype(F32), _R_RANK1: rank1, _R_RANK2: rank2,
                  _R_G1: gate1, _R_G2: gate2}
        record = jnp.zeros((SUBLANES, rows), F32)
        for r, val in fields.items():
            record = jnp.where(sub == r, val, record)
        route_ref[...] = record
        padded = jnp.concatenate([record, jnp.zeros((LANES - SUBLANES, rows), F32)], axis=0)
        gcol_ref[...] = padded.T

    x = x_ref[...].reshape(rows, D_MODEL)
    xb = x.astype(BF16)

    def conv_rows(r0, r1):
        w0, w1, w2 = convw_ref[0:1, :], convw_ref[1:2, :], convw_ref[2:3, :]
        out = []
        for j in range(r0 // tl, -(-r1 // tl)):
            a, b_ = max(r0, j * tl) - j * tl, min(r1, (j + 1) * tl) - j * tl
            base = j * upitch + SUBLANES
            src = pl.ds(j * tl + a, b_ - a)
            uext[base + a:base + b_, :] = proj_ref[src, _OFF_C:_OFF_C + D_CONV] * proj_ref[src, _OFF_H:_OFF_H + D_CONV]
            yc = convb_ref[...] + uext[base + a - 2:base + b_ - 2, :] * w0
            yc = yc + uext[base + a - 1:base + b_ - 1, :] * w1
            yc = yc + uext[base + a:base + b_, :] * w2
            out.append(_rms_norm(proj_ref[src, _OFF_B:_OFF_B + D_CONV] * yc, gconv_ref[...]).astype(BF16))
            if b_ == tl:
                tail = uext[base + tl - 2:base + tl, :]
                sconv_ref[j] = tail
                uext[base - 2:base, :] = tail
        return out[0] if len(out) == 1 else jnp.concatenate(out, axis=0)

    qkv = []
    qkv_chunks = [(c, c + PROJ_CHUNK) for c in range(_OFF_Q, D_IN, PROJ_CHUNK)]
    tail_rows = -(-rows // len(qkv_chunks) // SUBLANES) * SUBLANES
    for p, (c0, c1) in enumerate(qkv_chunks):
        r0, r1 = min(p * tail_rows, rows), min((p + 1) * tail_rows, rows)
        if defer_tail and r1 > r0:
            norm_rows(r0, r1, xprev_s[r0:r1, :], mixed_s[r0:r1, :])
        qkv.append(_dot(xb, w_in_ref[:, c0:c1]))
    qkv = jnp.concatenate(qkv, axis=1)
    if defer_tail:
        route((s > 0).astype(F32))
    q = (qkv[:, 0:D_ATTN] * (HEAD_DIM ** -0.5)).astype(BF16)
    k = qkv[:, D_ATTN:D_ATTN + D_KV]
    v = qkv[:, D_ATTN + D_KV:D_ATTN + 2 * D_KV]
    for j in range(nseq):
        kext[j * kpitch + WINDOW:(j + 1) * kpitch, :] = k[j * tl:(j + 1) * tl]
        vext[j * kpitch + WINDOW:(j + 1) * kpitch, :] = v[j * tl:(j + 1) * tl]

    nchunk = rows // CHUNK
    assert nseq == 1 or tl == CHUNK
    key_stride = CHUNK if nseq == 1 else kpitch
    head0 = lax.broadcasted_iota(jnp.int32, (1, D_KV), 1) < HEAD_DIM

    def windows(ext_ref):
        ext = ext_ref[...].astype(BF16)
        heads = (jnp.where(head0, ext, jnp.zeros_like(ext)), jnp.where(head0, jnp.zeros_like(ext), ext))
        return jnp.stack([jnp.concatenate([h[c * key_stride:c * key_stride + KEYS] for h in heads], axis=0)
                          for c in range(nchunk)])

    q3 = jnp.stack([jnp.concatenate([q[c * CHUNK:(c + 1) * CHUNK, g * D_KV:(g + 1) * D_KV] for g in range(GROUP)], axis=0)
                    for c in range(nchunk)])
    logits = lax.dot_general(q3, windows(kext), (((2,), (2,)), ((0,), (0,))), preferred_element_type=F32)
    s_ref[...] = logits.reshape(nchunk * QROWS, 2 * KEYS)

    assert 2 * KEYS == 3 * LANES
    mid0 = lax.broadcasted_iota(jnp.int32, (1, LANES), 1) < KEYS - LANES
    colk = lax.broadcasted_iota(jnp.int32, (1, 2 * KEYS), 1)
    key = jnp.where(colk < KEYS, colk, colk - KEYS)

    def row_block(i):
        r = i * SOFTMAX_ROWS
        return pl.ds(r, SOFTMAX_ROWS), pl.ds(r % QROWS, SOFTMAX_ROWS), r // QROWS

    def tiles(lg):
        return lg[:, 0:LANES], lg[:, LANES:2 * LANES], lg[:, 2 * LANES:3 * LANES]

    def max_rows(i, masked):
        rows_i, qrows_i, chunk_i = row_block(i)
        lg = s_ref[rows_i, :] + abias_ref[qrows_i, :]
        if masked:
            lg = lg + jnp.where(s * tl + chunk_i * CHUNK - WINDOW + key < 0, NEG_INF, 0.0).astype(F32)
        s_ref[rows_i, :] = lg
        t0, t1, t2 = tiles(lg)
        m0_ref[rows_i, :] = jnp.maximum(jnp.max(jnp.maximum(t0, jnp.where(mid0, t1, NEG_INF)), -1, keepdims=True),
                                        sink_ref[0, qrows_i, :])
        m1_ref[rows_i, :] = jnp.maximum(jnp.max(jnp.maximum(t2, jnp.where(mid0, NEG_INF, t1)), -1, keepdims=True),
                                        sink_ref[1, qrows_i, :])

    def exp_rows(i):
        rows_i, qrows_i, _ = row_block(i)
        t0, t1, t2 = tiles(s_ref[rows_i, :])
        m0, m1 = m0_ref[rows_i, :], m1_ref[rows_i, :]
        e = jnp.concatenate([jnp.exp(t0 - m0), jnp.exp(t1 - jnp.where(mid0, m0, m1)), jnp.exp(t2 - m1)], axis=-1)
        e_ref[rows_i, :] = e.astype(BF16)
        sinkden_ref[rows_i, :] = jnp.where(head0, jnp.exp(sink_ref[0, qrows_i, :] - m0),
                                           jnp.exp(sink_ref[1, qrows_i, :] - m1))

    n_it = nchunk * QROWS // SOFTMAX_ROWS
    n_masked = min(WINDOW // CHUNK, nchunk) * QROWS // SOFTMAX_ROWS if mask_history else 0
    proj_chunks = [(c, min(c + PROJ_CHUNK, _OFF_Q)) for c in range(0, _OFF_Q, PROJ_CHUNK)]
    piece = -(-2 * n_it // len(proj_chunks))
    steps = [functools.partial(max_rows, i, i < n_masked) for i in range(n_it)]
    steps += [functools.partial(exp_rows, i) for i in range(n_it)]
    for p, (c0, c1) in enumerate(proj_chunks):
        for step_fn in steps[p * piece:(p + 1) * piece]:
            step_fn()
        proj_ref[:, c0:c1] = _dot(xb, w_in_ref[:, c0:c1])
    for step_fn in steps[len(proj_chunks) * piece:]:
        step_fn()
    vsel = jnp.concatenate([windows(vext), jnp.broadcast_to(keysel_ref[...], (nchunk, 2 * KEYS, D_KV))], axis=-1)
    n_pieces = min(MIX_PIECES, nchunk)
    cpp = nchunk // n_pieces
    conv_parts, attn_parts = [], []
    for p in range(n_pieces):
        cs = slice(p * cpp, (p + 1) * cpp)
        o2 = lax.dot_general(e_ref[p * cpp * QROWS:(p + 1) * cpp * QROWS, :].reshape(cpp, QROWS, 2 * KEYS), vsel[cs],
                             (((2,), (1,)), ((0,), (0,))), preferred_element_type=F32)
        sinkden = sinkden_ref[p * cpp * QROWS:(p + 1) * cpp * QROWS, :].reshape(cpp, QROWS, D_KV)
        o = o2[..., 0:D_KV] * (1.0 / (o2[..., D_KV:2 * D_KV] + sinkden))
        attn_parts += [jnp.concatenate([o[c, g * CHUNK:(g + 1) * CHUNK, :] for g in range(GROUP)], axis=1)
                       for c in range(cpp)]
        conv_parts.append(conv_rows(p * cpp * CHUNK, (p + 1) * cpp * CHUNK))
    y_attn = jnp.concatenate(attn_parts, axis=0)
    n_conv = jnp.concatenate(conv_parts, axis=0)

    for j in range(nseq):
        sk_ref[j] = kext[j * kpitch + tl:(j + 1) * kpitch, :]
        sv_ref[j] = vext[j * kpitch + tl:(j + 1) * kpitch, :]
    if nseq == 1:
        kext[0:WINDOW, :] = kext[tl:tl + WINDOW, :]
        vext[0:WINDOW, :] = vext[tl:tl + WINDOW, :]

    n_attn = _rms_norm(y_attn, gattn_ref[...]).astype(BF16)
    mixed = _dot(n_attn, w_out_ref[D_CONV:D_CONV + D_ATTN, :]) + _dot(n_conv, w_out_ref[0:D_CONV, :])
    if defer_tail:
        xprev_s[...] = x
        mixed_s[...] = mixed
    else:
        norm_rows(0, rows, x, mixed)
        route(1.0)


def _mixer(x, cconv, ck, cv, counts_in, prm, consts, *, nseq, tl, mask_history, alpha):
    nb_total, t_total = x.shape[0], x.shape[1]
    nb, ns = nb_total // nseq, t_total // tl
    defer_tail = ns > 1
    steps = ns + 1 if defer_tail else ns

    def in_blk(s):
        return jnp.minimum(s, ns - 1)

    def out_blk(b, s):
        return b * ns + (jnp.maximum(s - 1, 0) if defer_tail else s)

    rows = nseq * tl
    n_tok = nb_total * t_total
    w_in, convw, convb, sinks, gconv, gattn, w_out, ln1g, ln1b, wr, br = prm
    abias, tri, keysel = consts

    def full(a):
        return pl.BlockSpec(a.shape, lambda b, s, _n=a.ndim: (0,) * _n)

    def seq_state(width, nrows):
        return pl.BlockSpec((nseq, nrows, width), lambda b, s: (b, 0, 0))

    in_specs = [
        pl.BlockSpec((nseq, tl, D_MODEL), lambda b, s: (b, in_blk(s), 0)),
        seq_state(D_CONV, CONV_W - 1), seq_state(D_KV, WINDOW), seq_state(D_KV, WINDOW),
        full(w_in), full(convw), full(convb), full(abias), full(sinks), full(gconv), full(gattn),
        full(w_out), full(ln1g), full(ln1b), full(wr), full(br), full(tri), full(keysel), full(counts_in),
    ]
    out_shape = [
        jax.ShapeDtypeStruct((n_tok, D_MODEL), F32),
        jax.ShapeDtypeStruct((SUBLANES, n_tok), F32),
        jax.ShapeDtypeStruct((n_tok, LANES), F32),
        jax.ShapeDtypeStruct((N_EXPERTS, LANES), F32),
        jax.ShapeDtypeStruct((nb_total, CONV_W - 1, D_CONV), F32),
        jax.ShapeDtypeStruct((nb_total, WINDOW, D_KV), F32),
        jax.ShapeDtypeStruct((nb_total, WINDOW, D_KV), F32),
    ]
    out_specs = [
        pl.BlockSpec((rows, D_MODEL), lambda b, s: (out_blk(b, s), 0)),
        pl.BlockSpec((SUBLANES, rows), lambda b, s: (0, out_blk(b, s))),
        pl.BlockSpec((rows, LANES), lambda b, s: (out_blk(b, s), 0)),
        pl.BlockSpec((N_EXPERTS, LANES), lambda b, s: (0, 0)),
        seq_state(D_CONV, CONV_W - 1), seq_state(D_KV, WINDOW), seq_state(D_KV, WINDOW),
    ]
    scratch = [
        pltpu.VMEM((nseq * (tl + SUBLANES), D_CONV), F32),
        pltpu.VMEM((nseq * (WINDOW + tl), D_KV), F32),
        pltpu.VMEM((nseq * (WINDOW + tl), D_KV), F32),
        pltpu.VMEM((N_EXPERTS, 1), F32),
        pltpu.VMEM((rows // CHUNK * QROWS, 2 * KEYS), F32),
        pltpu.VMEM((rows // CHUNK * QROWS, 2 * KEYS), BF16),
        pltpu.VMEM((rows // CHUNK * QROWS, D_KV), F32),
        pltpu.VMEM((rows // CHUNK * QROWS, LANES), F32),
        pltpu.VMEM((rows // CHUNK * QROWS, LANES), F32),
        pltpu.VMEM((rows, _OFF_Q), F32),
        pltpu.VMEM((rows, D_MODEL), BF16),
        pltpu.VMEM((rows, D_MODEL) if defer_tail else (SUBLANES, LANES), F32),
        pltpu.VMEM((rows, D_MODEL) if defer_tail else (SUBLANES, LANES), F32),
    ]
    kern = functools.partial(_mixer_kernel, nseq=nseq, tl=tl, alpha=alpha, mask_history=mask_history,
                             defer_tail=defer_tail)
    return pl.pallas_call(
        kern, grid=(nb, steps), in_specs=in_specs, out_specs=out_specs, out_shape=out_shape,
        scratch_shapes=scratch, name="mixer",
        compiler_params=pltpu.CompilerParams(dimension_semantics=("arbitrary", "arbitrary"),
                                             vmem_limit_bytes=VMEM_LIMIT),
    )(x, cconv, ck, cv, w_in, convw, convb, abias, sinks, gconv, gattn, w_out, ln1g, ln1b, wr, br, tri, keysel, counts_in)


_PAD_PIECES = tuple(2 ** k for k in range(EXPERT_BLOCK.bit_length() - 2, -1, -1))


def _tile_rows(row, n=1):
    if isinstance(row, int):
        return pl.ds(row * ROW_TILES, n * ROW_TILES)
    return pl.ds(pl.multiple_of(row * ROW_TILES, ROW_TILES), n * ROW_TILES)


def _row_copy(src, src_row, dst, dst_row, sem):
    return pltpu.make_async_copy(src.at[_tile_rows(src_row), :], dst.at[_tile_rows(dst_row), :], sem)


def _dispatch_kernel(zstart_ref, zcount_ref, tail_ref, pos1_ref, pos2_ref, xa_ref, xb_ref, xs_hbm, zeros_v, rowbuf,
                     sem, zsem, *, tb, nba):
    step = pl.program_id(0)

    def zero_copy(start, piece):
        return pltpu.make_async_copy(zeros_v.at[_tile_rows(0, piece), :], xs_hbm.at[_tile_rows(start, piece), :], zsem)

    def pad_copy(e, piece, taken):
        return zero_copy(zstart_ref[e] + taken, piece)

    def for_each_pad_piece(fn):
        def per_tail_block(j, carry):
            for part in range(EXPERT_BLOCK // _PAD_PIECES[0]):
                fn(zero_copy(tail_ref[0] + j * EXPERT_BLOCK + part * _PAD_PIECES[0], _PAD_PIECES[0]))
            return carry

        lax.fori_loop(0, tail_ref[1], per_tail_block, 0)

        def per_expert(e, carry):
            count = zcount_ref[e]
            taken = 0
            for piece in _PAD_PIECES:
                present = (count & piece) != 0

                @pl.when(present)
                def _(piece=piece, taken=taken):
                    fn(pad_copy(e, piece, taken))

                taken = taken + jnp.where(present, piece, 0)
            return carry

        lax.fori_loop(0, N_EXPERTS, per_expert, 0)

    @pl.when(step == 0)
    def _():
        zeros_v[...] = jnp.zeros_like(zeros_v)
        for_each_pad_piece(lambda c: c.start())

    def issue_rows(x_ref):
        n = tb // DISPATCH_PIECES

        def issue(t, carry):
            _row_copy(rowbuf, t, xs_hbm, pos1_ref[t], sem).start(priority=0)
            _row_copy(rowbuf, t, xs_hbm, pos2_ref[t], sem).start(priority=1)
            return carry

        for p in range(DISPATCH_PIECES):
            for j in range(ROW_TILES):
                rowbuf[pl.ds(p * n * ROW_TILES + j, n, stride=ROW_TILES), :] = x_ref[p * n:(p + 1) * n,
                                                                                     j * LANES:(j + 1) * LANES]
            lax.fori_loop(p * n, (p + 1) * n, issue, 0, unroll=8)

    @pl.when(step < nba)
    def _():
        issue_rows(xa_ref)

    @pl.when(step >= nba)
    def _():
        issue_rows(xb_ref)

    pltpu.make_async_copy(xs_hbm.at[_tile_rows(0, 2 * tb), :], xs_hbm.at[_tile_rows(0, 2 * tb), :], sem).wait()

    @pl.when(step == 0)
    def _():
        for_each_pad_piece(lambda c: c.wait())


def _dispatch(xa_rows, xb_rows, pos1, pos2, zstart, zcount, tail, n_slots, *, tb):
    nba, nbb = xa_rows.shape[0] // tb, xb_rows.shape[0] // tb
    kern = functools.partial(_dispatch_kernel, tb=tb, nba=nba)
    grid_spec = pltpu.PrefetchScalarGridSpec(
        num_scalar_prefetch=3, grid=(nba + nbb,),
        in_specs=[pl.BlockSpec((tb,), lambda i, *_: (i,), memory_space=pltpu.SMEM),
                  pl.BlockSpec((tb,), lambda i, *_: (i,), memory_space=pltpu.SMEM),
                  pl.BlockSpec((tb, D_MODEL), lambda i, *_: (jnp.minimum(i, nba - 1), 0)),
                  pl.BlockSpec((tb, D_MODEL), lambda i, *_: (jnp.maximum(i - nba, 0), 0))],
        out_specs=pl.BlockSpec(memory_space=pl.ANY),
        scratch_shapes=[pltpu.VMEM((_PAD_PIECES[0] * ROW_TILES, LANES), F32), pltpu.VMEM((tb * ROW_TILES, LANES), F32),
                        pltpu.SemaphoreType.DMA(()), pltpu.SemaphoreType.DMA(())],
    )
    return pl.pallas_call(
        kern, grid_spec=grid_spec, out_shape=jax.ShapeDtypeStruct((n_slots * ROW_TILES, LANES), F32),
        name="dispatch",
        compiler_params=pltpu.CompilerParams(dimension_semantics=("arbitrary",), vmem_limit_bytes=VMEM_LIMIT),
    )(zstart, zcount, tail, pos1, pos2, xa_rows, xb_rows)


def _expert_kernel(be_ref, src_ref, used_ref, xs_ref, wg_ref, wu_ref, wd_ref, yb_ref, wg_b, wu_b, wd_b):
    i = pl.program_id(0)
    e = be_ref[i]
    prev = be_ref[jnp.maximum(i - 1, 0)]

    @pl.when(jnp.logical_or(i == 0, e != prev))
    def _():
        wg_b[...] = wg_ref[...].astype(BF16)
        wu_b[...] = wu_ref[...].astype(BF16)
        wd_b[...] = wd_ref[...].astype(BF16)

    @pl.when(used_ref[i] != 0)
    def _():
        x = _load_rows(xs_ref, EXPERT_BLOCK).astype(BF16)
        g = _dot(x, wg_b[...])
        u = _dot(x, wu_b[...])
        hid = (g / (1.0 + jnp.exp(-g))) * u
        _store_rows(yb_ref, _dot(hid.astype(BF16), wd_b[...]), EXPERT_BLOCK)

    @pl.when(used_ref[i] == 0)
    def _():
        yb_ref[...] = jnp.zeros_like(yb_ref)


def _experts(xs, blk_e, blk_src, blk_used, w_gate, w_up, w_down, layer):
    n_slots = xs.shape[0] // ROW_TILES
    nblk = n_slots // EXPERT_BLOCK

    def wspec(shape):
        return pl.BlockSpec((None, None) + shape, lambda i, be, src, used: (layer, be[i], 0, 0))

    grid_spec = pltpu.PrefetchScalarGridSpec(
        num_scalar_prefetch=3, grid=(nblk,),
        in_specs=[pl.BlockSpec((EXPERT_BLOCK * ROW_TILES, LANES), lambda i, be, src, used: (src[i], 0)),
                  wspec((D_MODEL, D_EXP)), wspec((D_MODEL, D_EXP)), wspec((D_EXP, D_MODEL))],
        out_specs=pl.BlockSpec((EXPERT_BLOCK * ROW_TILES, LANES), lambda i, be, src, used: (i, 0)),
        scratch_shapes=[pltpu.VMEM((D_MODEL, D_EXP), BF16), pltpu.VMEM((D_MODEL, D_EXP), BF16),
                        pltpu.VMEM((D_EXP, D_MODEL), BF16)],
    )
    return pl.pallas_call(
        _expert_kernel, grid_spec=grid_spec, out_shape=jax.ShapeDtypeStruct((n_slots * ROW_TILES, LANES), F32),
        name="experts",
        compiler_params=pltpu.CompilerParams(dimension_semantics=("arbitrary",), vmem_limit_bytes=VMEM_LIMIT),
    )(blk_e, blk_src, blk_used, xs, w_gate, w_up, w_down)


def _combine_kernel(pos1_ref, pos2_ref, next1_ref, next2_ref, xa_ref, xb_ref, ga_ref, gb_ref, yb_hbm, g_ref, b_ref,
                    outa_ref, outb_ref, ybuf, sems, *, tb, alpha, nba, nsteps):
    step = pl.program_id(0)
    slot = step % 2
    other = 1 - slot

    def issue(p1_ref, p2_ref, to_slot, t):
        _row_copy(yb_hbm, p1_ref[t], ybuf.at[to_slot].at[0], t, sems.at[to_slot]).start(priority=0)
        _row_copy(yb_hbm, p2_ref[t], ybuf.at[to_slot].at[1], t, sems.at[to_slot]).start(priority=1)

    def wait_slot(which):
        for k in range(TOP_K):
            pltpu.make_async_copy(yb_hbm.at[_tile_rows(0, tb), :], ybuf.at[which].at[k], sems.at[which]).wait()

    @pl.when(step == 0)
    def _():
        def body(t, carry):
            issue(pos1_ref, pos2_ref, 0, t)
            return carry
        lax.fori_loop(0, tb, body, 0, unroll=8)

    wait_slot(slot)

    def finish(x1_ref, gcol_ref, out_ref):
        n = tb // COMBINE_PIECES
        for p in range(COMBINE_PIECES):
            for t in range(p * n, (p + 1) * n):
                issue(next1_ref, next2_ref, other, t)
            gates = gcol_ref[p * n:(p + 1) * n, :]
            y = (gates[:, _R_G1:_R_G1 + 1] * _load_rows(ybuf.at[slot].at[0], n, p * n)
                 + gates[:, _R_G2:_R_G2 + 1] * _load_rows(ybuf.at[slot].at[1], n, p * n))
            out_ref[p * n:(p + 1) * n, :] = _layer_norm(alpha * x1_ref[p * n:(p + 1) * n, :] + y, g_ref[...], b_ref[...])

    @pl.when(step < nba)
    def _():
        finish(xa_ref, ga_ref, outa_ref)

    @pl.when(step >= nba)
    def _():
        finish(xb_ref, gb_ref, outb_ref)

    @pl.when(step == nsteps - 1)
    def _():
        wait_slot(other)


def _combine(xa_rows, xb_rows, gcol_a, gcol_b, yb, pos1, pos2, ln2g, ln2b, *, tb, alpha):
    nba, nbb = xa_rows.shape[0] // tb, xb_rows.shape[0] // tb
    nsteps = nba + nbb
    kern = functools.partial(_combine_kernel, tb=tb, alpha=alpha, nba=nba, nsteps=nsteps)

    def first(i):
        return jnp.minimum(i, nba - 1)

    def second(i):
        return jnp.maximum(i - nba, 0)

    def nxt(i):
        return jnp.minimum(i + 1, nsteps - 1)

    smem = functools.partial(pl.BlockSpec, (tb,), memory_space=pltpu.SMEM)
    return pl.pallas_call(
        kern, grid=(nsteps,),
        in_specs=[smem(lambda i: (i,)), smem(lambda i: (i,)), smem(lambda i: (nxt(i),)), smem(lambda i: (nxt(i),)),
                  pl.BlockSpec((tb, D_MODEL), lambda i: (first(i), 0)),
                  pl.BlockSpec((tb, D_MODEL), lambda i: (second(i), 0)),
                  pl.BlockSpec((tb, LANES), lambda i: (first(i), 0)),
                  pl.BlockSpec((tb, LANES), lambda i: (second(i), 0)),
                  pl.BlockSpec(memory_space=pl.ANY),
                  pl.BlockSpec((1, D_MODEL), lambda i: (0, 0)),
                  pl.BlockSpec((1, D_MODEL), lambda i: (0, 0))],
        out_specs=[pl.BlockSpec((tb, D_MODEL), lambda i: (first(i), 0)),
                   pl.BlockSpec((tb, D_MODEL), lambda i: (second(i), 0))],
        out_shape=[jax.ShapeDtypeStruct((nba * tb, D_MODEL), F32), jax.ShapeDtypeStruct((nbb * tb, D_MODEL), F32)],
        scratch_shapes=[pltpu.VMEM((2, TOP_K, tb * ROW_TILES, LANES), F32), pltpu.SemaphoreType.DMA((2,))],
        name="combine",
        compiler_params=pltpu.CompilerParams(dimension_semantics=("arbitrary",), vmem_limit_bytes=VMEM_LIMIT),
    )(pos1, pos2, pos1, pos2, xa_rows, xb_rows, gcol_a, gcol_b, yb, ln2g, ln2b)


def _alibi_bias():
    slopes = np.asarray([2.0 ** (-8.0 * (h + 1) / N_HEADS) for h in range(N_HEADS)], np.float32)
    qi = np.arange(CHUNK, dtype=np.int32)[:, None]
    sj = np.arange(KEYS, dtype=np.int32)[None, :]
    dist = np.abs(qi + WINDOW - sj).astype(np.float32)
    bias = -slopes.reshape(N_KV_HEADS, GROUP, 1, 1) * dist
    bias = np.transpose(bias, (1, 2, 0, 3)).reshape(GROUP * CHUNK, N_KV_HEADS * KEYS)
    return jnp.asarray(bias, F32)


def _key_selector():
    row_head = np.arange(N_KV_HEADS * KEYS)[:, None] // KEYS
    lane_head = np.arange(D_KV)[None, :] // HEAD_DIM
    return jnp.asarray((row_head == lane_head).astype(np.float32), BF16)


def _strict_upper():
    r = np.arange(RANK_CHUNK)
    return jnp.asarray((r[:, None] < r[None, :]).astype(np.float32), BF16)


def _slot_plan(route, counts, n_tok):
    nblk = (n_tok * TOP_K) // EXPERT_BLOCK + N_EXPERTS
    experts = route[_R_E1:_R_E2 + 1].astype(jnp.int32)
    ranks = route[_R_RANK1:_R_RANK2 + 1].astype(jnp.int32)
    cnt = counts[:, 0].astype(jnp.int32)
    padded = (cnt + EXPERT_BLOCK - 1) // EXPERT_BLOCK * EXPERT_BLOCK
    pad_end = jnp.cumsum(padded)
    pad_start = pad_end - padded
    ids = jnp.arange(N_EXPERTS, dtype=jnp.int32)
    pos = jnp.sum(jnp.where(experts[..., None] == ids, pad_start, 0), axis=-1) + ranks
    blk_first = jnp.arange(nblk, dtype=jnp.int32) * EXPERT_BLOCK
    blk_e = jnp.minimum(jnp.sum((pad_end[None, :] <= blk_first[:, None]).astype(jnp.int32), axis=-1), N_EXPERTS - 1)
    blk_used = (blk_first < pad_end[-1]).astype(jnp.int32)
    blk_src = jnp.minimum(jnp.arange(nblk, dtype=jnp.int32), pad_end[-1] // EXPERT_BLOCK - 1)
    tail = jnp.stack([pad_end[-1], nblk - pad_end[-1] // EXPERT_BLOCK]).astype(jnp.int32)
    return pos[0], pos[1], pad_start + cnt, padded - cnt, tail, blk_e, blk_src, blk_used, nblk * EXPERT_BLOCK


def _layer(xp, xs, cache, mix_prm, consts, moe_prm, layer, *, tl_prompt, nseq_sample, alpha):
    (bp, tp, _), (bs, ts, _) = xp.shape, xs.shape
    np_tok, ns_tok = bp * tp, bs * ts
    tb = ROWS
    assert np_tok % tb == 0 and ns_tok % tb == 0
    zeros_conv = jnp.zeros((bp, CONV_W - 1, D_CONV), F32)
    zeros_kv = jnp.zeros((bp, WINDOW, D_KV), F32)
    zero_counts = jnp.zeros((N_EXPERTS, LANES), F32)
    x1p, route_p, gcol_p, counts_p, *state_p = _mixer(
        xp, zeros_conv, zeros_kv, zeros_kv, zero_counts, mix_prm, consts,
        nseq=1, tl=tl_prompt, mask_history=True, alpha=alpha)
    x1s, route_s, gcol_s, counts, *state_s = _mixer(
        xs, *cache, counts_p, mix_prm, consts, nseq=nseq_sample, tl=ts, mask_history=False, alpha=alpha)
    route = jnp.concatenate([route_p, route_s], axis=1)
    pos1, pos2, zstart, zcount, tail, blk_e, blk_src, blk_used, n_slots = _slot_plan(route, counts, np_tok + ns_tok)
    tb_dispatch = DISPATCH_ROWS if np_tok % DISPATCH_ROWS == 0 and ns_tok % DISPATCH_ROWS == 0 else tb
    sorted_rows = _dispatch(x1p, x1s, pos1, pos2, zstart, zcount, tail, n_slots, tb=tb_dispatch)
    w_gate, w_up, w_down, ln2g, ln2b = moe_prm
    yb = _experts(sorted_rows, blk_e, blk_src, blk_used, w_gate, w_up, w_down, layer)
    out_p, out_s = _combine(x1p, x1s, gcol_p, gcol_s, yb, pos1, pos2, ln2g, ln2b, tb=tb, alpha=alpha)
    return out_p.reshape(bp, tp, D_MODEL), out_s.reshape(bs, ts, D_MODEL), state_p, state_s


def kernel(x_prompt, x_sample, cache_conv, cache_k, cache_v, w_in, conv_w, conv_b, attn_sinks, g_conv, g_attn,
           w_out, ln1_g, ln1_b, router_group_w, router_group_b, router_expert_w, router_expert_b,
           expert_w_gate, expert_w_up, expert_w_down, ln2_g, ln2_b):
    depth = w_in.shape[0]
    alpha = (2 * depth) ** 0.25
    batch, seq = x_prompt.shape[0], x_prompt.shape[1]
    dec_batch, dec_seq = x_sample.shape[0], x_sample.shape[1]
    assert dec_seq == CHUNK and seq % CHUNK == 0
    consts = (_alibi_bias(), _strict_upper(), _key_selector())
    tl_prompt = min(MIXER_ROWS, seq)
    nseq_sample = min(MIXER_ROWS // dec_seq, dec_batch)
    assert seq % tl_prompt == 0 and dec_batch % nseq_sample == 0

    def regroup_heads(a, axis):
        shape = a.shape
        a = a.reshape(shape[:axis] + (N_KV_HEADS, GROUP, HEAD_DIM) + shape[axis + 1:])
        return jnp.swapaxes(a, axis, axis + 1).reshape(shape)

    w_in_all = jnp.concatenate([w_in[:, :, :_OFF_Q], regroup_heads(w_in[:, :, _OFF_Q:_OFF_K], 2), w_in[:, :, _OFF_K:]],
                               axis=2).astype(BF16)
    w_out_all = jnp.concatenate([w_out[:, :D_CONV], regroup_heads(w_out[:, D_CONV:], 1)], axis=1).astype(BF16)
    g_attn_all = regroup_heads(g_attn, 1)
    router_pad = LANES - N_EXPERTS - N_GROUPS
    wr_all = jnp.concatenate([router_expert_w, router_group_w, jnp.zeros((depth, D_MODEL, router_pad), F32)],
                             axis=2).astype(BF16)
    br_all = jnp.concatenate([router_expert_b, router_group_b, jnp.zeros((depth, router_pad), F32)], axis=1)
    sink_all = jnp.broadcast_to(jnp.repeat(attn_sinks.astype(F32), CHUNK, axis=1).reshape(depth, N_KV_HEADS, QROWS, 1),
                                (depth, N_KV_HEADS, QROWS, LANES))

    xp, xs = x_prompt, x_sample
    states = [[] for _ in range(6)]
    for l in range(depth):
        mix_prm = (w_in_all[l], conv_w[l], conv_b[l].reshape(1, D_CONV), sink_all[l],
                   g_conv[l].reshape(1, D_CONV), g_attn_all[l].reshape(1, D_ATTN), w_out_all[l],
                   ln1_g[l].reshape(1, D_MODEL), ln1_b[l].reshape(1, D_MODEL), wr_all[l], br_all[l].reshape(1, LANES))
        moe_prm = (expert_w_gate, expert_w_up, expert_w_down,
                   ln2_g[l].reshape(1, D_MODEL), ln2_b[l].reshape(1, D_MODEL))
        cache = (cache_conv[l], cache_k[l].reshape(dec_batch, WINDOW, D_KV), cache_v[l].reshape(dec_batch, WINDOW, D_KV))
        xp, xs, state_p, state_s = _layer(xp, xs, cache, mix_prm, consts, moe_prm, l,
                                          tl_prompt=tl_prompt, nseq_sample=nseq_sample, alpha=alpha)
        for lst, val in zip(states, state_p + state_s):
            lst.append(val)

    def kv(lst, nb):
        return jnp.stack(lst).reshape(depth, nb, WINDOW, N_KV_HEADS, HEAD_DIM)

    return (xp, xs, jnp.stack(states[0]), kv(states[1], batch), kv(states[2], batch),
            jnp.stack(states[3]), kv(states[4], dec_batch), kv(states[5], dec_batch))
```

```python
import functools

import numpy as np
import jax
import jax.numpy as jnp
from jax import lax
from jax.experimental import pallas as pl
from jax.experimental.pallas import tpu as pltpu

D_MODEL = 1024
D_CONV = 512
CONV_W = 3
N_HEADS = 8
N_KV_HEADS = 2
GROUP = N_HEADS // N_KV_HEADS
HEAD_DIM = 64
D_ATTN = N_HEADS * HEAD_DIM
D_KV = N_KV_HEADS * HEAD_DIM
WINDOW = 128
CHUNK = 64
KEYS = WINDOW + CHUNK
QROWS = GROUP * CHUNK
N_GROUPS = 4
EXP_PER_GROUP = 8
N_EXPERTS = N_GROUPS * EXP_PER_GROUP
TOP_K = 2
D_EXP = 512
LN_EPS = 1e-5
NEG_INF = -1e30

SUBLANES = 8
LANES = 128
ROW_TILES = D_MODEL // LANES
assert ROW_TILES == SUBLANES
ROWS = 512
MIXER_ROWS = 512
DISPATCH_ROWS = 1024
COMBINE_PIECES = 16
DISPATCH_PIECES = 8
EXPERT_BLOCK = 512
RANK_CHUNK = 256
SOFTMAX_ROWS = 64
PROJ_CHUNK = 2 * LANES
MIX_PIECES = 4
VMEM_LIMIT = 56 * 1024 * 1024

F32 = jnp.float32
BF16 = jnp.bfloat16

_OFF_B, _OFF_C, _OFF_H = 0, D_CONV, 2 * D_CONV
_OFF_Q = 3 * D_CONV
_OFF_K = _OFF_Q + D_ATTN
_OFF_V = _OFF_K + D_KV
D_IN = _OFF_V + D_KV

_R_E1, _R_E2, _R_RANK1, _R_RANK2, _R_G1, _R_G2 = 0, 1, 2, 3, 4, 5


def _load_rows(ref, n, first=0):
    return jnp.concatenate([ref[pl.ds(first * ROW_TILES + j, n, stride=ROW_TILES), :] for j in range(ROW_TILES)],
                           axis=1)


def _store_rows(ref, val, n):
    for j in range(ROW_TILES):
        ref[pl.ds(j, n, stride=ROW_TILES), :] = val[:, j * LANES:(j + 1) * LANES]


def _dot(a, b):
    return jnp.dot(a, b, preferred_element_type=F32)


def _rms_norm(x, g):
    return x * lax.rsqrt(jnp.mean(jnp.square(x), -1, keepdims=True) + LN_EPS) * g


def _layer_norm(x, g, b):
    mu = jnp.mean(x, -1, keepdims=True)
    xc = x - mu
    var = jnp.mean(jnp.square(xc), -1, keepdims=True)
    return xc * lax.rsqrt(var + LN_EPS) * g + b


def _mixer_kernel(x_ref, cconv_ref, ck_ref, cv_ref, w_in_ref, convw_ref, convb_ref, abias_ref,
                  sink_ref, gconv_ref, gattn_ref, w_out_ref, ln1g_ref, ln1b_ref, wr_ref, br_ref,
                  tri_ref, keysel_ref, counts_in_ref,
                  x1_ref, route_ref, gcol_ref, counts_ref, sconv_ref, sk_ref, sv_ref,
                  uext, kext, vext, cnt_s, s_ref, e_ref, sinkden_ref, m0_ref, m1_ref, proj_ref, x1b_s, xprev_s,
                  mixed_s, *, nseq, tl, alpha, mask_history, defer_tail):
    b = pl.program_id(0)
    s = pl.program_id(1)
    rows = nseq * tl
    upitch = tl + SUBLANES
    kpitch = WINDOW + tl

    @pl.when(jnp.logical_and(b == 0, s == 0))
    def _():
        cnt_s[...] = counts_in_ref[:, 0:1]
        if defer_tail:
            xprev_s[...] = jnp.zeros_like(xprev_s)
            mixed_s[...] = jnp.zeros_like(mixed_s)

    @pl.when(s == 0)
    def _():
        for j in range(nseq):
            uext[j * upitch + SUBLANES - 2:j * upitch + SUBLANES, :] = cconv_ref[j]
            kext[j * kpitch:j * kpitch + WINDOW, :] = ck_ref[j]
            vext[j * kpitch:j * kpitch + WINDOW, :] = cv_ref[j]

    def norm_rows(r0, r1, x_rows, mixed_rows):
        x1 = _layer_norm(alpha * x_rows + mixed_rows, ln1g_ref[...], ln1b_ref[...])
        x1_ref[r0:r1, :] = x1
        x1b_s[r0:r1, :] = x1.astype(BF16)

    def route(valid):
        logits_t = (_dot(x1b_s[...], wr_ref[...]) + br_ref[...]).T
        sub = lax.broadcasted_iota(jnp.int32, (SUBLANES, rows), 0)
        gl = jnp.where(sub < N_GROUPS, logits_t[N_EXPERTS:N_EXPERTS + SUBLANES, :], -jnp.inf)
        gmax = jnp.max(gl, axis=0, keepdims=True)
        grp = jnp.min(jnp.where(gl == gmax, sub, SUBLANES), axis=0, keepdims=True)
        p_grp = 1.0 / jnp.sum(jnp.exp(gl - gmax), axis=0, keepdims=True)
        el = logits_t[(N_GROUPS - 1) * EXP_PER_GROUP:N_GROUPS * EXP_PER_GROUP, :]
        for g in range(N_GROUPS - 2, -1, -1):
            el = jnp.where(grp == g, logits_t[g * EXP_PER_GROUP:(g + 1) * EXP_PER_GROUP, :], el)
        v1 = jnp.max(el, axis=0, keepdims=True)
        i1 = jnp.min(jnp.where(el == v1, sub, SUBLANES), axis=0, keepdims=True)
        el2 = jnp.where(sub == i1, -jnp.inf, el)
        v2 = jnp.max(el2, axis=0, keepdims=True)
        i2 = jnp.min(jnp.where(el2 == v2, sub, SUBLANES), axis=0, keepdims=True)
        e2 = jnp.exp(v2 - v1)
        gate1 = p_grp * (1.0 / (1.0 + e2))
        gate2 = p_grp * (e2 / (1.0 + e2))

        chosen = jnp.logical_or(sub == i1, sub == i2)
        onehot = jnp.concatenate(
            [jnp.where(jnp.logical_and(grp == g, chosen), 1.0, 0.0) for g in range(N_GROUPS)], axis=0)
        running = cnt_s[...]
        ranks = []
        for c in range(rows // RANK_CHUNK):
            oh = onehot[:, c * RANK_CHUNK:(c + 1) * RANK_CHUNK]
            ranks.append(_dot(oh.astype(BF16), tri_ref[...]) + running)
            running = running + jnp.sum(oh, axis=1, keepdims=True) * valid
        rank = jnp.concatenate(ranks, axis=1)
        cnt_s[...] = running
        counts_ref[...] = jnp.broadcast_to(running, (N_EXPERTS, LANES))

        ex1 = grp * EXP_PER_GROUP + i1
        ex2 = grp * EXP_PER_GROUP + i2
        erow = lax.broadcasted_iota(jnp.int32, (N_EXPERTS, rows), 0)
        rank1 = jnp.sum(jnp.where(erow == ex1, rank, 0.0), axis=0, keepdims=True)
        rank2 = jnp.sum(jnp.where(erow == ex2, rank, 0.0), axis=0, keepdims=True)
        fields = {_R_E1: ex1.astype(F32), _R_E2: ex2.astype(F32), _R_RANK1: rank1, _R_RANK2: rank2,
                  _R_G1: gate1, _R_G2: gate2}
        record = jnp.zeros((SUBLANES, rows), F32)
        for r, val in fields.items():
            record = jnp.where(sub == r, val, record)
        route_ref[...] = record
        padded = jnp.concatenate([record, jnp.zeros((LANES - SUBLANES, rows), F32)], axis=0)
        gcol_ref[...] = padded.T

    x = x_ref[...].reshape(rows, D_MODEL)
    xb = x.astype(BF16)

    def conv_rows(r0, r1):
        w0, w1, w2 = convw_ref[0:1, :], convw_ref[1:2, :], convw_ref[2:3, :]
        out = []
        for j in range(r0 // tl, -(-r1 // tl)):
            a, b_ = max(r0, j * tl) - j * tl, min(r1, (j + 1) * tl) - j * tl
            base = j * upitch + SUBLANES
            src = pl.ds(j * tl + a, b_ - a)
            uext[base + a:base + b_, :] = proj_ref[src, _OFF_C:_OFF_C + D_CONV] * proj_ref[src, _OFF_H:_OFF_H + D_CONV]
            yc = convb_ref[...] + uext[base + a - 2:base + b_ - 2, :] * w0
            yc = yc + uext[base + a - 1:base + b_ - 1, :] * w1
            yc = yc + uext[base + a:base + b_, :] * w2
            out.append(_rms_norm(proj_ref[src, _OFF_B:_OFF_B + D_CONV] * yc, gconv_ref[...]).astype(BF16))
            if b_ == tl:
                tail = uext[base + tl - 2:base + tl, :]
                sconv_ref[j] = tail
                uext[base - 2:base, :] = tail
        return out[0] if len(out) == 1 else jnp.concatenate(out, axis=0)

    qkv = []
    qkv_chunks = [(c, c + PROJ_CHUNK) for c in range(_OFF_Q, D_IN, PROJ_CHUNK)]
    tail_rows = -(-rows // len(qkv_chunks) // SUBLANES) * SUBLANES
    for p, (c0, c1) in enumerate(qkv_chunks):
        r0, r1 = min(p * tail_rows, rows), min((p + 1) * tail_rows, rows)
        if defer_tail and r1 > r0:
            norm_rows(r0, r1, xprev_s[r0:r1, :], mixed_s[r0:r1, :])
        qkv.append(_dot(xb, w_in_ref[:, c0:c1]))
    qkv = jnp.concatenate(qkv, axis=1)
    if defer_tail:
        route((s > 0).astype(F32))
    q = (qkv[:, 0:D_ATTN] * (HEAD_DIM ** -0.5)).astype(BF16)
    k = qkv[:, D_ATTN:D_ATTN + D_KV]
    v = qkv[:, D_ATTN + D_KV:D_ATTN + 2 * D_KV]
    for j in range(nseq):
        kext[j * kpitch + WINDOW:(j + 1) * kpitch, :] = k[j * tl:(j + 1) * tl]
        vext[j * kpitch + WINDOW:(j + 1) * kpitch, :] = v[j * tl:(j + 1) * tl]

    nchunk = rows // CHUNK
    assert nseq == 1 or tl == CHUNK
    key_stride = CHUNK if nseq == 1 else kpitch
    head0 = lax.broadcasted_iota(jnp.int32, (1, D_KV), 1) < HEAD_DIM

    def windows(ext_ref):
        ext = ext_ref[...].astype(BF16)
        heads = (jnp.where(head0, ext, jnp.zeros_like(ext)), jnp.where(head0, jnp.zeros_like(ext), ext))
        return jnp.stack([jnp.concatenate([h[c * key_stride:c * key_stride + KEYS] for h in heads], axis=0)
                          for c in range(nchunk)])

    q3 = jnp.stack([jnp.concatenate([q[c * CHUNK:(c + 1) * CHUNK, g * D_KV:(g + 1) * D_KV] for g in range(GROUP)], axis=0)
                    for c in range(nchunk)])
    logits = lax.dot_general(q3, windows(kext), (((2,), (2,)), ((0,), (0,))), preferred_element_type=F32)
    s_ref[...] = logits.reshape(nchunk * QROWS, 2 * KEYS)

    assert 2 * KEYS == 3 * LANES
    mid0 = lax.broadcasted_iota(jnp.int32, (1, LANES), 1) < KEYS - LANES
    colk = lax.broadcasted_iota(jnp.int32, (1, 2 * KEYS), 1)
    key = jnp.where(colk < KEYS, colk, colk - KEYS)

    def row_block(i):
        r = i * SOFTMAX_ROWS
        return pl.ds(r, SOFTMAX_ROWS), pl.ds(r % QROWS, SOFTMAX_ROWS), r // QROWS

    def tiles(lg):
        return lg[:, 0:LANES], lg[:, LANES:2 * LANES], lg[:, 2 * LANES:3 * LANES]

    def max_rows(i, masked):
        rows_i, qrows_i, chunk_i = row_block(i)
        lg = s_ref[rows_i, :] + abias_ref[qrows_i, :]
        if masked:
            lg = lg + jnp.where(s * tl + chunk_i * CHUNK - WINDOW + key < 0, NEG_INF, 0.0).astype(F32)
        s_ref[rows_i, :] = lg
        t0, t1, t2 = tiles(lg)
        m0_ref[rows_i, :] = jnp.maximum(jnp.max(jnp.maximum(t0, jnp.where(mid0, t1, NEG_INF)), -1, keepdims=True),
                                        sink_ref[0, qrows_i, :])
        m1_ref[rows_i, :] = jnp.maximum(jnp.max(jnp.maximum(t2, jnp.where(mid0, NEG_INF, t1)), -1, keepdims=True),
                                        sink_ref[1, qrows_i, :])

    def exp_rows(i):
        rows_i, qrows_i, _ = row_block(i)
        t0, t1, t2 = tiles(s_ref[rows_i, :])
        m0, m1 = m0_ref[rows_i, :], m1_ref[rows_i, :]
        e = jnp.concatenate([jnp.exp(t0 - m0), jnp.exp(t1 - jnp.where(mid0, m0, m1)), jnp.exp(t2 - m1)], axis=-1)
        e_ref[rows_i, :] = e.astype(BF16)
        sinkden_ref[rows_i, :] = jnp.where(head0, jnp.exp(sink_ref[0, qrows_i, :] - m0),
                                           jnp.exp(sink_ref[1, qrows_i, :] - m1))

    n_it = nchunk * QROWS // SOFTMAX_ROWS
    n_masked = min(WINDOW // CHUNK, nchunk) * QROWS // SOFTMAX_ROWS if mask_history else 0
    proj_chunks = [(c, min(c + PROJ_CHUNK, _OFF_Q)) for c in range(0, _OFF_Q, PROJ_CHUNK)]
    piece = -(-2 * n_it // len(proj_chunks))
    steps = [functools.partial(max_rows, i, i < n_masked) for i in range(n_it)]
    steps += [functools.partial(exp_rows, i) for i in range(n_it)]
    for p, (c0, c1) in enumerate(proj_chunks):
        for step_fn in steps[p * piece:(p + 1) * piece]:
            step_fn()
        proj_ref[:, c0:c1] = _dot(xb, w_in_ref[:, c0:c1])
    for step_fn in steps[len(proj_chunks) * piece:]:
        step_fn()
    vsel = jnp.concatenate([windows(vext), jnp.broadcast_to(keysel_ref[...], (nchunk, 2 * KEYS, D_KV))], axis=-1)
    n_pieces = min(MIX_PIECES, nchunk)
    cpp = nchunk // n_pieces
    conv_parts, attn_parts = [], []
    for p in range(n_pieces):
        cs = slice(p * cpp, (p + 1) * cpp)
        o2 = lax.dot_general(e_ref[p * cpp * QROWS:(p + 1) * cpp * QROWS, :].reshape(cpp, QROWS, 2 * KEYS), vsel[cs],
                             (((2,), (1,)), ((0,), (0,))), preferred_element_type=F32)
        sinkden = sinkden_ref[p * cpp * QROWS:(p + 1) * cpp * QROWS, :].reshape(cpp, QROWS, D_KV)
        o = o2[..., 0:D_KV] * (1.0 / (o2[..., D_KV:2 * D_KV] + sinkden))
        attn_parts += [jnp.concatenate([o[c, g * CHUNK:(g + 1) * CHUNK, :] for g in range(GROUP)], axis=1)
                       for c in range(cpp)]
        conv_parts.append(conv_rows(p * cpp * CHUNK, (p + 1) * cpp * CHUNK))
    y_attn = jnp.concatenate(attn_parts, axis=0)
    n_conv = jnp.concatenate(conv_parts, axis=0)

    for j in range(nseq):
        sk_ref[j] = kext[j * kpitch + tl:(j + 1) * kpitch, :]
        sv_ref[j] = vext[j * kpitch + tl:(j + 1) * kpitch, :]
    if nseq == 1:
        kext[0:WINDOW, :] = kext[tl:tl + WINDOW, :]
        vext[0:WINDOW, :] = vext[tl:tl + WINDOW, :]

    n_attn = _rms_norm(y_attn, gattn_ref[...]).astype(BF16)
    mixed = _dot(n_attn, w_out_ref[D_CONV:D_CONV + D_ATTN, :]) + _dot(n_conv, w_out_ref[0:D_CONV, :])
    if defer_tail:
        xprev_s[...] = x
        mixed_s[...] = mixed
    else:
        norm_rows(0, rows, x, mixed)
        route(1.0)


def _mixer(x, cconv, ck, cv, counts_in, prm, consts, *, nseq, tl, mask_history, alpha):
    nb_total, t_total = x.shape[0], x.shape[1]
    nb, ns = nb_total // nseq, t_total // tl
    defer_tail = False
    steps = ns + 1 if defer_tail else ns

    def in_blk(s):
        return jnp.minimum(s, ns - 1)

    def out_blk(b, s):
        return b * ns + (jnp.maximum(s - 1, 0) if defer_tail else s)

    rows = nseq * tl
    n_tok = nb_total * t_total
    w_in, convw, convb, sinks, gconv, gattn, w_out, ln1g, ln1b, wr, br = prm
    abias, tri, keysel = consts

    def full(a):
        return pl.BlockSpec(a.shape, lambda b, s, _n=a.ndim: (0,) * _n)

    def seq_state(width, nrows):
        return pl.BlockSpec((nseq, nrows, width), lambda b, s: (b, 0, 0))

    in_specs = [
        pl.BlockSpec((nseq, tl, D_MODEL), lambda b, s: (b, in_blk(s), 0)),
        seq_state(D_CONV, CONV_W - 1), seq_state(D_KV, WINDOW), seq_state(D_KV, WINDOW),
        full(w_in), full(convw), full(convb), full(abias), full(sinks), full(gconv), full(gattn),
        full(w_out), full(ln1g), full(ln1b), full(wr), full(br), full(tri), full(keysel), full(counts_in),
    ]
    out_shape = [
        jax.ShapeDtypeStruct((n_tok, D_MODEL), F32),
        jax.ShapeDtypeStruct((SUBLANES, n_tok), F32),
        jax.ShapeDtypeStruct((n_tok, LANES), F32),
        jax.ShapeDtypeStruct((N_EXPERTS, LANES), F32),
        jax.ShapeDtypeStruct((nb_total, CONV_W - 1, D_CONV), F32),
        jax.ShapeDtypeStruct((nb_total, WINDOW, D_KV), F32),
        jax.ShapeDtypeStruct((nb_total, WINDOW, D_KV), F32),
    ]
    out_specs = [
        pl.BlockSpec((rows, D_MODEL), lambda b, s: (out_blk(b, s), 0)),
        pl.BlockSpec((SUBLANES, rows), lambda b, s: (0, out_blk(b, s))),
        pl.BlockSpec((rows, LANES), lambda b, s: (out_blk(b, s), 0)),
        pl.BlockSpec((N_EXPERTS, LANES), lambda b, s: (0, 0)),
        seq_state(D_CONV, CONV_W - 1), seq_state(D_KV, WINDOW), seq_state(D_KV, WINDOW),
    ]
    scratch = [
        pltpu.VMEM((nseq * (tl + SUBLANES), D_CONV), F32),
        pltpu.VMEM((nseq * (WINDOW + tl), D_KV), F32),
        pltpu.VMEM((nseq * (WINDOW + tl), D_KV), F32),
        pltpu.VMEM((N_EXPERTS, 1), F32),
        pltpu.VMEM((rows // CHUNK * QROWS, 2 * KEYS), F32),
        pltpu.VMEM((rows // CHUNK * QROWS, 2 * KEYS), BF16),
        pltpu.VMEM((rows // CHUNK * QROWS, D_KV), F32),
        pltpu.VMEM((rows // CHUNK * QROWS, LANES), F32),
        pltpu.VMEM((rows // CHUNK * QROWS, LANES), F32),
        pltpu.VMEM((rows, _OFF_Q), F32),
        pltpu.VMEM((rows, D_MODEL), BF16),
        pltpu.VMEM((rows, D_MODEL) if defer_tail else (SUBLANES, LANES), F32),
        pltpu.VMEM((rows, D_MODEL) if defer_tail else (SUBLANES, LANES), F32),
    ]
    kern = functools.partial(_mixer_kernel, nseq=nseq, tl=tl, alpha=alpha, mask_history=mask_history,
                             defer_tail=defer_tail)
    return pl.pallas_call(
        kern, grid=(nb, steps), in_specs=in_specs, out_specs=out_specs, out_shape=out_shape,
        scratch_shapes=scratch, name="mixer",
        compiler_params=pltpu.CompilerParams(dimension_semantics=("arbitrary", "arbitrary"),
                                             vmem_limit_bytes=VMEM_LIMIT),
    )(x, cconv, ck, cv, w_in, convw, convb, abias, sinks, gconv, gattn, w_out, ln1g, ln1b, wr, br, tri, keysel, counts_in)


_PAD_PIECES = tuple(2 ** k for k in range(EXPERT_BLOCK.bit_length() - 2, -1, -1))


def _tile_rows(row, n=1):
    if isinstance(row, int):
        return pl.ds(row * ROW_TILES, n * ROW_TILES)
    return pl.ds(pl.multiple_of(row * ROW_TILES, ROW_TILES), n * ROW_TILES)


def _row_copy(src, src_row, dst, dst_row, sem):
    return pltpu.make_async_copy(src.at[_tile_rows(src_row), :], dst.at[_tile_rows(dst_row), :], sem)


def _dispatch_kernel(zstart_ref, zcount_ref, tail_ref, pos1_ref, pos2_ref, xa_ref, xb_ref, xs_hbm, zeros_v, rowbuf,
                     sem, zsem, *, tb, nba):
    step = pl.program_id(0)

    def zero_copy(start, piece):
        return pltpu.make_async_copy(zeros_v.at[_tile_rows(0, piece), :], xs_hbm.at[_tile_rows(start, piece), :], zsem)

    def pad_copy(e, piece, taken):
        return zero_copy(zstart_ref[e] + taken, piece)

    def for_each_pad_piece(fn):
        def per_tail_block(j, carry):
            for part in range(EXPERT_BLOCK // _PAD_PIECES[0]):
                fn(zero_copy(tail_ref[0] + j * EXPERT_BLOCK + part * _PAD_PIECES[0], _PAD_PIECES[0]))
            return carry

        lax.fori_loop(0, tail_ref[1], per_tail_block, 0)

        def per_expert(e, carry):
            count = zcount_ref[e]
            taken = 0
            for piece in _PAD_PIECES:
                present = (count & piece) != 0

                @pl.when(present)
                def _(piece=piece, taken=taken):
                    fn(pad_copy(e, piece, taken))

                taken = taken + jnp.where(present, piece, 0)
            return carry

        lax.fori_loop(0, N_EXPERTS, per_expert, 0)

    @pl.when(step == 0)
    def _():
        zeros_v[...] = jnp.zeros_like(zeros_v)
        for_each_pad_piece(lambda c: c.start())

    def issue_rows(x_ref):
        n = tb // DISPATCH_PIECES

        def issue(t, carry):
            _row_copy(rowbuf, t, xs_hbm, pos1_ref[t], sem).start(priority=0)
            _row_copy(rowbuf, t, xs_hbm, pos2_ref[t], sem).start(priority=1)
            return carry

        for p in range(DISPATCH_PIECES):
            for j in range(ROW_TILES):
                rowbuf[pl.ds(p * n * ROW_TILES + j, n, stride=ROW_TILES), :] = x_ref[p * n:(p + 1) * n,
                                                                                     j * LANES:(j + 1) * LANES]
            lax.fori_loop(p * n, (p + 1) * n, issue, 0, unroll=8)

    @pl.when(step < nba)
    def _():
        issue_rows(xa_ref)

    @pl.when(step >= nba)
    def _():
        issue_rows(xb_ref)

    pltpu.make_async_copy(xs_hbm.at[_tile_rows(0, 2 * tb), :], xs_hbm.at[_tile_rows(0, 2 * tb), :], sem).wait()

    @pl.when(step == 0)
    def _():
        for_each_pad_piece(lambda c: c.wait())


def _dispatch(xa_rows, xb_rows, pos1, pos2, zstart, zcount, tail, n_slots, *, tb):
    nba, nbb = xa_rows.shape[0] // tb, xb_rows.shape[0] // tb
    kern = functools.partial(_dispatch_kernel, tb=tb, nba=nba)
    grid_spec = pltpu.PrefetchScalarGridSpec(
        num_scalar_prefetch=3, grid=(nba + nbb,),
        in_specs=[pl.BlockSpec((tb,), lambda i, *_: (i,), memory_space=pltpu.SMEM),
                  pl.BlockSpec((tb,), lambda i, *_: (i,), memory_space=pltpu.SMEM),
                  pl.BlockSpec((tb, D_MODEL), lambda i, *_: (jnp.minimum(i, nba - 1), 0)),
                  pl.BlockSpec((tb, D_MODEL), lambda i, *_: (jnp.maximum(i - nba, 0), 0))],
        out_specs=pl.BlockSpec(memory_space=pl.ANY),
        scratch_shapes=[pltpu.VMEM((_PAD_PIECES[0] * ROW_TILES, LANES), F32), pltpu.VMEM((tb * ROW_TILES, LANES), F32),
                        pltpu.SemaphoreType.DMA(()), pltpu.SemaphoreType.DMA(())],
    )
    return pl.pallas_call(
        kern, grid_spec=grid_spec, out_shape=jax.ShapeDtypeStruct((n_slots * ROW_TILES, LANES), F32),
        name="dispatch",
        compiler_params=pltpu.CompilerParams(dimension_semantics=("arbitrary",), vmem_limit_bytes=VMEM_LIMIT),
    )(zstart, zcount, tail, pos1, pos2, xa_rows, xb_rows)


def _expert_kernel(be_ref, src_ref, used_ref, xs_ref, wg_ref, wu_ref, wd_ref, yb_ref, wg_b, wu_b, wd_b):
    i = pl.program_id(0)
    e = be_ref[i]
    prev = be_ref[jnp.maximum(i - 1, 0)]

    @pl.when(jnp.logical_or(i == 0, e != prev))
    def _():
        wg_b[...] = wg_ref[...].astype(BF16)
        wu_b[...] = wu_ref[...].astype(BF16)
        wd_b[...] = wd_ref[...].astype(BF16)

    @pl.when(used_ref[i] != 0)
    def _():
        x = _load_rows(xs_ref, EXPERT_BLOCK).astype(BF16)
        g = _dot(x, wg_b[...])
        u = _dot(x, wu_b[...])
        hid = (g / (1.0 + jnp.exp(-g))) * u
        _store_rows(yb_ref, _dot(hid.astype(BF16), wd_b[...]), EXPERT_BLOCK)

    @pl.when(used_ref[i] == 0)
    def _():
        yb_ref[...] = jnp.zeros_like(yb_ref)


def _experts(xs, blk_e, blk_src, blk_used, w_gate, w_up, w_down, layer):
    n_slots = xs.shape[0] // ROW_TILES
    nblk = n_slots // EXPERT_BLOCK

    def wspec(shape):
        return pl.BlockSpec((None, None) + shape, lambda i, be, src, used: (layer, be[i], 0, 0))

    grid_spec = pltpu.PrefetchScalarGridSpec(
        num_scalar_prefetch=3, grid=(nblk,),
        in_specs=[pl.BlockSpec((EXPERT_BLOCK * ROW_TILES, LANES), lambda i, be, src, used: (src[i], 0)),
                  wspec((D_MODEL, D_EXP)), wspec((D_MODEL, D_EXP)), wspec((D_EXP, D_MODEL))],
        out_specs=pl.BlockSpec((EXPERT_BLOCK * ROW_TILES, LANES), lambda i, be, src, used: (i, 0)),
        scratch_shapes=[pltpu.VMEM((D_MODEL, D_EXP), BF16), pltpu.VMEM((D_MODEL, D_EXP), BF16),
                        pltpu.VMEM((D_EXP, D_MODEL), BF16)],
    )
    return pl.pallas_call(
        _expert_kernel, grid_spec=grid_spec, out_shape=jax.ShapeDtypeStruct((n_slots * ROW_TILES, LANES), F32),
        name="experts",
        compiler_params=pltpu.CompilerParams(dimension_semantics=("arbitrary",), vmem_limit_bytes=VMEM_LIMIT),
    )(blk_e, blk_src, blk_used, xs, w_gate, w_up, w_down)


def _combine_kernel(pos1_ref, pos2_ref, next1_ref, next2_ref, xa_ref, xb_ref, ga_ref, gb_ref, yb_hbm, g_ref, b_ref,
                    outa_ref, outb_ref, ybuf, sems, *, tb, alpha, nba, nsteps):
    step = pl.program_id(0)
    slot = step % 2
    other = 1 - slot

    def issue(p1_ref, p2_ref, to_slot, t):
        _row_copy(yb_hbm, p1_ref[t], ybuf.at[to_slot].at[0], t, sems.at[to_slot]).start(priority=0)
        _row_copy(yb_hbm, p2_ref[t], ybuf.at[to_slot].at[1], t, sems.at[to_slot]).start(priority=1)

    def wait_slot(which):
        for k in range(TOP_K):
            pltpu.make_async_copy(yb_hbm.at[_tile_rows(0, tb), :], ybuf.at[which].at[k], sems.at[which]).wait()

    @pl.when(step == 0)
    def _():
        def body(t, carry):
            issue(pos1_ref, pos2_ref, 0, t)
            return carry
        lax.fori_loop(0, tb, body, 0, unroll=8)

    wait_slot(slot)

    def finish(x1_ref, gcol_ref, out_ref):
        n = tb // COMBINE_PIECES
        for p in range(COMBINE_PIECES):
            for t in range(p * n, (p + 1) * n):
                issue(next1_ref, next2_ref, other, t)
            gates = gcol_ref[p * n:(p + 1) * n, :]
            y = (gates[:, _R_G1:_R_G1 + 1] * _load_rows(ybuf.at[slot].at[0], n, p * n)
                 + gates[:, _R_G2:_R_G2 + 1] * _load_rows(ybuf.at[slot].at[1], n, p * n))
            out_ref[p * n:(p + 1) * n, :] = _layer_norm(alpha * x1_ref[p * n:(p + 1) * n, :] + y, g_ref[...], b_ref[...])

    @pl.when(step < nba)
    def _():
        finish(xa_ref, ga_ref, outa_ref)

    @pl.when(step >= nba)
    def _():
        finish(xb_ref, gb_ref, outb_ref)

    @pl.when(step == nsteps - 1)
    def _():
        wait_slot(other)


def _combine(xa_rows, xb_rows, gcol_a, gcol_b, yb, pos1, pos2, ln2g, ln2b, *, tb, alpha):
    nba, nbb = xa_rows.shape[0] // tb, xb_rows.shape[0] // tb
    nsteps = nba + nbb
    kern = functools.partial(_combine_kernel, tb=tb, alpha=alpha, nba=nba, nsteps=nsteps)

    def first(i):
        return jnp.minimum(i, nba - 1)

    def second(i):
        return jnp.maximum(i - nba, 0)

    def nxt(i):
        return jnp.minimum(i + 1, nsteps - 1)

    smem = functools.partial(pl.BlockSpec, (tb,), memory_space=pltpu.SMEM)
    return pl.pallas_call(
        kern, grid=(nsteps,),
        in_specs=[smem(lambda i: (i,)), smem(lambda i: (i,)), smem(lambda i: (nxt(i),)), smem(lambda i: (nxt(i),)),
                  pl.BlockSpec((tb, D_MODEL), lambda i: (first(i), 0)),
                  pl.BlockSpec((tb, D_MODEL), lambda i: (second(i), 0)),
                  pl.BlockSpec((tb, LANES), lambda i: (first(i), 0)),
                  pl.BlockSpec((tb, LANES), lambda i: (second(i), 0)),
                  pl.BlockSpec(memory_space=pl.ANY),
                  pl.BlockSpec((1, D_MODEL), lambda i: (0, 0)),
                  pl.BlockSpec((1, D_MODEL), lambda i: (0, 0))],
        out_specs=[pl.BlockSpec((tb, D_MODEL), lambda i: (first(i), 0)),
                   pl.BlockSpec((tb, D_MODEL), lambda i: (second(i), 0))],
        out_shape=[jax.ShapeDtypeStruct((nba * tb, D_MODEL), F32), jax.ShapeDtypeStruct((nbb * tb, D_MODEL), F32)],
        scratch_shapes=[pltpu.VMEM((2, TOP_K, tb * ROW_TILES, LANES), F32), pltpu.SemaphoreType.DMA((2,))],
        name="combine",
        compiler_params=pltpu.CompilerParams(dimension_semantics=("arbitrary",), vmem_limit_bytes=VMEM_LIMIT),
    )(pos1, pos2, pos1, pos2, xa_rows, xb_rows, gcol_a, gcol_b, yb, ln2g, ln2b)


def _alibi_bias():
    slopes = np.asarray([2.0 ** (-8.0 * (h + 1) / N_HEADS) for h in range(N_HEADS)], np.float32)
    qi = np.arange(CHUNK, dtype=np.int32)[:, None]
    sj = np.arange(KEYS, dtype=np.int32)[None, :]
    dist = np.abs(qi + WINDOW - sj).astype(np.float32)
    bias = -slopes.reshape(N_KV_HEADS, GROUP, 1, 1) * dist
    bias = np.transpose(bias, (1, 2, 0, 3)).reshape(GROUP * CHUNK, N_KV_HEADS * KEYS)
    return jnp.asarray(bias, F32)


def _key_selector():
    row_head = np.arange(N_KV_HEADS * KEYS)[:, None] // KEYS
    lane_head = np.arange(D_KV)[None, :] // HEAD_DIM
    return jnp.asarray((row_head == lane_head).astype(np.float32), BF16)


def _strict_upper():
    r = np.arange(RANK_CHUNK)
    return jnp.asarray((r[:, None] < r[None, :]).astype(np.float32), BF16)


def _slot_plan(route, counts, n_tok):
    nblk = (n_tok * TOP_K) // EXPERT_BLOCK + N_EXPERTS
    experts = route[_R_E1:_R_E2 + 1].astype(jnp.int32)
    ranks = route[_R_RANK1:_R_RANK2 + 1].astype(jnp.int32)
    cnt = counts[:, 0].astype(jnp.int32)
    padded = (cnt + EXPERT_BLOCK - 1) // EXPERT_BLOCK * EXPERT_BLOCK
    pad_end = jnp.cumsum(padded)
    pad_start = pad_end - padded
    ids = jnp.arange(N_EXPERTS, dtype=jnp.int32)
    pos = jnp.sum(jnp.where(experts[..., None] == ids, pad_start, 0), axis=-1) + ranks
    blk_first = jnp.arange(nblk, dtype=jnp.int32) * EXPERT_BLOCK
    blk_e = jnp.minimum(jnp.sum((pad_end[None, :] <= blk_first[:, None]).astype(jnp.int32), axis=-1), N_EXPERTS - 1)
    blk_used = (blk_first < pad_end[-1]).astype(jnp.int32)
    blk_src = jnp.minimum(jnp.arange(nblk, dtype=jnp.int32), pad_end[-1] // EXPERT_BLOCK - 1)
    tail = jnp.stack([pad_end[-1], nblk - pad_end[-1] // EXPERT_BLOCK]).astype(jnp.int32)
    return pos[0], pos[1], pad_start + cnt, padded - cnt, tail, blk_e, blk_src, blk_used, nblk * EXPERT_BLOCK


def _layer(xp, xs, cache, mix_prm, consts, moe_prm, layer, *, tl_prompt, nseq_sample, alpha):
    (bp, tp, _), (bs, ts, _) = xp.shape, xs.shape
    np_tok, ns_tok = bp * tp, bs * ts
    tb = ROWS
    assert np_tok % tb == 0 and ns_tok % tb == 0
    zeros_conv = jnp.zeros((bp, CONV_W - 1, D_CONV), F32)
    zeros_kv = jnp.zeros((bp, WINDOW, D_KV), F32)
    zero_counts = jnp.zeros((N_EXPERTS, LANES), F32)
    x1p, route_p, gcol_p, counts_p, *state_p = _mixer(
        xp, zeros_conv, zeros_kv, zeros_kv, zero_counts, mix_prm, consts,
        nseq=1, tl=tl_prompt, mask_history=True, alpha=alpha)
    x1s, route_s, gcol_s, counts, *state_s = _mixer(
        xs, *cache, counts_p, mix_prm, consts, nseq=nseq_sample, tl=ts, mask_history=False, alpha=alpha)
    route = jnp.concatenate([route_p, route_s], axis=1)
    pos1, pos2, zstart, zcount, tail, blk_e, blk_src, blk_used, n_slots = _slot_plan(route, counts, np_tok + ns_tok)
    tb_dispatch = DISPATCH_ROWS if np_tok % DISPATCH_ROWS == 0 and ns_tok % DISPATCH_ROWS == 0 else tb
    sorted_rows = _dispatch(x1p, x1s, pos1, pos2, zstart, zcount, tail, n_slots, tb=tb_dispatch)
    w_gate, w_up, w_down, ln2g, ln2b = moe_prm
    yb = _experts(sorted_rows, blk_e, blk_src, blk_used, w_gate, w_up, w_down, layer)
    out_p, out_s = _combine(x1p, x1s, gcol_p, gcol_s, yb, pos1, pos2, ln2g, ln2b, tb=tb, alpha=alpha)
    return out_p.reshape(bp, tp, D_MODEL), out_s.reshape(bs, ts, D_MODEL), state_p, state_s


def kernel(x_prompt, x_sample, cache_conv, cache_k, cache_v, w_in, conv_w, conv_b, attn_sinks, g_conv, g_attn,
           w_out, ln1_g, ln1_b, router_group_w, router_group_b, router_expert_w, router_expert_b,
           expert_w_gate, expert_w_up, expert_w_down, ln2_g, ln2_b):
    depth = w_in.shape[0]
    alpha = (2 * depth) ** 0.25
    batch, seq = x_prompt.shape[0], x_prompt.shape[1]
    dec_batch, dec_seq = x_sample.shape[0], x_sample.shape[1]
    assert dec_seq == CHUNK and seq % CHUNK == 0
    consts = (_alibi_bias(), _strict_upper(), _key_selector())
    tl_prompt = min(MIXER_ROWS, seq)
    nseq_sample = min(MIXER_ROWS // dec_seq, dec_batch)
    assert seq % tl_prompt == 0 and dec_batch % nseq_sample == 0

    def regroup_heads(a, axis):
        shape = a.shape
        a = a.reshape(shape[:axis] + (N_KV_HEADS, GROUP, HEAD_DIM) + shape[axis + 1:])
        return jnp.swapaxes(a, axis, axis + 1).reshape(shape)

    w_in_all = jnp.concatenate([w_in[:, :, :_OFF_Q], regroup_heads(w_in[:, :, _OFF_Q:_OFF_K], 2), w_in[:, :, _OFF_K:]],
                               axis=2).astype(BF16)
    w_out_all = jnp.concatenate([w_out[:, :D_CONV], regroup_heads(w_out[:, D_CONV:], 1)], axis=1).astype(BF16)
    g_attn_all = regroup_heads(g_attn, 1)
    router_pad = LANES - N_EXPERTS - N_GROUPS
    wr_all = jnp.concatenate([router_expert_w, router_group_w, jnp.zeros((depth, D_MODEL, router_pad), F32)],
                             axis=2).astype(BF16)
    br_all = jnp.concatenate([router_expert_b, router_group_b, jnp.zeros((depth, router_pad), F32)], axis=1)
    sink_all = jnp.broadcast_to(jnp.repeat(attn_sinks.astype(F32), CHUNK, axis=1).reshape(depth, N_KV_HEADS, QROWS, 1),
                                (depth, N_KV_HEADS, QROWS, LANES))

    xp, xs = x_prompt, x_sample
    states = [[] for _ in range(6)]
    for l in range(depth):
        mix_prm = (w_in_all[l], conv_w[l], conv_b[l].reshape(1, D_CONV), sink_all[l],
                   g_conv[l].reshape(1, D_CONV), g_attn_all[l].reshape(1, D_ATTN), w_out_all[l],
                   ln1_g[l].reshape(1, D_MODEL), ln1_b[l].reshape(1, D_MODEL), wr_all[l], br_all[l].reshape(1, LANES))
        moe_prm = (expert_w_gate, expert_w_up, expert_w_down,
                   ln2_g[l].reshape(1, D_MODEL), ln2_b[l].reshape(1, D_MODEL))
        cache = (cache_conv[l], cache_k[l].reshape(dec_batch, WINDOW, D_KV), cache_v[l].reshape(dec_batch, WINDOW, D_KV))
        xp, xs, state_p, state_s = _layer(xp, xs, cache, mix_prm, consts, moe_prm, l,
                                          tl_prompt=tl_prompt, nseq_sample=nseq_sample, alpha=alpha)
        for lst, val in zip(states, state_p + state_s):
            lst.append(val)

    def kv(lst, nb):
        return jnp.stack(lst).reshape(depth, nb, WINDOW, N_KV_HEADS, HEAD_DIM)

    return (xp, xs, jnp.stack(states[0]), kv(states[1], batch), kv(states[2], batch),
            jnp.stack(states[3]), kv(states[4], dec_batch), kv(states[5], dec_batch))
```

```python
import functools

import numpy as np
import jax
import jax.numpy as jnp
from jax import lax
from jax.experimental import pallas as pl
from jax.experimental.pallas import tpu as pltpu

D_MODEL = 1024
D_CONV = 512
CONV_W = 3
N_HEADS = 8
N_KV_HEADS = 2
GROUP = N_HEADS // N_KV_HEADS
HEAD_DIM = 64
D_ATTN = N_HEADS * HEAD_DIM
D_KV = N_KV_HEADS * HEAD_DIM
WINDOW = 128
CHUNK = 64
KEYS = WINDOW + CHUNK
QROWS = GROUP * CHUNK
N_GROUPS = 4
EXP_PER_GROUP = 8
N_EXPERTS = N_GROUPS * EXP_PER_GROUP
TOP_K = 2
D_EXP = 512
LN_EPS = 1e-5
NEG_INF = -1e30

SUBLANES = 8
LANES = 128
ROW_TILES = D_MODEL // LANES
assert ROW_TILES == SUBLANES
ROWS = 512
MIXER_ROWS = 512
DISPATCH_ROWS = 1024
COMBINE_PIECES = 16
DISPATCH_PIECES = 8
EXPERT_BLOCK = 512
RANK_CHUNK = 256
SOFTMAX_ROWS = 64
PROJ_CHUNK = 2 * LANES
MIX_PIECES = 4
VMEM_LIMIT = 56 * 1024 * 1024

F32 = jnp.float32
BF16 = jnp.bfloat16

_OFF_B, _OFF_C, _OFF_H = 0, D_CONV, 2 * D_CONV
_OFF_Q = 3 * D_CONV
_OFF_K = _OFF_Q + D_ATTN
_OFF_V = _OFF_K + D_KV
D_IN = _OFF_V + D_KV

_R_E1, _R_E2, _R_RANK1, _R_RANK2, _R_G1, _R_G2 = 0, 1, 2, 3, 4, 5


def _load_rows(ref, n, first=0):
    return jnp.concatenate([ref[pl.ds(first * ROW_TILES + j, n, stride=ROW_TILES), :] for j in range(ROW_TILES)],
                           axis=1)


def _store_rows(ref, val, n):
    for j in range(ROW_TILES):
        ref[pl.ds(j, n, stride=ROW_TILES), :] = val[:, j * LANES:(j + 1) * LANES]


def _dot(a, b):
    return jnp.dot(a, b, preferred_element_type=F32)


def _rms_norm(x, g):
    return x * lax.rsqrt(jnp.mean(jnp.square(x), -1, keepdims=True) + LN_EPS) * g


def _layer_norm(x, g, b):
    mu = jnp.mean(x, -1, keepdims=True)
    xc = x - mu
    var = jnp.mean(jnp.square(xc), -1, keepdims=True)
    return xc * lax.rsqrt(var + LN_EPS) * g + b


def _mixer_kernel(x_ref, cconv_ref, ck_ref, cv_ref, w_in_ref, convw_ref, convb_ref, abias_ref,
                  sink_ref, gconv_ref, gattn_ref, w_out_ref, ln1g_ref, ln1b_ref, wr_ref, br_ref,
                  tri_ref, keysel_ref, counts_in_ref,
                  x1_ref, route_ref, gcol_ref, counts_ref, sconv_ref, sk_ref, sv_ref,
                  uext, kext, vext, cnt_s, s_ref, e_ref, sinkden_ref, m0_ref, m1_ref, proj_ref, x1b_s, xprev_s,
                  mixed_s, *, nseq, tl, alpha, mask_history, defer_tail):
    b = pl.program_id(0)
    s = pl.program_id(1)
    rows = nseq * tl
    upitch = tl + SUBLANES
    kpitch = WINDOW + tl

    @pl.when(jnp.logical_and(b == 0, s == 0))
    def _():
        cnt_s[...] = counts_in_ref[:, 0:1]
        if defer_tail:
            xprev_s[...] = jnp.zeros_like(xprev_s)
            mixed_s[...] = jnp.zeros_like(mixed_s)

    @pl.when(s == 0)
    def _():
        for j in range(nseq):
            uext[j * upitch + SUBLANES - 2:j * upitch + SUBLANES, :] = cconv_ref[j]
            kext[j * kpitch:j * kpitch + WINDOW, :] = ck_ref[j]
            vext[j * kpitch:j * kpitch + WINDOW, :] = cv_ref[j]

    def norm_rows(r0, r1, x_rows, mixed_rows):
        x1 = _layer_norm(alpha * x_rows + mixed_rows, ln1g_ref[...], ln1b_ref[...])
        x1_ref[r0:r1, :] = x1
        x1b_s[r0:r1, :] = x1.astype(BF16)

    def route(valid):
        logits_t = (_dot(x1b_s[...], wr_ref[...]) + br_ref[...]).T
        sub = lax.broadcasted_iota(jnp.int32, (SUBLANES, rows), 0)
        gl = jnp.where(sub < N_GROUPS, logits_t[N_EXPERTS:N_EXPERTS + SUBLANES, :], -jnp.inf)
        gmax = jnp.max(gl, axis=0, keepdims=True)
        grp = jnp.min(jnp.where(gl == gmax, sub, SUBLANES), axis=0, keepdims=True)
        p_grp = 1.0 / jnp.sum(jnp.exp(gl - gmax), axis=0, keepdims=True)
        el = logits_t[(N_GROUPS - 1) * EXP_PER_GROUP:N_GROUPS * EXP_PER_GROUP, :]
        for g in range(N_GROUPS - 2, -1, -1):
            el = jnp.where(grp == g, logits_t[g * EXP_PER_GROUP:(g + 1) * EXP_PER_GROUP, :], el)
        v1 = jnp.max(el, axis=0, keepdims=True)
        i1 = jnp.min(jnp.where(el == v1, sub, SUBLANES), axis=0, keepdims=True)
        el2 = jnp.where(sub == i1, -jnp.inf, el)
        v2 = jnp.max(el2, axis=0, keepdims=True)
        i2 = jnp.min(jnp.where(el2 == v2, sub, SUBLANES), axis=0, keepdims=True)
        e2 = jnp.exp(v2 - v1)
        gate1 = p_grp * (1.0 / (1.0 + e2))
        gate2 = p_grp * (e2 / (1.0 + e2))

        chosen = jnp.logical_or(sub == i1, sub == i2)
        onehot = jnp.concatenate(
            [jnp.where(jnp.logical_and(grp == g, chosen), 1.0, 0.0) for g in range(N_GROUPS)], axis=0)
        running = cnt_s[...]
        ranks = []
        for c in range(rows // RANK_CHUNK):
            oh = onehot[:, c * RANK_CHUNK:(c + 1) * RANK_CHUNK]
            ranks.append(_dot(oh.astype(BF16), tri_ref[...]) + running)
            running = running + jnp.sum(oh, axis=1, keepdims=True) * valid
        rank = jnp.concatenate(ranks, axis=1)
        cnt_s[...] = running
        counts_ref[...] = jnp.broadcast_to(running, (N_EXPERTS, LANES))

        ex1 = grp * EXP_PER_GROUP + i1
        ex2 = grp * EXP_PER_GROUP + i2
        erow = lax.broadcasted_iota(jnp.int32, (N_EXPERTS, rows), 0)
        rank1 = jnp.sum(jnp.where(erow == ex1, rank, 0.0), axis=0, keepdims=True)
        rank2 = jnp.sum(jnp.where(erow == ex2, rank, 0.0), axis=0, keepdims=True)
        fields = {_R_E1: ex1.astype(F32), _R_E2: ex2.astype(F32), _R_RANK1: rank1, _R_RANK2: rank2,
                  _R_G1: gate1, _R_G2: gate2}
        record = jnp.zeros((SUBLANES, rows), F32)
        for r, val in fields.items():
            record = jnp.where(sub == r, val, record)
        route_ref[...] = record
        padded = jnp.concatenate([record, jnp.zeros((LANES - SUBLANES, rows), F32)], axis=0)
        gcol_ref[...] = padded.T

    x = x_ref[...].reshape(rows, D_MODEL)
    xb = x.astype(BF16)

    def conv_rows(r0, r1):
        w0, w1, w2 = convw_ref[0:1, :], convw_ref[1:2, :], convw_ref[2:3, :]
        out = []
        for j in range(r0 // tl, -(-r1 // tl)):
            a, b_ = max(r0, j * tl) - j * tl, min(r1, (j + 1) * tl) - j * tl
            base = j * upitch + SUBLANES
            src = pl.ds(j * tl + a, b_ - a)
            uext[base + a:base + b_, :] = proj_ref[src, _OFF_C:_OFF_C + D_CONV] * proj_ref[src, _OFF_H:_OFF_H + D_CONV]
            yc = convb_ref[...] + uext[base + a - 2:base + b_ - 2, :] * w0
            yc = yc + uext[base + a - 1:base + b_ - 1, :] * w1
            yc = yc + uext[base + a:base + b_, :] * w2
            out.append(_rms_norm(proj_ref[src, _OFF_B:_OFF_B + D_CONV] * yc, gconv_ref[...]).astype(BF16))
            if b_ == tl:
                tail = uext[base + tl - 2:base + tl, :]
                sconv_ref[j] = tail
                uext[base - 2:base, :] = tail
        return out[0] if len(out) == 1 else jnp.concatenate(out, axis=0)

    qkv = []
    qkv_chunks = [(c, c + PROJ_CHUNK) for c in range(_OFF_Q, D_IN, PROJ_CHUNK)]
    tail_rows = -(-rows // len(qkv_chunks) // SUBLANES) * SUBLANES
    for p, (c0, c1) in enumerate(qkv_chunks):
        r0, r1 = min(p * tail_rows, rows), min((p + 1) * tail_rows, rows)
        if defer_tail and r1 > r0:
            norm_rows(r0, r1, xprev_s[r0:r1, :], mixed_s[r0:r1, :])
        qkv.append(_dot(xb, w_in_ref[:, c0:c1]))
    qkv = jnp.concatenate(qkv, axis=1)
    if defer_tail:
        route((s > 0).astype(F32))
    q = (qkv[:, 0:D_ATTN] * (HEAD_DIM ** -0.5)).astype(BF16)
    k = qkv[:, D_ATTN:D_ATTN + D_KV]
    v = qkv[:, D_ATTN + D_KV:D_ATTN + 2 * D_KV]
    for j in range(nseq):
        kext[j * kpitch + WINDOW:(j + 1) * kpitch, :] = k[j * tl:(j + 1) * tl]
        vext[j * kpitch + WINDOW:(j + 1) * kpitch, :] = v[j * tl:(j + 1) * tl]

    nchunk = rows // CHUNK
    assert nseq == 1 or tl == CHUNK
    key_stride = CHUNK if nseq == 1 else kpitch
    head0 = lax.broadcasted_iota(jnp.int32, (1, D_KV), 1) < HEAD_DIM

    def windows(ext_ref):
        ext = ext_ref[...].astype(BF16)
        heads = (jnp.where(head0, ext, jnp.zeros_like(ext)), jnp.where(head0, jnp.zeros_like(ext), ext))
        return jnp.stack([jnp.concatenate([h[c * key_stride:c * key_stride + KEYS] for h in heads], axis=0)
                          for c in range(nchunk)])

    q3 = jnp.stack([jnp.concatenate([q[c * CHUNK:(c + 1) * CHUNK, g * D_KV:(g + 1) * D_KV] for g in range(GROUP)], axis=0)
                    for c in range(nchunk)])
    logits = lax.dot_general(q3, windows(kext), (((2,), (2,)), ((0,), (0,))), preferred_element_type=F32)
    s_ref[...] = logits.reshape(nchunk * QROWS, 2 * KEYS)

    assert 2 * KEYS == 3 * LANES
    mid0 = lax.broadcasted_iota(jnp.int32, (1, LANES), 1) < KEYS - LANES
    colk = lax.broadcasted_iota(jnp.int32, (1, 2 * KEYS), 1)
    key = jnp.where(colk < KEYS, colk, colk - KEYS)

    def row_block(i):
        r = i * SOFTMAX_ROWS
        return pl.ds(r, SOFTMAX_ROWS), pl.ds(r % QROWS, SOFTMAX_ROWS), r // QROWS

    def tiles(lg):
        return lg[:, 0:LANES], lg[:, LANES:2 * LANES], lg[:, 2 * LANES:3 * LANES]

    def max_rows(i, masked):
        rows_i, qrows_i, chunk_i = row_block(i)
        lg = s_ref[rows_i, :] + abias_ref[qrows_i, :]
        if masked:
            lg = lg + jnp.where(s * tl + chunk_i * CHUNK - WINDOW + key < 0, NEG_INF, 0.0).astype(F32)
        s_ref[rows_i, :] = lg
        t0, t1, t2 = tiles(lg)
        m0_ref[rows_i, :] = jnp.maximum(jnp.max(jnp.maximum(t0, jnp.where(mid0, t1, NEG_INF)), -1, keepdims=True),
                                        sink_ref[0, qrows_i, :])
        m1_ref[rows_i, :] = jnp.maximum(jnp.max(jnp.maximum(t2, jnp.where(mid0, NEG_INF, t1)), -1, keepdims=True),
                                        sink_ref[1, qrows_i, :])

    def exp_rows(i):
        rows_i, qrows_i, _ = row_block(i)
        t0, t1, t2 = tiles(s_ref[rows_i, :])
        m0, m1 = m0_ref[rows_i, :], m1_ref[rows_i, :]
        e = jnp.concatenate([jnp.exp(t0 - m0), jnp.exp(t1 - jnp.where(mid0, m0, m1)), jnp.exp(t2 - m1)], axis=-1)
        e_ref[rows_i, :] = e.astype(BF16)
        sinkden_ref[rows_i, :] = jnp.where(head0, jnp.exp(sink_ref[0, qrows_i, :] - m0),
                                           jnp.exp(sink_ref[1, qrows_i, :] - m1))

    n_it = nchunk * QROWS // SOFTMAX_ROWS
    n_masked = min(WINDOW // CHUNK, nchunk) * QROWS // SOFTMAX_ROWS if mask_history else 0
    proj_chunks = [(c, min(c + PROJ_CHUNK, _OFF_Q)) for c in range(0, _OFF_Q, PROJ_CHUNK)]
    piece = -(-2 * n_it // len(proj_chunks))
    steps = [functools.partial(max_rows, i, i < n_masked) for i in range(n_it)]
    steps += [functools.partial(exp_rows, i) for i in range(n_it)]
    for p, (c0, c1) in enumerate(proj_chunks):
        for step_fn in steps[p * piece:(p + 1) * piece]:
            step_fn()
        proj_ref[:, c0:c1] = _dot(xb, w_in_ref[:, c0:c1])
    for step_fn in steps[len(proj_chunks) * piece:]:
        step_fn()
    vsel = jnp.concatenate([windows(vext), jnp.broadcast_to(keysel_ref[...], (nchunk, 2 * KEYS, D_KV))], axis=-1)
    n_pieces = min(MIX_PIECES, nchunk)
    cpp = nchunk // n_pieces
    conv_parts, attn_parts = [], []
    for p in range(n_pieces):
        cs = slice(p * cpp, (p + 1) * cpp)
        o2 = lax.dot_general(e_ref[p * cpp * QROWS:(p + 1) * cpp * QROWS, :].reshape(cpp, QROWS, 2 * KEYS), vsel[cs],
                             (((2,), (1,)), ((0,), (0,))), preferred_element_type=F32)
        sinkden = sinkden_ref[p * cpp * QROWS:(p + 1) * cpp * QROWS, :].reshape(cpp, QROWS, D_KV)
        o = o2[..., 0:D_KV] * (1.0 / (o2[..., D_KV:2 * D_KV] + sinkden))
        attn_parts += [jnp.concatenate([o[c, g * CHUNK:(g + 1) * CHUNK, :] for g in range(GROUP)], axis=1)
                       for c in range(cpp)]
        conv_parts.append(conv_rows(p * cpp * CHUNK, (p + 1) * cpp * CHUNK))
    y_attn = jnp.concatenate(attn_parts, axis=0)
    n_conv = jnp.concatenate(conv_parts, axis=0)

    for j in range(nseq):
        sk_ref[j] = kext[j * kpitch + tl:(j + 1) * kpitch, :]
        sv_ref[j] = vext[j * kpitch + tl:(j + 1) * kpitch, :]
    if nseq == 1:
        kext[0:WINDOW, :] = kext[tl:tl + WINDOW, :]
        vext[0:WINDOW, :] = vext[tl:tl + WINDOW, :]

    n_attn = _rms_norm(y_attn, gattn_ref[...]).astype(BF16)
    mixed = _dot(n_attn, w_out_ref[D_CONV:D_CONV + D_ATTN, :]) + _dot(n_conv, w_out_ref[0:D_CONV, :])
    if defer_tail:
        xprev_s[...] = x
        mixed_s[...] = mixed
    else:
        norm_rows(0, rows, x, mixed)
        route(1.0)


def _mixer(x, cconv, ck, cv, counts_in, prm, consts, *, nseq, tl, mask_history, alpha):
    nb_total, t_total = x.shape[0], x.shape[1]
    nb, ns = nb_total // nseq, t_total // tl
    defer_tail = ns > 1
    steps = ns + 1 if defer_tail else ns

    def in_blk(s):
        return jnp.minimum(s, ns - 1)

    def out_blk(b, s):
        return b * ns + (jnp.maximum(s - 1, 0) if defer_tail else s)

    rows = nseq * tl
    n_tok = nb_total * t_total
    w_in, convw, convb, sinks, gconv, gattn, w_out, ln1g, ln1b, wr, br = prm
    abias, tri, keysel = consts

    def full(a):
        return pl.BlockSpec(a.shape, lambda b, s, _n=a.ndim: (0,) * _n)

    def seq_state(width, nrows):
        return pl.BlockSpec((nseq, nrows, width), lambda b, s: (b, 0, 0))

    in_specs = [
        pl.BlockSpec((nseq, tl, D_MODEL), lambda b, s: (b, in_blk(s), 0)),
        seq_state(D_CONV, CONV_W - 1), seq_state(D_KV, WINDOW), seq_state(D_KV, WINDOW),
        full(w_in), full(convw), full(convb), full(abias), full(sinks), full(gconv), full(gattn),
        full(w_out), full(ln1g), full(ln1b), full(wr), full(br), full(tri), full(keysel), full(counts_in),
    ]
    out_shape = [
        jax.ShapeDtypeStruct((n_tok, D_MODEL), F32),
        jax.ShapeDtypeStruct((SUBLANES, n_tok), F32),
        jax.ShapeDtypeStruct((n_tok, LANES), F32),
        jax.ShapeDtypeStruct((N_EXPERTS, LANES), F32),
        jax.ShapeDtypeStruct((nb_total, CONV_W - 1, D_CONV), F32),
        jax.ShapeDtypeStruct((nb_total, WINDOW, D_KV), F32),
        jax.ShapeDtypeStruct((nb_total, WINDOW, D_KV), F32),
    ]
    out_specs = [
        pl.BlockSpec((rows, D_MODEL), lambda b, s: (out_blk(b, s), 0)),
        pl.BlockSpec((SUBLANES, rows), lambda b, s: (0, out_blk(b, s))),
        pl.BlockSpec((rows, LANES), lambda b, s: (out_blk(b, s), 0)),
        pl.BlockSpec((N_EXPERTS, LANES), lambda b, s: (0, 0)),
        seq_state(D_CONV, CONV_W - 1), seq_state(D_KV, WINDOW), seq_state(D_KV, WINDOW),
    ]
    scratch = [
        pltpu.VMEM((nseq * (tl + SUBLANES), D_CONV), F32),
        pltpu.VMEM((nseq * (WINDOW + tl), D_KV), F32),
        pltpu.VMEM((nseq * (WINDOW + tl), D_KV), F32),
        pltpu.VMEM((N_EXPERTS, 1), F32),
        pltpu.VMEM((rows // CHUNK * QROWS, 2 * KEYS), F32),
        pltpu.VMEM((rows // CHUNK * QROWS, 2 * KEYS), BF16),
        pltpu.VMEM((rows // CHUNK * QROWS, D_KV), F32),
        pltpu.VMEM((rows // CHUNK * QROWS, LANES), F32),
        pltpu.VMEM((rows // CHUNK * QROWS, LANES), F32),
        pltpu.VMEM((rows, _OFF_Q), F32),
        pltpu.VMEM((rows, D_MODEL), BF16),
        pltpu.VMEM((rows, D_MODEL) if defer_tail else (SUBLANES, LANES), F32),
        pltpu.VMEM((rows, D_MODEL) if defer_tail else (SUBLANES, LANES), F32),
    ]
    kern = functools.partial(_mixer_kernel, nseq=nseq, tl=tl, alpha=alpha, mask_history=mask_history,
                             defer_tail=defer_tail)
    return pl.pallas_call(
        kern, grid=(nb, steps), in_specs=in_specs, out_specs=out_specs, out_shape=out_shape,
        scratch_shapes=scratch, name="mixer",
        compiler_params=pltpu.CompilerParams(dimension_semantics=("arbitrary", "arbitrary"),
                                             vmem_limit_bytes=VMEM_LIMIT),
    )(x, cconv, ck, cv, w_in, convw, convb, abias, sinks, gconv, gattn, w_out, ln1g, ln1b, wr, br, tri, keysel, counts_in)


_PAD_PIECES = tuple(2 ** k for k in range(EXPERT_BLOCK.bit_length() - 2, -1, -1))


def _tile_rows(row, n=1):
    if isinstance(row, int):
        return pl.ds(row * ROW_TILES, n * ROW_TILES)
    return pl.ds(pl.multiple_of(row * ROW_TILES, ROW_TILES), n * ROW_TILES)


def _row_copy(src, src_row, dst, dst_row, sem):
    return pltpu.make_async_copy(src.at[_tile_rows(src_row), :], dst.at[_tile_rows(dst_row), :], sem)


def _dispatch_kernel(zstart_ref, zcount_ref, tail_ref, pos1_ref, pos2_ref, xa_ref, xb_ref, xs_hbm, zeros_v, rowbuf,
                     sem, zsem, *, tb, nba):
    step = pl.program_id(0)

    def zero_copy(start, piece):
        return pltpu.make_async_copy(zeros_v.at[_tile_rows(0, piece), :], xs_hbm.at[_tile_rows(start, piece), :], zsem)

    def pad_copy(e, piece, taken):
        return zero_copy(zstart_ref[e] + taken, piece)

    def for_each_pad_piece(fn):
        def per_tail_block(j, carry):
            for part in range(EXPERT_BLOCK // _PAD_PIECES[0]):
                fn(zero_copy(tail_ref[0] + j * EXPERT_BLOCK + part * _PAD_PIECES[0], _PAD_PIECES[0]))
            return carry

        lax.fori_loop(0, tail_ref[1], per_tail_block, 0)

        def per_expert(e, carry):
            count = zcount_ref[e]
            taken = 0
            for piece in _PAD_PIECES:
                present = (count & piece) != 0

                @pl.when(present)
                def _(piece=piece, taken=taken):
                    fn(pad_copy(e, piece, taken))

                taken = taken + jnp.where(present, piece, 0)
            return carry

        lax.fori_loop(0, N_EXPERTS, per_expert, 0)

    @pl.when(step == 0)
    def _():
        zeros_v[...] = jnp.zeros_like(zeros_v)
        for_each_pad_piece(lambda c: c.start())

    def issue_rows(x_ref):
        n = tb // DISPATCH_PIECES

        def issue(t, carry):
            _row_copy(rowbuf, t, xs_hbm, pos1_ref[t], sem).start(priority=0)
            _row_copy(rowbuf, t, xs_hbm, pos2_ref[t], sem).start(priority=1)
            return carry

        for p in range(DISPATCH_PIECES):
            for j in range(ROW_TILES):
                rowbuf[pl.ds(p * n * ROW_TILES + j, n, stride=ROW_TILES), :] = x_ref[p * n:(p + 1) * n,
                                                                                     j * LANES:(j + 1) * LANES]
            lax.fori_loop(p * n, (p + 1) * n, issue, 0, unroll=8)

    @pl.when(step < nba)
    def _():
        issue_rows(xa_ref)

    @pl.when(step >= nba)
    def _():
        issue_rows(xb_ref)

    pltpu.make_async_copy(xs_hbm.at[_tile_rows(0, 2 * tb), :], xs_hbm.at[_tile_rows(0, 2 * tb), :], sem).wait()

    @pl.when(step == 0)
    def _():
        for_each_pad_piece(lambda c: c.wait())


def _dispatch(xa_rows, xb_rows, pos1, pos2, zstart, zcount, tail, n_slots, *, tb):
    nba, nbb = xa_rows.shape[0] // tb, xb_rows.shape[0] // tb
    kern = functools.partial(_dispatch_kernel, tb=tb, nba=nba)
    grid_spec = pltpu.PrefetchScalarGridSpec(
        num_scalar_prefetch=3, grid=(nba + nbb,),
        in_specs=[pl.BlockSpec((tb,), lambda i, *_: (i,), memory_space=pltpu.SMEM),
                  pl.BlockSpec((tb,), lambda i, *_: (i,), memory_space=pltpu.SMEM),
                  pl.BlockSpec((tb, D_MODEL), lambda i, *_: (jnp.minimum(i, nba - 1), 0)),
                  pl.BlockSpec((tb, D_MODEL), lambda i, *_: (jnp.maximum(i - nba, 0), 0))],
        out_specs=pl.BlockSpec(memory_space=pl.ANY),
        scratch_shapes=[pltpu.VMEM((_PAD_PIECES[0] * ROW_TILES, LANES), F32), pltpu.VMEM((tb * ROW_TILES, LANES), F32),
                        pltpu.SemaphoreType.DMA(()), pltpu.SemaphoreType.DMA(())],
    )
    return pl.pallas_call(
        kern, grid_spec=grid_spec, out_shape=jax.ShapeDtypeStruct((n_slots * ROW_TILES, LANES), F32),
        name="dispatch",
        compiler_params=pltpu.CompilerParams(dimension_semantics=("arbitrary",), vmem_limit_bytes=VMEM_LIMIT),
    )(zstart, zcount, tail, pos1, pos2, xa_rows, xb_rows)


def _expert_kernel(be_ref, src_ref, used_ref, xs_ref, wg_ref, wu_ref, wd_ref, yb_ref, wg_b, wu_b, wd_b):
    i = pl.program_id(0)
    e = be_ref[i]
    prev = be_ref[jnp.maximum(i - 1, 0)]

    @pl.when(jnp.logical_or(i == 0, e != prev))
    def _():
        wg_b[...] = wg_ref[...].astype(BF16)
        wu_b[...] = wu_ref[...].astype(BF16)
        wd_b[...] = wd_ref[...].astype(BF16)

    @pl.when(used_ref[i] != 0)
    def _():
        x = _load_rows(xs_ref, EXPERT_BLOCK).astype(BF16)
        y = None
        for c in range(0, D_EXP, PROJ_CHUNK):
            g = _dot(x, wg_b[:, c:c + PROJ_CHUNK])
            u = _dot(x, wu_b[:, c:c + PROJ_CHUNK])
            hid = ((g / (1.0 + jnp.exp(-g))) * u).astype(BF16)
            part = _dot(hid, wd_b[c:c + PROJ_CHUNK, :])
            y = part if y is None else y + part
        _store_rows(yb_ref, y, EXPERT_BLOCK)

    @pl.when(used_ref[i] == 0)
    def _():
        yb_ref[...] = jnp.zeros_like(yb_ref)


def _experts(xs, blk_e, blk_src, blk_used, w_gate, w_up, w_down, layer):
    n_slots = xs.shape[0] // ROW_TILES
    nblk = n_slots // EXPERT_BLOCK

    def wspec(shape):
        return pl.BlockSpec((None, None) + shape, lambda i, be, src, used: (layer, be[i], 0, 0))

    grid_spec = pltpu.PrefetchScalarGridSpec(
        num_scalar_prefetch=3, grid=(nblk,),
        in_specs=[pl.BlockSpec((EXPERT_BLOCK * ROW_TILES, LANES), lambda i, be, src, used: (src[i], 0)),
                  wspec((D_MODEL, D_EXP)), wspec((D_MODEL, D_EXP)), wspec((D_EXP, D_MODEL))],
        out_specs=pl.BlockSpec((EXPERT_BLOCK * ROW_TILES, LANES), lambda i, be, src, used: (i, 0)),
        scratch_shapes=[pltpu.VMEM((D_MODEL, D_EXP), BF16), pltpu.VMEM((D_MODEL, D_EXP), BF16),
                        pltpu.VMEM((D_EXP, D_MODEL), BF16)],
    )
    return pl.pallas_call(
        _expert_kernel, grid_spec=grid_spec, out_shape=jax.ShapeDtypeStruct((n_slots * ROW_TILES, LANES), F32),
        name="experts",
        compiler_params=pltpu.CompilerParams(dimension_semantics=("arbitrary",), vmem_limit_bytes=VMEM_LIMIT),
    )(blk_e, blk_src, blk_used, xs, w_gate, w_up, w_down)


def _combine_kernel(pos1_ref, pos2_ref, next1_ref, next2_ref, xa_ref, xb_ref, ga_ref, gb_ref, yb_hbm, g_ref, b_ref,
                    outa_ref, outb_ref, ybuf, sems, *, tb, alpha, nba, nsteps):
    step = pl.program_id(0)
    slot = step % 2
    other = 1 - slot

    def issue(p1_ref, p2_ref, to_slot, t):
        _row_copy(yb_hbm, p1_ref[t], ybuf.at[to_slot].at[0], t, sems.at[to_slot]).start(priority=0)
        _row_copy(yb_hbm, p2_ref[t], ybuf.at[to_slot].at[1], t, sems.at[to_slot]).start(priority=1)

    def wait_slot(which):
        for k in range(TOP_K):
            pltpu.make_async_copy(yb_hbm.at[_tile_rows(0, tb), :], ybuf.at[which].at[k], sems.at[which]).wait()

    @pl.when(step == 0)
    def _():
        def body(t, carry):
            issue(pos1_ref, pos2_ref, 0, t)
            return carry
        lax.fori_loop(0, tb, body, 0, unroll=8)

    wait_slot(slot)

    def finish(x1_ref, gcol_ref, out_ref):
        n = tb // COMBINE_PIECES
        for p in range(COMBINE_PIECES):
            for t in range(p * n, (p + 1) * n):
                issue(next1_ref, next2_ref, other, t)
            gates = gcol_ref[p * n:(p + 1) * n, :]
            y = (gates[:, _R_G1:_R_G1 + 1] * _load_rows(ybuf.at[slot].at[0], n, p * n)
                 + gates[:, _R_G2:_R_G2 + 1] * _load_rows(ybuf.at[slot].at[1], n, p * n))
            out_ref[p * n:(p + 1) * n, :] = _layer_norm(alpha * x1_ref[p * n:(p + 1) * n, :] + y, g_ref[...], b_ref[...])

    @pl.when(step < nba)
    def _():
        finish(xa_ref, ga_ref, outa_ref)

    @pl.when(step >= nba)
    def _():
        finish(xb_ref, gb_ref, outb_ref)

    @pl.when(step == nsteps - 1)
    def _():
        wait_slot(other)


def _combine(xa_rows, xb_rows, gcol_a, gcol_b, yb, pos1, pos2, ln2g, ln2b, *, tb, alpha):
    nba, nbb = xa_rows.shape[0] // tb, xb_rows.shape[0] // tb
    nsteps = nba + nbb
    kern = functools.partial(_combine_kernel, tb=tb, alpha=alpha, nba=nba, nsteps=nsteps)

    def first(i):
        return jnp.minimum(i, nba - 1)

    def second(i):
        return jnp.maximum(i - nba, 0)

    def nxt(i):
        return jnp.minimum(i + 1, nsteps - 1)

    smem = functools.partial(pl.BlockSpec, (tb,), memory_space=pltpu.SMEM)
    return pl.pallas_call(
        kern, grid=(nsteps,),
        in_specs=[smem(lambda i: (i,)), smem(lambda i: (i,)), smem(lambda i: (nxt(i),)), smem(lambda i: (nxt(i),)),
                  pl.BlockSpec((tb, D_MODEL), lambda i: (first(i), 0)),
                  pl.BlockSpec((tb, D_MODEL), lambda i: (second(i), 0)),
                  pl.BlockSpec((tb, LANES), lambda i: (first(i), 0)),
                  pl.BlockSpec((tb, LANES), lambda i: (second(i), 0)),
                  pl.BlockSpec(memory_space=pl.ANY),
                  pl.BlockSpec((1, D_MODEL), lambda i: (0, 0)),
                  pl.BlockSpec((1, D_MODEL), lambda i: (0, 0))],
        out_specs=[pl.BlockSpec((tb, D_MODEL), lambda i: (first(i), 0)),
                   pl.BlockSpec((tb, D_MODEL), lambda i: (second(i), 0))],
        out_shape=[jax.ShapeDtypeStruct((nba * tb, D_MODEL), F32), jax.ShapeDtypeStruct((nbb * tb, D_MODEL), F32)],
        scratch_shapes=[pltpu.VMEM((2, TOP_K, tb * ROW_TILES, LANES), F32), pltpu.SemaphoreType.DMA((2,))],
        name="combine",
        compiler_params=pltpu.CompilerParams(dimension_semantics=("arbitrary",), vmem_limit_bytes=VMEM_LIMIT),
    )(pos1, pos2, pos1, pos2, xa_rows, xb_rows, gcol_a, gcol_b, yb, ln2g, ln2b)


def _alibi_bias():
    slopes = np.asarray([2.0 ** (-8.0 * (h + 1) / N_HEADS) for h in range(N_HEADS)], np.float32)
    qi = np.arange(CHUNK, dtype=np.int32)[:, None]
    sj = np.arange(KEYS, dtype=np.int32)[None, :]
    dist = np.abs(qi + WINDOW - sj).astype(np.float32)
    bias = -slopes.reshape(N_KV_HEADS, GROUP, 1, 1) * dist
    bias = np.transpose(bias, (1, 2, 0, 3)).reshape(GROUP * CHUNK, N_KV_HEADS * KEYS)
    return jnp.asarray(bias, F32)


def _key_selector():
    row_head = np.arange(N_KV_HEADS * KEYS)[:, None] // KEYS
    lane_head = np.arange(D_KV)[None, :] // HEAD_DIM
    return jnp.asarray((row_head == lane_head).astype(np.float32), BF16)


def _strict_upper():
    r = np.arange(RANK_CHUNK)
    return jnp.asarray((r[:, None] < r[None, :]).astype(np.float32), BF16)


def _slot_plan(route, counts, n_tok):
    nblk = (n_tok * TOP_K) // EXPERT_BLOCK + N_EXPERTS
    experts = route[_R_E1:_R_E2 + 1].astype(jnp.int32)
    ranks = route[_R_RANK1:_R_RANK2 + 1].astype(jnp.int32)
    cnt = counts[:, 0].astype(jnp.int32)
    padded = (cnt + EXPERT_BLOCK - 1) // EXPERT_BLOCK * EXPERT_BLOCK
    pad_end = jnp.cumsum(padded)
    pad_start = pad_end - padded
    ids = jnp.arange(N_EXPERTS, dtype=jnp.int32)
    pos = jnp.sum(jnp.where(experts[..., None] == ids, pad_start, 0), axis=-1) + ranks
    blk_first = jnp.arange(nblk, dtype=jnp.int32) * EXPERT_BLOCK
    blk_e = jnp.minimum(jnp.sum((pad_end[None, :] <= blk_first[:, None]).astype(jnp.int32), axis=-1), N_EXPERTS - 1)
    blk_used = (blk_first < pad_end[-1]).astype(jnp.int32)
    blk_src = jnp.minimum(jnp.arange(nblk, dtype=jnp.int32), pad_end[-1] // EXPERT_BLOCK - 1)
    tail = jnp.stack([pad_end[-1], nblk - pad_end[-1] // EXPERT_BLOCK]).astype(jnp.int32)
    return pos[0], pos[1], pad_start + cnt, padded - cnt, tail, blk_e, blk_src, blk_used, nblk * EXPERT_BLOCK


def _layer(xp, xs, cache, mix_prm, consts, moe_prm, layer, *, tl_prompt, nseq_sample, alpha):
    (bp, tp, _), (bs, ts, _) = xp.shape, xs.shape
    np_tok, ns_tok = bp * tp, bs * ts
    tb = ROWS
    assert np_tok % tb == 0 and ns_tok % tb == 0
    zeros_conv = jnp.zeros((bp, CONV_W - 1, D_CONV), F32)
    zeros_kv = jnp.zeros((bp, WINDOW, D_KV), F32)
    zero_counts = jnp.zeros((N_EXPERTS, LANES), F32)
    x1p, route_p, gcol_p, counts_p, *state_p = _mixer(
        xp, zeros_conv, zeros_kv, zeros_kv, zero_counts, mix_prm, consts,
        nseq=1, tl=tl_prompt, mask_history=True, alpha=alpha)
    x1s, route_s, gcol_s, counts, *state_s = _mixer(
        xs, *cache, counts_p, mix_prm, consts, nseq=nseq_sample, tl=ts, mask_history=False, alpha=alpha)
    route = jnp.concatenate([route_p, route_s], axis=1)
    pos1, pos2, zstart, zcount, tail, blk_e, blk_src, blk_used, n_slots = _slot_plan(route, counts, np_tok + ns_tok)
    tb_dispatch = DISPATCH_ROWS if np_tok % DISPATCH_ROWS == 0 and ns_tok % DISPATCH_ROWS == 0 else tb
    sorted_rows = _dispatch(x1p, x1s, pos1, pos2, zstart, zcount, tail, n_slots, tb=tb_dispatch)
    w_gate, w_up, w_down, ln2g, ln2b = moe_prm
    yb = _experts(sorted_rows, blk_e, blk_src, blk_used, w_gate, w_up, w_down, layer)
    out_p, out_s = _combine(x1p, x1s, gcol_p, gcol_s, yb, pos1, pos2, ln2g, ln2b, tb=tb, alpha=alpha)
    return out_p.reshape(bp, tp, D_MODEL), out_s.reshape(bs, ts, D_MODEL), state_p, state_s


def kernel(x_prompt, x_sample, cache_conv, cache_k, cache_v, w_in, conv_w, conv_b, attn_sinks, g_conv, g_attn,
           w_out, ln1_g, ln1_b, router_group_w, router_group_b, router_expert_w, router_expert_b,
           expert_w_gate, expert_w_up, expert_w_down, ln2_g, ln2_b):
    depth = w_in.shape[0]
    alpha = (2 * depth) ** 0.25
    batch, seq = x_prompt.shape[0], x_prompt.shape[1]
    dec_batch, dec_seq = x_sample.shape[0], x_sample.shape[1]
    assert dec_seq == CHUNK and seq % CHUNK == 0
    consts = (_alibi_bias(), _strict_upper(), _key_selector())
    tl_prompt = min(MIXER_ROWS, seq)
    nseq_sample = min(MIXER_ROWS // dec_seq, dec_batch)
    assert seq % tl_prompt == 0 and dec_batch % nseq_sample == 0

    def regroup_heads(a, axis):
        shape = a.shape
        a = a.reshape(shape[:axis] + (N_KV_HEADS, GROUP, HEAD_DIM) + shape[axis + 1:])
        return jnp.swapaxes(a, axis, axis + 1).reshape(shape)

    w_in_all = jnp.concatenate([w_in[:, :, :_OFF_Q], regroup_heads(w_in[:, :, _OFF_Q:_OFF_K], 2), w_in[:, :, _OFF_K:]],
                               axis=2).astype(BF16)
    w_out_all = jnp.concatenate([w_out[:, :D_CONV], regroup_heads(w_out[:, D_CONV:], 1)], axis=1).astype(BF16)
    g_attn_all = regroup_heads(g_attn, 1)
    router_pad = LANES - N_EXPERTS - N_GROUPS
    wr_all = jnp.concatenate([router_expert_w, router_group_w, jnp.zeros((depth, D_MODEL, router_pad), F32)],
                             axis=2).astype(BF16)
    br_all = jnp.concatenate([router_expert_b, router_group_b, jnp.zeros((depth, router_pad), F32)], axis=1)
    sink_all = jnp.broadcast_to(jnp.repeat(attn_sinks.astype(F32), CHUNK, axis=1).reshape(depth, N_KV_HEADS, QROWS, 1),
                                (depth, N_KV_HEADS, QROWS, LANES))

    xp, xs = x_prompt, x_sample
    states = [[] for _ in range(6)]
    for l in range(depth):
        mix_prm = (w_in_all[l], conv_w[l], conv_b[l].reshape(1, D_CONV), sink_all[l],
                   g_conv[l].reshape(1, D_CONV), g_attn_all[l].reshape(1, D_ATTN), w_out_all[l],
                   ln1_g[l].reshape(1, D_MODEL), ln1_b[l].reshape(1, D_MODEL), wr_all[l], br_all[l].reshape(1, LANES))
        moe_prm = (expert_w_gate, expert_w_up, expert_w_down,
                   ln2_g[l].reshape(1, D_MODEL), ln2_b[l].reshape(1, D_MODEL))
        cache = (cache_conv[l], cache_k[l].reshape(dec_batch, WINDOW, D_KV), cache_v[l].reshape(dec_batch, WINDOW, D_KV))
        xp, xs, state_p, state_s = _layer(xp, xs, cache, mix_prm, consts, moe_prm, l,
                                          tl_prompt=tl_prompt, nseq_sample=nseq_sample, alpha=alpha)
        for lst, val in zip(states, state_p + state_s):
            lst.append(val)

    def kv(lst, nb):
        return jnp.stack(lst).reshape(depth, nb, WINDOW, N_KV_HEADS, HEAD_DIM)

    return (xp, xs, jnp.stack(states[0]), kv(states[1], batch), kv(states[2], batch),
            jnp.stack(states[3]), kv(states[4], dec_batch), kv(states[5], dec_batch))
```
